```python
import jax, jax.numpy as jnp
from jax import lax
import numpy as np

D_MODEL = 1024
BATCH = 8
SEQ = 8192
DEPTH = 4

MLA_HEADS = 8
MLA_NOPE = 64
MLA_ROPE = 32
MLA_V = 64
Q_LORA = 384
KV_LORA = 256
MLA_WIDTH = MLA_HEADS * MLA_V

DIL_PAIRS = ((128, 1), (512, 4), (2048, 16))
DIL_GROUPS = 3
DIL_HEADS = 8
DIL_HD = 64
DIL_WIDTH = DIL_HEADS * DIL_HD
ROT_DIM = DIL_HD // 4

MIX_WIDTH = MLA_WIDTH + DIL_WIDTH
ROPE_THETA = 500000.0
Q_BLOCK = 128
EPS = 1e-6

IN_SPLITS = (Q_LORA, KV_LORA, MLA_ROPE, MLA_WIDTH, 3 * DIL_GROUPS * DIL_WIDTH, DIL_WIDTH)
IN_WIDTH = Q_LORA + KV_LORA + MLA_ROPE + MLA_WIDTH + 3 * DIL_GROUPS * DIL_WIDTH + DIL_WIDTH

kernel_name = "hymba_mla_dilated_window_encoder"


def rms_norm(x, g):
    xf = x.astype(jnp.float32)
    y = xf * lax.rsqrt(jnp.mean(xf * xf, axis=-1, keepdims=True) + EPS)
    return (y * g.astype(jnp.float32)).astype(x.dtype)


def rope_tables(seq, dim):
    inv = 1.0 / (ROPE_THETA ** (jnp.arange(0, dim, 2, dtype=jnp.float32) / dim))
    ang = jnp.arange(seq, dtype=jnp.float32)[:, None] * inv[None, :]
    return jnp.cos(ang), jnp.sin(ang)


def apply_rope(x, cos, sin):
    xf = x.astype(jnp.float32)
    x1, x2 = jnp.split(xf, 2, axis=-1)
    c = cos[:, None, :]
    s = sin[:, None, :]
    return jnp.concatenate([x1 * c - x2 * s, x1 * s + x2 * c], axis=-1).astype(x.dtype)


def partial_rope(x, cos, sin):
    return jnp.concatenate([apply_rope(x[..., :ROT_DIM], cos, sin), x[..., ROT_DIM:]], axis=-1)


def mla_attention(c_q, c_kv, k_r, q_norm_g, kv_norm_g, w_uq, w_ukv, cos, sin):
    B, S, _ = c_q.shape
    q = (rms_norm(c_q, q_norm_g) @ w_uq).reshape(B, S, MLA_HEADS, MLA_NOPE + MLA_ROPE)
    q_nope = q[..., :MLA_NOPE]
    q_rope = apply_rope(q[..., MLA_NOPE:], cos, sin)
    kv = (rms_norm(c_kv, kv_norm_g) @ w_ukv).reshape(B, S, MLA_HEADS, MLA_NOPE + MLA_V)
    k_nope = kv[..., :MLA_NOPE]
    v = kv[..., MLA_NOPE:]
    k_rope = apply_rope(k_r[:, :, None, :], cos, sin)[:, :, 0, :]
    scale = (MLA_NOPE + MLA_ROPE) ** -0.5
    nb = S // Q_BLOCK
    qn_b = q_nope.reshape(B, nb, Q_BLOCK, MLA_HEADS, MLA_NOPE).swapaxes(0, 1)
    qr_b = q_rope.reshape(B, nb, Q_BLOCK, MLA_HEADS, MLA_ROPE).swapaxes(0, 1)

    def block(args):
        qn, qr = args
        s = (jnp.einsum('bqhd,bkhd->bhqk', qn, k_nope).astype(jnp.float32)
             + jnp.einsum('bqhr,bkr->bhqk', qr, k_rope).astype(jnp.float32)) * scale
        p = jax.nn.softmax(s, axis=-1)
        return jnp.einsum('bhqk,bkhd->bqhd', p.astype(v.dtype), v)

    o = lax.map(block, (qn_b, qr_b))
    return o.swapaxes(0, 1).reshape(B, S, MLA_WIDTH)


def to_strided(t, d):
    B, S = t.shape[:2]
    rest = t.shape[2:]
    return t.reshape(B, S // d, d, *rest).swapaxes(1, 2).reshape(B * d, S // d, *rest)


def from_strided(t, B, d):
    L = t.shape[1]
    rest = t.shape[2:]
    return t.reshape(B, d, L, *rest).swapaxes(1, 2).reshape(B, L * d, *rest)


def banded_attention(q, k, v, half):
    N, L, H, Dh = q.shape
    nb = -(-L // half)
    Lp = nb * half
    pad = Lp - L
    qp = jnp.pad(q, ((0, 0), (0, pad), (0, 0), (0, 0))).reshape(N, nb, half, H, Dh)

    def key_windows(t):
        tp = jnp.pad(t, ((0, 0), (half, pad + half), (0, 0), (0, 0))).reshape(N, nb + 2, half, H, Dh)
        return jnp.concatenate([tp[:, :-2], tp[:, 1:-1], tp[:, 2:]], axis=2)

    kw = key_windows(k)
    vw = key_windows(v)
    qpos = jnp.arange(Lp).reshape(nb, half)
    kpos = (jnp.arange(nb)[:, None] - 1) * half + jnp.arange(3 * half)[None, :]
    valid = ((jnp.abs(qpos[:, :, None] - kpos[:, None, :]) <= half)
             & (kpos[:, None, :] >= 0) & (kpos[:, None, :] < L))
    s = jnp.einsum('nbqhd,nbkhd->nbhqk', qp, kw).astype(jnp.float32) * (Dh ** -0.5)
    s = jnp.where(valid[None, :, None], s, -jnp.inf)
    m = jnp.max(s, axis=-1, keepdims=True)
    e = jnp.exp(s - m)
    l = jnp.sum(e, axis=-1, keepdims=True)
    o = jnp.einsum('nbhqk,nbkhd->nbqhd', (e / l).astype(v.dtype), vw)
    lse = (m + jnp.log(l))[..., 0]
    o = o.reshape(N, Lp, H, Dh)[:, :L]
    lse = lse.swapaxes(2, 3).reshape(N, Lp, H)[:, :L]
    return o, lse


def dilated_attention(dil_qkv, cos, sin):
    B, S, _ = dil_qkv.shape
    qkv = dil_qkv.reshape(B, S, DIL_GROUPS, 3, DIL_HEADS, DIL_HD)
    outs, lses = [], []
    for g, (window, dil) in enumerate(DIL_PAIRS):
        q = partial_rope(qkv[:, :, g, 0], cos, sin)
        k = partial_rope(qkv[:, :, g, 1], cos, sin)
        v = qkv[:, :, g, 2]
        o, lse = banded_attention(to_strided(q, dil), to_strided(k, dil), to_strided(v, dil),
                                  window // (2 * dil))
        outs.append(from_strided(o, B, dil))
        lses.append(from_strided(lse, B, dil))
    alpha = jax.nn.softmax(jnp.stack(lses, axis=0), axis=0)
    out = jnp.einsum('gbsh,gbshd->bshd', alpha, jnp.stack(outs, axis=0).astype(jnp.float32))
    return out.astype(dil_qkv.dtype).reshape(B, S, DIL_WIDTH)


def _fwd_setup_inputs(seed: int = 0) -> dict:
    key = jax.random.key(seed)
    ks = jax.random.split(key, 10)
    f32 = jnp.float32
    x = jax.random.normal(ks[0], (BATCH, SEQ, D_MODEL), f32)
    norm_g = 1.0 + 0.02 * jax.random.normal(ks[1], (DEPTH, D_MODEL), f32)
    w_in = jax.random.normal(ks[2], (DEPTH, D_MODEL, IN_WIDTH), f32) * D_MODEL ** -0.5
    q_norm_g = 1.0 + 0.02 * jax.random.normal(ks[3], (DEPTH, Q_LORA), f32)
    kv_norm_g = 1.0 + 0.02 * jax.random.normal(ks[4], (DEPTH, KV_LORA), f32)
    w_uq = jax.random.normal(ks[5], (DEPTH, Q_LORA, MLA_HEADS * (MLA_NOPE + MLA_ROPE)), f32) * Q_LORA ** -0.5
    w_ukv = jax.random.normal(ks[6], (DEPTH, KV_LORA, MLA_HEADS * (MLA_NOPE + MLA_V)), f32) * KV_LORA ** -0.5
    w_out = jax.random.normal(ks[7], (DEPTH, MIX_WIDTH, D_MODEL), f32) * MIX_WIDTH ** -0.5
    final_g = 1.0 + 0.02 * jax.random.normal(ks[8], (D_MODEL,), f32)
    return {"x": x, "norm_g": norm_g, "w_in": w_in, "q_norm_g": q_norm_g, "kv_norm_g": kv_norm_g,
            "w_uq": w_uq, "w_ukv": w_ukv, "w_out": w_out, "final_g": final_g}


def _fwd_reference(x, norm_g, w_in, q_norm_g, kv_norm_g, w_uq, w_ukv, w_out, final_g):
    S = x.shape[1]
    cos_m, sin_m = rope_tables(S, MLA_ROPE)
    cos_d, sin_d = rope_tables(S, ROT_DIM)
    split_points = [sum(IN_SPLITS[:i + 1]) for i in range(len(IN_SPLITS) - 1)]
    for layer in range(DEPTH):
        h = rms_norm(x, norm_g[layer])
        p = h @ w_in[layer]
        c_q, c_kv, k_r, gate_a, dil_qkv, gate_b = jnp.split(p, split_points, axis=-1)
        a = mla_attention(c_q, c_kv, k_r, q_norm_g[layer], kv_norm_g[layer],
                          w_uq[layer], w_ukv[layer], cos_m, sin_m) * jax.nn.silu(gate_a)
        b = dilated_attention(dil_qkv, cos_d, sin_d) * jax.nn.silu(gate_b)
        x = x + jnp.concatenate([a, b], axis=-1) @ w_out[layer]
    return rms_norm(x, final_g)


import jax as _jax
import jax.numpy as _jnp

TWIN_FORMAT = 'train_step'
FWD_PARAMS = ['x', 'norm_g', 'w_in', 'q_norm_g', 'kv_norm_g', 'w_uq', 'w_ukv', 'w_out', 'final_g']
TWIN_WEIGHTS = ['norm_g', 'w_in', 'q_norm_g', 'kv_norm_g', 'w_uq', 'w_ukv', 'w_out', 'final_g']
TWIN_DIFF_INPUT = 'x'
TWIN_INPUTS = ['x', 'norm_g', 'w_in', 'q_norm_g', 'kv_norm_g', 'w_uq', 'w_ukv', 'w_out', 'final_g', 'loss_target', 'm_norm_g', 'm_w_in', 'm_q_norm_g', 'm_kv_norm_g', 'm_w_uq', 'm_w_ukv', 'm_w_out', 'm_final_g', 'v_norm_g', 'v_w_in', 'v_q_norm_g', 'v_kv_norm_g', 'v_w_uq', 'v_w_ukv', 'v_w_out', 'v_final_g']
TWIN_OUTPUTS = ['loss', 'grad_x', 'grad_norm_g', 'grad_w_in', 'grad_q_norm_g', 'grad_kv_norm_g', 'grad_w_uq', 'grad_w_ukv', 'grad_w_out', 'grad_final_g', 'delta_norm_g', 'delta_w_in', 'delta_q_norm_g', 'delta_kv_norm_g', 'delta_w_uq', 'delta_w_ukv', 'delta_w_out', 'delta_final_g', 'new_m_norm_g', 'new_m_w_in', 'new_m_q_norm_g', 'new_m_kv_norm_g', 'new_m_w_uq', 'new_m_w_ukv', 'new_m_w_out', 'new_m_final_g', 'new_v_norm_g', 'new_v_w_in', 'new_v_q_norm_g', 'new_v_kv_norm_g', 'new_v_w_uq', 'new_v_w_ukv', 'new_v_w_out', 'new_v_final_g']
TWIN_LEAF_KINDS = {'loss': 'loss', 'grad_x': 'grad_x', 'grad_norm_g': 'grad_w', 'grad_w_in': 'grad_w', 'grad_q_norm_g': 'grad_w', 'grad_kv_norm_g': 'grad_w', 'grad_w_uq': 'grad_w', 'grad_w_ukv': 'grad_w', 'grad_w_out': 'grad_w', 'grad_final_g': 'grad_w', 'delta_norm_g': 'delta_w', 'delta_w_in': 'delta_w', 'delta_q_norm_g': 'delta_w', 'delta_kv_norm_g': 'delta_w', 'delta_w_uq': 'delta_w', 'delta_w_ukv': 'delta_w', 'delta_w_out': 'delta_w', 'delta_final_g': 'delta_w', 'new_m_norm_g': 'new_m', 'new_m_w_in': 'new_m', 'new_m_q_norm_g': 'new_m', 'new_m_kv_norm_g': 'new_m', 'new_m_w_uq': 'new_m', 'new_m_w_ukv': 'new_m', 'new_m_w_out': 'new_m', 'new_m_final_g': 'new_m', 'new_v_norm_g': 'new_v', 'new_v_w_in': 'new_v', 'new_v_q_norm_g': 'new_v', 'new_v_kv_norm_g': 'new_v', 'new_v_w_uq': 'new_v', 'new_v_w_ukv': 'new_v', 'new_v_w_out': 'new_v', 'new_v_final_g': 'new_v'}


def _forward(args):
    return _fwd_reference(*[args[k] for k in FWD_PARAMS])


def _output_shape():
    def fwd():
        inp = _fwd_setup_inputs(0)
        return _fwd_reference(*[inp[k] for k in FWD_PARAMS])
    out = _jax.eval_shape(fwd)
    return out.shape, out.dtype

N_MICROBATCH = 1
ADAM_LR = 0.001
ADAM_B1 = 0.9
ADAM_B2 = 0.999
ADAM_EPS = 1e-08
ADAM_WD = 0.01
ADAM_STEP = 10
PER_EXAMPLE_BATCH_AXIS = {'x': 0, 'loss_target': 0}
SHARED_INPUTS = []
_WEIGHT_DTYPES = {'norm_g': _jnp.float32, 'w_in': _jnp.float32, 'q_norm_g': _jnp.float32, 'kv_norm_g': _jnp.float32, 'w_uq': _jnp.float32, 'w_ukv': _jnp.float32, 'w_out': _jnp.float32, 'final_g': _jnp.float32}
MOMENT_SCALE = {'norm_g': 3.611139e-02, 'w_in': 1.407992e-02, 'q_norm_g': 1.983571e-02, 'kv_norm_g': 3.551665e-02, 'w_uq': 1.397540e-02, 'w_ukv': 1.625260e-02, 'w_out': 1.790650e-02, 'final_g': 6.401274e+01}


def _to_microbatches(a, axis):
    t = _jnp.moveaxis(a, axis, 0)
    t = t.reshape((N_MICROBATCH, t.shape[0] // N_MICROBATCH) + t.shape[1:])
    return _jnp.moveaxis(t, 1, axis + 1)


def setup_inputs(seed: int = 0) -> dict:
    inp = _fwd_setup_inputs(seed)
    key = _jax.random.fold_in(_jax.random.key(seed), 7919)
    shape, _ = _output_shape()
    out = dict(inp)
    out["loss_target"] = _jax.random.normal(_jax.random.fold_in(key, 0), shape, _jnp.float32)
    for i, name in enumerate(TWIN_WEIGHTS):
        w = inp[name].astype(_jnp.float32)
        if MOMENT_SCALE is None:
            s = _jnp.sqrt(_jnp.mean(_jnp.square(w)) + 1e-30)
        else:
            s = MOMENT_SCALE[name]
        km, kv = _jax.random.split(_jax.random.fold_in(key, i + 1))
        out[name] = w
        out["m_" + name] = s * _jax.random.normal(km, w.shape, _jnp.float32)
        out["v_" + name] = (s * s) * _jax.random.uniform(kv, w.shape, _jnp.float32, 0.5, 1.5)
    if N_MICROBATCH > 1:
        for name, axis in PER_EXAMPLE_BATCH_AXIS.items():
            out[name] = _to_microbatches(out[name], axis)
    return {'x': out['x'], 'norm_g': out['norm_g'], 'w_in': out['w_in'], 'q_norm_g': out['q_norm_g'], 'kv_norm_g': out['kv_norm_g'], 'w_uq': out['w_uq'], 'w_ukv': out['w_ukv'], 'w_out': out['w_out'], 'final_g': out['final_g'], 'loss_target': out['loss_target'], 'm_norm_g': out['m_norm_g'], 'm_w_in': out['m_w_in'], 'm_q_norm_g': out['m_q_norm_g'], 'm_kv_norm_g': out['m_kv_norm_g'], 'm_w_uq': out['m_w_uq'], 'm_w_ukv': out['m_w_ukv'], 'm_w_out': out['m_w_out'], 'm_final_g': out['m_final_g'], 'v_norm_g': out['v_norm_g'], 'v_w_in': out['v_w_in'], 'v_q_norm_g': out['v_q_norm_g'], 'v_kv_norm_g': out['v_kv_norm_g'], 'v_w_uq': out['v_w_uq'], 'v_w_ukv': out['v_w_ukv'], 'v_w_out': out['v_w_out'], 'v_final_g': out['v_final_g']}


def _loss(weights, diff, rest, loss_target):
    with _jax.named_scope("forward"):
        args = {**rest, TWIN_DIFF_INPUT: diff, **{k: w.astype(_WEIGHT_DTYPES[k]) for k, w in weights.items()}}
        y = _forward(args)
    with _jax.named_scope("loss_head"):
        err = _jnp.square(y.astype(_jnp.float32) - loss_target)
        return 0.5 * _jnp.sum(_jnp.mean(err, axis=-1)) if err.ndim else 0.5 * err


def _adamw(w, g, m, v):
    m = ADAM_B1 * m + (1.0 - ADAM_B1) * g
    v = ADAM_B2 * v + (1.0 - ADAM_B2) * _jnp.square(g)
    m_hat = m / (1.0 - ADAM_B1 ** ADAM_STEP)
    v_hat = v / (1.0 - ADAM_B2 ** ADAM_STEP)
    delta = -ADAM_LR * (m_hat / (_jnp.sqrt(v_hat) + ADAM_EPS) + ADAM_WD * w)
    return delta, m, v


def reference(x, norm_g, w_in, q_norm_g, kv_norm_g, w_uq, w_ukv, w_out, final_g, loss_target, m_norm_g, m_w_in, m_q_norm_g, m_kv_norm_g, m_w_uq, m_w_ukv, m_w_out, m_final_g, v_norm_g, v_w_in, v_q_norm_g, v_kv_norm_g, v_w_uq, v_w_ukv, v_w_out, v_final_g):
    given = dict(x=x, norm_g=norm_g, w_in=w_in, q_norm_g=q_norm_g, kv_norm_g=kv_norm_g, w_uq=w_uq, w_ukv=w_ukv, w_out=w_out, final_g=final_g, loss_target=loss_target, m_norm_g=m_norm_g, m_w_in=m_w_in, m_q_norm_g=m_q_norm_g, m_kv_norm_g=m_kv_norm_g, m_w_uq=m_w_uq, m_w_ukv=m_w_ukv, m_w_out=m_w_out, m_final_g=m_final_g, v_norm_g=v_norm_g, v_w_in=v_w_in, v_q_norm_g=v_q_norm_g, v_kv_norm_g=v_kv_norm_g, v_w_uq=v_w_uq, v_w_ukv=v_w_ukv, v_w_out=v_w_out, v_final_g=v_final_g)
    weights = {n: given[n] for n in TWIN_WEIGHTS}
    shared = {n: given[n] for n in SHARED_INPUTS}
    per_example = {n: given[n] for n in ['x']}
    grad_fn = _jax.value_and_grad(_loss, argnums=(0, 1))

    def one_microbatch(ex, loss_target):
        ex = dict(ex)
        diff = ex.pop(TWIN_DIFF_INPUT)
        return grad_fn(weights, diff, {**shared, **ex}, loss_target)

    if N_MICROBATCH == 1:
        loss, (grad_w, grad_x) = one_microbatch(per_example, given["loss_target"])
    else:
        def body(carry, xs):
            loss_sum, grad_sum = carry
            l_k, (gw_k, gx_k) = one_microbatch(xs[0], xs[1])
            with _jax.named_scope("update"):
                return (loss_sum + l_k, _jax.tree.map(_jnp.add, grad_sum, gw_k)), gx_k

        init = (_jnp.zeros((), _jnp.float32), _jax.tree.map(_jnp.zeros_like, weights))
        (loss, grad_w), grad_x = _jax.lax.scan(body, init, (per_example, given["loss_target"]))
    with _jax.named_scope("update"):
        delta_w, new_m, new_v = {}, {}, {}
        for n in TWIN_WEIGHTS:
            delta_w[n], new_m[n], new_v[n] = _adamw(weights[n], grad_w[n], given["m_" + n], given["v_" + n])
    return (loss, grad_x, *[grad_w[n] for n in TWIN_WEIGHTS], *[delta_w[n] for n in TWIN_WEIGHTS],
            *[new_m[n] for n in TWIN_WEIGHTS], *[new_v[n] for n in TWIN_WEIGHTS])
```

```python
import functools

import jax
import jax.numpy as jnp
from jax import lax
from jax.experimental import pallas as pl
from jax.experimental.pallas import tpu as pltpu

F32 = jnp.float32
BF16 = jnp.bfloat16

D_MODEL = 1024
DEPTH = 4
HEADS = 8
NOPE = 64
ROPE = 32
VDIM = 64
Q_LORA = 384
KV_LORA = 256
DIL_PAIRS = ((128, 1), (512, 4), (2048, 16))
DIL_HD = 64
DIL_W = 512
ROT = 16
HALF = 64
THETA = 500000.0
EPS = 1e-6
IN_WIDTH = 6304
N_DEV = 8

LANE = 128
CB_CKV, CB_CQ, CB_GA, CB_DIL, CB_GB, CB_KR = 0, 3, 6, 14, 50, 54
NP = 55 * LANE
HP = HEADS * LANE

ADAM_LR = 0.001
ADAM_B1 = 0.9
ADAM_B2 = 0.999
ADAM_EPS = 1e-08
ADAM_WD = 0.01
ADAM_STEP = 10

VMEM_LIMIT = 48 * 1024 * 1024
ROW_TILE = 512

_NT = (((1,), (1,)), ((), ()))
_NN = (((1,), (0,)), ((), ()))
_TN = (((0,), (0,)), ((), ()))


def _params(n_axes):
    return pltpu.CompilerParams(dimension_semantics=("arbitrary",) * n_axes, vmem_limit_bytes=VMEM_LIMIT)


def _pick(n, cands):
    for c in cands:
        if n % c == 0:
            return c
    raise ValueError(f"no tile for {n}")


def _mm(a, b, mode, name, out_dtype=F32, res=None):
    if mode == "nn":
        (M, K), (K2, N) = a.shape, b.shape
    elif mode == "nt":
        (M, K), (N, K2) = a.shape, b.shape
    else:
        (K, M), (K2, N) = a.shape, b.shape
    assert K == K2, (a.shape, b.shape, mode)
    tm = _pick(M, (512, 384, 256, 128))
    tn = _pick(N, (1408, 1024, 768, 640, 512, 384, 256, 128))
    tk = _pick(K, (1408, 1024, 768, 640, 512, 384, 256, 128))
    nk = K // tk
    dims = {"nn": _NN, "nt": _NT, "tn": _TN}[mode]

    def body(*refs):
        if res is not None:
            a_ref, b_ref, r_ref, o_ref = refs[:4]
        else:
            a_ref, b_ref, o_ref = refs[:3]
            r_ref = None
        part = lax.dot_general(a_ref[...].astype(BF16), b_ref[...].astype(BF16), dims, preferred_element_type=F32)

        def finish(acc):
            if r_ref is not None:
                acc = acc + r_ref[...]
            o_ref[...] = acc.astype(out_dtype)

        if nk == 1:
            finish(part)
        else:
            acc_ref = refs[-1]
            k = pl.program_id(2)

            @pl.when(k == 0)
            def _():
                acc_ref[...] = part

            @pl.when(k > 0)
            def _():
                acc_ref[...] += part

            @pl.when(k == nk - 1)
            def _():
                finish(acc_ref[...])

    if mode == "nn":
        a_spec = pl.BlockSpec((tm, tk), lambda i, j, k: (i, k))
        b_spec = pl.BlockSpec((tk, tn), lambda i, j, k: (k, j))
    elif mode == "nt":
        a_spec = pl.BlockSpec((tm, tk), lambda i, j, k: (i, k))
        b_spec = pl.BlockSpec((tn, tk), lambda i, j, k: (j, k))
    else:
        a_spec = pl.BlockSpec((tk, tm), lambda i, j, k: (k, i))
        b_spec = pl.BlockSpec((tk, tn), lambda i, j, k: (k, j))
    o_spec = pl.BlockSpec((tm, tn), lambda i, j, k: (i, j))
    in_specs = [a_spec, b_spec] + ([o_spec] if res is not None else [])
    args = (a, b) + ((res,) if res is not None else ())
    return pl.pallas_call(
        body, name=name, grid=(M // tm, N // tn, nk), out_shape=jax.ShapeDtypeStruct((M, N), out_dtype),
        in_specs=in_specs, out_specs=o_spec,
        scratch_shapes=[pltpu.VMEM((tm, tn), F32)] if nk > 1 else [],
        compiler_params=_params(3),
    )(*args)


def _rms_fwd(src, cb, width, g, name):
    S = src.shape[0]
    tm = ROW_TILE

    def body(x_ref, g_ref, o_ref):
        x = x_ref[...]
        r = lax.rsqrt(jnp.mean(x * x, axis=-1, keepdims=True) + EPS)
        o_ref[...] = (x * r * g_ref[...]).astype(BF16)

    return pl.pallas_call(
        body, name=name, grid=(S // tm,), out_shape=jax.ShapeDtypeStruct((S, width), BF16),
        in_specs=[pl.BlockSpec((tm, width), lambda i: (i, cb)), pl.BlockSpec((1, width), lambda i: (0, 0))],
        out_specs=pl.BlockSpec((tm, width), lambda i: (i, 0)), compiler_params=_params(1),
    )(src, g.reshape(1, width))


def _rms_bwd(src, cb, width, dy, g, name, out_dtype, res=None):
    S = src.shape[0]
    tm = ROW_TILE

    def body(*refs):
        if res is not None:
            x_ref, dy_ref, g_ref, r_ref, dx_ref, dg_ref = refs
        else:
            x_ref, dy_ref, g_ref, dx_ref, dg_ref = refs
            r_ref = None
        x = x_ref[...]
        dy = dy_ref[...]
        r = lax.rsqrt(jnp.mean(x * x, axis=-1, keepdims=True) + EPS)
        dyg = dy * g_ref[...]
        c = jnp.mean(dyg * x, axis=-1, keepdims=True)
        dx = r * dyg - x * (r * r * r) * c
        if r_ref is not None:
            dx = dx + r_ref[...]
        dx_ref[...] = dx.astype(out_dtype)
        part = jnp.sum(dy * x * r, axis=0, keepdims=True)

        @pl.when(pl.program_id(0) == 0)
        def _():
            dg_ref[...] = part

        @pl.when(pl.program_id(0) > 0)
        def _():
            dg_ref[...] += part

    row = pl.BlockSpec((tm, width), lambda i: (i, 0))
    in_specs = [pl.BlockSpec((tm, width), lambda i: (i, cb)), row, pl.BlockSpec((1, width), lambda i: (0, 0))]
    args = [src, dy, g.reshape(1, width)]
    if res is not None:
        in_specs.append(row)
        args.append(res)
    return pl.pallas_call(
        body, name=name, grid=(S // tm,),
        out_shape=(jax.ShapeDtypeStruct((S, width), out_dtype), jax.ShapeDtypeStruct((1, width), F32)),
        in_specs=in_specs, out_specs=(row, pl.BlockSpec((1, width), lambda i: (0, 0))),
        compiler_params=_params(1),
    )(*args)


def _rot(x, c, s1, s2, h):
    return x * c + pltpu.roll(x, LANE - h, 1) * s1 + pltpu.roll(x, h, 1) * s2


def _rope_tables(S):
    def tables(dim):
        inv = 1.0 / (THETA ** (jnp.arange(0, dim, 2, dtype=F32) / dim))
        ang = jnp.arange(S, dtype=F32)[:, None] * inv[None, :]
        return jnp.cos(ang), jnp.sin(ang)

    cm, sm = tables(ROPE)
    cd, sd = tables(ROT)
    z = lambda n: jnp.zeros((S, n), F32)
    o = lambda n: jnp.ones((S, n), F32)
    mla = dict(
        c_q=jnp.concatenate([o(64), cm, cm, z(32)], 1),
        c_kr=jnp.concatenate([z(64), cm, cm, z(32)], 1),
        s1=jnp.concatenate([z(64), -sm, z(16), z(32)], 1),
        s2=jnp.concatenate([z(64), z(16), sm, z(32)], 1),
    )
    one = lambda a, b, c: jnp.concatenate([a, b, c, a, b, c], 1)
    dil = dict(c=one(cd, cd, o(48)), s1=one(-sd, z(8), z(48)), s2=one(z(8), sd, z(48)))
    return mla, dil


def _rope_cols(src, cb, nblk, c, s1, s2, h, name, out_dtype):
    S = src.shape[0]
    tm = ROW_TILE

    def body(x_ref, c_ref, s1_ref, s2_ref, o_ref):
        o_ref[...] = _rot(x_ref[...], c_ref[...], s1_ref[...], s2_ref[...], h).astype(out_dtype)

    tab = pl.BlockSpec((tm, LANE), lambda i, j: (i, 0))
    return pl.pallas_call(
        body, name=name, grid=(S // tm, nblk), out_shape=jax.ShapeDtypeStruct((S, nblk * LANE), out_dtype),
        in_specs=[pl.BlockSpec((tm, LANE), lambda i, j: (i, cb + j)), tab, tab, tab],
        out_specs=pl.BlockSpec((tm, LANE), lambda i, j: (i, j)), compiler_params=_params(2),
    )(src, c, s1, s2)


def _k_assemble(kpre, p, mla, name):
    S = kpre.shape[0]
    tm = ROW_TILE

    def body(k_ref, kr_ref, c_ref, s1_ref, s2_ref, o_ref):
        o_ref[...] = (k_ref[...] + _rot(kr_ref[...], c_ref[...], s1_ref[...], s2_ref[...], ROPE // 2)).astype(BF16)

    tab = pl.BlockSpec((tm, LANE), lambda i, j: (i, 0))
    return pl.pallas_call(
        body, name=name, grid=(S // tm, HEADS), out_shape=jax.ShapeDtypeStruct((S, HP), BF16),
        in_specs=[pl.BlockSpec((tm, LANE), lambda i, j: (i, j)), pl.BlockSpec((tm, LANE), lambda i, j: (i, CB_KR)), tab, tab, tab],
        out_specs=pl.BlockSpec((tm, LANE), lambda i, j: (i, j)), compiler_params=_params(2),
    )(kpre, p, mla["c_kr"], mla["s1"], mla["s2"])


def _kr_bwd(dk, mla, name):
    S = dk.shape[0]
    tm = ROW_TILE

    def body(dk_ref, c_ref, s1_ref, s2_ref, o_ref):
        t = dk_ref[:, 0:LANE]
        for h in range(1, HEADS):
            t = t + dk_ref[:, h * LANE:(h + 1) * LANE]
        lane = lax.broadcasted_iota(jnp.int32, (1, LANE), 1)
        t = jnp.where((lane >= NOPE) & (lane < NOPE + ROPE), t, 0.0)
        o_ref[...] = _rot(t, c_ref[...], -s1_ref[...], -s2_ref[...], ROPE // 2).astype(BF16)

    tab = pl.BlockSpec((tm, LANE), lambda i: (i, 0))
    return pl.pallas_call(
        body, name=name, grid=(S // tm,), out_shape=jax.ShapeDtypeStruct((S, LANE), BF16),
        in_specs=[pl.BlockSpec((tm, HP), lambda i: (i, 0)), tab, tab, tab],
        out_specs=tab, compiler_params=_params(1),
    )(dk, mla["c_kr"], mla["s1"], mla["s2"])


def _dil_prep(p, dil, name):
    S = p.shape[0]
    tm = ROW_TILE
    nblk = 36

    def body(x_ref, c_ref, s1_ref, s2_ref, o_ref):
        t = (pl.program_id(1) // 4) % 3

        @pl.when(t == 0)
        def _():
            o_ref[...] = (_rot(x_ref[...], c_ref[...], s1_ref[...], s2_ref[...], ROT // 2) * (DIL_HD ** -0.5)).astype(BF16)

        @pl.when(t == 1)
        def _():
            o_ref[...] = _rot(x_ref[...], c_ref[...], s1_ref[...], s2_ref[...], ROT // 2).astype(BF16)

        @pl.when(t == 2)
        def _():
            o_ref[...] = x_ref[...].astype(BF16)

    tab = pl.BlockSpec((tm, LANE), lambda i, j: (i, 0))
    return pl.pallas_call(
        body, name=name, grid=(S // tm, nblk), out_shape=jax.ShapeDtypeStruct((S, nblk * LANE), BF16),
        in_specs=[pl.BlockSpec((tm, LANE), lambda i, j: (i, CB_DIL + j)), tab, tab, tab],
        out_specs=pl.BlockSpec((tm, LANE), lambda i, j: (i, j)), compiler_params=_params(2),
    )(p, dil["c"], dil["s1"], dil["s2"])


def _flash_fwd(q, k, v, name):
    S = q.shape[0]
    tq = tk = 512
    nk = S // tk
    scale = (NOPE + ROPE) ** -0.5

    def body(q_ref, k_ref, v_ref, o_ref, lse_ref, m_s, l_s, acc_s):
        j = pl.program_id(2)

        @pl.when(j == 0)
        def _():
            m_s[...] = jnp.full(m_s.shape, -jnp.inf, F32)
            l_s[...] = jnp.zeros(l_s.shape, F32)
            acc_s[...] = jnp.zeros(acc_s.shape, F32)

        s = lax.dot_general(q_ref[...], k_ref[...], _NT, preferred_element_type=F32) * scale
        m_prev = m_s[...]
        m_new = jnp.maximum(m_prev, jnp.max(s, axis=-1, keepdims=True))
        alpha = jnp.exp(m_prev - m_new)
        e = jnp.exp(s - m_new)
        l_s[...] = alpha * l_s[...] + jnp.sum(e, axis=-1, keepdims=True)
        acc_s[...] = alpha * acc_s[...] + jnp.dot(e.astype(BF16), v_ref[...], preferred_element_type=F32)
        m_s[...] = m_new

        @pl.when(j == nk - 1)
        def _():
            l = l_s[...]
            o_ref[...] = acc_s[...] / l
            lse_ref[...] = jnp.broadcast_to(m_s[...] + jnp.log(l), lse_ref.shape)

    qs = pl.BlockSpec((tq, LANE), lambda h, i, j: (i, h))
    ks = pl.BlockSpec((tk, LANE), lambda h, i, j: (j, h))
    return pl.pallas_call(
        body, name=name, grid=(HEADS, S // tq, nk),
        out_shape=(jax.ShapeDtypeStruct((S, HP), F32), jax.ShapeDtypeStruct((S, HP), F32)),
        in_specs=[qs, ks, ks], out_specs=(qs, qs),
        scratch_shapes=[pltpu.VMEM((tq, 1), F32), pltpu.VMEM((tq, 1), F32), pltpu.VMEM((tq, LANE), F32)],
        compiler_params=_params(3),
    )(q, k, v)


def _flash_bwd(q, k, v, o, do, lse, name):
    S = q.shape[0]
    tq = tk = 512
    nq = S // tq
    scale = (NOPE + ROPE) ** -0.5

    def body(q_ref, k_ref, v_ref, o_ref, do_ref, lse_ref, dq_ref, dk_ref, dv_ref, dk_s, dv_s):
        j = pl.program_id(1)
        i = pl.program_id(2)
        qv = q_ref[...]
        kv = k_ref[...]
        do = do_ref[...]
        dob = do.astype(BF16)
        s = lax.dot_general(qv, kv, _NT, preferred_element_type=F32) * scale
        p = jnp.exp(s - lse_ref[:, 0:1])
        dp = lax.dot_general(dob, v_ref[...], _NT, preferred_element_type=F32)
        delta = jnp.sum(do * o_ref[...], axis=-1, keepdims=True)
        ds = (p * (dp - delta)).astype(BF16)
        dv_part = lax.dot_general(p.astype(BF16), dob, _TN, preferred_element_type=F32)
        dk_part = lax.dot_general(ds, qv, _TN, preferred_element_type=F32)
        dq_part = jnp.dot(ds, kv, preferred_element_type=F32) * scale

        @pl.when(i == 0)
        def _():
            dk_s[...] = dk_part
            dv_s[...] = dv_part

        @pl.when(i > 0)
        def _():
            dk_s[...] += dk_part
            dv_s[...] += dv_part

        rows = pl.ds(pl.multiple_of(i * tq, tq), tq)

        @pl.when(j == 0)
        def _():
            dq_ref[rows, :] = dq_part

        @pl.when(j > 0)
        def _():
            dq_ref[rows, :] += dq_part

        @pl.when(i == nq - 1)
        def _():
            dk_ref[...] = dk_s[...] * scale
            dv_ref[...] = dv_s[...].astype(BF16)

    qs = pl.BlockSpec((tq, LANE), lambda h, j, i: (i, h))
    ks = pl.BlockSpec((tk, LANE), lambda h, j, i: (j, h))
    return pl.pallas_call(
        body, name=name, grid=(HEADS, S // tk, nq),
        out_shape=(jax.ShapeDtypeStruct((S, HP), F32), jax.ShapeDtypeStruct((S, HP), F32), jax.ShapeDtypeStruct((S, HP), BF16)),
        in_specs=[qs, ks, ks, qs, qs, qs],
        out_specs=(pl.BlockSpec((S, LANE), lambda h, j, i: (0, h)), ks, ks),
        scratch_shapes=[pltpu.VMEM((tk, LANE), F32), pltpu.VMEM((tk, LANE), F32)],
        compiler_params=_params(3),
    )(q, k, v, o, do, lse)


BAND_TQ = 128


def _band_specs(n64, col, center_first):
    rpb = BAND_TQ // HALF
    prev = pl.BlockSpec((HALF, LANE), lambda r, hp, i: (jnp.maximum(rpb * i - 1, 0), col(r, hp)))
    cur = pl.BlockSpec((BAND_TQ, LANE), lambda r, hp, i: (i, col(r, hp)))
    nxt = pl.BlockSpec((HALF, LANE), lambda r, hp, i: (jnp.minimum(rpb * i + rpb, n64 - 1), col(r, hp)))
    return [prev, cur, nxt]


def _lo_lanes():
    return lax.broadcasted_iota(jnp.int32, (1, LANE), 1) < DIL_HD


def _band_fwd(dilr, g, d, name):
    S = dilr.shape[0]
    L = S // d
    tq = BAND_TQ
    W = dilr.shape[1] // LANE
    view = dilr.reshape(L, d * dilr.shape[1])

    def body(q_ref, kp_ref, kc_ref, kn_ref, vp_ref, vc_ref, vn_ref, o_ref, lse_ref):
        i = pl.program_id(2)
        q = q_ref[...]
        kcat = jnp.concatenate([kp_ref[...], kc_ref[...], kn_ref[...]], axis=0)
        vcat = jnp.concatenate([vp_ref[...], vc_ref[...], vn_ref[...]], axis=0)
        lo = _lo_lanes()
        qpos = i * tq + lax.broadcasted_iota(jnp.int32, (tq, 1), 0)
        kpos = i * tq - HALF + lax.broadcasted_iota(jnp.int32, (1, tq + 2 * HALF), 1)
        valid = (jnp.abs(qpos - kpos) <= HALF) & (kpos >= 0) & (kpos < L)
        outs, lses = [], []
        for hh in range(2):
            sel = lo if hh == 0 else jnp.logical_not(lo)
            qm = jnp.where(sel, q, jnp.zeros_like(q))
            s = lax.dot_general(qm, kcat, _NT, preferred_element_type=F32)
            s = jnp.where(valid, s, -jnp.inf)
            m = jnp.max(s, axis=-1, keepdims=True)
            e = jnp.exp(s - m)
            l = jnp.sum(e, axis=-1, keepdims=True)
            outs.append(jnp.dot(e.astype(BF16), vcat, preferred_element_type=F32) / l)
            lses.append(m + jnp.log(l))
        o_ref[...] = jnp.where(lo, outs[0], outs[1])
        lse_ref[...] = jnp.where(lo, lses[0], lses[1])

    n64 = L // HALF
    qcol = lambda r, hp: r * W + g * 12 + hp
    kcol = lambda r, hp: r * W + g * 12 + 4 + hp
    vcol = lambda r, hp: r * W + g * 12 + 8 + hp
    ocol = lambda r, hp: r * 4 + hp
    out_spec = pl.BlockSpec((tq, LANE), lambda r, hp, i: (i, ocol(r, hp)))
    o, lse = pl.pallas_call(
        body, name=name, grid=(d, 4, L // tq),
        out_shape=(jax.ShapeDtypeStruct((L, d * DIL_W), F32), jax.ShapeDtypeStruct((L, d * DIL_W), F32)),
        in_specs=[pl.BlockSpec((tq, LANE), lambda r, hp, i: (i, qcol(r, hp)))] + _band_specs(n64, kcol, False) + _band_specs(n64, vcol, False),
        out_specs=(out_spec, out_spec), compiler_params=_params(3),
    )(view, view, view, view, view, view, view)
    return o.reshape(S, DIL_W), lse.reshape(S, DIL_W)


def _band_bwd_q(dilr, g, d, do, lse, dlt, dil, name):
    S = dilr.shape[0]
    L = S // d
    tq = BAND_TQ
    W = dilr.shape[1] // LANE
    view = dilr.reshape(L, d * dilr.shape[1])
    v4 = lambda t: t.reshape(L, d * DIL_W)
    tv = lambda t: t.reshape(L, d * LANE)

    def body(q_ref, kp_ref, kc_ref, kn_ref, vp_ref, vc_ref, vn_ref, do_ref, lse_ref, dlt_ref, c_ref, s1_ref, s2_ref, dq_ref):
        i = pl.program_id(2)
        q = q_ref[...]
        do = do_ref[...]
        kcat = jnp.concatenate([kp_ref[...], kc_ref[...], kn_ref[...]], axis=0)
        vcat = jnp.concatenate([vp_ref[...], vc_ref[...], vn_ref[...]], axis=0)
        lo = _lo_lanes()
        qpos = i * tq + lax.broadcasted_iota(jnp.int32, (tq, 1), 0)
        kpos = i * tq - HALF + lax.broadcasted_iota(jnp.int32, (1, tq + 2 * HALF), 1)
        valid = (jnp.abs(qpos - kpos) <= HALF) & (kpos >= 0) & (kpos < L)
        dqs = []
        for hh in range(2):
            sel = lo if hh == 0 else jnp.logical_not(lo)
            c0 = hh * DIL_HD
            qm = jnp.where(sel, q, jnp.zeros_like(q))
            dom = jnp.where(sel, do, 0.0).astype(BF16)
            s = lax.dot_general(qm, kcat, _NT, preferred_element_type=F32)
            p = jnp.where(valid, jnp.exp(s - lse_ref[:, c0:c0 + 1]), 0.0)
            dp = lax.dot_general(dom, vcat, _NT, preferred_element_type=F32)
            ds = (p * (dp - dlt_ref[:, c0:c0 + 1])).astype(BF16)
            dqs.append(jnp.dot(ds, kcat, preferred_element_type=F32))
        dq = jnp.where(lo, dqs[0], dqs[1])
        dq_ref[...] = (_rot(dq, c_ref[...], -s1_ref[...], -s2_ref[...], ROT // 2) * (DIL_HD ** -0.5)).astype(BF16)

    n64 = L // HALF
    qcol = lambda r, hp: r * W + g * 12 + hp
    kcol = lambda r, hp: r * W + g * 12 + 4 + hp
    vcol = lambda r, hp: r * W + g * 12 + 8 + hp
    row = pl.BlockSpec((tq, LANE), lambda r, hp, i: (i, r * 4 + hp))
    tab = pl.BlockSpec((tq, LANE), lambda r, hp, i: (i, r))
    dq = pl.pallas_call(
        body, name=name, grid=(d, 4, L // tq), out_shape=jax.ShapeDtypeStruct((L, d * DIL_W), BF16),
        in_specs=[pl.BlockSpec((tq, LANE), lambda r, hp, i: (i, qcol(r, hp)))] + _band_specs(n64, kcol, False) + _band_specs(n64, vcol, False)
        + [row, row, row, tab, tab, tab],
        out_specs=row, compiler_params=_params(3),
    )(view, view, view, view, view, view, view, v4(do), v4(lse), v4(dlt), tv(dil["c"]), tv(dil["s1"]), tv(dil["s2"]))
    return dq.reshape(S, DIL_W)


def _band_bwd_kv(dilr, g, d, do, lse, dlt, dil, name):
    S = dilr.shape[0]
    L = S // d
    tq = BAND_TQ
    W = dilr.shape[1] // LANE
    view = dilr.reshape(L, d * dilr.shape[1])
    v4 = lambda t: t.reshape(L, d * DIL_W)
    tv = lambda t: t.reshape(L, d * LANE)

    def body(k_ref, v_ref, qp_ref, qc_ref, qn_ref, dop_ref, doc_ref, don_ref, lp_ref, lc_ref, ln_ref, tp_ref, tc_ref, tn_ref,
             c_ref, s1_ref, s2_ref, dk_ref, dv_ref):
        j = pl.program_id(2)
        k = k_ref[...]
        v = v_ref[...]
        qcat = jnp.concatenate([qp_ref[...], qc_ref[...], qn_ref[...]], axis=0)
        docat = jnp.concatenate([dop_ref[...], doc_ref[...], don_ref[...]], axis=0)
        lcat = jnp.concatenate([lp_ref[...], lc_ref[...], ln_ref[...]], axis=0)
        tcat = jnp.concatenate([tp_ref[...], tc_ref[...], tn_ref[...]], axis=0)
        lo = _lo_lanes()
        qpos = j * tq - HALF + lax.broadcasted_iota(jnp.int32, (tq + 2 * HALF, 1), 0)
        kpos = j * tq + lax.broadcasted_iota(jnp.int32, (1, tq), 1)
        valid = (jnp.abs(qpos - kpos) <= HALF) & (qpos >= 0) & (qpos < L)
        dk = jnp.zeros((tq, LANE), F32)
        dv = jnp.zeros((tq, LANE), F32)
        for hh in range(2):
            sel = lo if hh == 0 else jnp.logical_not(lo)
            c0 = hh * DIL_HD
            qm = jnp.where(sel, qcat, jnp.zeros_like(qcat))
            dom = jnp.where(sel, docat, 0.0).astype(BF16)
            s = lax.dot_general(qm, k, _NT, preferred_element_type=F32)
            p = jnp.where(valid, jnp.exp(s - lcat[:, c0:c0 + 1]), 0.0)
            dv = dv + lax.dot_general(p.astype(BF16), dom, _TN, preferred_element_type=F32)
            dp = lax.dot_general(dom, v, _NT, preferred_element_type=F32)
            ds = (p * (dp - tcat[:, c0:c0 + 1])).astype(BF16)
            dk = dk + lax.dot_general(ds, qm, _TN, preferred_element_type=F32)
        dk_ref[...] = _rot(dk, c_ref[...], -s1_ref[...], -s2_ref[...], ROT // 2).astype(BF16)
        dv_ref[...] = dv.astype(BF16)

    n64 = L // HALF
    qcol = lambda r, hp: r * W + g * 12 + hp
    kcol = lambda r, hp: r * W + g * 12 + 4 + hp
    vcol = lambda r, hp: r * W + g * 12 + 8 + hp
    ocol = lambda r, hp: r * 4 + hp
    row = pl.BlockSpec((tq, LANE), lambda r, hp, i: (i, ocol(r, hp)))
    tab = pl.BlockSpec((tq, LANE), lambda r, hp, i: (i, r))
    halo = _band_specs(n64, ocol, False)
    dk, dv = pl.pallas_call(
        body, name=name, grid=(d, 4, L // tq),
        out_shape=(jax.ShapeDtypeStruct((L, d * DIL_W), BF16), jax.ShapeDtypeStruct((L, d * DIL_W), BF16)),
        in_specs=[pl.BlockSpec((tq, LANE), lambda r, hp, i: (i, kcol(r, hp))), pl.BlockSpec((tq, LANE), lambda r, hp, i: (i, vcol(r, hp)))]
        + _band_specs(n64, qcol, False) + halo + halo + halo + [tab, tab, tab],
        out_specs=(row, row), compiler_params=_params(3),
    )(view, view, view, view, view, v4(do), v4(do), v4(do), v4(lse), v4(lse), v4(lse), v4(dlt), v4(dlt), v4(dlt),
      tv(dil["c"]), tv(dil["s1"]), tv(dil["s2"]))
    return dk.reshape(S, DIL_W), dv.reshape(S, DIL_W)


def _sigmoid(x):
    return 1.0 / (1.0 + jnp.exp(-x))


def _gate_a(o, p, name):
    S = o.shape[0]
    tm = ROW_TILE

    def body(o_ref, g_ref, a_ref):
        g = g_ref[...]
        a_ref[...] = (o_ref[...] * (g * _sigmoid(g))).astype(BF16)

    blk = pl.BlockSpec((tm, LANE), lambda i, h: (i, h))
    return pl.pallas_call(
        body, name=name, grid=(S // tm, HEADS), out_shape=jax.ShapeDtypeStruct((S, HP), BF16),
        in_specs=[blk, pl.BlockSpec((tm, LANE), lambda i, h: (i, CB_GA + h))], out_specs=blk, compiler_params=_params(2),
    )(o, p)


def _gate_a_bwd(da, o, p, name):
    S = o.shape[0]
    tm = ROW_TILE

    def body(da_ref, o_ref, g_ref, do_ref, dg_ref):
        g = g_ref[...]
        da = da_ref[...]
        sg = _sigmoid(g)
        do_ref[...] = da * (g * sg)
        dg_ref[...] = (da * o_ref[...] * (sg * (1.0 + g * (1.0 - sg)))).astype(BF16)

    blk = pl.BlockSpec((tm, LANE), lambda i, h: (i, h))
    return pl.pallas_call(
        body, name=name, grid=(S // tm, HEADS),
        out_shape=(jax.ShapeDtypeStruct((S, HP), F32), jax.ShapeDtypeStruct((S, HP), BF16)),
        in_specs=[blk, blk, pl.BlockSpec((tm, LANE), lambda i, h: (i, CB_GA + h))], out_specs=(blk, blk), compiler_params=_params(2),
    )(da, o, p)


def _merge_weights(l0, l1, l2):
    mx = jnp.maximum(jnp.maximum(l0, l1), l2)
    e0, e1, e2 = jnp.exp(l0 - mx), jnp.exp(l1 - mx), jnp.exp(l2 - mx)
    den = e0 + e1 + e2
    return e0 / den, e1 / den, e2 / den


def _gate_b(os_, ls_, p, name):
    S = p.shape[0]
    tm = ROW_TILE

    def body(o0, o1, o2, l0, l1, l2, g_ref, b_ref):
        a0, a1, a2 = _merge_weights(l0[...], l1[...], l2[...])
        bm = a0 * o0[...] + a1 * o1[...] + a2 * o2[...]
        g = g_ref[...]
        b_ref[...] = (bm * (g * _sigmoid(g))).astype(BF16)

    blk = pl.BlockSpec((tm, LANE), lambda i, h: (i, h))
    return pl.pallas_call(
        body, name=name, grid=(S // tm, 4), out_shape=jax.ShapeDtypeStruct((S, DIL_W), BF16),
        in_specs=[blk] * 6 + [pl.BlockSpec((tm, LANE), lambda i, h: (i, CB_GB + h))], out_specs=blk, compiler_params=_params(2),
    )(*os_, *ls_, p)


def _gate_b_bwd(db, os_, ls_, p, name):
    S = p.shape[0]
    tm = ROW_TILE

    def body(db_ref, o0, o1, o2, l0, l1, l2, g_ref, dg_ref, d0, d1, d2, t0, t1, t2):
        a0, a1, a2 = _merge_weights(l0[...], l1[...], l2[...])
        bm = a0 * o0[...] + a1 * o1[...] + a2 * o2[...]
        g = g_ref[...]
        db = db_ref[...]
        sg = _sigmoid(g)
        dbm = db * (g * sg)
        dg_ref[...] = (db * bm * (sg * (1.0 + g * (1.0 - sg)))).astype(BF16)
        prod = dbm * bm
        lo = _lo_lanes()
        tl = jnp.sum(jnp.where(lo, prod, 0.0), axis=-1, keepdims=True)
        th = jnp.sum(jnp.where(lo, 0.0, prod), axis=-1, keepdims=True)
        t = jnp.where(lo, tl, th)
        d0[...] = a0 * dbm
        d1[...] = a1 * dbm
        d2[...] = a2 * dbm
        t0[...] = a0 * t
        t1[...] = a1 * t
        t2[...] = a2 * t

    blk = pl.BlockSpec((tm, LANE), lambda i, h: (i, h))
    f = jax.ShapeDtypeStruct((S, DIL_W), F32)
    outs = pl.pallas_call(
        body, name=name, grid=(S // tm, 4),
        out_shape=(jax.ShapeDtypeStruct((S, DIL_W), BF16), f, f, f, f, f, f),
        in_specs=[blk] * 7 + [pl.BlockSpec((tm, LANE), lambda i, h: (i, CB_GB + h))], out_specs=(blk,) * 7, compiler_params=_params(2),
    )(db, *os_, *ls_, p)
    return outs[0], outs[1:4], outs[4:7]


def _loss_head(x, target, g, name):
    S, D = x.shape
    tm = ROW_TILE

    def body(x_ref, t_ref, g_ref, dx_ref, dg_ref, loss_ref):
        xv = x_ref[...]
        gv = g_ref[...]
        r = lax.rsqrt(jnp.mean(xv * xv, axis=-1, keepdims=True) + EPS)
        xr = xv * r
        err = xr * gv - t_ref[...]
        lpart = 0.5 * jnp.sum(jnp.mean(err * err, axis=-1, keepdims=True), axis=0, keepdims=True)
        dy = err / D
        dyg = dy * gv
        c = jnp.mean(dyg * xv, axis=-1, keepdims=True)
        dx_ref[...] = r * dyg - xv * (r * r * r) * c
        gpart = jnp.sum(dy * xr, axis=0, keepdims=True)

        @pl.when(pl.program_id(0) == 0)
        def _():
            dg_ref[...] = gpart
            loss_ref[...] = jnp.broadcast_to(lpart, loss_ref.shape)

        @pl.when(pl.program_id(0) > 0)
        def _():
            dg_ref[...] += gpart
            loss_ref[...] += jnp.broadcast_to(lpart, loss_ref.shape)

    row = pl.BlockSpec((tm, D), lambda i: (i, 0))
    vec = pl.BlockSpec((1, D), lambda i: (0, 0))
    return pl.pallas_call(
        body, name=name, grid=(S // tm,),
        out_shape=(jax.ShapeDtypeStruct((S, D), F32), jax.ShapeDtypeStruct((1, D), F32), jax.ShapeDtypeStruct((8, LANE), F32)),
        in_specs=[row, row, vec], out_specs=(row, vec, pl.BlockSpec((8, LANE), lambda i: (0, 0))), compiler_params=_params(1),
    )(x, target, g.reshape(1, D))


def _adamw(parts, w, m, v, name):
    R = w.shape[0]
    tr = _pick(R, (768, 512, 256, 128, 64))

    def body(p_ref, w_ref, m_ref, v_ref, g_ref, d_ref, nm_ref, nv_ref):
        g = p_ref[0]
        for k in range(1, N_DEV):
            g = g + p_ref[k]
        m2 = ADAM_B1 * m_ref[...] + (1.0 - ADAM_B1) * g
        v2 = ADAM_B2 * v_ref[...] + (1.0 - ADAM_B2) * (g * g)
        m_hat = m2 / (1.0 - ADAM_B1 ** ADAM_STEP)
        v_hat = v2 / (1.0 - ADAM_B2 ** ADAM_STEP)
        g_ref[...] = g
        d_ref[...] = -ADAM_LR * (m_hat / (jnp.sqrt(v_hat) + ADAM_EPS) + ADAM_WD * w_ref[...])
        nm_ref[...] = m2
        nv_ref[...] = v2

    blk = pl.BlockSpec((tr, LANE), lambda i: (i, 0))
    f = jax.ShapeDtypeStruct((R, LANE), F32)
    return pl.pallas_call(
        body, name=name, grid=(R // tr,), out_shape=(f, f, f, f),
        in_specs=[pl.BlockSpec((N_DEV, tr, LANE), lambda i: (0, i, 0)), blk, blk, blk], out_specs=(blk,) * 4,
        compiler_params=_params(1),
    )(parts, w, m, v)


def _exchange(src, same_src, name):
    blk_shape = src.shape if same_src else src.shape[1:]

    def body(src_ref, out_ref, send_sems, recv_sems, local_sem):
        x, y, c = lax.axis_index("x"), lax.axis_index("y"), lax.axis_index("c")
        me = 4 * x + 2 * y + c

        def block(j):
            return src_ref if same_src else src_ref.at[j]

        mine = pltpu.make_async_copy(block(me), out_ref.at[me], local_sem)
        mine.start()
        copies = []
        for k in range(1, N_DEV):
            px = 1 - x if (k >> 2) & 1 else x
            py = 1 - y if (k >> 1) & 1 else y
            pc = 1 - c if k & 1 else c
            cp = pltpu.make_async_remote_copy(
                src_ref=block(4 * px + 2 * py + pc), dst_ref=out_ref.at[me],
                send_sem=send_sems.at[k - 1], recv_sem=recv_sems.at[k - 1],
                device_id=(px, py, pc), device_id_type=pl.DeviceIdType.MESH)
            cp.start()
            copies.append(cp)
        for cp in copies:
            cp.wait()
        mine.wait()

    hbm = pl.BlockSpec(memory_space=pltpu.HBM)
    return pl.pallas_call(
        body, name=name, out_shape=jax.ShapeDtypeStruct((N_DEV,) + tuple(blk_shape), src.dtype),
        in_specs=[hbm], out_specs=hbm,
        scratch_shapes=[pltpu.SemaphoreType.DMA((N_DEV - 1,)), pltpu.SemaphoreType.DMA((N_DEV - 1,)), pltpu.SemaphoreType.DMA],
    )(src)


def _pack_shards(w_in, w_uq, w_ukv, w_out):
    flat = jnp.concatenate([w_in.reshape(-1), w_uq.reshape(-1), w_ukv.reshape(-1), w_out.reshape(-1)])
    return flat.reshape(-1, LANE)


_SHARD_SHAPES = ((DEPTH, D_MODEL, IN_WIDTH // N_DEV), (DEPTH, Q_LORA, HEADS * (NOPE + ROPE) // N_DEV),
                 (DEPTH, KV_LORA, HEADS * (NOPE + VDIM) // N_DEV), (DEPTH, D_MODEL // N_DEV, D_MODEL))


def _unpack_shards(packed):
    lead = packed.shape[:-2]
    flat = packed.reshape(lead + (-1,))
    out, off = [], 0
    for shp in _SHARD_SHAPES:
        n = shp[0] * shp[1] * shp[2]
        out.append(flat[..., off:off + n].reshape(lead + shp))
        off += n
    return out


def _full_weights(gathered):
    w_in, w_uq, w_ukv, w_out = _unpack_shards(gathered)
    cat_cols = lambda t: jnp.moveaxis(t, 0, 2).reshape(t.shape[1], t.shape[2], -1)
    return cat_cols(w_in), cat_cols(w_uq), cat_cols(w_ukv), jnp.moveaxis(w_out, 0, 1).reshape(DEPTH, D_MODEL, D_MODEL)


def _grad_blocks(g_in, g_uq, g_ukv, g_out):
    split_cols = lambda t: jnp.moveaxis(t.reshape(t.shape[0], t.shape[1], N_DEV, -1), 2, 0).reshape(N_DEV, -1)
    rows = jnp.moveaxis(g_out.reshape(DEPTH, N_DEV, D_MODEL // N_DEV, D_MODEL), 1, 0).reshape(N_DEV, -1)
    flat = jnp.concatenate([split_cols(g_in), split_cols(g_uq), split_cols(g_ukv), rows], axis=1)
    return flat.reshape(N_DEV, -1, LANE)


def _layer_weights(w_in, w_uq, w_ukv, w_out):
    z = lambda r, n: jnp.zeros((r, n), BF16)
    c_q, c_kv, k_r = w_in[:, 0:384], w_in[:, 384:640], w_in[:, 640:672]
    gate_a, dil, gate_b = w_in[:, 672:1184], w_in[:, 1184:5792], w_in[:, 5792:6304]
    ga_pad = jnp.pad(gate_a.reshape(D_MODEL, HEADS, VDIM), ((0, 0), (0, 0), (0, LANE - VDIM))).reshape(D_MODEL, HP)
    w_p = jnp.concatenate([c_kv, z(D_MODEL, 128), c_q, ga_pad, dil, gate_b, z(D_MODEL, 64), k_r, z(D_MODEL, 32)], axis=1)
    uq = jnp.pad(w_uq.reshape(Q_LORA, HEADS, NOPE + ROPE), ((0, 0), (0, 0), (0, LANE - NOPE - ROPE))).reshape(Q_LORA, HP)
    ukv = w_ukv.reshape(KV_LORA, HEADS, NOPE + VDIM)
    pad64 = lambda t: jnp.pad(t, ((0, 0), (0, 0), (0, LANE - 64))).reshape(KV_LORA, HP)
    uk, uv = pad64(ukv[..., :NOPE]), pad64(ukv[..., NOPE:])
    wa = jnp.pad(w_out[:HEADS * VDIM].reshape(HEADS, VDIM, D_MODEL), ((0, 0), (0, LANE - VDIM), (0, 0))).reshape(HP, D_MODEL)
    wb = w_out[HEADS * VDIM:]
    return dict(p=w_p, uq=uq, uk=uk, uv=uv, wa=wa, wb=wb)


def _unpad_grads(g):
    gp = g["p"]
    seg = lambda cb, n: gp[:, cb * LANE:cb * LANE + n]
    ga = seg(CB_GA, HP).reshape(D_MODEL, HEADS, LANE)[..., :VDIM].reshape(D_MODEL, HEADS * VDIM)
    k_r = gp[:, CB_KR * LANE + NOPE:CB_KR * LANE + NOPE + ROPE]
    g_in = jnp.concatenate([seg(CB_CQ, Q_LORA), seg(CB_CKV, KV_LORA), k_r, ga, seg(CB_DIL, 9 * DIL_W), seg(CB_GB, DIL_W)], axis=1)
    g_uq = g["uq"].reshape(Q_LORA, HEADS, LANE)[..., :NOPE + ROPE].reshape(Q_LORA, -1)
    uk = g["uk"].reshape(KV_LORA, HEADS, LANE)[..., :NOPE]
    uv = g["uv"].reshape(KV_LORA, HEADS, LANE)[..., :VDIM]
    g_ukv = jnp.concatenate([uk, uv], axis=-1).reshape(KV_LORA, -1)
    wa = g["wa"].reshape(HEADS, LANE, D_MODEL)[:, :VDIM].reshape(HEADS * VDIM, D_MODEL)
    g_out = jnp.concatenate([wa, g["wb"]], axis=0)
    return g_in, g_uq, g_ukv, g_out


def _layer_fwd(x, w, norm_g, q_norm_g, kv_norm_g, mla, dil, l):
    n = lambda s: f"l{l}_{s}"
    h = _rms_fwd(x, 0, D_MODEL, norm_g, n("norm"))
    p = _mm(h, w["p"], "nn", n("in_proj"))
    cqn = _rms_fwd(p, CB_CQ * LANE // Q_LORA, Q_LORA, q_norm_g, n("q_norm"))
    ckvn = _rms_fwd(p, CB_CKV, KV_LORA, kv_norm_g, n("kv_norm"))
    qp = _mm(cqn, w["uq"], "nn", n("q_up"))
    kpre = _mm(ckvn, w["uk"], "nn", n("k_up"))
    v = _mm(ckvn, w["uv"], "nn", n("v_up"), out_dtype=BF16)
    q = _rope_cols(qp, 0, HEADS, mla["c_q"], mla["s1"], mla["s2"], ROPE // 2, n("q_rope"), BF16)
    k = _k_assemble(kpre, p, mla, n("k_asm"))
    o, lse = _flash_fwd(q, k, v, n("mla_fwd"))
    a = _gate_a(o, p, n("gate_a"))
    dilr = _dil_prep(p, dil, n("dil_prep"))
    os_, ls_ = [], []
    for g, (_, d) in enumerate(DIL_PAIRS):
        og, lg = _band_fwd(dilr, g, d, n(f"band{g}_fwd"))
        os_.append(og)
        ls_.append(lg)
    b = _gate_b(os_, ls_, p, n("gate_b"))
    x1 = _mm(a, w["wa"], "nn", n("out_a"), res=x)
    x2 = _mm(b, w["wb"], "nn", n("out_b"), res=x1)
    saved = dict(x=x, h=h, p=p, cqn=cqn, ckvn=ckvn, q=q, k=k, v=v, o=o, lse=lse, a=a, dilr=dilr, os=os_, ls=ls_, b=b)
    return x2, saved


def _layer_bwd(dx, sv, w, norm_g, q_norm_g, kv_norm_g, mla, dil, l):
    n = lambda s: f"l{l}_{s}"
    S = dx.shape[0]
    g = {}
    da = _mm(dx, w["wa"], "nt", n("d_a"))
    db = _mm(dx, w["wb"], "nt", n("d_b"))
    g["wa"] = _mm(sv["a"], dx, "tn", n("dw_a"))
    g["wb"] = _mm(sv["b"], dx, "tn", n("dw_b"))
    do, dga = _gate_a_bwd(da, sv["o"], sv["p"], n("gate_a_bwd"))
    dgb, dos, dts = _gate_b_bwd(db, sv["os"], sv["ls"], sv["p"], n("gate_b_bwd"))
    ddil = []
    for gi, (_, d) in enumerate(DIL_PAIRS):
        dq = _band_bwd_q(sv["dilr"], gi, d, dos[gi], sv["ls"][gi], dts[gi], dil, n(f"band{gi}_bwd_q"))
        dk, dv = _band_bwd_kv(sv["dilr"], gi, d, dos[gi], sv["ls"][gi], dts[gi], dil, n(f"band{gi}_bwd_kv"))
        ddil += [dq, dk, dv]
    dq, dk, dv = _flash_bwd(sv["q"], sv["k"], sv["v"], sv["o"], do, sv["lse"], n("mla_bwd"))
    dqp = _rope_cols(dq, 0, HEADS, mla["c_q"], -mla["s1"], -mla["s2"], ROPE // 2, n("q_rope_bwd"), BF16)
    dkr = _kr_bwd(dk, mla, n("kr_bwd"))
    g["uq"] = _mm(sv["cqn"], dqp, "tn", n("dw_uq"))
    g["uk"] = _mm(sv["ckvn"], dk, "tn", n("dw_uk"))
    g["uv"] = _mm(sv["ckvn"], dv, "tn", n("dw_uv"))
    dcqn = _mm(dqp, w["uq"], "nt", n("d_cqn"))
    dckvn = _mm(dk, w["uk"], "nt", n("d_ckvn_k"))
    dckvn = _mm(dv, w["uv"], "nt", n("d_ckvn_v"), res=dckvn)
    dcq, g_qn = _rms_bwd(sv["p"], CB_CQ * LANE // Q_LORA, Q_LORA, dcqn, q_norm_g, n("q_norm_bwd"), BF16)
    dckv, g_kvn = _rms_bwd(sv["p"], CB_CKV, KV_LORA, dckvn, kv_norm_g, n("kv_norm_bwd"), BF16)
    dp = jnp.concatenate([dckv, jnp.zeros((S, LANE), BF16), dcq, dga] + ddil + [dgb, dkr], axis=1)
    g["p"] = _mm(sv["h"], dp, "tn", n("dw_in"))
    dh = _mm(dp, w["p"], "nt", n("d_h"))
    dx_in, g_n = _rms_bwd(sv["x"], 0, D_MODEL, dh, norm_g, n("norm_bwd"), F32, res=dx)
    return dx_in, g, g_n, g_qn, g_kvn


_SMALL_SIZES = (DEPTH * D_MODEL, DEPTH * Q_LORA, DEPTH * KV_LORA, D_MODEL)
_SMALL_ROWS = 64


def _pack_small(norm, qn, kvn, fin, extra=None):
    flat = jnp.concatenate([norm.reshape(-1), qn.reshape(-1), kvn.reshape(-1), fin.reshape(-1)])
    n = flat.shape[0]
    tail = jnp.zeros((_SMALL_ROWS * LANE - n,), F32)
    if extra is not None:
        tail = tail.at[0].set(extra)
    return jnp.concatenate([flat, tail]).reshape(_SMALL_ROWS, LANE)


def _unpack_small(packed):
    flat = packed.reshape(-1)
    shapes = ((DEPTH, D_MODEL), (DEPTH, Q_LORA), (DEPTH, KV_LORA), (D_MODEL,))
    out, off = [], 0
    for shp, n in zip(shapes, _SMALL_SIZES):
        out.append(flat[off:off + n].reshape(shp))
        off += n
    return out, flat[off]


def kernel(x, norm_g, w_in, q_norm_g, kv_norm_g, w_uq, w_ukv, w_out, final_g, loss_target, m_norm_g, m_w_in, m_q_norm_g, m_kv_norm_g, m_w_uq, m_w_ukv, m_w_out, m_final_g, v_norm_g, v_w_in, v_q_norm_g, v_kv_norm_g, v_w_uq, v_w_ukv, v_w_out, v_final_g):
    S = x.shape[1]
    xs = x.reshape(S, D_MODEL)
    target = loss_target.reshape(S, D_MODEL)

    w_packed = _pack_shards(w_in, w_uq, w_ukv, w_out)
    gathered = _exchange(w_packed.astype(BF16), True, "gather_weights")
    full = _full_weights(gathered)
    ws = [_layer_weights(*(t[l] for t in full)) for l in range(DEPTH)]
    mla, dil = _rope_tables(S)

    saved = []
    h = xs
    for l in range(DEPTH):
        h, sv = _layer_fwd(h, ws[l], norm_g[l], q_norm_g[l], kv_norm_g[l], mla, dil, l)
        saved.append(sv)
    dx, g_final, loss_tile = _loss_head(h, target, final_g, "loss_head")
    g_layers, g_norm, g_qn, g_kvn = [None] * DEPTH, [None] * DEPTH, [None] * DEPTH, [None] * DEPTH
    for l in reversed(range(DEPTH)):
        dx, g, g_norm[l], g_qn[l], g_kvn[l] = _layer_bwd(dx, saved[l], ws[l], norm_g[l], q_norm_g[l], kv_norm_g[l], mla, dil, l)
        g_layers[l] = _unpad_grads(g)

    g_full = [jnp.stack([g_layers[l][i] for l in range(DEPTH)]) for i in range(4)]
    parts = _exchange(_grad_blocks(*g_full), False, "exchange_grads")
    outs = _adamw(parts, w_packed, _pack_shards(m_w_in, m_w_uq, m_w_ukv, m_w_out), _pack_shards(v_w_in, v_w_uq, v_w_ukv, v_w_out), "adamw")
    g_sh, d_sh, m_sh, v_sh = (_unpack_shards(t) for t in outs)

    small = _pack_small(jnp.concatenate(g_norm, 0), jnp.concatenate(g_qn, 0), jnp.concatenate(g_kvn, 0), g_final, extra=loss_tile[0, 0])
    small_parts = _exchange(small, True, "gather_small")
    souts = _adamw(small_parts, _pack_small(norm_g, q_norm_g, kv_norm_g, final_g), _pack_small(m_norm_g, m_q_norm_g, m_kv_norm_g, m_final_g),
                   _pack_small(v_norm_g, v_q_norm_g, v_kv_norm_g, v_final_g), "adamw_small")
    (g_sm, loss), (d_sm, _), (m_sm, _), (v_sm, _) = (_unpack_small(t) for t in souts)

    def order(sm, sh):
        return (sm[0], sh[0], sm[1], sm[2], sh[1], sh[2], sh[3], sm[3])

    return (loss, dx.reshape(1, S, D_MODEL), *order(g_sm, g_sh), *order(d_sm, d_sh), *order(m_sm, m_sh), *order(v_sm, v_sh))
```

```python
import math

import jax
import jax.numpy as jnp
from jax import lax
from jax.experimental import pallas as pl
from jax.experimental.pallas import tpu as pltpu

F32 = jnp.float32
BF16 = jnp.bfloat16

D_MODEL = 1024
DEPTH = 4
HEADS = 8
NOPE = 64
ROPE = 32
VDIM = 64
Q_LORA = 384
KV_LORA = 256
DIL_PAIRS = ((128, 1), (512, 4), (2048, 16))
DIL_HD = 64
DIL_W = 512
ROT = 16
HALF = 64
THETA = 500000.0
EPS = 1e-6
IN_WIDTH = 6304
N_DEV = 8

LANE = 128
CB_GA, CB_DIL, CB_GB, CB_CKV, CB_KR, CB_CQ = 0, 8, 44, 48, 50, 51
NP = 54 * LANE
HP = HEADS * LANE

ADAM_LR = 0.001
ADAM_B1 = 0.9
ADAM_B2 = 0.999
ADAM_EPS = 1e-08
ADAM_WD = 0.01
ADAM_STEP = 10

VMEM_LIMIT = 48 * 1024 * 1024
ROW_TILE = 512
SUB = 128

_NT = (((1,), (1,)), ((), ()))
_NN = (((1,), (0,)), ((), ()))
_TN = (((0,), (0,)), ((), ()))


def _params(n_axes):
    return pltpu.CompilerParams(dimension_semantics=("arbitrary",) * n_axes, vmem_limit_bytes=VMEM_LIMIT)


def _pick(n, cands):
    for c in cands:
        if n % c == 0:
            return c
    raise ValueError(f"no tile for {n}")


def _mm(a, b, mode, name, out_dtype=F32, res=None):
    if mode == "nn":
        (M, K), (K2, N) = a.shape, b.shape
    elif mode == "nt":
        (M, K), (N, K2) = a.shape, b.shape
    else:
        (K, M), (K2, N) = a.shape, b.shape
    assert K == K2, (a.shape, b.shape, mode)
    tm = _pick(M, (512, 384, 256, 128))
    tn = _pick(N, (1152, 1024, 768, 640, 512, 384, 256, 128))
    tk = _pick(K, (1152, 1024, 768, 640, 512, 384, 256, 128))
    nk = K // tk
    dims = {"nn": _NN, "nt": _NT, "tn": _TN}[mode]

    def body(*refs):
        if res is not None:
            a_ref, b_ref, r_ref, o_ref = refs[:4]
        else:
            a_ref, b_ref, o_ref = refs[:3]
            r_ref = None
        part = lax.dot_general(a_ref[...].astype(BF16), b_ref[...].astype(BF16), dims, preferred_element_type=F32)

        def finish(acc):
            if r_ref is not None:
                acc = acc + r_ref[...]
            o_ref[...] = acc.astype(out_dtype)

        if nk == 1:
            finish(part)
        else:
            acc_ref = refs[-1]
            k = pl.program_id(2)

            @pl.when(k == 0)
            def _():
                acc_ref[...] = part

            @pl.when(k > 0)
            def _():
                acc_ref[...] += part

            @pl.when(k == nk - 1)
            def _():
                finish(acc_ref[...])

    if mode == "nn":
        a_spec = pl.BlockSpec((tm, tk), lambda i, j, k: (i, k))
        b_spec = pl.BlockSpec((tk, tn), lambda i, j, k: (k, j))
    elif mode == "nt":
        a_spec = pl.BlockSpec((tm, tk), lambda i, j, k: (i, k))
        b_spec = pl.BlockSpec((tn, tk), lambda i, j, k: (j, k))
    else:
        a_spec = pl.BlockSpec((tk, tm), lambda i, j, k: (k, i))
        b_spec = pl.BlockSpec((tk, tn), lambda i, j, k: (k, j))
    o_spec = pl.BlockSpec((tm, tn), lambda i, j, k: (i, j))
    in_specs = [a_spec, b_spec] + ([o_spec] if res is not None else [])
    args = (a, b) + ((res,) if res is not None else ())
    return pl.pallas_call(
        body, name=name, grid=(M // tm, N // tn, nk), out_shape=jax.ShapeDtypeStruct((M, N), out_dtype),
        in_specs=in_specs, out_specs=o_spec,
        scratch_shapes=[pltpu.VMEM((tm, tn), F32)] if nk > 1 else [],
        compiler_params=_params(3),
    )(*args)


def _rms_fwd(src, cb, width, g, name):
    S = src.shape[0]
    tm = ROW_TILE

    def body(x_ref, g_ref, o_ref):
        x = x_ref[...]
        r = lax.rsqrt(jnp.mean(x * x, axis=-1, keepdims=True) + EPS)
        o_ref[...] = (x * r * g_ref[...]).astype(BF16)

    return pl.pallas_call(
        body, name=name, grid=(S // tm,), out_shape=jax.ShapeDtypeStruct((S, width), BF16),
        in_specs=[pl.BlockSpec((tm, width), lambda i: (i, cb)), pl.BlockSpec((1, width), lambda i: (0, 0))],
        out_specs=pl.BlockSpec((tm, width), lambda i: (i, 0)), compiler_params=_params(1),
    )(src, g.reshape(1, width))


def _rms_bwd(src, cb, width, dy, g, name, out_dtype, res=None):
    S = src.shape[0]
    tm = ROW_TILE

    def body(*refs):
        if res is not None:
            x_ref, dy_ref, g_ref, r_ref, dx_ref, dg_ref = refs
        else:
            x_ref, dy_ref, g_ref, dx_ref, dg_ref = refs
            r_ref = None
        x = x_ref[...]
        dy = dy_ref[...]
        r = lax.rsqrt(jnp.mean(x * x, axis=-1, keepdims=True) + EPS)
        dyg = dy * g_ref[...]
        c = jnp.mean(dyg * x, axis=-1, keepdims=True)
        dx = r * dyg - x * (r * r * r) * c
        if r_ref is not None:
            dx = dx + r_ref[...]
        dx_ref[...] = dx.astype(out_dtype)
        part = jnp.sum(dy * x * r, axis=0, keepdims=True)

        @pl.when(pl.program_id(0) == 0)
        def _():
            dg_ref[...] = part

        @pl.when(pl.program_id(0) > 0)
        def _():
            dg_ref[...] += part

    row = pl.BlockSpec((tm, width), lambda i: (i, 0))
    in_specs = [pl.BlockSpec((tm, width), lambda i: (i, cb)), row, pl.BlockSpec((1, width), lambda i: (0, 0))]
    args = [src, dy, g.reshape(1, width)]
    if res is not None:
        in_specs.append(row)
        args.append(res)
    return pl.pallas_call(
        body, name=name, grid=(S // tm,),
        out_shape=(jax.ShapeDtypeStruct((S, width), out_dtype), jax.ShapeDtypeStruct((1, width), F32)),
        in_specs=in_specs, out_specs=(row, pl.BlockSpec((1, width), lambda i: (0, 0))),
        compiler_params=_params(1),
    )(*args)


def _rot(x, c, s1, s2, h):
    return x * c + pltpu.roll(x, x.shape[1] - h, 1) * s1 + pltpu.roll(x, h, 1) * s2


def _rope_tables(S):
    def tables(dim):
        inv = 1.0 / (THETA ** (jnp.arange(0, dim, 2, dtype=F32) / dim))
        ang = jnp.arange(S, dtype=F32)[:, None] * inv[None, :]
        return jnp.cos(ang), jnp.sin(ang)

    cm, sm = tables(ROPE)
    cd, sd = tables(ROT)
    z = lambda n: jnp.zeros((S, n), F32)
    o = lambda n: jnp.ones((S, n), F32)
    mla = dict(
        c_q=jnp.concatenate([o(64), cm, cm, z(32)], 1),
        c_kr=jnp.concatenate([z(64), cm, cm, z(32)], 1),
        s1=jnp.concatenate([z(64), -sm, z(16), z(32)], 1),
        s2=jnp.concatenate([z(64), z(16), sm, z(32)], 1),
    )
    one = lambda a, b, c: jnp.concatenate([a, b, c, a, b, c], 1)
    dil = dict(c=one(cd, cd, o(48)), s1=one(-sd, z(8), z(48)), s2=one(z(8), sd, z(48)))
    return mla, dil


def _rope_heads(src, c, s1, s2, name, out_dtype):
    S = src.shape[0]
    tm = ROW_TILE

    def body(x_ref, c_ref, s1_ref, s2_ref, o_ref):
        cv, s1v, s2v = c_ref[...], s1_ref[...], s2_ref[...]
        for h in range(HEADS):
            sl = slice(h * LANE, (h + 1) * LANE)
            o_ref[:, sl] = _rot(x_ref[:, sl], cv, s1v, s2v, ROPE // 2).astype(out_dtype)

    tab = pl.BlockSpec((tm, LANE), lambda i: (i, 0))
    wide = pl.BlockSpec((tm, HP), lambda i: (i, 0))
    return pl.pallas_call(
        body, name=name, grid=(S // tm,), out_shape=jax.ShapeDtypeStruct((S, HP), out_dtype),
        in_specs=[wide, tab, tab, tab], out_specs=wide, compiler_params=_params(1),
    )(src, c, s1, s2)


def _k_assemble(kpre, p, mla, name):
    S = kpre.shape[0]
    tm = ROW_TILE

    def body(k_ref, kr_ref, c_ref, s1_ref, s2_ref, o_ref):
        r = _rot(kr_ref[...], c_ref[...], s1_ref[...], s2_ref[...], ROPE // 2)
        for h in range(HEADS):
            sl = slice(h * LANE, (h + 1) * LANE)
            o_ref[:, sl] = (k_ref[:, sl] + r).astype(BF16)

    tab = pl.BlockSpec((tm, LANE), lambda i: (i, 0))
    wide = pl.BlockSpec((tm, HP), lambda i: (i, 0))
    return pl.pallas_call(
        body, name=name, grid=(S // tm,), out_shape=jax.ShapeDtypeStruct((S, HP), BF16),
        in_specs=[wide, pl.BlockSpec((tm, LANE), lambda i: (i, CB_KR)), tab, tab, tab],
        out_specs=wide, compiler_params=_params(1),
    )(kpre, p, mla["c_kr"], mla["s1"], mla["s2"])


def _kr_bwd(dk, mla, name):
    S = dk.shape[0]
    tm = ROW_TILE

    def body(dk_ref, c_ref, s1_ref, s2_ref, o_ref):
        t = dk_ref[:, 0:LANE]
        for h in range(1, HEADS):
            t = t + dk_ref[:, h * LANE:(h + 1) * LANE]
        lane = lax.broadcasted_iota(jnp.int32, (1, LANE), 1)
        t = jnp.where((lane >= NOPE) & (lane < NOPE + ROPE), t, 0.0)
        o_ref[...] = _rot(t, c_ref[...], -s1_ref[...], -s2_ref[...], ROPE // 2).astype(BF16)

    tab = pl.BlockSpec((tm, LANE), lambda i: (i, 0))
    return pl.pallas_call(
        body, name=name, grid=(S // tm,), out_shape=jax.ShapeDtypeStruct((S, LANE), BF16),
        in_specs=[pl.BlockSpec((tm, HP), lambda i: (i, 0)), tab, tab, tab],
        out_specs=tab, compiler_params=_params(1),
    )(dk, mla["c_kr"], mla["s1"], mla["s2"])


def _dil_prep(p, dil, name):
    S = p.shape[0]
    tm = ROW_TILE
    first = CB_DIL * LANE // DIL_W

    def body(x_ref, c_ref, s1_ref, s2_ref, o_ref):
        t = pl.program_id(1) % 3
        rep = DIL_W // LANE

        def roped():
            return _rot(x_ref[...], jnp.tile(c_ref[...], (1, rep)), jnp.tile(s1_ref[...], (1, rep)),
                        jnp.tile(s2_ref[...], (1, rep)), ROT // 2)

        @pl.when(t == 0)
        def _():
            o_ref[...] = (roped() * (DIL_HD ** -0.5)).astype(BF16)

        @pl.when(t == 1)
        def _():
            o_ref[...] = roped().astype(BF16)

        @pl.when(t == 2)
        def _():
            o_ref[...] = x_ref[...].astype(BF16)

    tab = pl.BlockSpec((tm, LANE), lambda i, j: (i, 0))
    return pl.pallas_call(
        body, name=name, grid=(S // tm, 9), out_shape=jax.ShapeDtypeStruct((S, 9 * DIL_W), BF16),
        in_specs=[pl.BlockSpec((tm, DIL_W), lambda i, j: (i, first + j)), tab, tab, tab],
        out_specs=pl.BlockSpec((tm, DIL_W), lambda i, j: (i, j)), compiler_params=_params(2),
    )(p, dil["c"], dil["s1"], dil["s2"])


def _flash_fwd(q, k, v, name):
    S = q.shape[0]
    tq = 512
    tk = _pick(S, (1024, 512))
    nk = S // tk
    c2 = (NOPE + ROPE) ** -0.5 * math.log2(math.e)

    def body(q_ref, k_ref, v_ref, o_ref, lse_ref, m_s, acc_s):
        j = pl.program_id(2)

        @pl.when(j == 0)
        def _():
            m_s[...] = jnp.full(m_s.shape, -jnp.inf, F32)
            acc_s[...] = jnp.zeros(acc_s.shape, F32)

        lane = lax.broadcasted_iota(jnp.int32, (1, LANE), 1)
        vv = jnp.where(lane == VDIM, jnp.ones((), BF16), v_ref[...])
        t = lax.dot_general(q_ref[...], k_ref[...], _NT, preferred_element_type=F32) * c2
        m_prev = m_s[...]
        m_new = jnp.maximum(m_prev, jnp.max(t, axis=-1, keepdims=True))
        alpha = jnp.exp2(m_prev - m_new)
        e = jnp.exp2(t - jnp.tile(m_new, (1, tk // LANE)))
        acc_s[...] = alpha * acc_s[...] + jnp.dot(e.astype(BF16), vv, preferred_element_type=F32)
        m_s[...] = m_new

        @pl.when(j == nk - 1)
        def _():
            acc = acc_s[...]
            l = acc[:, VDIM:VDIM + 1]
            o_ref[...] = jnp.where(lane < VDIM, acc / l, 0.0)
            lse_ref[...] = (m_s[...] + jnp.log2(l)) * math.log(2.0)

    qs = pl.BlockSpec((tq, LANE), lambda h, i, j: (i, h))
    ks = pl.BlockSpec((tk, LANE), lambda h, i, j: (j, h))
    return pl.pallas_call(
        body, name=name, grid=(HEADS, S // tq, nk),
        out_shape=(jax.ShapeDtypeStruct((S, HP), F32), jax.ShapeDtypeStruct((S, HP), F32)),
        in_specs=[qs, ks, ks], out_specs=(qs, qs),
        scratch_shapes=[pltpu.VMEM((tq, LANE), F32), pltpu.VMEM((tq, LANE), F32)],
        compiler_params=_params(3),
    )(q, k, v)


def _flash_bwd(q, k, v, o, do, lse, name):
    S = q.shape[0]
    tq = tk = 512
    nq = S // tq
    scale = (NOPE + ROPE) ** -0.5

    def body(q_ref, k_ref, v_ref, o_ref, do_ref, lse_ref, dq_ref, dk_ref, dv_ref, dk_s, dv_s):
        j = pl.program_id(1)
        i = pl.program_id(2)
        qv = q_ref[...]
        kv = k_ref[...]
        do = do_ref[...]
        dob = do.astype(BF16)
        s = lax.dot_general(qv, kv, _NT, preferred_element_type=F32) * scale
        p = jnp.exp(s - lse_ref[:, 0:1])
        dp = lax.dot_general(dob, v_ref[...], _NT, preferred_element_type=F32)
        delta = jnp.sum(do * o_ref[...], axis=-1, keepdims=True)
        ds = (p * (dp - delta)).astype(BF16)
        dv_part = lax.dot_general(p.astype(BF16), dob, _TN, preferred_element_type=F32)
        dk_part = lax.dot_general(ds, qv, _TN, preferred_element_type=F32)
        dq_part = jnp.dot(ds, kv, preferred_element_type=F32) * scale

        @pl.when(i == 0)
        def _():
            dk_s[...] = dk_part
            dv_s[...] = dv_part

        @pl.when(i > 0)
        def _():
            dk_s[...] += dk_part
            dv_s[...] += dv_part

        rows = pl.ds(pl.multiple_of(i * tq, tq), tq)

        @pl.when(j == 0)
        def _():
            dq_ref[rows, :] = dq_part

        @pl.when(j > 0)
        def _():
            dq_ref[rows, :] += dq_part

        @pl.when(i == nq - 1)
        def _():
            dk_ref[...] = dk_s[...] * scale
            dv_ref[...] = dv_s[...].astype(BF16)

    qs = pl.BlockSpec((tq, LANE), lambda h, j, i: (i, h))
    ks = pl.BlockSpec((tk, LANE), lambda h, j, i: (j, h))
    return pl.pallas_call(
        body, name=name, grid=(HEADS, S // tk, nq),
        out_shape=(jax.ShapeDtypeStruct((S, HP), F32), jax.ShapeDtypeStruct((S, HP), F32), jax.ShapeDtypeStruct((S, HP), BF16)),
        in_specs=[qs, ks, ks, qs, qs, qs],
        out_specs=(pl.BlockSpec((S, LANE), lambda h, j, i: (0, h)), ks, ks),
        scratch_shapes=[pltpu.VMEM((tk, LANE), F32), pltpu.VMEM((tk, LANE), F32)],
        compiler_params=_params(3),
    )(q, k, v, o, do, lse)


def _band_tiles(L):
    tq = min(512, L)
    return tq, tq // SUB, tq // HALF, L // HALF


def _halo_specs(tq, rpb, n64, col):
    prev = pl.BlockSpec((HALF, DIL_W), lambda r, i: (jnp.maximum(rpb * i - 1, 0), col(r)))
    cur = pl.BlockSpec((tq, DIL_W), lambda r, i: (i, col(r)))
    nxt = pl.BlockSpec((HALF, DIL_W), lambda r, i: (jnp.minimum(rpb * i + rpb, n64 - 1), col(r)))
    return [prev, cur, nxt]


def _fill(buf, prev_ref, cur_ref, next_ref, tq):
    buf[0:HALF, :] = prev_ref[...]
    buf[HALF:HALF + tq, :] = cur_ref[...]
    buf[HALF + tq:HALF + tq + HALF, :] = next_ref[...]


def _lo_lanes():
    return lax.broadcasted_iota(jnp.int32, (1, LANE), 1) < DIL_HD


def _band_valid(q0, k0, nq, nk, L, bound_q):
    qpos = q0 + lax.broadcasted_iota(jnp.int32, (nq, 1), 0)
    kpos = k0 + lax.broadcasted_iota(jnp.int32, (1, nk), 1)
    side = qpos if bound_q else kpos
    return (jnp.abs(qpos - kpos) <= HALF) & (side >= 0) & (side < L)


def _band_fwd(dilr, g, d, name):
    S = dilr.shape[0]
    L = S // d
    tq, nsub, rpb, n64 = _band_tiles(L)
    view = dilr.reshape(L, d * dilr.shape[1])
    win = SUB + 2 * HALF

    def body(q_ref, kp_ref, kc_ref, kn_ref, vp_ref, vc_ref, vn_ref, o_ref, lse_ref, kbuf, vbuf):
        i = pl.program_id(1)
        _fill(kbuf, kp_ref, kc_ref, kn_ref, tq)
        _fill(vbuf, vp_ref, vc_ref, vn_ref, tq)
        lo = _lo_lanes()

        def sub(a, carry):
            r0 = pl.multiple_of(a * SUB, SUB)
            valid = _band_valid(i * tq + r0, i * tq + r0 - HALF, SUB, win, L, False)
            for hp in range(4):
                cs = slice(hp * LANE, (hp + 1) * LANE)
                q = q_ref[pl.ds(r0, SUB), cs]
                kw = kbuf[pl.ds(r0, win), cs]
                vw = vbuf[pl.ds(r0, win), cs]
                outs, lses = [], []
                for hh in range(2):
                    sel = lo if hh == 0 else jnp.logical_not(lo)
                    qm = jnp.where(sel, q, jnp.zeros_like(q))
                    s = lax.dot_general(qm, kw, _NT, preferred_element_type=F32)
                    s = jnp.where(valid, s, -jnp.inf)
                    m = jnp.max(s, axis=-1, keepdims=True)
                    e = jnp.exp(s - m)
                    l = jnp.sum(e, axis=-1, keepdims=True)
                    outs.append(jnp.dot(e.astype(BF16), vw, preferred_element_type=F32) / l)
                    lses.append(m + jnp.log(l))
                o_ref[pl.ds(r0, SUB), cs] = jnp.where(lo, outs[0], outs[1])
                lse_ref[pl.ds(r0, SUB), cs] = jnp.where(lo, lses[0], lses[1])
            return carry

        lax.fori_loop(0, nsub, sub, 0)

    out_spec = pl.BlockSpec((tq, DIL_W), lambda r, i: (i, r))
    o, lse = pl.pallas_call(
        body, name=name, grid=(d, L // tq),
        out_shape=(jax.ShapeDtypeStruct((L, d * DIL_W), F32), jax.ShapeDtypeStruct((L, d * DIL_W), F32)),
        in_specs=[pl.BlockSpec((tq, DIL_W), lambda r, i: (i, r * 9 + g * 3))]
        + _halo_specs(tq, rpb, n64, lambda r: r * 9 + g * 3 + 1) + _halo_specs(tq, rpb, n64, lambda r: r * 9 + g * 3 + 2),
        out_specs=(out_spec, out_spec),
        scratch_shapes=[pltpu.VMEM((tq + 2 * HALF, DIL_W), BF16), pltpu.VMEM((tq + 2 * HALF, DIL_W), BF16)],
        compiler_params=_params(2),
    )(view, view, view, view, view, view, view)
    return o.reshape(S, DIL_W), lse.reshape(S, DIL_W)


def _band_bwd_q(dilr, g, d, do, lse, dlt, dil, name):
    S = dilr.shape[0]
    L = S // d
    tq, nsub, rpb, n64 = _band_tiles(L)
    view = dilr.reshape(L, d * dilr.shape[1])
    v4 = lambda t: t.reshape(L, d * DIL_W)
    tv = lambda t: t.reshape(L, d * LANE)
    win = SUB + 2 * HALF

    def body(q_ref, kp_ref, kc_ref, kn_ref, vp_ref, vc_ref, vn_ref, do_ref, lse_ref, dlt_ref, c_ref, s1_ref, s2_ref, dq_ref,
             kbuf, vbuf):
        i = pl.program_id(1)
        _fill(kbuf, kp_ref, kc_ref, kn_ref, tq)
        _fill(vbuf, vp_ref, vc_ref, vn_ref, tq)
        lo = _lo_lanes()

        def sub(a, carry):
            r0 = pl.multiple_of(a * SUB, SUB)
            rows = pl.ds(r0, SUB)
            valid = _band_valid(i * tq + r0, i * tq + r0 - HALF, SUB, win, L, False)
            cv, s1v, s2v = c_ref[rows, :], s1_ref[rows, :], s2_ref[rows, :]
            for hp in range(4):
                cs = slice(hp * LANE, (hp + 1) * LANE)
                q = q_ref[rows, cs]
                do = do_ref[rows, cs]
                lse_v = lse_ref[rows, cs]
                dlt_v = dlt_ref[rows, cs]
                kw = kbuf[pl.ds(r0, win), cs]
                vw = vbuf[pl.ds(r0, win), cs]
                dqs = []
                for hh in range(2):
                    sel = lo if hh == 0 else jnp.logical_not(lo)
                    c0 = hh * DIL_HD
                    qm = jnp.where(sel, q, jnp.zeros_like(q))
                    dom = jnp.where(sel, do, 0.0).astype(BF16)
                    s = lax.dot_general(qm, kw, _NT, preferred_element_type=F32)
                    p = jnp.where(valid, jnp.exp(s - lse_v[:, c0:c0 + 1]), 0.0)
                    dp = lax.dot_general(dom, vw, _NT, preferred_element_type=F32)
                    ds = (p * (dp - dlt_v[:, c0:c0 + 1])).astype(BF16)
                    dqs.append(jnp.dot(ds, kw, preferred_element_type=F32))
                dq = jnp.where(lo, dqs[0], dqs[1])
                dq_ref[rows, cs] = (_rot(dq, cv, -s1v, -s2v, ROT // 2) * (DIL_HD ** -0.5)).astype(BF16)
            return carry

        lax.fori_loop(0, nsub, sub, 0)

    row = pl.BlockSpec((tq, DIL_W), lambda r, i: (i, r))
    tab = pl.BlockSpec((tq, LANE), lambda r, i: (i, r))
    dq = pl.pallas_call(
        body, name=name, grid=(d, L // tq), out_shape=jax.ShapeDtypeStruct((L, d * DIL_W), BF16),
        in_specs=[pl.BlockSpec((tq, DIL_W), lambda r, i: (i, r * 9 + g * 3))]
        + _halo_specs(tq, rpb, n64, lambda r: r * 9 + g * 3 + 1) + _halo_specs(tq, rpb, n64, lambda r: r * 9 + g * 3 + 2)
        + [row, row, row, tab, tab, tab],
        out_specs=row,
        scratch_shapes=[pltpu.VMEM((tq + 2 * HALF, DIL_W), BF16), pltpu.VMEM((tq + 2 * HALF, DIL_W), BF16)],
        compiler_params=_params(2),
    )(view, view, view, view, view, view, view, v4(do), v4(lse), v4(dlt), tv(dil["c"]), tv(dil["s1"]), tv(dil["s2"]))
    return dq.reshape(S, DIL_W)


def _band_bwd_kv(dilr, g, d, do, lse, dlt, dil, name):
    S = dilr.shape[0]
    L = S // d
    tq, nsub, rpb, n64 = _band_tiles(L)
    view = dilr.reshape(L, d * dilr.shape[1])
    v4 = lambda t: t.reshape(L, d * DIL_W)
    tv = lambda t: t.reshape(L, d * LANE)
    win = SUB + 2 * HALF

    def body(k_ref, v_ref, qp_ref, qc_ref, qn_ref, dop_ref, doc_ref, don_ref, lp_ref, lc_ref, ln_ref, tp_ref, tc_ref, tn_ref,
             c_ref, s1_ref, s2_ref, dk_ref, dv_ref, qbuf, dobuf, lbuf, tbuf):
        j = pl.program_id(1)
        _fill(qbuf, qp_ref, qc_ref, qn_ref, tq)
        _fill(dobuf, dop_ref, doc_ref, don_ref, tq)
        _fill(lbuf, lp_ref, lc_ref, ln_ref, tq)
        _fill(tbuf, tp_ref, tc_ref, tn_ref, tq)
        lo = _lo_lanes()

        def sub(a, carry):
            r0 = pl.multiple_of(a * SUB, SUB)
            rows = pl.ds(r0, SUB)
            wrows = pl.ds(r0, win)
            valid = _band_valid(j * tq + r0 - HALF, j * tq + r0, win, SUB, L, True)
            cv, s1v, s2v = c_ref[rows, :], s1_ref[rows, :], s2_ref[rows, :]
            for hp in range(4):
                cs = slice(hp * LANE, (hp + 1) * LANE)
                k = k_ref[rows, cs]
                v = v_ref[rows, cs]
                qw = qbuf[wrows, cs]
                dow = dobuf[wrows, cs]
                lw = lbuf[wrows, cs]
                tw = tbuf[wrows, cs]
                dk = jnp.zeros((SUB, LANE), F32)
                dv = jnp.zeros((SUB, LANE), F32)
                for hh in range(2):
                    sel = lo if hh == 0 else jnp.logical_not(lo)
                    c0 = hh * DIL_HD
                    qm = jnp.where(sel, qw, jnp.zeros_like(qw))
                    dom = jnp.where(sel, dow, 0.0).astype(BF16)
                    s = lax.dot_general(qm, k, _NT, preferred_element_type=F32)
                    p = jnp.where(valid, jnp.exp(s - lw[:, c0:c0 + 1]), 0.0)
                    dv = dv + lax.dot_general(p.astype(BF16), dom, _TN, preferred_element_type=F32)
                    dp = lax.dot_general(dom, v, _NT, preferred_element_type=F32)
                    ds = (p * (dp - tw[:, c0:c0 + 1])).astype(BF16)
                    dk = dk + lax.dot_general(ds, qm, _TN, preferred_element_type=F32)
                dk_ref[rows, cs] = _rot(dk, cv, -s1v, -s2v, ROT // 2).astype(BF16)
                dv_ref[rows, cs] = dv.astype(BF16)
            return carry

        lax.fori_loop(0, nsub, sub, 0)

    row = pl.BlockSpec((tq, DIL_W), lambda r, i: (i, r))
    tab = pl.BlockSpec((tq, LANE), lambda r, i: (i, r))
    halo = _halo_specs(tq, rpb, n64, lambda r: r)
    hb = tq + 2 * HALF
    dk, dv = pl.pallas_call(
        body, name=name, grid=(d, L // tq),
        out_shape=(jax.ShapeDtypeStruct((L, d * DIL_W), BF16), jax.ShapeDtypeStruct((L, d * DIL_W), BF16)),
        in_specs=[pl.BlockSpec((tq, DIL_W), lambda r, i: (i, r * 9 + g * 3 + 1)), pl.BlockSpec((tq, DIL_W), lambda r, i: (i, r * 9 + g * 3 + 2))]
        + _halo_specs(tq, rpb, n64, lambda r: r * 9 + g * 3) + halo + halo + halo + [tab, tab, tab],
        out_specs=(row, row),
        scratch_shapes=[pltpu.VMEM((hb, DIL_W), BF16), pltpu.VMEM((hb, DIL_W), F32), pltpu.VMEM((hb, DIL_W), F32), pltpu.VMEM((hb, DIL_W), F32)],
        compiler_params=_params(2),
    )(view, view, view, view, view, v4(do), v4(do), v4(do), v4(lse), v4(lse), v4(lse), v4(dlt), v4(dlt), v4(dlt),
      tv(dil["c"]), tv(dil["s1"]), tv(dil["s2"]))
    return dk.reshape(S, DIL_W), dv.reshape(S, DIL_W)


def _sigmoid(x):
    return 1.0 / (1.0 + jnp.exp(-x))


def _gate_a(o, p, name):
    S = o.shape[0]
    tm = ROW_TILE

    def body(o_ref, g_ref, a_ref):
        g = g_ref[...]
        a_ref[...] = (o_ref[...] * (g * _sigmoid(g))).astype(BF16)

    blk = pl.BlockSpec((tm, HP), lambda i: (i, 0))
    return pl.pallas_call(
        body, name=name, grid=(S // tm,), out_shape=jax.ShapeDtypeStruct((S, HP), BF16),
        in_specs=[blk, pl.BlockSpec((tm, HP), lambda i: (i, CB_GA * LANE // HP))], out_specs=blk, compiler_params=_params(1),
    )(o, p)


def _gate_a_bwd(da, o, p, name):
    S = o.shape[0]
    tm = ROW_TILE

    def body(da_ref, o_ref, g_ref, do_ref, dg_ref):
        g = g_ref[...]
        da = da_ref[...]
        sg = _sigmoid(g)
        do_ref[...] = da * (g * sg)
        dg_ref[...] = (da * o_ref[...] * (sg * (1.0 + g * (1.0 - sg)))).astype(BF16)

    blk = pl.BlockSpec((tm, HP), lambda i: (i, 0))
    return pl.pallas_call(
        body, name=name, grid=(S // tm,),
        out_shape=(jax.ShapeDtypeStruct((S, HP), F32), jax.ShapeDtypeStruct((S, HP), BF16)),
        in_specs=[blk, blk, pl.BlockSpec((tm, HP), lambda i: (i, CB_GA * LANE // HP))], out_specs=(blk, blk), compiler_params=_params(1),
    )(da, o, p)


def _merge_weights(l0, l1, l2):
    mx = jnp.maximum(jnp.maximum(l0, l1), l2)
    e0, e1, e2 = jnp.exp(l0 - mx), jnp.exp(l1 - mx), jnp.exp(l2 - mx)
    den = e0 + e1 + e2
    return e0 / den, e1 / den, e2 / den


def _gate_b(os_, ls_, p, name):
    S = p.shape[0]
    tm = ROW_TILE

    def body(o0, o1, o2, l0, l1, l2, g_ref, b_ref):
        a0, a1, a2 = _merge_weights(l0[...], l1[...], l2[...])
        bm = a0 * o0[...] + a1 * o1[...] + a2 * o2[...]
        g = g_ref[...]
        b_ref[...] = (bm * (g * _sigmoid(g))).astype(BF16)

    blk = pl.BlockSpec((tm, DIL_W), lambda i: (i, 0))
    return pl.pallas_call(
        body, name=name, grid=(S // tm,), out_shape=jax.ShapeDtypeStruct((S, DIL_W), BF16),
        in_specs=[blk] * 6 + [pl.BlockSpec((tm, DIL_W), lambda i: (i, CB_GB * LANE // DIL_W))], out_specs=blk, compiler_params=_params(1),
    )(*os_, *ls_, p)


def _gate_b_bwd(db, os_, ls_, p, name):
    S = p.shape[0]
    tm = ROW_TILE

    def body(db_ref, o0, o1, o2, l0, l1, l2, g_ref, dg_ref, d0, d1, d2, t0, t1, t2):
        a0, a1, a2 = _merge_weights(l0[...], l1[...], l2[...])
        bm = a0 * o0[...] + a1 * o1[...] + a2 * o2[...]
        g = g_ref[...]
        db = db_ref[...]
        sg = _sigmoid(g)
        dbm = db * (g * sg)
        dg_ref[...] = (db * bm * (sg * (1.0 + g * (1.0 - sg)))).astype(BF16)
        prod = dbm * bm
        lo = _lo_lanes()
        parts = []
        for hp in range(DIL_W // LANE):
            pc = prod[:, hp * LANE:(hp + 1) * LANE]
            tl = jnp.sum(jnp.where(lo, pc, 0.0), axis=-1, keepdims=True)
            th = jnp.sum(jnp.where(lo, 0.0, pc), axis=-1, keepdims=True)
            parts.append(jnp.where(lo, tl, th))
        t = jnp.concatenate(parts, axis=1)
        d0[...] = a0 * dbm
        d1[...] = a1 * dbm
        d2[...] = a2 * dbm
        t0[...] = a0 * t
        t1[...] = a1 * t
        t2[...] = a2 * t

    blk = pl.BlockSpec((tm, DIL_W), lambda i: (i, 0))
    f = jax.ShapeDtypeStruct((S, DIL_W), F32)
    outs = pl.pallas_call(
        body, name=name, grid=(S // tm,),
        out_shape=(jax.ShapeDtypeStruct((S, DIL_W), BF16), f, f, f, f, f, f),
        in_specs=[blk] * 7 + [pl.BlockSpec((tm, DIL_W), lambda i: (i, CB_GB * LANE // DIL_W))], out_specs=(blk,) * 7, compiler_params=_params(1),
    )(db, *os_, *ls_, p)
    return outs[0], outs[1:4], outs[4:7]


def _loss_head(x, target, g, name):
    S, D = x.shape
    tm = ROW_TILE

    def body(x_ref, t_ref, g_ref, dx_ref, dg_ref, loss_ref):
        xv = x_ref[...]
        gv = g_ref[...]
        r = lax.rsqrt(jnp.mean(xv * xv, axis=-1, keepdims=True) + EPS)
        xr = xv * r
        err = xr * gv - t_ref[...]
        lpart = 0.5 * jnp.sum(jnp.mean(err * err, axis=-1, keepdims=True), axis=0, keepdims=True)
        dy = err / D
        dyg = dy * gv
        c = jnp.mean(dyg * xv, axis=-1, keepdims=True)
        dx_ref[...] = r * dyg - xv * (r * r * r) * c
        gpart = jnp.sum(dy * xr, axis=0, keepdims=True)

        @pl.when(pl.program_id(0) == 0)
        def _():
            dg_ref[...] = gpart
            loss_ref[...] = jnp.broadcast_to(lpart, loss_ref.shape)

        @pl.when(pl.program_id(0) > 0)
        def _():
            dg_ref[...] += gpart
            loss_ref[...] += jnp.broadcast_to(lpart, loss_ref.shape)

    row = pl.BlockSpec((tm, D), lambda i: (i, 0))
    vec = pl.BlockSpec((1, D), lambda i: (0, 0))
    return pl.pallas_call(
        body, name=name, grid=(S // tm,),
        out_shape=(jax.ShapeDtypeStruct((S, D), F32), jax.ShapeDtypeStruct((1, D), F32), jax.ShapeDtypeStruct((1, D), F32)),
        in_specs=[row, row, vec], out_specs=(row, vec, vec), compiler_params=_params(1),
    )(x, target, g.reshape(1, D))


def _adamw(parts, w, m, v, name):
    R, C = w.shape
    tr = _pick(R, (128, 64, 32, 16, 8))

    def body(p_ref, w_ref, m_ref, v_ref, g_ref, d_ref, nm_ref, nv_ref):
        g = p_ref[0]
        for k in range(1, N_DEV):
            g = g + p_ref[k]
        m2 = ADAM_B1 * m_ref[...] + (1.0 - ADAM_B1) * g
        v2 = ADAM_B2 * v_ref[...] + (1.0 - ADAM_B2) * (g * g)
        m_hat = m2 / (1.0 - ADAM_B1 ** ADAM_STEP)
        v_hat = v2 / (1.0 - ADAM_B2 ** ADAM_STEP)
        g_ref[...] = g
        d_ref[...] = -ADAM_LR * (m_hat / (jnp.sqrt(v_hat) + ADAM_EPS) + ADAM_WD * w_ref[...])
        nm_ref[...] = m2
        nv_ref[...] = v2

    blk = pl.BlockSpec((tr, C), lambda i: (i, 0))
    f = jax.ShapeDtypeStruct((R, C), F32)
    return pl.pallas_call(
        body, name=name, grid=(R // tr,), out_shape=(f, f, f, f),
        in_specs=[pl.BlockSpec((N_DEV, tr, C), lambda i: (0, i, 0)), blk, blk, blk], out_specs=(blk,) * 4,
        compiler_params=_params(1),
    )(parts, w, m, v)


def _exchange(srcs, same_src, name):
    n = len(srcs)
    shapes = [tuple(s.shape) if same_src else tuple(s.shape[1:]) for s in srcs]

    def body(*refs):
        src_refs, out_refs = refs[:n], refs[n:2 * n]
        send_sems, recv_sems, local_sems = refs[2 * n:]
        x, y, c = lax.axis_index("x"), lax.axis_index("y"), lax.axis_index("c")
        me = 4 * x + 2 * y + c

        def block(t, j):
            return src_refs[t] if same_src else src_refs[t].at[j]

        local = [pltpu.make_async_copy(block(t, me), out_refs[t].at[me], local_sems.at[t]) for t in range(n)]
        for cp in local:
            cp.start()
        copies = []
        for k in range(1, N_DEV):
            px = 1 - x if (k >> 2) & 1 else x
            py = 1 - y if (k >> 1) & 1 else y
            pc = 1 - c if k & 1 else c
            for t in range(n):
                cp = pltpu.make_async_remote_copy(
                    src_ref=block(t, 4 * px + 2 * py + pc), dst_ref=out_refs[t].at[me],
                    send_sem=send_sems.at[(k - 1) * n + t], recv_sem=recv_sems.at[(k - 1) * n + t],
                    device_id=(px, py, pc), device_id_type=pl.DeviceIdType.MESH)
                cp.start()
                copies.append(cp)
        for cp in copies:
            cp.wait()
        for cp in local:
            cp.wait()

    hbm = pl.BlockSpec(memory_space=pltpu.HBM)
    outs = pl.pallas_call(
        body, name=name, out_shape=tuple(jax.ShapeDtypeStruct((N_DEV,) + shp, s.dtype) for shp, s in zip(shapes, srcs)),
        in_specs=[hbm] * n, out_specs=(hbm,) * n,
        scratch_shapes=[pltpu.SemaphoreType.DMA(((N_DEV - 1) * n,)), pltpu.SemaphoreType.DMA(((N_DEV - 1) * n,)), pltpu.SemaphoreType.DMA((n,))],
    )(*srcs)
    return list(outs)


def _full_weights(g_in, g_uq, g_ukv, g_out):
    cat_cols = lambda t: jnp.moveaxis(t, 0, 2).reshape(t.shape[1], t.shape[2], -1)
    return cat_cols(g_in), cat_cols(g_uq), cat_cols(g_ukv), jnp.moveaxis(g_out, 0, 1).reshape(DEPTH, D_MODEL, D_MODEL)


def _grad_blocks(g_in, g_uq, g_ukv, g_out):
    split_cols = lambda t: jnp.moveaxis(t.reshape(t.shape[0], t.shape[1], N_DEV, -1), 2, 0)
    rows = jnp.moveaxis(g_out.reshape(DEPTH, N_DEV, D_MODEL // N_DEV, D_MODEL), 1, 0)
    return [split_cols(g_in), split_cols(g_uq), split_cols(g_ukv), rows]


def _layer_weights(w_in, w_uq, w_ukv, w_out):
    z = lambda r, n: jnp.zeros((r, n), BF16)
    c_q, c_kv, k_r = w_in[:, 0:384], w_in[:, 384:640], w_in[:, 640:672]
    gate_a, dil, gate_b = w_in[:, 672:1184], w_in[:, 1184:5792], w_in[:, 5792:6304]
    ga_pad = jnp.pad(gate_a.reshape(D_MODEL, HEADS, VDIM), ((0, 0), (0, 0), (0, LANE - VDIM))).reshape(D_MODEL, HP)
    w_p = jnp.concatenate([ga_pad, dil, gate_b, c_kv, z(D_MODEL, 64), k_r, z(D_MODEL, 32), c_q], axis=1)
    uq = jnp.pad(w_uq.reshape(Q_LORA, HEADS, NOPE + ROPE), ((0, 0), (0, 0), (0, LANE - NOPE - ROPE))).reshape(Q_LORA, HP)
    ukv = w_ukv.reshape(KV_LORA, HEADS, NOPE + VDIM)
    pad64 = lambda t: jnp.pad(t, ((0, 0), (0, 0), (0, LANE - 64))).reshape(KV_LORA, HP)
    uk, uv = pad64(ukv[..., :NOPE]), pad64(ukv[..., NOPE:])
    wa = jnp.pad(w_out[:HEADS * VDIM].reshape(HEADS, VDIM, D_MODEL), ((0, 0), (0, LANE - VDIM), (0, 0))).reshape(HP, D_MODEL)
    wb = w_out[HEADS * VDIM:]
    return dict(p=w_p, uq=uq, uk=uk, uv=uv, wa=wa, wb=wb)


def _unpad_grads(g):
    gp = g["p"]
    seg = lambda cb, n: gp[:, cb * LANE:cb * LANE + n]
    ga = seg(CB_GA, HP).reshape(D_MODEL, HEADS, LANE)[..., :VDIM].reshape(D_MODEL, HEADS * VDIM)
    k_r = gp[:, CB_KR * LANE + NOPE:CB_KR * LANE + NOPE + ROPE]
    g_in = jnp.concatenate([seg(CB_CQ, Q_LORA), seg(CB_CKV, KV_LORA), k_r, ga, seg(CB_DIL, 9 * DIL_W), seg(CB_GB, DIL_W)], axis=1)
    g_uq = g["uq"].reshape(Q_LORA, HEADS, LANE)[..., :NOPE + ROPE].reshape(Q_LORA, -1)
    uk = g["uk"].reshape(KV_LORA, HEADS, LANE)[..., :NOPE]
    uv = g["uv"].reshape(KV_LORA, HEADS, LANE)[..., :VDIM]
    g_ukv = jnp.concatenate([uk, uv], axis=-1).reshape(KV_LORA, -1)
    wa = g["wa"].reshape(HEADS, LANE, D_MODEL)[:, :VDIM].reshape(HEADS * VDIM, D_MODEL)
    g_out = jnp.concatenate([wa, g["wb"]], axis=0)
    return g_in, g_uq, g_ukv, g_out


def _layer_fwd(x, w, norm_g, q_norm_g, kv_norm_g, mla, dil, l):
    n = lambda s: f"l{l}_{s}"
    h = _rms_fwd(x, 0, D_MODEL, norm_g, n("norm"))
    p = _mm(h, w["p"], "nn", n("in_proj"))
    cqn = _rms_fwd(p, CB_CQ * LANE // Q_LORA, Q_LORA, q_norm_g, n("q_norm"))
    ckvn = _rms_fwd(p, CB_CKV * LANE // KV_LORA, KV_LORA, kv_norm_g, n("kv_norm"))
    qp = _mm(cqn, w["uq"], "nn", n("q_up"))
    kpre = _mm(ckvn, w["uk"], "nn", n("k_up"))
    v = _mm(ckvn, w["uv"], "nn", n("v_up"), out_dtype=BF16)
    q = _rope_heads(qp, mla["c_q"], mla["s1"], mla["s2"], n("q_rope"), BF16)
    k = _k_assemble(kpre, p, mla, n("k_asm"))
    o, lse = _flash_fwd(q, k, v, n("mla_fwd"))
    a = _gate_a(o, p, n("gate_a"))
    dilr = _dil_prep(p, dil, n("dil_prep"))
    os_, ls_ = [], []
    for g, (_, d) in enumerate(DIL_PAIRS):
        og, lg = _band_fwd(dilr, g, d, n(f"band{g}_fwd"))
        os_.append(og)
        ls_.append(lg)
    b = _gate_b(os_, ls_, p, n("gate_b"))
    x1 = _mm(a, w["wa"], "nn", n("out_a"), res=x)
    x2 = _mm(b, w["wb"], "nn", n("out_b"), res=x1)
    saved = dict(x=x, h=h, p=p, cqn=cqn, ckvn=ckvn, q=q, k=k, v=v, o=o, lse=lse, a=a, dilr=dilr, os=os_, ls=ls_, b=b)
    return x2, saved


def _layer_bwd(dx, sv, w, norm_g, q_norm_g, kv_norm_g, mla, dil, l):
    n = lambda s: f"l{l}_{s}"
    g = {}
    da = _mm(dx, w["wa"], "nt", n("d_a"))
    db = _mm(dx, w["wb"], "nt", n("d_b"))
    g["wa"] = _mm(sv["a"], dx, "tn", n("dw_a"))
    g["wb"] = _mm(sv["b"], dx, "tn", n("dw_b"))
    do, dga = _gate_a_bwd(da, sv["o"], sv["p"], n("gate_a_bwd"))
    dgb, dos, dts = _gate_b_bwd(db, sv["os"], sv["ls"], sv["p"], n("gate_b_bwd"))
    ddil = []
    for gi, (_, d) in enumerate(DIL_PAIRS):
        dq = _band_bwd_q(sv["dilr"], gi, d, dos[gi], sv["ls"][gi], dts[gi], dil, n(f"band{gi}_bwd_q"))
        dk, dv = _band_bwd_kv(sv["dilr"], gi, d, dos[gi], sv["ls"][gi], dts[gi], dil, n(f"band{gi}_bwd_kv"))
        ddil += [dq, dk, dv]
    dq, dk, dv = _flash_bwd(sv["q"], sv["k"], sv["v"], sv["o"], do, sv["lse"], n("mla_bwd"))
    dqp = _rope_heads(dq, mla["c_q"], -mla["s1"], -mla["s2"], n("q_rope_bwd"), BF16)
    dkr = _kr_bwd(dk, mla, n("kr_bwd"))
    g["uq"] = _mm(sv["cqn"], dqp, "tn", n("dw_uq"))
    g["uk"] = _mm(sv["ckvn"], dk, "tn", n("dw_uk"))
    g["uv"] = _mm(sv["ckvn"], dv, "tn", n("dw_uv"))
    dcqn = _mm(dqp, w["uq"], "nt", n("d_cqn"))
    dckvn = _mm(dk, w["uk"], "nt", n("d_ckvn_k"))
    dckvn = _mm(dv, w["uv"], "nt", n("d_ckvn_v"), res=dckvn)
    dcq, g_qn = _rms_bwd(sv["p"], CB_CQ * LANE // Q_LORA, Q_LORA, dcqn, q_norm_g, n("q_norm_bwd"), BF16)
    dckv, g_kvn = _rms_bwd(sv["p"], CB_CKV * LANE // KV_LORA, KV_LORA, dckvn, kv_norm_g, n("kv_norm_bwd"), BF16)
    dp = jnp.concatenate([dga] + ddil + [dgb, dckv, dkr, dcq], axis=1)
    g["p"] = _mm(sv["h"], dp, "tn", n("dw_in"))
    dh = _mm(dp, w["p"], "nt", n("d_h"))
    dx_in, g_n = _rms_bwd(sv["x"], 0, D_MODEL, dh, norm_g, n("norm_bwd"), F32, res=dx)
    return dx_in, g, g_n, g_qn, g_kvn


_SMALL_ROWS = 16


def _pack_small(norm, qn, kvn, fin, loss_row=None):
    padc = lambda t: jnp.pad(t, ((0, 0), (0, D_MODEL - t.shape[1])))
    extra = jnp.zeros((1, D_MODEL), F32) if loss_row is None else loss_row
    return jnp.concatenate([norm, padc(qn), padc(kvn), fin.reshape(1, D_MODEL), extra, jnp.zeros((2, D_MODEL), F32)], axis=0)


def _unpack_small(p):
    return (p[0:4], p[4:8, :Q_LORA], p[8:12, :KV_LORA], p[12]), p[13, 0]


def kernel(x, norm_g, w_in, q_norm_g, kv_norm_g, w_uq, w_ukv, w_out, final_g, loss_target, m_norm_g, m_w_in, m_q_norm_g, m_kv_norm_g, m_w_uq, m_w_ukv, m_w_out, m_final_g, v_norm_g, v_w_in, v_q_norm_g, v_kv_norm_g, v_w_uq, v_w_ukv, v_w_out, v_final_g):
    S = x.shape[1]
    xs = x.reshape(S, D_MODEL)
    target = loss_target.reshape(S, D_MODEL)

    gathered = _exchange([t.astype(BF16) for t in (w_in, w_uq, w_ukv, w_out)], True, "gather_weights")
    full = _full_weights(*gathered)
    ws = [_layer_weights(*(t[l] for t in full)) for l in range(DEPTH)]
    mla, dil = _rope_tables(S)

    saved = []
    h = xs
    for l in range(DEPTH):
        h, sv = _layer_fwd(h, ws[l], norm_g[l], q_norm_g[l], kv_norm_g[l], mla, dil, l)
        saved.append(sv)
    dx, g_final, loss_row = _loss_head(h, target, final_g, "loss_head")
    g_layers, g_norm, g_qn, g_kvn = [None] * DEPTH, [None] * DEPTH, [None] * DEPTH, [None] * DEPTH
    for l in reversed(range(DEPTH)):
        dx, g, g_norm[l], g_qn[l], g_kvn[l] = _layer_bwd(dx, saved[l], ws[l], norm_g[l], q_norm_g[l], kv_norm_g[l], mla, dil, l)
        g_layers[l] = _unpad_grads(g)

    g_full = [jnp.stack([g_layers[l][i] for l in range(DEPTH)]) for i in range(4)]
    parts = _exchange(_grad_blocks(*g_full), False, "exchange_grads")
    sh = []
    for t, (pt, w, m, v) in enumerate(zip(parts, (w_in, w_uq, w_ukv, w_out), (m_w_in, m_w_uq, m_w_ukv, m_w_out), (v_w_in, v_w_uq, v_w_ukv, v_w_out))):
        two = lambda a: a.reshape(-1, a.shape[-1])
        outs = _adamw(pt.reshape(N_DEV, -1, pt.shape[-1]), two(w), two(m), two(v), f"adamw_{t}")
        sh.append([o.reshape(w.shape) for o in outs])

    small = _pack_small(jnp.concatenate(g_norm, 0), jnp.concatenate(g_qn, 0), jnp.concatenate(g_kvn, 0), g_final, loss_row)
    (small_parts,) = _exchange([small], True, "gather_small")
    souts = _adamw(small_parts, _pack_small(norm_g, q_norm_g, kv_norm_g, final_g), _pack_small(m_norm_g, m_q_norm_g, m_kv_norm_g, m_final_g),
                   _pack_small(v_norm_g, v_q_norm_g, v_kv_norm_g, v_final_g), "adamw_small")
    (g_sm, loss), (d_sm, _), (m_sm, _), (v_sm, _) = (_unpack_small(t) for t in souts)

    def order(sm, k):
        return (sm[0], sh[0][k], sm[1], sm[2], sh[1][k], sh[2][k], sh[3][k], sm[3])

    return (loss, dx.reshape(1, S, D_MODEL), *order(g_sm, 0), *order(d_sm, 1), *order(m_sm, 2), *order(v_sm, 3))
```

```python
import math

import jax
import jax.numpy as jnp
from jax import lax
from jax.experimental import pallas as pl
from jax.experimental.pallas import tpu as pltpu

F32 = jnp.float32
BF16 = jnp.bfloat16

D_MODEL = 1024
DEPTH = 4
HEADS = 8
NOPE = 64
ROPE = 32
VDIM = 64
Q_LORA = 384
KV_LORA = 256
DIL_PAIRS = ((128, 1), (512, 4), (2048, 16))
DIL_HD = 64
DIL_W = 512
ROT = 16
HALF = 64
THETA = 500000.0
EPS = 1e-6
IN_WIDTH = 6304
N_DEV = 8

LANE = 128
CB_GA, CB_DIL, CB_GB, CB_CKV, CB_KR, CB_CQ = 0, 8, 44, 48, 50, 51
NP = 54 * LANE
HP = HEADS * LANE

ADAM_LR = 0.001
ADAM_B1 = 0.9
ADAM_B2 = 0.999
ADAM_EPS = 1e-08
ADAM_WD = 0.01
ADAM_STEP = 10

VMEM_LIMIT = 48 * 1024 * 1024
ROW_TILE = 512
SUB = 128

_NT = (((1,), (1,)), ((), ()))
_NN = (((1,), (0,)), ((), ()))
_TN = (((0,), (0,)), ((), ()))


def _params(n_axes):
    return pltpu.CompilerParams(dimension_semantics=("arbitrary",) * n_axes, vmem_limit_bytes=VMEM_LIMIT)


def _pick(n, cands):
    for c in cands:
        if n % c == 0:
            return c
    raise ValueError(f"no tile for {n}")


def _mm(a, b, mode, name, out_dtype=F32, res=None):
    if mode == "nn":
        (M, K), (K2, N) = a.shape, b.shape
    elif mode == "nt":
        (M, K), (N, K2) = a.shape, b.shape
    else:
        (K, M), (K2, N) = a.shape, b.shape
    assert K == K2, (a.shape, b.shape, mode)
    tm = _pick(M, (512, 384, 256, 128))
    tn = _pick(N, (1152, 1024, 768, 640, 512, 384, 256, 128))
    tk = _pick(K, (1152, 1024, 768, 640, 512, 384, 256, 128))
    nk = K // tk
    dims = {"nn": _NN, "nt": _NT, "tn": _TN}[mode]

    def body(*refs):
        if res is not None:
            a_ref, b_ref, r_ref, o_ref = refs[:4]
        else:
            a_ref, b_ref, o_ref = refs[:3]
            r_ref = None
        part = lax.dot_general(a_ref[...].astype(BF16), b_ref[...].astype(BF16), dims, preferred_element_type=F32)

        def finish(acc):
            if r_ref is not None:
                acc = acc + r_ref[...]
            o_ref[...] = acc.astype(out_dtype)

        if nk == 1:
            finish(part)
        else:
            acc_ref = refs[-1]
            k = pl.program_id(2)

            @pl.when(k == 0)
            def _():
                acc_ref[...] = part

            @pl.when(k > 0)
            def _():
                acc_ref[...] += part

            @pl.when(k == nk - 1)
            def _():
                finish(acc_ref[...])

    if mode == "nn":
        a_spec = pl.BlockSpec((tm, tk), lambda i, j, k: (i, k))
        b_spec = pl.BlockSpec((tk, tn), lambda i, j, k: (k, j))
    elif mode == "nt":
        a_spec = pl.BlockSpec((tm, tk), lambda i, j, k: (i, k))
        b_spec = pl.BlockSpec((tn, tk), lambda i, j, k: (j, k))
    else:
        a_spec = pl.BlockSpec((tk, tm), lambda i, j, k: (k, i))
        b_spec = pl.BlockSpec((tk, tn), lambda i, j, k: (k, j))
    o_spec = pl.BlockSpec((tm, tn), lambda i, j, k: (i, j))
    in_specs = [a_spec, b_spec] + ([o_spec] if res is not None else [])
    args = (a, b) + ((res,) if res is not None else ())
    return pl.pallas_call(
        body, name=name, grid=(M // tm, N // tn, nk), out_shape=jax.ShapeDtypeStruct((M, N), out_dtype),
        in_specs=in_specs, out_specs=o_spec,
        scratch_shapes=[pltpu.VMEM((tm, tn), F32)] if nk > 1 else [],
        compiler_params=_params(3),
    )(*args)


def _rms_fwd(src, cb, width, g, name):
    S = src.shape[0]
    tm = ROW_TILE

    def body(x_ref, g_ref, o_ref):
        x = x_ref[...]
        r = lax.rsqrt(jnp.mean(x * x, axis=-1, keepdims=True) + EPS)
        o_ref[...] = (x * r * g_ref[...]).astype(BF16)

    return pl.pallas_call(
        body, name=name, grid=(S // tm,), out_shape=jax.ShapeDtypeStruct((S, width), BF16),
        in_specs=[pl.BlockSpec((tm, width), lambda i: (i, cb)), pl.BlockSpec((1, width), lambda i: (0, 0))],
        out_specs=pl.BlockSpec((tm, width), lambda i: (i, 0)), compiler_params=_params(1),
    )(src, g.reshape(1, width))


def _rms_bwd(src, cb, width, dy, g, name, out_dtype, res=None):
    S = src.shape[0]
    tm = ROW_TILE

    def body(*refs):
        if res is not None:
            x_ref, dy_ref, g_ref, r_ref, dx_ref, dg_ref = refs
        else:
            x_ref, dy_ref, g_ref, dx_ref, dg_ref = refs
            r_ref = None
        x = x_ref[...]
        dy = dy_ref[...]
        r = lax.rsqrt(jnp.mean(x * x, axis=-1, keepdims=True) + EPS)
        dyg = dy * g_ref[...]
        c = jnp.mean(dyg * x, axis=-1, keepdims=True)
        dx = r * dyg - x * (r * r * r) * c
        if r_ref is not None:
            dx = dx + r_ref[...]
        dx_ref[...] = dx.astype(out_dtype)
        part = jnp.sum(dy * x * r, axis=0, keepdims=True)

        @pl.when(pl.program_id(0) == 0)
        def _():
            dg_ref[...] = part

        @pl.when(pl.program_id(0) > 0)
        def _():
            dg_ref[...] += part

    row = pl.BlockSpec((tm, width), lambda i: (i, 0))
    in_specs = [pl.BlockSpec((tm, width), lambda i: (i, cb)), row, pl.BlockSpec((1, width), lambda i: (0, 0))]
    args = [src, dy, g.reshape(1, width)]
    if res is not None:
        in_specs.append(row)
        args.append(res)
    return pl.pallas_call(
        body, name=name, grid=(S // tm,),
        out_shape=(jax.ShapeDtypeStruct((S, width), out_dtype), jax.ShapeDtypeStruct((1, width), F32)),
        in_specs=in_specs, out_specs=(row, pl.BlockSpec((1, width), lambda i: (0, 0))),
        compiler_params=_params(1),
    )(*args)


def _rot(x, c, s1, s2, h):
    return x * c + pltpu.roll(x, x.shape[1] - h, 1) * s1 + pltpu.roll(x, h, 1) * s2


def _rope_tables(S):
    def tables(dim):
        inv = 1.0 / (THETA ** (jnp.arange(0, dim, 2, dtype=F32) / dim))
        ang = jnp.arange(S, dtype=F32)[:, None] * inv[None, :]
        return jnp.cos(ang), jnp.sin(ang)

    cm, sm = tables(ROPE)
    cd, sd = tables(ROT)
    z = lambda n: jnp.zeros((S, n), F32)
    o = lambda n: jnp.ones((S, n), F32)
    mla = dict(
        c_q=jnp.concatenate([o(64), cm, cm, z(32)], 1),
        c_kr=jnp.concatenate([z(64), cm, cm, z(32)], 1),
        s1=jnp.concatenate([z(64), -sm, z(16), z(32)], 1),
        s2=jnp.concatenate([z(64), z(16), sm, z(32)], 1),
    )
    one = lambda a, b, c: jnp.concatenate([a, b, c, a, b, c], 1)
    dil = dict(c=one(cd, cd, o(48)), s1=one(-sd, z(8), z(48)), s2=one(z(8), sd, z(48)))
    return mla, dil


def _rope_heads(src, c, s1, s2, name, out_dtype):
    S = src.shape[0]
    tm = ROW_TILE

    def body(x_ref, c_ref, s1_ref, s2_ref, o_ref):
        cv, s1v, s2v = c_ref[...], s1_ref[...], s2_ref[...]
        for h in range(HEADS):
            sl = slice(h * LANE, (h + 1) * LANE)
            o_ref[:, sl] = _rot(x_ref[:, sl], cv, s1v, s2v, ROPE // 2).astype(out_dtype)

    tab = pl.BlockSpec((tm, LANE), lambda i: (i, 0))
    wide = pl.BlockSpec((tm, HP), lambda i: (i, 0))
    return pl.pallas_call(
        body, name=name, grid=(S // tm,), out_shape=jax.ShapeDtypeStruct((S, HP), out_dtype),
        in_specs=[wide, tab, tab, tab], out_specs=wide, compiler_params=_params(1),
    )(src, c, s1, s2)


def _k_assemble(kpre, p, mla, name):
    S = kpre.shape[0]
    tm = ROW_TILE

    def body(k_ref, kr_ref, c_ref, s1_ref, s2_ref, o_ref):
        r = _rot(kr_ref[...], c_ref[...], s1_ref[...], s2_ref[...], ROPE // 2)
        for h in range(HEADS):
            sl = slice(h * LANE, (h + 1) * LANE)
            o_ref[:, sl] = (k_ref[:, sl] + r).astype(BF16)

    tab = pl.BlockSpec((tm, LANE), lambda i: (i, 0))
    wide = pl.BlockSpec((tm, HP), lambda i: (i, 0))
    return pl.pallas_call(
        body, name=name, grid=(S // tm,), out_shape=jax.ShapeDtypeStruct((S, HP), BF16),
        in_specs=[wide, pl.BlockSpec((tm, LANE), lambda i: (i, CB_KR)), tab, tab, tab],
        out_specs=wide, compiler_params=_params(1),
    )(kpre, p, mla["c_kr"], mla["s1"], mla["s2"])


def _kr_bwd(dk, mla, name):
    S = dk.shape[0]
    tm = ROW_TILE

    def body(dk_ref, c_ref, s1_ref, s2_ref, o_ref):
        t = dk_ref[:, 0:LANE]
        for h in range(1, HEADS):
            t = t + dk_ref[:, h * LANE:(h + 1) * LANE]
        lane = lax.broadcasted_iota(jnp.int32, (1, LANE), 1)
        t = jnp.where((lane >= NOPE) & (lane < NOPE + ROPE), t, 0.0)
        o_ref[...] = _rot(t, c_ref[...], -s1_ref[...], -s2_ref[...], ROPE // 2).astype(BF16)

    tab = pl.BlockSpec((tm, LANE), lambda i: (i, 0))
    return pl.pallas_call(
        body, name=name, grid=(S // tm,), out_shape=jax.ShapeDtypeStruct((S, LANE), BF16),
        in_specs=[pl.BlockSpec((tm, HP), lambda i: (i, 0)), tab, tab, tab],
        out_specs=tab, compiler_params=_params(1),
    )(dk, mla["c_kr"], mla["s1"], mla["s2"])


def _dil_prep(p, dil, g, name):
    S = p.shape[0]
    tm = ROW_TILE
    first = CB_DIL * LANE // DIL_W + 3 * g

    def body(x_ref, c_ref, s1_ref, s2_ref, o_ref):
        t = pl.program_id(1)
        rep = DIL_W // LANE

        def roped():
            return _rot(x_ref[...], jnp.tile(c_ref[...], (1, rep)), jnp.tile(s1_ref[...], (1, rep)),
                        jnp.tile(s2_ref[...], (1, rep)), ROT // 2)

        @pl.when(t == 0)
        def _():
            o_ref[...] = (roped() * (DIL_HD ** -0.5)).astype(BF16)

        @pl.when(t == 1)
        def _():
            o_ref[...] = roped().astype(BF16)

        @pl.when(t == 2)
        def _():
            o_ref[...] = x_ref[...].astype(BF16)

    tab = pl.BlockSpec((tm, LANE), lambda i, j: (i, 0))
    return pl.pallas_call(
        body, name=name, grid=(S // tm, 3), out_shape=jax.ShapeDtypeStruct((S, 3 * DIL_W), BF16),
        in_specs=[pl.BlockSpec((tm, DIL_W), lambda i, j: (i, first + j)), tab, tab, tab],
        out_specs=pl.BlockSpec((tm, DIL_W), lambda i, j: (i, j)), compiler_params=_params(2),
    )(p, dil["c"], dil["s1"], dil["s2"])


def _flash_fwd(q, k, v, name):
    S = q.shape[0]
    tq = tk = _pick(S, (1024, 512))
    nk = S // tk
    c2 =(NOPE + ROPE) ** -0.5 * math.log2(math.e)

    def body(q_ref, k_ref, v_ref, o_ref, lse_ref, m_s, acc_s):
        j = pl.program_id(2)

        @pl.when(j == 0)
        def _():
            m_s[...] = jnp.full(m_s.shape, -jnp.inf, F32)
            acc_s[...] = jnp.zeros(acc_s.shape, F32)

        lane = lax.broadcasted_iota(jnp.int32, (1, LANE), 1)
        vv = jnp.where(lane == VDIM, jnp.ones((), BF16), v_ref[...])
        t = lax.dot_general(q_ref[...], k_ref[...], _NT, preferred_element_type=F32) * c2
        m_prev = m_s[...]
        m_new = jnp.maximum(m_prev, jnp.max(t, axis=-1, keepdims=True))
        alpha = jnp.exp2(m_prev - m_new)
        e = jnp.exp2(t - jnp.tile(m_new, (1, tk // LANE)))
        acc_s[...] = alpha * acc_s[...] + jnp.dot(e.astype(BF16), vv, preferred_element_type=F32)
        m_s[...] = m_new

        @pl.when(j == nk - 1)
        def _():
            acc = acc_s[...]
            l = acc[:, VDIM:VDIM + 1]
            o_ref[...] = jnp.where(lane < VDIM, acc / l, 0.0)
            lse_ref[...] = (m_s[...] + jnp.log2(l)) * math.log(2.0)

    qs = pl.BlockSpec((tq, LANE), lambda h, i, j: (i, h))
    ks = pl.BlockSpec((tk, LANE), lambda h, i, j: (j, h))
    return pl.pallas_call(
        body, name=name, grid=(HEADS, S // tq, nk),
        out_shape=(jax.ShapeDtypeStruct((S, HP), F32), jax.ShapeDtypeStruct((S, HP), F32)),
        in_specs=[qs, ks, ks], out_specs=(qs, qs),
        scratch_shapes=[pltpu.VMEM((tq, LANE), F32), pltpu.VMEM((tq, LANE), F32)],
        compiler_params=_params(3),
    )(q, k, v)


def _flash_bwd(q, k, v, o, do, lse, name, tq=1024, tk=1024, sub=1024):
    S = q.shape[0]
    nq = S // tq
    scale = (NOPE + ROPE) ** -0.5
    nsub = tq // sub

    def body(q_ref, k_ref, v_ref, o_ref, do_ref, lse_ref, dq_ref, dk_ref, dv_ref, dk_s, dv_s):
        j = pl.program_id(1)
        i = pl.program_id(2)
        kv = k_ref[...]
        vv = v_ref[...]
        dk_part = None
        dv_part = None
        dq_parts = []
        for u in range(nsub):
            rs = slice(u * sub, (u + 1) * sub)
            qv = q_ref[rs, :]
            do = do_ref[rs, :]
            dob = do.astype(BF16)
            s = lax.dot_general(qv, kv, _NT, preferred_element_type=F32) * scale
            p = jnp.exp(s - lse_ref[rs, 0:1])
            dp = lax.dot_general(dob, vv, _NT, preferred_element_type=F32)
            delta = jnp.sum(do * o_ref[rs, :], axis=-1, keepdims=True)
            ds = (p * (dp - delta)).astype(BF16)
            dvp = lax.dot_general(p.astype(BF16), dob, _TN, preferred_element_type=F32)
            dkp = lax.dot_general(ds, qv, _TN, preferred_element_type=F32)
            dv_part = dvp if dv_part is None else dv_part + dvp
            dk_part = dkp if dk_part is None else dk_part + dkp
            dq_parts.append(jnp.dot(ds, kv, preferred_element_type=F32) * scale)
        dq_part = jnp.concatenate(dq_parts, axis=0)

        @pl.when(i == 0)
        def _():
            dk_s[...] = dk_part
            dv_s[...] = dv_part

        @pl.when(i > 0)
        def _():
            dk_s[...] += dk_part
            dv_s[...] += dv_part

        rows = pl.ds(pl.multiple_of(i * tq, tq), tq)

        @pl.when(j == 0)
        def _():
            dq_ref[rows, :] = dq_part

        @pl.when(j > 0)
        def _():
            dq_ref[rows, :] += dq_part

        @pl.when(i == nq - 1)
        def _():
            dk_ref[...] = dk_s[...] * scale
            dv_ref[...] = dv_s[...].astype(BF16)

    qs = pl.BlockSpec((tq, LANE), lambda h, j, i: (i, h))
    ks = pl.BlockSpec((tk, LANE), lambda h, j, i: (j, h))
    return pl.pallas_call(
        body, name=name, grid=(HEADS, S // tk, nq),
        out_shape=(jax.ShapeDtypeStruct((S, HP), F32), jax.ShapeDtypeStruct((S, HP), F32), jax.ShapeDtypeStruct((S, HP), BF16)),
        in_specs=[qs, ks, ks, qs, qs, qs],
        out_specs=(pl.BlockSpec((S, LANE), lambda h, j, i: (0, h)), ks, ks),
        scratch_shapes=[pltpu.VMEM((tk, LANE), F32), pltpu.VMEM((tk, LANE), F32)],
        compiler_params=_params(3),
    )(q, k, v, o, do, lse)


def _band_tiles(L):
    tq = min(512, L)
    return tq, tq // SUB, tq // HALF, L // HALF


def _halo_specs(tq, rpb, n64, col):
    prev = pl.BlockSpec((HALF, DIL_W), lambda r, i: (jnp.maximum(rpb * i - 1, 0), col(r)))
    cur = pl.BlockSpec((tq, DIL_W), lambda r, i: (i, col(r)))
    nxt = pl.BlockSpec((HALF, DIL_W), lambda r, i: (jnp.minimum(rpb * i + rpb, n64 - 1), col(r)))
    return [prev, cur, nxt]


def _fill(buf, prev_ref, cur_ref, next_ref, tq):
    buf[0:HALF, :] = prev_ref[...]
    buf[HALF:HALF + tq, :] = cur_ref[...]
    buf[HALF + tq:HALF + tq + HALF, :] = next_ref[...]


def _lo_lanes():
    return lax.broadcasted_iota(jnp.int32, (1, LANE), 1) < DIL_HD


def _band_valid(q0, k0, nq, nk, L, bound_q):
    qpos = q0 + lax.broadcasted_iota(jnp.int32, (nq, 1), 0)
    kpos = k0 + lax.broadcasted_iota(jnp.int32, (1, nk), 1)
    side = qpos if bound_q else kpos
    return (jnp.abs(qpos - kpos) <= HALF) & (side >= 0) & (side < L)


def _band_fwd(dilr, d, name):
    S = dilr.shape[0]
    L = S // d
    tq, nsub, rpb, n64 = _band_tiles(L)
    view = dilr.reshape(L, d * dilr.shape[1])
    win = SUB + 2 * HALF

    def body(q_ref, kp_ref, kc_ref, kn_ref, vp_ref, vc_ref, vn_ref, o_ref, lse_ref, kbuf, vbuf):
        i = pl.program_id(1)
        _fill(kbuf, kp_ref, kc_ref, kn_ref, tq)
        _fill(vbuf, vp_ref, vc_ref, vn_ref, tq)
        lo = _lo_lanes()

        def sub(a, carry):
            r0 = pl.multiple_of(a * SUB, SUB)
            valid = _band_valid(i * tq + r0, i * tq + r0 - HALF, SUB, win, L, False)
            for hp in range(4):
                cs = slice(hp * LANE, (hp + 1) * LANE)
                q = q_ref[pl.ds(r0, SUB), cs]
                kw = kbuf[pl.ds(r0, win), cs]
                vw = vbuf[pl.ds(r0, win), cs]
                outs, lses = [], []
                for hh in range(2):
                    sel = lo if hh == 0 else jnp.logical_not(lo)
                    qm = jnp.where(sel, q, jnp.zeros_like(q))
                    s = lax.dot_general(qm, kw, _NT, preferred_element_type=F32)
                    s = jnp.where(valid, s, -jnp.inf)
                    m = jnp.max(s, axis=-1, keepdims=True)
                    e = jnp.exp(s - m)
                    l = jnp.sum(e, axis=-1, keepdims=True)
                    outs.append(jnp.dot(e.astype(BF16), vw, preferred_element_type=F32) / l)
                    lses.append(m + jnp.log(l))
                o_ref[pl.ds(r0, SUB), cs] = jnp.where(lo, outs[0], outs[1])
                lse_ref[pl.ds(r0, SUB), cs] = jnp.where(lo, lses[0], lses[1])
            return carry

        lax.fori_loop(0, nsub, sub, 0)

    out_spec = pl.BlockSpec((tq, DIL_W), lambda r, i: (i, r))
    o, lse = pl.pallas_call(
        body, name=name, grid=(d, L // tq),
        out_shape=(jax.ShapeDtypeStruct((L, d * DIL_W), F32), jax.ShapeDtypeStruct((L, d * DIL_W), F32)),
        in_specs=[pl.BlockSpec((tq, DIL_W), lambda r, i: (i, r * 3))]
        + _halo_specs(tq, rpb, n64, lambda r: r * 3 + 1) + _halo_specs(tq, rpb, n64, lambda r: r * 3 + 2),
        out_specs=(out_spec, out_spec),
        scratch_shapes=[pltpu.VMEM((tq + 2 * HALF, DIL_W), BF16), pltpu.VMEM((tq + 2 * HALF, DIL_W), BF16)],
        compiler_params=_params(2),
    )(view, view, view, view, view, view, view)
    return o.reshape(S, DIL_W), lse.reshape(S, DIL_W)


def _band_bwd_q(dilr, d, do, lse, dlt, dil, name):
    S = dilr.shape[0]
    L = S // d
    tq, nsub, rpb, n64 = _band_tiles(L)
    view = dilr.reshape(L, d * dilr.shape[1])
    v4 = lambda t: t.reshape(L, d * DIL_W)
    tv = lambda t: t.reshape(L, d * LANE)
    win = SUB + 2 * HALF

    def body(q_ref, kp_ref, kc_ref, kn_ref, vp_ref, vc_ref, vn_ref, do_ref, lse_ref, dlt_ref, c_ref, s1_ref, s2_ref, dq_ref,
             kbuf, vbuf):
        i = pl.program_id(1)
        _fill(kbuf, kp_ref, kc_ref, kn_ref, tq)
        _fill(vbuf, vp_ref, vc_ref, vn_ref, tq)
        lo = _lo_lanes()

        def sub(a, carry):
            r0 = pl.multiple_of(a * SUB, SUB)
            rows = pl.ds(r0, SUB)
            valid = _band_valid(i * tq + r0, i * tq + r0 - HALF, SUB, win, L, False)
            cv, s1v, s2v = c_ref[rows, :], s1_ref[rows, :], s2_ref[rows, :]
            for hp in range(4):
                cs = slice(hp * LANE, (hp + 1) * LANE)
                q = q_ref[rows, cs]
                do = do_ref[rows, cs]
                lse_v = lse_ref[rows, cs]
                dlt_v = dlt_ref[rows, cs]
                kw = kbuf[pl.ds(r0, win), cs]
                vw = vbuf[pl.ds(r0, win), cs]
                dqs = []
                for hh in range(2):
                    sel = lo if hh == 0 else jnp.logical_not(lo)
                    c0 = hh * DIL_HD
                    qm = jnp.where(sel, q, jnp.zeros_like(q))
                    dom = jnp.where(sel, do, 0.0).astype(BF16)
                    s = lax.dot_general(qm, kw, _NT, preferred_element_type=F32)
                    p = jnp.where(valid, jnp.exp(s - lse_v[:, c0:c0 + 1]), 0.0)
                    dp = lax.dot_general(dom, vw, _NT, preferred_element_type=F32)
                    ds = (p * (dp - dlt_v[:, c0:c0 + 1])).astype(BF16)
                    dqs.append(jnp.dot(ds, kw, preferred_element_type=F32))
                dq = jnp.where(lo, dqs[0], dqs[1])
                dq_ref[rows, cs] = (_rot(dq, cv, -s1v, -s2v, ROT // 2) * (DIL_HD ** -0.5)).astype(BF16)
            return carry

        lax.fori_loop(0, nsub, sub, 0)

    row = pl.BlockSpec((tq, DIL_W), lambda r, i: (i, r))
    tab = pl.BlockSpec((tq, LANE), lambda r, i: (i, r))
    dq = pl.pallas_call(
        body, name=name, grid=(d, L // tq), out_shape=jax.ShapeDtypeStruct((L, d * DIL_W), BF16),
        in_specs=[pl.BlockSpec((tq, DIL_W), lambda r, i: (i, r * 3))]
        + _halo_specs(tq, rpb, n64, lambda r: r * 3 + 1) + _halo_specs(tq, rpb, n64, lambda r: r * 3 + 2)
        + [row, row, row, tab, tab, tab],
        out_specs=row,
        scratch_shapes=[pltpu.VMEM((tq + 2 * HALF, DIL_W), BF16), pltpu.VMEM((tq + 2 * HALF, DIL_W), BF16)],
        compiler_params=_params(2),
    )(view, view, view, view, view, view, view, v4(do), v4(lse), v4(dlt), tv(dil["c"]), tv(dil["s1"]), tv(dil["s2"]))
    return dq.reshape(S, DIL_W)


def _band_bwd_kv(dilr, d, do, lse, dlt, dil, name):
    S = dilr.shape[0]
    L = S // d
    tq, nsub, rpb, n64 = _band_tiles(L)
    view = dilr.reshape(L, d * dilr.shape[1])
    v4 = lambda t: t.reshape(L, d * DIL_W)
    tv = lambda t: t.reshape(L, d * LANE)
    win = SUB + 2 * HALF

    def body(k_ref, v_ref, qp_ref, qc_ref, qn_ref, dop_ref, doc_ref, don_ref, lp_ref, lc_ref, ln_ref, tp_ref, tc_ref, tn_ref,
             c_ref, s1_ref, s2_ref, dk_ref, dv_ref, qbuf, dobuf, lbuf, tbuf):
        j = pl.program_id(1)
        _fill(qbuf, qp_ref, qc_ref, qn_ref, tq)
        _fill(dobuf, dop_ref, doc_ref, don_ref, tq)
        _fill(lbuf, lp_ref, lc_ref, ln_ref, tq)
        _fill(tbuf, tp_ref, tc_ref, tn_ref, tq)
        lo = _lo_lanes()

        def sub(a, carry):
            r0 = pl.multiple_of(a * SUB, SUB)
            rows = pl.ds(r0, SUB)
            wrows = pl.ds(r0, win)
            valid = _band_valid(j * tq + r0 - HALF, j * tq + r0, win, SUB, L, True)
            cv, s1v, s2v = c_ref[rows, :], s1_ref[rows, :], s2_ref[rows, :]
            for hp in range(4):
                cs = slice(hp * LANE, (hp + 1) * LANE)
                k = k_ref[rows, cs]
                v = v_ref[rows, cs]
                qw = qbuf[wrows, cs]
                dow = dobuf[wrows, cs]
                lw = lbuf[wrows, cs]
                tw = tbuf[wrows, cs]
                dk = jnp.zeros((SUB, LANE), F32)
                dv = jnp.zeros((SUB, LANE), F32)
                for hh in range(2):
                    sel = lo if hh == 0 else jnp.logical_not(lo)
                    c0 = hh * DIL_HD
                    qm = jnp.where(sel, qw, jnp.zeros_like(qw))
                    dom = jnp.where(sel, dow, 0.0).astype(BF16)
                    s = lax.dot_general(qm, k, _NT, preferred_element_type=F32)
                    p = jnp.where(valid, jnp.exp(s - lw[:, c0:c0 + 1]), 0.0)
                    dv = dv + lax.dot_general(p.astype(BF16), dom, _TN, preferred_element_type=F32)
                    dp = lax.dot_general(dom, v, _NT, preferred_element_type=F32)
                    ds = (p * (dp - tw[:, c0:c0 + 1])).astype(BF16)
                    dk = dk + lax.dot_general(ds, qm, _TN, preferred_element_type=F32)
                dk_ref[rows, cs] = _rot(dk, cv, -s1v, -s2v, ROT // 2).astype(BF16)
                dv_ref[rows, cs] = dv.astype(BF16)
            return carry

        lax.fori_loop(0, nsub, sub, 0)

    row = pl.BlockSpec((tq, DIL_W), lambda r, i: (i, r))
    tab = pl.BlockSpec((tq, LANE), lambda r, i: (i, r))
    halo = _halo_specs(tq, rpb, n64, lambda r: r)
    hb = tq + 2 * HALF
    dk, dv = pl.pallas_call(
        body, name=name, grid=(d, L // tq),
        out_shape=(jax.ShapeDtypeStruct((L, d * DIL_W), BF16), jax.ShapeDtypeStruct((L, d * DIL_W), BF16)),
        in_specs=[pl.BlockSpec((tq, DIL_W), lambda r, i: (i, r * 3 + 1)), pl.BlockSpec((tq, DIL_W), lambda r, i: (i, r * 3 + 2))]
        + _halo_specs(tq, rpb, n64, lambda r: r * 3) + halo + halo + halo + [tab, tab, tab],
        out_specs=(row, row),
        scratch_shapes=[pltpu.VMEM((hb, DIL_W), BF16), pltpu.VMEM((hb, DIL_W), F32), pltpu.VMEM((hb, DIL_W), F32), pltpu.VMEM((hb, DIL_W), F32)],
        compiler_params=_params(2),
    )(view, view, view, view, view, v4(do), v4(do), v4(do), v4(lse), v4(lse), v4(lse), v4(dlt), v4(dlt), v4(dlt),
      tv(dil["c"]), tv(dil["s1"]), tv(dil["s2"]))
    return dk.reshape(S, DIL_W), dv.reshape(S, DIL_W)


def _sigmoid(x):
    return 1.0 / (1.0 + jnp.exp(-x))


def _gate_a(o, p, name):
    S = o.shape[0]
    tm = ROW_TILE

    def body(o_ref, g_ref, a_ref):
        g = g_ref[...]
        a_ref[...] = (o_ref[...] * (g * _sigmoid(g))).astype(BF16)

    blk = pl.BlockSpec((tm, HP), lambda i: (i, 0))
    return pl.pallas_call(
        body, name=name, grid=(S // tm,), out_shape=jax.ShapeDtypeStruct((S, HP), BF16),
        in_specs=[blk, pl.BlockSpec((tm, HP), lambda i: (i, CB_GA * LANE // HP))], out_specs=blk, compiler_params=_params(1),
    )(o, p)


def _gate_a_bwd(da, o, p, name):
    S = o.shape[0]
    tm = ROW_TILE

    def body(da_ref, o_ref, g_ref, do_ref, dg_ref):
        g = g_ref[...]
        da = da_ref[...]
        sg = _sigmoid(g)
        do_ref[...] = da * (g * sg)
        dg_ref[...] = (da * o_ref[...] * (sg * (1.0 + g * (1.0 - sg)))).astype(BF16)

    blk = pl.BlockSpec((tm, HP), lambda i: (i, 0))
    return pl.pallas_call(
        body, name=name, grid=(S // tm,),
        out_shape=(jax.ShapeDtypeStruct((S, HP), F32), jax.ShapeDtypeStruct((S, HP), BF16)),
        in_specs=[blk, blk, pl.BlockSpec((tm, HP), lambda i: (i, CB_GA * LANE // HP))], out_specs=(blk, blk), compiler_params=_params(1),
    )(da, o, p)


def _merge_weights(l0, l1, l2):
    mx = jnp.maximum(jnp.maximum(l0, l1), l2)
    e0, e1, e2 = jnp.exp(l0 - mx), jnp.exp(l1 - mx), jnp.exp(l2 - mx)
    den = e0 + e1 + e2
    return e0 / den, e1 / den, e2 / den


def _gate_b(os_, ls_, p, name):
    S = p.shape[0]
    tm = ROW_TILE

    def body(o0, o1, o2, l0, l1, l2, g_ref, b_ref):
        a0, a1, a2 = _merge_weights(l0[...], l1[...], l2[...])
        bm = a0 * o0[...] + a1 * o1[...] + a2 * o2[...]
        g = g_ref[...]
        b_ref[...] = (bm * (g * _sigmoid(g))).astype(BF16)

    blk = pl.BlockSpec((tm, DIL_W), lambda i: (i, 0))
    return pl.pallas_call(
        body, name=name, grid=(S // tm,), out_shape=jax.ShapeDtypeStruct((S, DIL_W), BF16),
        in_specs=[blk] * 6 + [pl.BlockSpec((tm, DIL_W), lambda i: (i, CB_GB * LANE // DIL_W))], out_specs=blk, compiler_params=_params(1),
    )(*os_, *ls_, p)


def _gate_b_bwd(db, os_, ls_, p, name):
    S = p.shape[0]
    tm = ROW_TILE

    def body(db_ref, o0, o1, o2, l0, l1, l2, g_ref, dg_ref, d0, d1, d2, t0, t1, t2):
        a0, a1, a2 = _merge_weights(l0[...], l1[...], l2[...])
        bm = a0 * o0[...] + a1 * o1[...] + a2 * o2[...]
        g = g_ref[...]
        db = db_ref[...]
        sg = _sigmoid(g)
        dbm = db * (g * sg)
        dg_ref[...] = (db * bm * (sg * (1.0 + g * (1.0 - sg)))).astype(BF16)
        prod = dbm * bm
        lo = _lo_lanes()
        parts = []
        for hp in range(DIL_W // LANE):
            pc = prod[:, hp * LANE:(hp + 1) * LANE]
            tl = jnp.sum(jnp.where(lo, pc, 0.0), axis=-1, keepdims=True)
            th = jnp.sum(jnp.where(lo, 0.0, pc), axis=-1, keepdims=True)
            parts.append(jnp.where(lo, tl, th))
        t = jnp.concatenate(parts, axis=1)
        d0[...] = a0 * dbm
        d1[...] = a1 * dbm
        d2[...] = a2 * dbm
        t0[...] = a0 * t
        t1[...] = a1 * t
        t2[...] = a2 * t

    blk = pl.BlockSpec((tm, DIL_W), lambda i: (i, 0))
    f = jax.ShapeDtypeStruct((S, DIL_W), F32)
    outs = pl.pallas_call(
        body, name=name, grid=(S // tm,),
        out_shape=(jax.ShapeDtypeStruct((S, DIL_W), BF16), f, f, f, f, f, f),
        in_specs=[blk] * 7 + [pl.BlockSpec((tm, DIL_W), lambda i: (i, CB_GB * LANE // DIL_W))], out_specs=(blk,) * 7, compiler_params=_params(1),
    )(db, *os_, *ls_, p)
    return outs[0], outs[1:4], outs[4:7]


def _loss_head(x, target, g, name):
    S, D = x.shape
    tm = ROW_TILE

    def body(x_ref, t_ref, g_ref, dx_ref, dg_ref, loss_ref):
        xv = x_ref[...]
        gv = g_ref[...]
        r = lax.rsqrt(jnp.mean(xv * xv, axis=-1, keepdims=True) + EPS)
        xr = xv * r
        err = xr * gv - t_ref[...]
        lpart = 0.5 * jnp.sum(jnp.mean(err * err, axis=-1, keepdims=True), axis=0, keepdims=True)
        dy = err / D
        dyg = dy * gv
        c = jnp.mean(dyg * xv, axis=-1, keepdims=True)
        dx_ref[...] = r * dyg - xv * (r * r * r) * c
        gpart = jnp.sum(dy * xr, axis=0, keepdims=True)

        @pl.when(pl.program_id(0) == 0)
        def _():
            dg_ref[...] = gpart
            loss_ref[...] = jnp.broadcast_to(lpart, loss_ref.shape)

        @pl.when(pl.program_id(0) > 0)
        def _():
            dg_ref[...] += gpart
            loss_ref[...] += jnp.broadcast_to(lpart, loss_ref.shape)

    row = pl.BlockSpec((tm, D), lambda i: (i, 0))
    vec = pl.BlockSpec((1, D), lambda i: (0, 0))
    return pl.pallas_call(
        body, name=name, grid=(S // tm,),
        out_shape=(jax.ShapeDtypeStruct((S, D), F32), jax.ShapeDtypeStruct((1, D), F32), jax.ShapeDtypeStruct((1, D), F32)),
        in_specs=[row, row, vec], out_specs=(row, vec, vec), compiler_params=_params(1),
    )(x, target, g.reshape(1, D))


def _adamw(parts, w, m, v, name):
    R, C = w.shape
    tr = _pick(R, (128, 64, 32, 16, 8))

    def body(p_ref, w_ref, m_ref, v_ref, g_ref, d_ref, nm_ref, nv_ref):
        g = p_ref[0].astype(F32)
        for k in range(1, N_DEV):
            g = g + p_ref[k].astype(F32)
        m2 = ADAM_B1 * m_ref[...] + (1.0 - ADAM_B1) * g
        v2 = ADAM_B2 * v_ref[...] + (1.0 - ADAM_B2) * (g * g)
        m_hat = m2 / (1.0 - ADAM_B1 ** ADAM_STEP)
        v_hat = v2 / (1.0 - ADAM_B2 ** ADAM_STEP)
        g_ref[...] = g
        d_ref[...] = -ADAM_LR * (m_hat / (jnp.sqrt(v_hat) + ADAM_EPS) + ADAM_WD * w_ref[...])
        nm_ref[...] = m2
        nv_ref[...] = v2

    blk = pl.BlockSpec((tr, C), lambda i: (i, 0))
    f = jax.ShapeDtypeStruct((R, C), F32)
    return pl.pallas_call(
        body, name=name, grid=(R // tr,), out_shape=(f, f, f, f),
        in_specs=[pl.BlockSpec((N_DEV, tr, C), lambda i: (0, i, 0)), blk, blk, blk], out_specs=(blk,) * 4,
        compiler_params=_params(1),
    )(parts, w, m, v)


def _exchange(srcs, same_src, name):
    n = len(srcs)
    shapes = [tuple(s.shape) if same_src else tuple(s.shape[1:]) for s in srcs]

    def body(*refs):
        src_refs, out_refs = refs[:n], refs[n:2 * n]
        send_sems, recv_sems, local_sems = refs[2 * n:]
        x, y, c = lax.axis_index("x"), lax.axis_index("y"), lax.axis_index("c")
        me = 4 * x + 2 * y + c

        def block(t, j):
            return src_refs[t] if same_src else src_refs[t].at[j]

        local = [pltpu.make_async_copy(block(t, me), out_refs[t].at[me], local_sems.at[t]) for t in range(n)]
        for cp in local:
            cp.start()
        copies = []
        for k in range(1, N_DEV):
            px = 1 - x if (k >> 2) & 1 else x
            py = 1 - y if (k >> 1) & 1 else y
            pc = 1 - c if k & 1 else c
            for t in range(n):
                cp = pltpu.make_async_remote_copy(
                    src_ref=block(t, 4 * px + 2 * py + pc), dst_ref=out_refs[t].at[me],
                    send_sem=send_sems.at[(k - 1) * n + t], recv_sem=recv_sems.at[(k - 1) * n + t],
                    device_id=(px, py, pc), device_id_type=pl.DeviceIdType.MESH)
                cp.start()
                copies.append(cp)
        for cp in copies:
            cp.wait()
        for cp in local:
            cp.wait()

    hbm = pl.BlockSpec(memory_space=pltpu.HBM)
    outs = pl.pallas_call(
        body, name=name, out_shape=tuple(jax.ShapeDtypeStruct((N_DEV,) + shp, s.dtype) for shp, s in zip(shapes, srcs)),
        in_specs=[hbm] * n, out_specs=(hbm,) * n,
        scratch_shapes=[pltpu.SemaphoreType.DMA(((N_DEV - 1) * n,)), pltpu.SemaphoreType.DMA(((N_DEV - 1) * n,)), pltpu.SemaphoreType.DMA((n,))],
    )(*srcs)
    return list(outs)


def _full_weights(g_in, g_uq, g_ukv, g_out):
    cat_cols = lambda t: jnp.moveaxis(t, 0, 2).reshape(t.shape[1], t.shape[2], -1)
    return cat_cols(g_in), cat_cols(g_uq), cat_cols(g_ukv), jnp.moveaxis(g_out, 0, 1).reshape(DEPTH, D_MODEL, D_MODEL)


def _grad_blocks(g_in, g_uq, g_ukv, g_out):
    split_cols = lambda t: jnp.moveaxis(t.astype(BF16).reshape(t.shape[0], t.shape[1], N_DEV, -1), 2, 0)
    rows = jnp.moveaxis(g_out.astype(BF16).reshape(DEPTH, N_DEV, D_MODEL // N_DEV, D_MODEL), 1, 0)
    return [split_cols(g_in), split_cols(g_uq), split_cols(g_ukv), rows]


def _layer_weights(w_in, w_uq, w_ukv, w_out):
    z = lambda r, n: jnp.zeros((r, n), BF16)
    c_q, c_kv, k_r = w_in[:, 0:384], w_in[:, 384:640], w_in[:, 640:672]
    gate_a, dil, gate_b = w_in[:, 672:1184], w_in[:, 1184:5792], w_in[:, 5792:6304]
    ga_pad = jnp.pad(gate_a.reshape(D_MODEL, HEADS, VDIM), ((0, 0), (0, 0), (0, LANE - VDIM))).reshape(D_MODEL, HP)
    w_p = jnp.concatenate([ga_pad, dil, gate_b, c_kv, z(D_MODEL, 64), k_r, z(D_MODEL, 32), c_q], axis=1)
    uq = jnp.pad(w_uq.reshape(Q_LORA, HEADS, NOPE + ROPE), ((0, 0), (0, 0), (0, LANE - NOPE - ROPE))).reshape(Q_LORA, HP)
    ukv = w_ukv.reshape(KV_LORA, HEADS, NOPE + VDIM)
    pad64 = lambda t: jnp.pad(t, ((0, 0), (0, 0), (0, LANE - 64))).reshape(KV_LORA, HP)
    uk, uv = pad64(ukv[..., :NOPE]), pad64(ukv[..., NOPE:])
    wa = jnp.pad(w_out[:HEADS * VDIM].reshape(HEADS, VDIM, D_MODEL), ((0, 0), (0, LANE - VDIM), (0, 0))).reshape(HP, D_MODEL)
    wb = w_out[HEADS * VDIM:]
    return dict(p=w_p, uq=uq, uk=uk, uv=uv, wa=wa, wb=wb)


def _unpad_grads(g):
    gp = g["p"]
    seg = lambda cb, n: gp[:, cb * LANE:cb * LANE + n]
    ga = seg(CB_GA, HP).reshape(D_MODEL, HEADS, LANE)[..., :VDIM].reshape(D_MODEL, HEADS * VDIM)
    k_r = gp[:, CB_KR * LANE + NOPE:CB_KR * LANE + NOPE + ROPE]
    g_in = jnp.concatenate([seg(CB_CQ, Q_LORA), seg(CB_CKV, KV_LORA), k_r, ga, seg(CB_DIL, 9 * DIL_W), seg(CB_GB, DIL_W)], axis=1)
    g_uq = g["uq"].reshape(Q_LORA, HEADS, LANE)[..., :NOPE + ROPE].reshape(Q_LORA, -1)
    uk = g["uk"].reshape(KV_LORA, HEADS, LANE)[..., :NOPE]
    uv = g["uv"].reshape(KV_LORA, HEADS, LANE)[..., :VDIM]
    g_ukv = jnp.concatenate([uk, uv], axis=-1).reshape(KV_LORA, -1)
    wa = g["wa"].reshape(HEADS, LANE, D_MODEL)[:, :VDIM].reshape(HEADS * VDIM, D_MODEL)
    g_out = jnp.concatenate([wa, g["wb"]], axis=0)
    return g_in, g_uq, g_ukv, g_out


def _layer_fwd(x, w, norm_g, q_norm_g, kv_norm_g, mla, dil, l):
    n = lambda s: f"l{l}_{s}"
    h = _rms_fwd(x, 0, D_MODEL, norm_g, n("norm"))
    p = _mm(h, w["p"], "nn", n("in_proj"))
    cqn = _rms_fwd(p, CB_CQ * LANE // Q_LORA, Q_LORA, q_norm_g, n("q_norm"))
    ckvn = _rms_fwd(p, CB_CKV * LANE // KV_LORA, KV_LORA, kv_norm_g, n("kv_norm"))
    qp = _mm(cqn, w["uq"], "nn", n("q_up"))
    kpre = _mm(ckvn, w["uk"], "nn", n("k_up"))
    v = _mm(ckvn, w["uv"], "nn", n("v_up"), out_dtype=BF16)
    q = _rope_heads(qp, mla["c_q"], mla["s1"], mla["s2"], n("q_rope"), BF16)
    k = _k_assemble(kpre, p, mla, n("k_asm"))
    o, lse = _flash_fwd(q, k, v, n("mla_fwd"))
    a = _gate_a(o, p, n("gate_a"))
    dilr, os_, ls_ = [], [], []
    for g, (_, d) in enumerate(DIL_PAIRS):
        dilr.append(_dil_prep(p, dil, g, n(f"dil_prep{g}")))
        og, lg = _band_fwd(dilr[g], d, n(f"band{g}_fwd"))
        os_.append(og)
        ls_.append(lg)
    b = _gate_b(os_, ls_, p, n("gate_b"))
    x1 = _mm(a, w["wa"], "nn", n("out_a"), res=x)
    x2 = _mm(b, w["wb"], "nn", n("out_b"), res=x1)
    saved = dict(x=x, h=h, p=p, cqn=cqn, ckvn=ckvn, q=q, k=k, v=v, o=o, lse=lse, a=a, dilr=dilr, os=os_, ls=ls_, b=b)
    return x2, saved


def _layer_bwd(dx, sv, w, norm_g, q_norm_g, kv_norm_g, mla, dil, l):
    n = lambda s: f"l{l}_{s}"
    g = {}
    da = _mm(dx, w["wa"], "nt", n("d_a"))
    db = _mm(dx, w["wb"], "nt", n("d_b"))
    g["wa"] = _mm(sv["a"], dx, "tn", n("dw_a"))
    g["wb"] = _mm(sv["b"], dx, "tn", n("dw_b"))
    do, dga = _gate_a_bwd(da, sv["o"], sv["p"], n("gate_a_bwd"))
    dgb, dos, dts = _gate_b_bwd(db, sv["os"], sv["ls"], sv["p"], n("gate_b_bwd"))
    ddil = []
    for gi, (_, d) in enumerate(DIL_PAIRS):
        dq = _band_bwd_q(sv["dilr"][gi], d, dos[gi], sv["ls"][gi], dts[gi], dil, n(f"band{gi}_bwd_q"))
        dk, dv = _band_bwd_kv(sv["dilr"][gi], d, dos[gi], sv["ls"][gi], dts[gi], dil, n(f"band{gi}_bwd_kv"))
        ddil += [dq, dk, dv]
    dq, dk, dv = _flash_bwd(sv["q"], sv["k"], sv["v"], sv["o"], do, sv["lse"], n("mla_bwd"))
    dqp = _rope_heads(dq, mla["c_q"], -mla["s1"], -mla["s2"], n("q_rope_bwd"), BF16)
    dkr = _kr_bwd(dk, mla, n("kr_bwd"))
    g["uq"] = _mm(sv["cqn"], dqp, "tn", n("dw_uq"))
    g["uk"] = _mm(sv["ckvn"], dk, "tn", n("dw_uk"))
    g["uv"] = _mm(sv["ckvn"], dv, "tn", n("dw_uv"))
    dcqn = _mm(dqp, w["uq"], "nt", n("d_cqn"))
    dckvn = _mm(dk, w["uk"], "nt", n("d_ckvn_k"))
    dckvn = _mm(dv, w["uv"], "nt", n("d_ckvn_v"), res=dckvn)
    dcq, g_qn = _rms_bwd(sv["p"], CB_CQ * LANE // Q_LORA, Q_LORA, dcqn, q_norm_g, n("q_norm_bwd"), BF16)
    dckv, g_kvn = _rms_bwd(sv["p"], CB_CKV * LANE // KV_LORA, KV_LORA, dckvn, kv_norm_g, n("kv_norm_bwd"), BF16)
    dp = jnp.concatenate([dga] + ddil + [dgb, dckv, dkr, dcq], axis=1)
    g["p"] = _mm(sv["h"], dp, "tn", n("dw_in"))
    dh = _mm(dp, w["p"], "nt", n("d_h"))
    dx_in, g_n = _rms_bwd(sv["x"], 0, D_MODEL, dh, norm_g, n("norm_bwd"), F32, res=dx)
    return dx_in, g, g_n, g_qn, g_kvn


_SMALL_ROWS = 16


def _pack_small(norm, qn, kvn, fin, loss_row=None):
    padc = lambda t: jnp.pad(t, ((0, 0), (0, D_MODEL - t.shape[1])))
    extra = jnp.zeros((1, D_MODEL), F32) if loss_row is None else loss_row
    return jnp.concatenate([norm, padc(qn), padc(kvn), fin.reshape(1, D_MODEL), extra, jnp.zeros((2, D_MODEL), F32)], axis=0)


def _unpack_small(p):
    return (p[0:4], p[4:8, :Q_LORA], p[8:12, :KV_LORA], p[12]), p[13, 0]


def kernel(x, norm_g, w_in, q_norm_g, kv_norm_g, w_uq, w_ukv, w_out, final_g, loss_target, m_norm_g, m_w_in, m_q_norm_g, m_kv_norm_g, m_w_uq, m_w_ukv, m_w_out, m_final_g, v_norm_g, v_w_in, v_q_norm_g, v_kv_norm_g, v_w_uq, v_w_ukv, v_w_out, v_final_g):
    S = x.shape[1]
    xs = x.reshape(S, D_MODEL)
    target = loss_target.reshape(S, D_MODEL)

    gathered = _exchange([t.astype(BF16) for t in (w_in, w_uq, w_ukv, w_out)], True, "gather_weights")
    full = _full_weights(*gathered)
    ws = [_layer_weights(*(t[l] for t in full)) for l in range(DEPTH)]
    mla, dil = _rope_tables(S)

    saved = []
    h = xs
    for l in range(DEPTH):
        h, sv = _layer_fwd(h, ws[l], norm_g[l], q_norm_g[l], kv_norm_g[l], mla, dil, l)
        saved.append(sv)
    dx, g_final, loss_row = _loss_head(h, target, final_g, "loss_head")
    g_layers, g_norm, g_qn, g_kvn = [None] * DEPTH, [None] * DEPTH, [None] * DEPTH, [None] * DEPTH
    for l in reversed(range(DEPTH)):
        dx, g, g_norm[l], g_qn[l], g_kvn[l] = _layer_bwd(dx, saved[l], ws[l], norm_g[l], q_norm_g[l], kv_norm_g[l], mla, dil, l)
        g_layers[l] = _unpad_grads(g)

    g_full = [jnp.stack([g_layers[l][i] for l in range(DEPTH)]) for i in range(4)]
    parts = _exchange(_grad_blocks(*g_full), False, "exchange_grads")
    sh = []
    for t, (pt, w, m, v) in enumerate(zip(parts, (w_in, w_uq, w_ukv, w_out), (m_w_in, m_w_uq, m_w_ukv, m_w_out), (v_w_in, v_w_uq, v_w_ukv, v_w_out))):
        two = lambda a: a.reshape(-1, a.shape[-1])
        outs = _adamw(pt.reshape(N_DEV, -1, pt.shape[-1]), two(w), two(m), two(v), f"adamw_{t}")
        sh.append([o.reshape(w.shape) for o in outs])

    small = _pack_small(jnp.concatenate(g_norm, 0), jnp.concatenate(g_qn, 0), jnp.concatenate(g_kvn, 0), g_final, loss_row)
    (small_parts,) = _exchange([small], True, "gather_small")
    souts = _adamw(small_parts, _pack_small(norm_g, q_norm_g, kv_norm_g, final_g), _pack_small(m_norm_g, m_q_norm_g, m_kv_norm_g, m_final_g),
                   _pack_small(v_norm_g, v_q_norm_g, v_kv_norm_g, v_final_g), "adamw_small")
    (g_sm, loss), (d_sm, _), (m_sm, _), (v_sm, _) = (_unpack_small(t) for t in souts)

    def order(sm, k):
        return (sm[0], sh[0][k], sm[1], sm[2], sh[1][k], sh[2][k], sh[3][k], sm[3])

    return (loss, dx.reshape(1, S, D_MODEL), *order(g_sm, 0), *order(d_sm, 1), *order(m_sm, 2), *order(v_sm, 3))
```

```python
import math

import jax
import jax.numpy as jnp
from jax import lax
from jax.experimental import pallas as pl
from jax.experimental.pallas import tpu as pltpu

F32 = jnp.float32
BF16 = jnp.bfloat16

D_MODEL = 1024
DEPTH = 4
HEADS = 8
NOPE = 64
ROPE = 32
VDIM = 64
Q_LORA = 384
KV_LORA = 256
DIL_PAIRS = ((128, 1), (512, 4), (2048, 16))
DIL_HD = 64
DIL_W = 512
ROT = 16
HALF = 64
THETA = 500000.0
EPS = 1e-6
IN_WIDTH = 6304
N_DEV = 8

LANE = 128
CB_GA, CB_DIL, CB_GB, CB_CKV, CB_KR, CB_CQ = 0, 8, 44, 48, 50, 51
NP = 54 * LANE
HP = HEADS * LANE

ADAM_LR = 0.001
ADAM_B1 = 0.9
ADAM_B2 = 0.999
ADAM_EPS = 1e-08
ADAM_WD = 0.01
ADAM_STEP = 10

VMEM_LIMIT = 48 * 1024 * 1024
ROW_TILE = 512
SUB = 128

_NT = (((1,), (1,)), ((), ()))
_NN = (((1,), (0,)), ((), ()))
_TN = (((0,), (0,)), ((), ()))


def _params(n_axes):
    return pltpu.CompilerParams(dimension_semantics=("arbitrary",) * n_axes, vmem_limit_bytes=VMEM_LIMIT)


def _pick(n, cands):
    for c in cands:
        if n % c == 0:
            return c
    raise ValueError(f"no tile for {n}")


def _mm(a, b, mode, name, out_dtype=F32, res=None):
    if mode == "nn":
        (M, K), (K2, N) = a.shape, b.shape
    elif mode == "nt":
        (M, K), (N, K2) = a.shape, b.shape
    else:
        (K, M), (K2, N) = a.shape, b.shape
    assert K == K2, (a.shape, b.shape, mode)
    tm = _pick(M, (1024, 512, 384, 256, 128))
    tn = _pick(N, (1152, 1024, 768, 640, 512, 384, 256, 128))
    tk = _pick(K, (1152, 1024, 768, 640, 512, 384, 256, 128))
    nk = K // tk
    dims = {"nn": _NN, "nt": _NT, "tn": _TN}[mode]

    def body(*refs):
        if res is not None:
            a_ref, b_ref, r_ref, o_ref = refs[:4]
        else:
            a_ref, b_ref, o_ref = refs[:3]
            r_ref = None
        part = lax.dot_general(a_ref[...].astype(BF16), b_ref[...].astype(BF16), dims, preferred_element_type=F32)

        def finish(acc):
            if r_ref is not None:
                acc = acc + r_ref[...]
            o_ref[...] = acc.astype(out_dtype)

        if nk == 1:
            finish(part)
        else:
            acc_ref = refs[-1]
            k = pl.program_id(2)

            @pl.when(k == 0)
            def _():
                acc_ref[...] = part

            @pl.when(k > 0)
            def _():
                acc_ref[...] += part

            @pl.when(k == nk - 1)
            def _():
                finish(acc_ref[...])

    if mode == "nn":
        a_spec = pl.BlockSpec((tm, tk), lambda i, j, k: (i, k))
        b_spec = pl.BlockSpec((tk, tn), lambda i, j, k: (k, j))
    elif mode == "nt":
        a_spec = pl.BlockSpec((tm, tk), lambda i, j, k: (i, k))
        b_spec = pl.BlockSpec((tn, tk), lambda i, j, k: (j, k))
    else:
        a_spec = pl.BlockSpec((tk, tm), lambda i, j, k: (k, i))
        b_spec = pl.BlockSpec((tk, tn), lambda i, j, k: (k, j))
    o_spec = pl.BlockSpec((tm, tn), lambda i, j, k: (i, j))
    in_specs = [a_spec, b_spec] + ([o_spec] if res is not None else [])
    args = (a, b) + ((res,) if res is not None else ())
    return pl.pallas_call(
        body, name=name, grid=(M // tm, N // tn, nk), out_shape=jax.ShapeDtypeStruct((M, N), out_dtype),
        in_specs=in_specs, out_specs=o_spec,
        scratch_shapes=[pltpu.VMEM((tm, tn), F32)] if nk > 1 else [],
        compiler_params=_params(3),
    )(*args)


def _rms_fwd(src, cb, width, g, name):
    S = src.shape[0]
    tm = ROW_TILE

    def body(x_ref, g_ref, o_ref):
        x = x_ref[...]
        r = lax.rsqrt(jnp.mean(x * x, axis=-1, keepdims=True) + EPS)
        o_ref[...] = (x * r * g_ref[...]).astype(BF16)

    return pl.pallas_call(
        body, name=name, grid=(S // tm,), out_shape=jax.ShapeDtypeStruct((S, width), BF16),
        in_specs=[pl.BlockSpec((tm, width), lambda i: (i, cb)), pl.BlockSpec((1, width), lambda i: (0, 0))],
        out_specs=pl.BlockSpec((tm, width), lambda i: (i, 0)), compiler_params=_params(1),
    )(src, g.reshape(1, width))


def _rms_bwd(src, cb, width, dy, g, name, out_dtype, res=None):
    S = src.shape[0]
    tm = ROW_TILE

    def body(*refs):
        if res is not None:
            x_ref, dy_ref, g_ref, r_ref, dx_ref, dg_ref = refs
        else:
            x_ref, dy_ref, g_ref, dx_ref, dg_ref = refs
            r_ref = None
        x = x_ref[...]
        dy = dy_ref[...]
        r = lax.rsqrt(jnp.mean(x * x, axis=-1, keepdims=True) + EPS)
        dyg = dy * g_ref[...]
        c = jnp.mean(dyg * x, axis=-1, keepdims=True)
        dx = r * dyg - x * (r * r * r) * c
        if r_ref is not None:
            dx = dx + r_ref[...]
        dx_ref[...] = dx.astype(out_dtype)
        part = jnp.sum(dy * x * r, axis=0, keepdims=True)

        @pl.when(pl.program_id(0) == 0)
        def _():
            dg_ref[...] = part

        @pl.when(pl.program_id(0) > 0)
        def _():
            dg_ref[...] += part

    row = pl.BlockSpec((tm, width), lambda i: (i, 0))
    in_specs = [pl.BlockSpec((tm, width), lambda i: (i, cb)), row, pl.BlockSpec((1, width), lambda i: (0, 0))]
    args = [src, dy, g.reshape(1, width)]
    if res is not None:
        in_specs.append(row)
        args.append(res)
    return pl.pallas_call(
        body, name=name, grid=(S // tm,),
        out_shape=(jax.ShapeDtypeStruct((S, width), out_dtype), jax.ShapeDtypeStruct((1, width), F32)),
        in_specs=in_specs, out_specs=(row, pl.BlockSpec((1, width), lambda i: (0, 0))),
        compiler_params=_params(1),
    )(*args)


def _rot(x, c, s1, s2, h):
    return x * c + pltpu.roll(x, x.shape[1] - h, 1) * s1 + pltpu.roll(x, h, 1) * s2


def _rope_tables(S):
    def tables(dim):
        inv = 1.0 / (THETA ** (jnp.arange(0, dim, 2, dtype=F32) / dim))
        ang = jnp.arange(S, dtype=F32)[:, None] * inv[None, :]
        return jnp.cos(ang), jnp.sin(ang)

    cm, sm = tables(ROPE)
    cd, sd = tables(ROT)
    z = lambda n: jnp.zeros((S, n), F32)
    o = lambda n: jnp.ones((S, n), F32)
    mla = dict(
        c_q=jnp.concatenate([o(64), cm, cm, z(32)], 1),
        c_kr=jnp.concatenate([z(64), cm, cm, z(32)], 1),
        s1=jnp.concatenate([z(64), -sm, z(16), z(32)], 1),
        s2=jnp.concatenate([z(64), z(16), sm, z(32)], 1),
    )
    one = lambda a, b, c: jnp.concatenate([a, b, c, a, b, c], 1)
    dil = dict(c=one(cd, cd, o(48)), s1=one(-sd, z(8), z(48)), s2=one(z(8), sd, z(48)))
    return mla, dil


def _rope_heads(src, c, s1, s2, name, out_dtype):
    S = src.shape[0]
    tm = ROW_TILE

    def body(x_ref, c_ref, s1_ref, s2_ref, o_ref):
        cv, s1v, s2v = c_ref[...], s1_ref[...], s2_ref[...]
        for h in range(HEADS):
            sl = slice(h * LANE, (h + 1) * LANE)
            o_ref[:, sl] = _rot(x_ref[:, sl], cv, s1v, s2v, ROPE // 2).astype(out_dtype)

    tab = pl.BlockSpec((tm, LANE), lambda i: (i, 0))
    wide = pl.BlockSpec((tm, HP), lambda i: (i, 0))
    return pl.pallas_call(
        body, name=name, grid=(S // tm,), out_shape=jax.ShapeDtypeStruct((S, HP), out_dtype),
        in_specs=[wide, tab, tab, tab], out_specs=wide, compiler_params=_params(1),
    )(src, c, s1, s2)


def _k_assemble(kpre, p, mla, name):
    S = kpre.shape[0]
    tm = ROW_TILE

    def body(k_ref, kr_ref, c_ref, s1_ref, s2_ref, o_ref):
        r = _rot(kr_ref[...], c_ref[...], s1_ref[...], s2_ref[...], ROPE // 2)
        for h in range(HEADS):
            sl = slice(h * LANE, (h + 1) * LANE)
            o_ref[:, sl] = (k_ref[:, sl] + r).astype(BF16)

    tab = pl.BlockSpec((tm, LANE), lambda i: (i, 0))
    wide = pl.BlockSpec((tm, HP), lambda i: (i, 0))
    return pl.pallas_call(
        body, name=name, grid=(S // tm,), out_shape=jax.ShapeDtypeStruct((S, HP), BF16),
        in_specs=[wide, pl.BlockSpec((tm, LANE), lambda i: (i, CB_KR)), tab, tab, tab],
        out_specs=wide, compiler_params=_params(1),
    )(kpre, p, mla["c_kr"], mla["s1"], mla["s2"])


def _kr_bwd(dk, mla, name):
    S = dk.shape[0]
    tm = ROW_TILE

    def body(dk_ref, c_ref, s1_ref, s2_ref, o_ref):
        t = dk_ref[:, 0:LANE]
        for h in range(1, HEADS):
            t = t + dk_ref[:, h * LANE:(h + 1) * LANE]
        lane = lax.broadcasted_iota(jnp.int32, (1, LANE), 1)
        t = jnp.where((lane >= NOPE) & (lane < NOPE + ROPE), t, 0.0)
        o_ref[...] = _rot(t, c_ref[...], -s1_ref[...], -s2_ref[...], ROPE // 2).astype(BF16)

    tab = pl.BlockSpec((tm, LANE), lambda i: (i, 0))
    return pl.pallas_call(
        body, name=name, grid=(S // tm,), out_shape=jax.ShapeDtypeStruct((S, LANE), BF16),
        in_specs=[pl.BlockSpec((tm, HP), lambda i: (i, 0)), tab, tab, tab],
        out_specs=tab, compiler_params=_params(1),
    )(dk, mla["c_kr"], mla["s1"], mla["s2"])


def _dil_prep(p, dil, g, name):
    S = p.shape[0]
    tm = ROW_TILE
    first = CB_DIL * LANE // DIL_W + 3 * g

    def body(x_ref, c_ref, s1_ref, s2_ref, o_ref):
        t = pl.program_id(1)
        rep = DIL_W // LANE

        def roped():
            return _rot(x_ref[...], jnp.tile(c_ref[...], (1, rep)), jnp.tile(s1_ref[...], (1, rep)),
                        jnp.tile(s2_ref[...], (1, rep)), ROT // 2)

        @pl.when(t == 0)
        def _():
            o_ref[...] = (roped() * (DIL_HD ** -0.5)).astype(BF16)

        @pl.when(t == 1)
        def _():
            o_ref[...] = roped().astype(BF16)

        @pl.when(t == 2)
        def _():
            o_ref[...] = x_ref[...].astype(BF16)

    tab = pl.BlockSpec((tm, LANE), lambda i, j: (i, 0))
    return pl.pallas_call(
        body, name=name, grid=(S // tm, 3), out_shape=jax.ShapeDtypeStruct((S, 3 * DIL_W), BF16),
        in_specs=[pl.BlockSpec((tm, DIL_W), lambda i, j: (i, first + j)), tab, tab, tab],
        out_specs=pl.BlockSpec((tm, DIL_W), lambda i, j: (i, j)), compiler_params=_params(2),
    )(p, dil["c"], dil["s1"], dil["s2"])


def _flash_fwd(q, k, v, name):
    S = q.shape[0]
    tq = tk = _pick(S, (1024, 512))
    nk = S // tk
    c2 =(NOPE + ROPE) ** -0.5 * math.log2(math.e)

    def body(q_ref, k_ref, v_ref, o_ref, lse_ref, m_s, acc_s):
        j = pl.program_id(2)

        @pl.when(j == 0)
        def _():
            m_s[...] = jnp.full(m_s.shape, -jnp.inf, F32)
            acc_s[...] = jnp.zeros(acc_s.shape, F32)

        lane = lax.broadcasted_iota(jnp.int32, (1, LANE), 1)
        vv = jnp.where(lane == VDIM, jnp.ones((), BF16), v_ref[...])
        t = lax.dot_general(q_ref[...], k_ref[...], _NT, preferred_element_type=F32) * c2
        m_prev = m_s[...]
        m_new = jnp.maximum(m_prev, jnp.max(t, axis=-1, keepdims=True))
        alpha = jnp.exp2(m_prev - m_new)
        e = jnp.exp2(t - jnp.tile(m_new, (1, tk // LANE)))
        acc_s[...] = alpha * acc_s[...] + jnp.dot(e.astype(BF16), vv, preferred_element_type=F32)
        m_s[...] = m_new

        @pl.when(j == nk - 1)
        def _():
            acc = acc_s[...]
            l = acc[:, VDIM:VDIM + 1]
            o_ref[...] = jnp.where(lane < VDIM, acc / l, 0.0)
            lse_ref[...] = (m_s[...] + jnp.log2(l)) * math.log(2.0)

    qs = pl.BlockSpec((tq, LANE), lambda h, i, j: (i, h))
    ks = pl.BlockSpec((tk, LANE), lambda h, i, j: (j, h))
    return pl.pallas_call(
        body, name=name, grid=(HEADS, S // tq, nk),
        out_shape=(jax.ShapeDtypeStruct((S, HP), F32), jax.ShapeDtypeStruct((S, HP), F32)),
        in_specs=[qs, ks, ks], out_specs=(qs, qs),
        scratch_shapes=[pltpu.VMEM((tq, LANE), F32), pltpu.VMEM((tq, LANE), F32)],
        compiler_params=_params(3),
    )(q, k, v)


def _flash_bwd(q, k, v, o, do, lse, name, tq=1024, tk=1024, sub=1024):
    S = q.shape[0]
    nq = S // tq
    scale = (NOPE + ROPE) ** -0.5
    nsub = tq // sub

    def body(q_ref, k_ref, v_ref, o_ref, do_ref, lse_ref, dq_ref, dk_ref, dv_ref, dk_s, dv_s):
        j = pl.program_id(1)
        i = pl.program_id(2)
        kv = k_ref[...]
        vv = v_ref[...]
        dk_part = None
        dv_part = None
        dq_parts = []
        for u in range(nsub):
            rs = slice(u * sub, (u + 1) * sub)
            qv = q_ref[rs, :]
            do = do_ref[rs, :]
            dob = do.astype(BF16)
            s = lax.dot_general(qv, kv, _NT, preferred_element_type=F32) * scale
            p = jnp.exp(s - lse_ref[rs, 0:1])
            dp = lax.dot_general(dob, vv, _NT, preferred_element_type=F32)
            delta = jnp.sum(do * o_ref[rs, :], axis=-1, keepdims=True)
            ds = (p * (dp - delta)).astype(BF16)
            dvp = lax.dot_general(p.astype(BF16), dob, _TN, preferred_element_type=F32)
            dkp = lax.dot_general(ds, qv, _TN, preferred_element_type=F32)
            dv_part = dvp if dv_part is None else dv_part + dvp
            dk_part = dkp if dk_part is None else dk_part + dkp
            dq_parts.append(jnp.dot(ds, kv, preferred_element_type=F32) * scale)
        dq_part = jnp.concatenate(dq_parts, axis=0)

        @pl.when(i == 0)
        def _():
            dk_s[...] = dk_part
            dv_s[...] = dv_part

        @pl.when(i > 0)
        def _():
            dk_s[...] += dk_part
            dv_s[...] += dv_part

        rows = pl.ds(pl.multiple_of(i * tq, tq), tq)

        @pl.when(j == 0)
        def _():
            dq_ref[rows, :] = dq_part

        @pl.when(j > 0)
        def _():
            dq_ref[rows, :] += dq_part

        @pl.when(i == nq - 1)
        def _():
            dk_ref[...] = dk_s[...] * scale
            dv_ref[...] = dv_s[...].astype(BF16)

    qs = pl.BlockSpec((tq, LANE), lambda h, j, i: (i, h))
    ks = pl.BlockSpec((tk, LANE), lambda h, j, i: (j, h))
    return pl.pallas_call(
        body, name=name, grid=(HEADS, S // tk, nq),
        out_shape=(jax.ShapeDtypeStruct((S, HP), F32), jax.ShapeDtypeStruct((S, HP), F32), jax.ShapeDtypeStruct((S, HP), BF16)),
        in_specs=[qs, ks, ks, qs, qs, qs],
        out_specs=(pl.BlockSpec((S, LANE), lambda h, j, i: (0, h)), ks, ks),
        scratch_shapes=[pltpu.VMEM((tk, LANE), F32), pltpu.VMEM((tk, LANE), F32)],
        compiler_params=_params(3),
    )(q, k, v, o, do, lse)


def _band_tiles(L):
    tq = min(512, L)
    return tq, tq // SUB, tq // HALF, L // HALF


def _halo_specs(tq, rpb, n64, col):
    prev = pl.BlockSpec((HALF, DIL_W), lambda r, i: (jnp.maximum(rpb * i - 1, 0), col(r)))
    cur = pl.BlockSpec((tq, DIL_W), lambda r, i: (i, col(r)))
    nxt = pl.BlockSpec((HALF, DIL_W), lambda r, i: (jnp.minimum(rpb * i + rpb, n64 - 1), col(r)))
    return [prev, cur, nxt]


def _fill(buf, prev_ref, cur_ref, next_ref, tq):
    buf[0:HALF, :] = prev_ref[...]
    buf[HALF:HALF + tq, :] = cur_ref[...]
    buf[HALF + tq:HALF + tq + HALF, :] = next_ref[...]


def _lo_lanes():
    return lax.broadcasted_iota(jnp.int32, (1, LANE), 1) < DIL_HD


def _stack_heads(x, lo):
    zero = jnp.zeros_like(x)
    return jnp.concatenate([jnp.where(lo, x, zero), jnp.where(lo, zero, x)], axis=0)


def _stack_cols(x):
    return jnp.concatenate([x[:, 0:1], x[:, DIL_HD:DIL_HD + 1]], axis=0)


def _band_valid(q0, k0, nq, nk, L, bound_q):
    qpos = q0 + lax.broadcasted_iota(jnp.int32, (nq, 1), 0)
    kpos = k0 + lax.broadcasted_iota(jnp.int32, (1, nk), 1)
    side = qpos if bound_q else kpos
    return (jnp.abs(qpos - kpos) <= HALF) & (side >= 0) & (side < L)


def _band_fwd(dilr, d, name):
    S = dilr.shape[0]
    L = S // d
    tq, nsub, rpb, n64 = _band_tiles(L)
    view = dilr.reshape(L, d * dilr.shape[1])
    win = SUB + 2 * HALF

    def body(q_ref, kp_ref, kc_ref, kn_ref, vp_ref, vc_ref, vn_ref, o_ref, lse_ref, kbuf, vbuf):
        i = pl.program_id(1)
        _fill(kbuf, kp_ref, kc_ref, kn_ref, tq)
        _fill(vbuf, vp_ref, vc_ref, vn_ref, tq)
        lo = _lo_lanes()
        for a in range(nsub):
            r0 = a * SUB
            rows = slice(r0, r0 + SUB)
            valid = _band_valid(i * tq + r0, i * tq + r0 - HALF, SUB, win, L, False)
            valid2 = jnp.concatenate([valid, valid], axis=0)
            for hp in range(4):
                cs = slice(hp * LANE, (hp + 1) * LANE)
                q = q_ref[rows, cs]
                kw = kbuf[r0:r0 + win, cs]
                vw = vbuf[r0:r0 + win, cs]
                q2 = _stack_heads(q, lo)
                s = lax.dot_general(q2, kw, _NT, preferred_element_type=F32)
                s = jnp.where(valid2, s, -jnp.inf)
                m = jnp.max(s, axis=-1, keepdims=True)
                e = jnp.exp(s - m)
                l = jnp.sum(e, axis=-1, keepdims=True)
                o2 = jnp.dot(e.astype(BF16), vw, preferred_element_type=F32) / l
                lse2 = m + jnp.log(l)
                o_ref[rows, cs] = jnp.where(lo, o2[:SUB], o2[SUB:])
                lse_ref[rows, cs] = jnp.where(lo, lse2[:SUB], lse2[SUB:])

    out_spec = pl.BlockSpec((tq, DIL_W), lambda r, i: (i, r))
    o, lse = pl.pallas_call(
        body, name=name, grid=(d, L // tq),
        out_shape=(jax.ShapeDtypeStruct((L, d * DIL_W), F32), jax.ShapeDtypeStruct((L, d * DIL_W), F32)),
        in_specs=[pl.BlockSpec((tq, DIL_W), lambda r, i: (i, r * 3))]
        + _halo_specs(tq, rpb, n64, lambda r: r * 3 + 1) + _halo_specs(tq, rpb, n64, lambda r: r * 3 + 2),
        out_specs=(out_spec, out_spec),
        scratch_shapes=[pltpu.VMEM((tq + 2 * HALF, DIL_W), BF16), pltpu.VMEM((tq + 2 * HALF, DIL_W), BF16)],
        compiler_params=_params(2),
    )(view, view, view, view, view, view, view)
    return o.reshape(S, DIL_W), lse.reshape(S, DIL_W)


def _band_bwd_q(dilr, d, do, lse, dlt, dil, name):
    S = dilr.shape[0]
    L = S // d
    tq, nsub, rpb, n64 = _band_tiles(L)
    view = dilr.reshape(L, d * dilr.shape[1])
    v4 = lambda t: t.reshape(L, d * DIL_W)
    tv = lambda t: t.reshape(L, d * LANE)
    win = SUB + 2 * HALF

    def body(q_ref, kp_ref, kc_ref, kn_ref, vp_ref, vc_ref, vn_ref, do_ref, lse_ref, dlt_ref, c_ref, s1_ref, s2_ref, dq_ref,
             kbuf, vbuf):
        i = pl.program_id(1)
        _fill(kbuf, kp_ref, kc_ref, kn_ref, tq)
        _fill(vbuf, vp_ref, vc_ref, vn_ref, tq)
        lo = _lo_lanes()
        for a in range(nsub):
            r0 = a * SUB
            rows = slice(r0, r0 + SUB)
            valid = _band_valid(i * tq + r0, i * tq + r0 - HALF, SUB, win, L, False)
            valid2 = jnp.concatenate([valid, valid], axis=0)
            cv, s1v, s2v = c_ref[rows, :], s1_ref[rows, :], s2_ref[rows, :]
            for hp in range(4):
                cs = slice(hp * LANE, (hp + 1) * LANE)
                kw = kbuf[r0:r0 + win, cs]
                vw = vbuf[r0:r0 + win, cs]
                q2 = _stack_heads(q_ref[rows, cs], lo)
                do2 = _stack_heads(do_ref[rows, cs], lo).astype(BF16)
                s = lax.dot_general(q2, kw, _NT, preferred_element_type=F32)
                p = jnp.where(valid2, jnp.exp(s - _stack_cols(lse_ref[rows, cs])), 0.0)
                dp = lax.dot_general(do2, vw, _NT, preferred_element_type=F32)
                ds = (p * (dp - _stack_cols(dlt_ref[rows, cs]))).astype(BF16)
                dq2 = jnp.dot(ds, kw, preferred_element_type=F32)
                dq = jnp.where(lo, dq2[:SUB], dq2[SUB:])
                dq_ref[rows, cs] = (_rot(dq, cv, -s1v, -s2v, ROT // 2) * (DIL_HD ** -0.5)).astype(BF16)

    row = pl.BlockSpec((tq, DIL_W), lambda r, i: (i, r))
    tab = pl.BlockSpec((tq, LANE), lambda r, i: (i, r))
    dq = pl.pallas_call(
        body, name=name, grid=(d, L // tq), out_shape=jax.ShapeDtypeStruct((L, d * DIL_W), BF16),
        in_specs=[pl.BlockSpec((tq, DIL_W), lambda r, i: (i, r * 3))]
        + _halo_specs(tq, rpb, n64, lambda r: r * 3 + 1) + _halo_specs(tq, rpb, n64, lambda r: r * 3 + 2)
        + [row, row, row, tab, tab, tab],
        out_specs=row,
        scratch_shapes=[pltpu.VMEM((tq + 2 * HALF, DIL_W), BF16), pltpu.VMEM((tq + 2 * HALF, DIL_W), BF16)],
        compiler_params=_params(2),
    )(view, view, view, view, view, view, view, v4(do), v4(lse), v4(dlt), tv(dil["c"]), tv(dil["s1"]), tv(dil["s2"]))
    return dq.reshape(S, DIL_W)


def _band_bwd_kv(dilr, d, do, lse, dlt, dil, name):
    S = dilr.shape[0]
    L = S // d
    tq, nsub, rpb, n64 = _band_tiles(L)
    view = dilr.reshape(L, d * dilr.shape[1])
    v4 = lambda t: t.reshape(L, d * DIL_W)
    tv = lambda t: t.reshape(L, d * LANE)
    win = SUB + 2 * HALF

    def body(k_ref, v_ref, qp_ref, qc_ref, qn_ref, dop_ref, doc_ref, don_ref, lp_ref, lc_ref, ln_ref, tp_ref, tc_ref, tn_ref,
             c_ref, s1_ref, s2_ref, dk_ref, dv_ref, qbuf, dobuf, lbuf, tbuf):
        j = pl.program_id(1)
        _fill(qbuf, qp_ref, qc_ref, qn_ref, tq)
        _fill(dobuf, dop_ref, doc_ref, don_ref, tq)
        _fill(lbuf, lp_ref, lc_ref, ln_ref, tq)
        _fill(tbuf, tp_ref, tc_ref, tn_ref, tq)
        lo = _lo_lanes()
        for a in range(nsub):
            r0 = a * SUB
            rows = slice(r0, r0 + SUB)
            wrows = slice(r0, r0 + win)
            valid = _band_valid(j * tq + r0 - HALF, j * tq + r0, win, SUB, L, True)
            valid2 = jnp.concatenate([valid, valid], axis=0)
            cv, s1v, s2v = c_ref[rows, :], s1_ref[rows, :], s2_ref[rows, :]
            for hp in range(4):
                cs = slice(hp * LANE, (hp + 1) * LANE)
                k = k_ref[rows, cs]
                v = v_ref[rows, cs]
                q2 = _stack_heads(qbuf[wrows, cs], lo)
                do2 = _stack_heads(dobuf[wrows, cs], lo).astype(BF16)
                s = lax.dot_general(q2, k, _NT, preferred_element_type=F32)
                p = jnp.where(valid2, jnp.exp(s - _stack_cols(lbuf[wrows, cs])), 0.0)
                dv = lax.dot_general(p.astype(BF16), do2, _TN, preferred_element_type=F32)
                dp = lax.dot_general(do2, v, _NT, preferred_element_type=F32)
                ds = (p * (dp - _stack_cols(tbuf[wrows, cs]))).astype(BF16)
                dk = lax.dot_general(ds, q2, _TN, preferred_element_type=F32)
                dk_ref[rows, cs] = _rot(dk, cv, -s1v, -s2v, ROT // 2).astype(BF16)
                dv_ref[rows, cs] = dv.astype(BF16)

    row = pl.BlockSpec((tq, DIL_W), lambda r, i: (i, r))
    tab = pl.BlockSpec((tq, LANE), lambda r, i: (i, r))
    halo = _halo_specs(tq, rpb, n64, lambda r: r)
    hb = tq + 2 * HALF
    dk, dv = pl.pallas_call(
        body, name=name, grid=(d, L // tq),
        out_shape=(jax.ShapeDtypeStruct((L, d * DIL_W), BF16), jax.ShapeDtypeStruct((L, d * DIL_W), BF16)),
        in_specs=[pl.BlockSpec((tq, DIL_W), lambda r, i: (i, r * 3 + 1)), pl.BlockSpec((tq, DIL_W), lambda r, i: (i, r * 3 + 2))]
        + _halo_specs(tq, rpb, n64, lambda r: r * 3) + halo + halo + halo + [tab, tab, tab],
        out_specs=(row, row),
        scratch_shapes=[pltpu.VMEM((hb, DIL_W), BF16), pltpu.VMEM((hb, DIL_W), F32), pltpu.VMEM((hb, DIL_W), F32), pltpu.VMEM((hb, DIL_W), F32)],
        compiler_params=_params(2),
    )(view, view, view, view, view, v4(do), v4(do), v4(do), v4(lse), v4(lse), v4(lse), v4(dlt), v4(dlt), v4(dlt),
      tv(dil["c"]), tv(dil["s1"]), tv(dil["s2"]))
    return dk.reshape(S, DIL_W), dv.reshape(S, DIL_W)


def _sigmoid(x):
    return 1.0 / (1.0 + jnp.exp(-x))


def _gate_a(o, p, name):
    S = o.shape[0]
    tm = ROW_TILE

    def body(o_ref, g_ref, a_ref):
        g = g_ref[...]
        a_ref[...] = (o_ref[...] * (g * _sigmoid(g))).astype(BF16)

    blk = pl.BlockSpec((tm, HP), lambda i: (i, 0))
    return pl.pallas_call(
        body, name=name, grid=(S // tm,), out_shape=jax.ShapeDtypeStruct((S, HP), BF16),
        in_specs=[blk, pl.BlockSpec((tm, HP), lambda i: (i, CB_GA * LANE // HP))], out_specs=blk, compiler_params=_params(1),
    )(o, p)


def _gate_a_bwd(da, o, p, name):
    S = o.shape[0]
    tm = ROW_TILE

    def body(da_ref, o_ref, g_ref, do_ref, dg_ref):
        g = g_ref[...]
        da = da_ref[...]
        sg = _sigmoid(g)
        do_ref[...] = da * (g * sg)
        dg_ref[...] = (da * o_ref[...] * (sg * (1.0 + g * (1.0 - sg)))).astype(BF16)

    blk = pl.BlockSpec((tm, HP), lambda i: (i, 0))
    return pl.pallas_call(
        body, name=name, grid=(S // tm,),
        out_shape=(jax.ShapeDtypeStruct((S, HP), F32), jax.ShapeDtypeStruct((S, HP), BF16)),
        in_specs=[blk, blk, pl.BlockSpec((tm, HP), lambda i: (i, CB_GA * LANE // HP))], out_specs=(blk, blk), compiler_params=_params(1),
    )(da, o, p)


def _merge_weights(l0, l1, l2):
    mx = jnp.maximum(jnp.maximum(l0, l1), l2)
    e0, e1, e2 = jnp.exp(l0 - mx), jnp.exp(l1 - mx), jnp.exp(l2 - mx)
    den = e0 + e1 + e2
    return e0 / den, e1 / den, e2 / den


def _gate_b(os_, ls_, p, name):
    S = p.shape[0]
    tm = ROW_TILE

    def body(o0, o1, o2, l0, l1, l2, g_ref, b_ref):
        a0, a1, a2 = _merge_weights(l0[...], l1[...], l2[...])
        bm = a0 * o0[...] + a1 * o1[...] + a2 * o2[...]
        g = g_ref[...]
        b_ref[...] = (bm * (g * _sigmoid(g))).astype(BF16)

    blk = pl.BlockSpec((tm, DIL_W), lambda i: (i, 0))
    return pl.pallas_call(
        body, name=name, grid=(S // tm,), out_shape=jax.ShapeDtypeStruct((S, DIL_W), BF16),
        in_specs=[blk] * 6 + [pl.BlockSpec((tm, DIL_W), lambda i: (i, CB_GB * LANE // DIL_W))], out_specs=blk, compiler_params=_params(1),
    )(*os_, *ls_, p)


def _gate_b_bwd(db, os_, ls_, p, name):
    S = p.shape[0]
    tm = ROW_TILE

    def body(db_ref, o0, o1, o2, l0, l1, l2, g_ref, dg_ref, d0, d1, d2, t0, t1, t2):
        a0, a1, a2 = _merge_weights(l0[...], l1[...], l2[...])
        bm = a0 * o0[...] + a1 * o1[...] + a2 * o2[...]
        g = g_ref[...]
        db = db_ref[...]
        sg = _sigmoid(g)
        dbm = db * (g * sg)
        dg_ref[...] = (db * bm * (sg * (1.0 + g * (1.0 - sg)))).astype(BF16)
        prod = dbm * bm
        lo = _lo_lanes()
        parts = []
        for hp in range(DIL_W // LANE):
            pc = prod[:, hp * LANE:(hp + 1) * LANE]
            tl = jnp.sum(jnp.where(lo, pc, 0.0), axis=-1, keepdims=True)
            th = jnp.sum(jnp.where(lo, 0.0, pc), axis=-1, keepdims=True)
            parts.append(jnp.where(lo, tl, th))
        t = jnp.concatenate(parts, axis=1)
        d0[...] = a0 * dbm
        d1[...] = a1 * dbm
        d2[...] = a2 * dbm
        t0[...] = a0 * t
        t1[...] = a1 * t
        t2[...] = a2 * t

    blk = pl.BlockSpec((tm, DIL_W), lambda i: (i, 0))
    f = jax.ShapeDtypeStruct((S, DIL_W), F32)
    outs = pl.pallas_call(
        body, name=name, grid=(S // tm,),
        out_shape=(jax.ShapeDtypeStruct((S, DIL_W), BF16), f, f, f, f, f, f),
        in_specs=[blk] * 7 + [pl.BlockSpec((tm, DIL_W), lambda i: (i, CB_GB * LANE // DIL_W))], out_specs=(blk,) * 7, compiler_params=_params(1),
    )(db, *os_, *ls_, p)
    return outs[0], outs[1:4], outs[4:7]


def _loss_head(x, target, g, name):
    S, D = x.shape
    tm = ROW_TILE

    def body(x_ref, t_ref, g_ref, dx_ref, dg_ref, loss_ref):
        xv = x_ref[...]
        gv = g_ref[...]
        r = lax.rsqrt(jnp.mean(xv * xv, axis=-1, keepdims=True) + EPS)
        xr = xv * r
        err = xr * gv - t_ref[...]
        lpart = 0.5 * jnp.sum(jnp.mean(err * err, axis=-1, keepdims=True), axis=0, keepdims=True)
        dy = err / D
        dyg = dy * gv
        c = jnp.mean(dyg * xv, axis=-1, keepdims=True)
        dx_ref[...] = r * dyg - xv * (r * r * r) * c
        gpart = jnp.sum(dy * xr, axis=0, keepdims=True)

        @pl.when(pl.program_id(0) == 0)
        def _():
            dg_ref[...] = gpart
            loss_ref[...] = jnp.broadcast_to(lpart, loss_ref.shape)

        @pl.when(pl.program_id(0) > 0)
        def _():
            dg_ref[...] += gpart
            loss_ref[...] += jnp.broadcast_to(lpart, loss_ref.shape)

    row = pl.BlockSpec((tm, D), lambda i: (i, 0))
    vec = pl.BlockSpec((1, D), lambda i: (0, 0))
    return pl.pallas_call(
        body, name=name, grid=(S // tm,),
        out_shape=(jax.ShapeDtypeStruct((S, D), F32), jax.ShapeDtypeStruct((1, D), F32), jax.ShapeDtypeStruct((1, D), F32)),
        in_specs=[row, row, vec], out_specs=(row, vec, vec), compiler_params=_params(1),
    )(x, target, g.reshape(1, D))


def _adamw(parts, w, m, v, name):
    R, C = w.shape
    tr = _pick(R, (128, 64, 32, 16, 8))

    def body(p_ref, w_ref, m_ref, v_ref, g_ref, d_ref, nm_ref, nv_ref):
        g = p_ref[0].astype(F32)
        for k in range(1, N_DEV):
            g = g + p_ref[k].astype(F32)
        m2 = ADAM_B1 * m_ref[...] + (1.0 - ADAM_B1) * g
        v2 = ADAM_B2 * v_ref[...] + (1.0 - ADAM_B2) * (g * g)
        m_hat = m2 / (1.0 - ADAM_B1 ** ADAM_STEP)
        v_hat = v2 / (1.0 - ADAM_B2 ** ADAM_STEP)
        g_ref[...] = g
        d_ref[...] = -ADAM_LR * (m_hat / (jnp.sqrt(v_hat) + ADAM_EPS) + ADAM_WD * w_ref[...])
        nm_ref[...] = m2
        nv_ref[...] = v2

    blk = pl.BlockSpec((tr, C), lambda i: (i, 0))
    f = jax.ShapeDtypeStruct((R, C), F32)
    return pl.pallas_call(
        body, name=name, grid=(R // tr,), out_shape=(f, f, f, f),
        in_specs=[pl.BlockSpec((N_DEV, tr, C), lambda i: (0, i, 0)), blk, blk, blk], out_specs=(blk,) * 4,
        compiler_params=_params(1),
    )(parts, w, m, v)


def _exchange(srcs, same_src, name):
    n = len(srcs)
    shapes = [tuple(s.shape) if same_src else tuple(s.shape[1:]) for s in srcs]

    def body(*refs):
        src_refs, out_refs = refs[:n], refs[n:2 * n]
        send_sems, recv_sems, local_sems = refs[2 * n:]
        x, y, c = lax.axis_index("x"), lax.axis_index("y"), lax.axis_index("c")
        me = 4 * x + 2 * y + c

        def block(t, j):
            return src_refs[t] if same_src else src_refs[t].at[j]

        local = [pltpu.make_async_copy(block(t, me), out_refs[t].at[me], local_sems.at[t]) for t in range(n)]
        for cp in local:
            cp.start()
        copies = []
        for k in range(1, N_DEV):
            px = 1 - x if (k >> 2) & 1 else x
            py = 1 - y if (k >> 1) & 1 else y
            pc = 1 - c if k & 1 else c
            for t in range(n):
                cp = pltpu.make_async_remote_copy(
                    src_ref=block(t, 4 * px + 2 * py + pc), dst_ref=out_refs[t].at[me],
                    send_sem=send_sems.at[(k - 1) * n + t], recv_sem=recv_sems.at[(k - 1) * n + t],
                    device_id=(px, py, pc), device_id_type=pl.DeviceIdType.MESH)
                cp.start()
                copies.append(cp)
        for cp in copies:
            cp.wait()
        for cp in local:
            cp.wait()

    hbm = pl.BlockSpec(memory_space=pltpu.HBM)
    outs = pl.pallas_call(
        body, name=name, out_shape=tuple(jax.ShapeDtypeStruct((N_DEV,) + shp, s.dtype) for shp, s in zip(shapes, srcs)),
        in_specs=[hbm] * n, out_specs=(hbm,) * n,
        scratch_shapes=[pltpu.SemaphoreType.DMA(((N_DEV - 1) * n,)), pltpu.SemaphoreType.DMA(((N_DEV - 1) * n,)), pltpu.SemaphoreType.DMA((n,))],
    )(*srcs)
    return list(outs)


def _full_weights(g_in, g_uq, g_ukv, g_out):
    cat_cols = lambda t: jnp.moveaxis(t, 0, 2).reshape(t.shape[1], t.shape[2], -1)
    return cat_cols(g_in), cat_cols(g_uq), cat_cols(g_ukv), jnp.moveaxis(g_out, 0, 1).reshape(DEPTH, D_MODEL, D_MODEL)


def _grad_blocks(g_in, g_uq, g_ukv, g_out):
    split_cols = lambda t: jnp.moveaxis(t.astype(BF16).reshape(t.shape[0], t.shape[1], N_DEV, -1), 2, 0)
    rows = jnp.moveaxis(g_out.astype(BF16).reshape(DEPTH, N_DEV, D_MODEL // N_DEV, D_MODEL), 1, 0)
    return [split_cols(g_in), split_cols(g_uq), split_cols(g_ukv), rows]


def _layer_weights(w_in, w_uq, w_ukv, w_out):
    z = lambda r, n: jnp.zeros((r, n), BF16)
    c_q, c_kv, k_r = w_in[:, 0:384], w_in[:, 384:640], w_in[:, 640:672]
    gate_a, dil, gate_b = w_in[:, 672:1184], w_in[:, 1184:5792], w_in[:, 5792:6304]
    ga_pad = jnp.pad(gate_a.reshape(D_MODEL, HEADS, VDIM), ((0, 0), (0, 0), (0, LANE - VDIM))).reshape(D_MODEL, HP)
    w_p = jnp.concatenate([ga_pad, dil, gate_b, c_kv, z(D_MODEL, 64), k_r, z(D_MODEL, 32), c_q], axis=1)
    uq = jnp.pad(w_uq.reshape(Q_LORA, HEADS, NOPE + ROPE), ((0, 0), (0, 0), (0, LANE - NOPE - ROPE))).reshape(Q_LORA, HP)
    ukv = w_ukv.reshape(KV_LORA, HEADS, NOPE + VDIM)
    pad64 = lambda t: jnp.pad(t, ((0, 0), (0, 0), (0, LANE - 64))).reshape(KV_LORA, HP)
    uk, uv = pad64(ukv[..., :NOPE]), pad64(ukv[..., NOPE:])
    wa = jnp.pad(w_out[:HEADS * VDIM].reshape(HEADS, VDIM, D_MODEL), ((0, 0), (0, LANE - VDIM), (0, 0))).reshape(HP, D_MODEL)
    wb = w_out[HEADS * VDIM:]
    return dict(p=w_p, uq=uq, uk=uk, uv=uv, wa=wa, wb=wb)


def _unpad_grads(g):
    gp = g["p"]
    seg = lambda cb, n: gp[:, cb * LANE:cb * LANE + n]
    ga = seg(CB_GA, HP).reshape(D_MODEL, HEADS, LANE)[..., :VDIM].reshape(D_MODEL, HEADS * VDIM)
    k_r = gp[:, CB_KR * LANE + NOPE:CB_KR * LANE + NOPE + ROPE]
    g_in = jnp.concatenate([seg(CB_CQ, Q_LORA), seg(CB_CKV, KV_LORA), k_r, ga, seg(CB_DIL, 9 * DIL_W), seg(CB_GB, DIL_W)], axis=1)
    g_uq = g["uq"].reshape(Q_LORA, HEADS, LANE)[..., :NOPE + ROPE].reshape(Q_LORA, -1)
    uk = g["uk"].reshape(KV_LORA, HEADS, LANE)[..., :NOPE]
    uv = g["uv"].reshape(KV_LORA, HEADS, LANE)[..., :VDIM]
    g_ukv = jnp.concatenate([uk, uv], axis=-1).reshape(KV_LORA, -1)
    wa = g["wa"].reshape(HEADS, LANE, D_MODEL)[:, :VDIM].reshape(HEADS * VDIM, D_MODEL)
    g_out = jnp.concatenate([wa, g["wb"]], axis=0)
    return g_in, g_uq, g_ukv, g_out


def _layer_fwd(x, w, norm_g, q_norm_g, kv_norm_g, mla, dil, l):
    n = lambda s: f"l{l}_{s}"
    h = _rms_fwd(x, 0, D_MODEL, norm_g, n("norm"))
    p = _mm(h, w["p"], "nn", n("in_proj"))
    cqn = _rms_fwd(p, CB_CQ * LANE // Q_LORA, Q_LORA, q_norm_g, n("q_norm"))
    ckvn = _rms_fwd(p, CB_CKV * LANE // KV_LORA, KV_LORA, kv_norm_g, n("kv_norm"))
    qp = _mm(cqn, w["uq"], "nn", n("q_up"))
    kpre = _mm(ckvn, w["uk"], "nn", n("k_up"))
    v = _mm(ckvn, w["uv"], "nn", n("v_up"), out_dtype=BF16)
    q = _rope_heads(qp, mla["c_q"], mla["s1"], mla["s2"], n("q_rope"), BF16)
    k = _k_assemble(kpre, p, mla, n("k_asm"))
    o, lse = _flash_fwd(q, k, v, n("mla_fwd"))
    a = _gate_a(o, p, n("gate_a"))
    dilr, os_, ls_ = [], [], []
    for g, (_, d) in enumerate(DIL_PAIRS):
        dilr.append(_dil_prep(p, dil, g, n(f"dil_prep{g}")))
        og, lg = _band_fwd(dilr[g], d, n(f"band{g}_fwd"))
        os_.append(og)
        ls_.append(lg)
    b = _gate_b(os_, ls_, p, n("gate_b"))
    x1 = _mm(a, w["wa"], "nn", n("out_a"), res=x)
    x2 = _mm(b, w["wb"], "nn", n("out_b"), res=x1)
    saved = dict(x=x, h=h, p=p, cqn=cqn, ckvn=ckvn, q=q, k=k, v=v, o=o, lse=lse, a=a, dilr=dilr, os=os_, ls=ls_, b=b)
    return x2, saved


def _layer_bwd(dx, sv, w, norm_g, q_norm_g, kv_norm_g, mla, dil, l):
    n = lambda s: f"l{l}_{s}"
    g = {}
    da = _mm(dx, w["wa"], "nt", n("d_a"))
    db = _mm(dx, w["wb"], "nt", n("d_b"))
    g["wa"] = _mm(sv["a"], dx, "tn", n("dw_a"))
    g["wb"] = _mm(sv["b"], dx, "tn", n("dw_b"))
    do, dga = _gate_a_bwd(da, sv["o"], sv["p"], n("gate_a_bwd"))
    dgb, dos, dts = _gate_b_bwd(db, sv["os"], sv["ls"], sv["p"], n("gate_b_bwd"))
    ddil = []
    for gi, (_, d) in enumerate(DIL_PAIRS):
        dq = _band_bwd_q(sv["dilr"][gi], d, dos[gi], sv["ls"][gi], dts[gi], dil, n(f"band{gi}_bwd_q"))
        dk, dv = _band_bwd_kv(sv["dilr"][gi], d, dos[gi], sv["ls"][gi], dts[gi], dil, n(f"band{gi}_bwd_kv"))
        ddil += [dq, dk, dv]
    dq, dk, dv = _flash_bwd(sv["q"], sv["k"], sv["v"], sv["o"], do, sv["lse"], n("mla_bwd"))
    dqp = _rope_heads(dq, mla["c_q"], -mla["s1"], -mla["s2"], n("q_rope_bwd"), BF16)
    dkr = _kr_bwd(dk, mla, n("kr_bwd"))
    g["uq"] = _mm(sv["cqn"], dqp, "tn", n("dw_uq"))
    g["uk"] = _mm(sv["ckvn"], dk, "tn", n("dw_uk"))
    g["uv"] = _mm(sv["ckvn"], dv, "tn", n("dw_uv"))
    dcqn = _mm(dqp, w["uq"], "nt", n("d_cqn"))
    dckvn = _mm(dk, w["uk"], "nt", n("d_ckvn_k"))
    dckvn = _mm(dv, w["uv"], "nt", n("d_ckvn_v"), res=dckvn)
    dcq, g_qn = _rms_bwd(sv["p"], CB_CQ * LANE // Q_LORA, Q_LORA, dcqn, q_norm_g, n("q_norm_bwd"), BF16)
    dckv, g_kvn = _rms_bwd(sv["p"], CB_CKV * LANE // KV_LORA, KV_LORA, dckvn, kv_norm_g, n("kv_norm_bwd"), BF16)
    dp = jnp.concatenate([dga] + ddil + [dgb, dckv, dkr, dcq], axis=1)
    g["p"] = _mm(sv["h"], dp, "tn", n("dw_in"))
    dh = _mm(dp, w["p"], "nt", n("d_h"))
    dx_in, g_n = _rms_bwd(sv["x"], 0, D_MODEL, dh, norm_g, n("norm_bwd"), F32, res=dx)
    return dx_in, g, g_n, g_qn, g_kvn


_SMALL_ROWS = 16


def _pack_small(norm, qn, kvn, fin, loss_row=None):
    padc = lambda t: jnp.pad(t, ((0, 0), (0, D_MODEL - t.shape[1])))
    extra = jnp.zeros((1, D_MODEL), F32) if loss_row is None else loss_row
    return jnp.concatenate([norm, padc(qn), padc(kvn), fin.reshape(1, D_MODEL), extra, jnp.zeros((2, D_MODEL), F32)], axis=0)


def _unpack_small(p):
    return (p[0:4], p[4:8, :Q_LORA], p[8:12, :KV_LORA], p[12]), p[13, 0]


def kernel(x, norm_g, w_in, q_norm_g, kv_norm_g, w_uq, w_ukv, w_out, final_g, loss_target, m_norm_g, m_w_in, m_q_norm_g, m_kv_norm_g, m_w_uq, m_w_ukv, m_w_out, m_final_g, v_norm_g, v_w_in, v_q_norm_g, v_kv_norm_g, v_w_uq, v_w_ukv, v_w_out, v_final_g):
    S = x.shape[1]
    xs = x.reshape(S, D_MODEL)
    target = loss_target.reshape(S, D_MODEL)

    gathered = _exchange([t.astype(BF16) for t in (w_in, w_uq, w_ukv, w_out)], True, "gather_weights")
    full = _full_weights(*gathered)
    ws = [_layer_weights(*(t[l] for t in full)) for l in range(DEPTH)]
    mla, dil = _rope_tables(S)

    saved = []
    h = xs
    for l in range(DEPTH):
        h, sv = _layer_fwd(h, ws[l], norm_g[l], q_norm_g[l], kv_norm_g[l], mla, dil, l)
        saved.append(sv)
    dx, g_final, loss_row = _loss_head(h, target, final_g, "loss_head")
    g_layers, g_norm, g_qn, g_kvn = [None] * DEPTH, [None] * DEPTH, [None] * DEPTH, [None] * DEPTH
    for l in reversed(range(DEPTH)):
        dx, g, g_norm[l], g_qn[l], g_kvn[l] = _layer_bwd(dx, saved[l], ws[l], norm_g[l], q_norm_g[l], kv_norm_g[l], mla, dil, l)
        g_layers[l] = _unpad_grads(g)

    g_full = [jnp.stack([g_layers[l][i] for l in range(DEPTH)]) for i in range(4)]
    parts = _exchange(_grad_blocks(*g_full), False, "exchange_grads")
    sh = []
    for t, (pt, w, m, v) in enumerate(zip(parts, (w_in, w_uq, w_ukv, w_out), (m_w_in, m_w_uq, m_w_ukv, m_w_out), (v_w_in, v_w_uq, v_w_ukv, v_w_out))):
        two = lambda a: a.reshape(-1, a.shape[-1])
        outs = _adamw(pt.reshape(N_DEV, -1, pt.shape[-1]), two(w), two(m), two(v), f"adamw_{t}")
        sh.append([o.reshape(w.shape) for o in outs])

    small = _pack_small(jnp.concatenate(g_norm, 0), jnp.concatenate(g_qn, 0), jnp.concatenate(g_kvn, 0), g_final, loss_row)
    (small_parts,) = _exchange([small], True, "gather_small")
    souts = _adamw(small_parts, _pack_small(norm_g, q_norm_g, kv_norm_g, final_g), _pack_small(m_norm_g, m_q_norm_g, m_kv_norm_g, m_final_g),
                   _pack_small(v_norm_g, v_q_norm_g, v_kv_norm_g, v_final_g), "adamw_small")
    (g_sm, loss), (d_sm, _), (m_sm, _), (v_sm, _) = (_unpack_small(t) for t in souts)

    def order(sm, k):
        return (sm[0], sh[0][k], sm[1], sm[2], sh[1][k], sh[2][k], sh[3][k], sm[3])

    return (loss, dx.reshape(1, S, D_MODEL), *order(g_sm, 0), *order(d_sm, 1), *order(m_sm, 2), *order(v_sm, 3))
```

```python
import functools
import math

import jax
import jax.numpy as jnp
from jax import lax
from jax.experimental import pallas as pl
from jax.experimental.pallas import tpu as pltpu

F32 = jnp.float32
BF16 = jnp.bfloat16

D_MODEL = 1024
DEPTH = 4
HEADS = 8
NOPE = 64
ROPE = 32
VDIM = 64
Q_LORA = 384
KV_LORA = 256
DIL_PAIRS = ((128, 1), (512, 4), (2048, 16))
DIL_HD = 64
DIL_W = 512
ROT = 16
HALF = 64
THETA = 500000.0
EPS = 1e-6
IN_WIDTH = 6304
N_DEV = 8

LANE = 128
CB_GA, CB_DIL, CB_GB, CB_CKV, CB_KR, CB_CQ = 0, 8, 44, 48, 50, 51
NP = 54 * LANE
HP = HEADS * LANE

ADAM_LR = 0.001
ADAM_B1 = 0.9
ADAM_B2 = 0.999
ADAM_EPS = 1e-08
ADAM_WD = 0.01
ADAM_STEP = 10

VMEM_LIMIT = 48 * 1024 * 1024
ROW_TILE = 512
SUB = 128

_NT = (((1,), (1,)), ((), ()))
_NN = (((1,), (0,)), ((), ()))
_TN = (((0,), (0,)), ((), ()))


def _params(n_axes):
    return pltpu.CompilerParams(dimension_semantics=("arbitrary",) * n_axes, vmem_limit_bytes=VMEM_LIMIT)


def _pick(n, cands):
    for c in cands:
        if n % c == 0:
            return c
    raise ValueError(f"no tile for {n}")


def _mm(a, b, mode, name, out_dtype=F32, res=None):
    if mode == "nn":
        (M, K), (K2, N) = a.shape, b.shape
    elif mode == "nt":
        (M, K), (N, K2) = a.shape, b.shape
    else:
        (K, M), (K2, N) = a.shape, b.shape
    assert K == K2, (a.shape, b.shape, mode)
    tm = _pick(M, (1024, 512, 384, 256, 128))
    tn = _pick(N, (1152, 1024, 768, 640, 512, 384, 256, 128))
    tk = _pick(K, (1152, 1024, 768, 640, 512, 384, 256, 128))
    nk = K // tk
    dims = {"nn": _NN, "nt": _NT, "tn": _TN}[mode]

    def body(*refs):
        if res is not None:
            a_ref, b_ref, r_ref, o_ref = refs[:4]
        else:
            a_ref, b_ref, o_ref = refs[:3]
            r_ref = None
        part = lax.dot_general(a_ref[...].astype(BF16), b_ref[...].astype(BF16), dims, preferred_element_type=F32)

        def finish(acc):
            if r_ref is not None:
                acc = acc + r_ref[...]
            o_ref[...] = acc.astype(out_dtype)

        if nk == 1:
            finish(part)
        else:
            acc_ref = refs[-1]
            k = pl.program_id(2)

            @pl.when(k == 0)
            def _():
                acc_ref[...] = part

            @pl.when(k > 0)
            def _():
                acc_ref[...] += part

            @pl.when(k == nk - 1)
            def _():
                finish(acc_ref[...])

    if mode == "nn":
        a_spec = pl.BlockSpec((tm, tk), lambda i, j, k: (i, k))
        b_spec = pl.BlockSpec((tk, tn), lambda i, j, k: (k, j))
    elif mode == "nt":
        a_spec = pl.BlockSpec((tm, tk), lambda i, j, k: (i, k))
        b_spec = pl.BlockSpec((tn, tk), lambda i, j, k: (j, k))
    else:
        a_spec = pl.BlockSpec((tk, tm), lambda i, j, k: (k, i))
        b_spec = pl.BlockSpec((tk, tn), lambda i, j, k: (k, j))
    o_spec = pl.BlockSpec((tm, tn), lambda i, j, k: (i, j))
    in_specs = [a_spec, b_spec] + ([o_spec] if res is not None else [])
    args = (a, b) + ((res,) if res is not None else ())
    return pl.pallas_call(
        body, name=name, grid=(M // tm, N // tn, nk), out_shape=jax.ShapeDtypeStruct((M, N), out_dtype),
        in_specs=in_specs, out_specs=o_spec,
        scratch_shapes=[pltpu.VMEM((tm, tn), F32)] if nk > 1 else [],
        compiler_params=_params(3),
    )(*args)


def _rms_fwd(src, cb, width, g, name):
    S = src.shape[0]
    tm = ROW_TILE

    def body(x_ref, g_ref, o_ref):
        x = x_ref[...]
        r = lax.rsqrt(jnp.mean(x * x, axis=-1, keepdims=True) + EPS)
        o_ref[...] = (x * r * g_ref[...]).astype(BF16)

    return pl.pallas_call(
        body, name=name, grid=(S // tm,), out_shape=jax.ShapeDtypeStruct((S, width), BF16),
        in_specs=[pl.BlockSpec((tm, width), lambda i: (i, cb)), pl.BlockSpec((1, width), lambda i: (0, 0))],
        out_specs=pl.BlockSpec((tm, width), lambda i: (i, 0)), compiler_params=_params(1),
    )(src, g.reshape(1, width))


def _rms_bwd(src, cb, width, dy, g, name, out_dtype, res=None):
    S = src.shape[0]
    tm = ROW_TILE

    def body(*refs):
        if res is not None:
            x_ref, dy_ref, g_ref, r_ref, dx_ref, dg_ref = refs
        else:
            x_ref, dy_ref, g_ref, dx_ref, dg_ref = refs
            r_ref = None
        x = x_ref[...]
        dy = dy_ref[...]
        r = lax.rsqrt(jnp.mean(x * x, axis=-1, keepdims=True) + EPS)
        dyg = dy * g_ref[...]
        c = jnp.mean(dyg * x, axis=-1, keepdims=True)
        dx = r * dyg - x * (r * r * r) * c
        if r_ref is not None:
            dx = dx + r_ref[...]
        dx_ref[...] = dx.astype(out_dtype)
        part = jnp.sum(dy * x * r, axis=0, keepdims=True)

        @pl.when(pl.program_id(0) == 0)
        def _():
            dg_ref[...] = part

        @pl.when(pl.program_id(0) > 0)
        def _():
            dg_ref[...] += part

    row = pl.BlockSpec((tm, width), lambda i: (i, 0))
    in_specs = [pl.BlockSpec((tm, width), lambda i: (i, cb)), row, pl.BlockSpec((1, width), lambda i: (0, 0))]
    args = [src, dy, g.reshape(1, width)]
    if res is not None:
        in_specs.append(row)
        args.append(res)
    return pl.pallas_call(
        body, name=name, grid=(S // tm,),
        out_shape=(jax.ShapeDtypeStruct((S, width), out_dtype), jax.ShapeDtypeStruct((1, width), F32)),
        in_specs=in_specs, out_specs=(row, pl.BlockSpec((1, width), lambda i: (0, 0))),
        compiler_params=_params(1),
    )(*args)


def _rot(x, c, s1, s2, h):
    return x * c + pltpu.roll(x, x.shape[1] - h, 1) * s1 + pltpu.roll(x, h, 1) * s2


def _rope_tables(S):
    def tables(dim):
        inv = 1.0 / (THETA ** (jnp.arange(0, dim, 2, dtype=F32) / dim))
        ang = jnp.arange(S, dtype=F32)[:, None] * inv[None, :]
        return jnp.cos(ang), jnp.sin(ang)

    cm, sm = tables(ROPE)
    cd, sd = tables(ROT)
    z = lambda n: jnp.zeros((S, n), F32)
    o = lambda n: jnp.ones((S, n), F32)
    mla = dict(
        c_q=jnp.concatenate([o(64), cm, cm, z(32)], 1),
        c_kr=jnp.concatenate([z(64), cm, cm, z(32)], 1),
        s1=jnp.concatenate([z(64), -sm, z(16), z(32)], 1),
        s2=jnp.concatenate([z(64), z(16), sm, z(32)], 1),
    )
    one = lambda a, b, c: jnp.concatenate([a, b, c, a, b, c], 1)
    dil = dict(c=one(cd, cd, o(48)), s1=one(-sd, z(8), z(48)), s2=one(z(8), sd, z(48)))
    return mla, dil


def _rope_heads(src, c, s1, s2, name, out_dtype):
    S = src.shape[0]
    tm = ROW_TILE

    def body(x_ref, c_ref, s1_ref, s2_ref, o_ref):
        cv, s1v, s2v = c_ref[...], s1_ref[...], s2_ref[...]
        for h in range(HEADS):
            sl = slice(h * LANE, (h + 1) * LANE)
            o_ref[:, sl] = _rot(x_ref[:, sl], cv, s1v, s2v, ROPE // 2).astype(out_dtype)

    tab = pl.BlockSpec((tm, LANE), lambda i: (i, 0))
    wide = pl.BlockSpec((tm, HP), lambda i: (i, 0))
    return pl.pallas_call(
        body, name=name, grid=(S // tm,), out_shape=jax.ShapeDtypeStruct((S, HP), out_dtype),
        in_specs=[wide, tab, tab, tab], out_specs=wide, compiler_params=_params(1),
    )(src, c, s1, s2)


def _k_assemble(kpre, p, mla, name):
    S = kpre.shape[0]
    tm = ROW_TILE

    def body(k_ref, kr_ref, c_ref, s1_ref, s2_ref, o_ref):
        r = _rot(kr_ref[...], c_ref[...], s1_ref[...], s2_ref[...], ROPE // 2)
        for h in range(HEADS):
            sl = slice(h * LANE, (h + 1) * LANE)
            o_ref[:, sl] = (k_ref[:, sl] + r).astype(BF16)

    tab = pl.BlockSpec((tm, LANE), lambda i: (i, 0))
    wide = pl.BlockSpec((tm, HP), lambda i: (i, 0))
    return pl.pallas_call(
        body, name=name, grid=(S // tm,), out_shape=jax.ShapeDtypeStruct((S, HP), BF16),
        in_specs=[wide, pl.BlockSpec((tm, LANE), lambda i: (i, CB_KR)), tab, tab, tab],
        out_specs=wide, compiler_params=_params(1),
    )(kpre, p, mla["c_kr"], mla["s1"], mla["s2"])


def _kr_bwd(dk, mla, name):
    S = dk.shape[0]
    tm = ROW_TILE

    def body(dk_ref, c_ref, s1_ref, s2_ref, o_ref):
        t = dk_ref[:, 0:LANE]
        for h in range(1, HEADS):
            t = t + dk_ref[:, h * LANE:(h + 1) * LANE]
        lane = lax.broadcasted_iota(jnp.int32, (1, LANE), 1)
        t = jnp.where((lane >= NOPE) & (lane < NOPE + ROPE), t, 0.0)
        o_ref[...] = _rot(t, c_ref[...], -s1_ref[...], -s2_ref[...], ROPE // 2).astype(BF16)

    tab = pl.BlockSpec((tm, LANE), lambda i: (i, 0))
    return pl.pallas_call(
        body, name=name, grid=(S // tm,), out_shape=jax.ShapeDtypeStruct((S, LANE), BF16),
        in_specs=[pl.BlockSpec((tm, HP), lambda i: (i, 0)), tab, tab, tab],
        out_specs=tab, compiler_params=_params(1),
    )(dk, mla["c_kr"], mla["s1"], mla["s2"])


def _dil_prep(p, dil, g, name):
    S = p.shape[0]
    tm = ROW_TILE
    first = CB_DIL * LANE // DIL_W + 3 * g

    def body(x_ref, c_ref, s1_ref, s2_ref, o_ref):
        t = pl.program_id(1)
        rep = DIL_W // LANE

        def roped():
            return _rot(x_ref[...], jnp.tile(c_ref[...], (1, rep)), jnp.tile(s1_ref[...], (1, rep)),
                        jnp.tile(s2_ref[...], (1, rep)), ROT // 2)

        @pl.when(t == 0)
        def _():
            o_ref[...] = (roped() * (DIL_HD ** -0.5)).astype(BF16)

        @pl.when(t == 1)
        def _():
            o_ref[...] = roped().astype(BF16)

        @pl.when(t == 2)
        def _():
            o_ref[...] = x_ref[...].astype(BF16)

    tab = pl.BlockSpec((tm, LANE), lambda i, j: (i, 0))
    return pl.pallas_call(
        body, name=name, grid=(S // tm, 3), out_shape=jax.ShapeDtypeStruct((S, 3 * DIL_W), BF16),
        in_specs=[pl.BlockSpec((tm, DIL_W), lambda i, j: (i, first + j)), tab, tab, tab],
        out_specs=pl.BlockSpec((tm, DIL_W), lambda i, j: (i, j)), compiler_params=_params(2),
    )(p, dil["c"], dil["s1"], dil["s2"])


def _grid_ends(dims):
    ids = [pl.program_id(a) for a in range(len(dims))]
    first = functools.reduce(jnp.logical_and, [i == 0 for i in ids])
    last = functools.reduce(jnp.logical_and, [i == n - 1 for i, n in zip(ids, dims)])
    return first, last


def _flash_fwd(q, k, v, name, comm=None):
    S = q.shape[0]
    tq = tk = _pick(S, (1024, 512))
    nk = S // tk
    c2 = (NOPE + ROPE) ** -0.5 * math.log2(math.e)
    srcs, same_src = comm if comm is not None else ([], True)
    n = len(srcs)
    grid = (HEADS, S // tq, nk)

    def body(*refs):
        q_ref, k_ref, v_ref = refs[:3]
        o_ref, lse_ref = refs[3 + n:5 + n]
        m_s, acc_s = refs[5 + 2 * n:7 + 2 * n]
        ex = (refs[3:3 + n], refs[5 + n:5 + 2 * n], *refs[7 + 2 * n:], same_src)
        j = pl.program_id(2)
        if n:
            first, last = _grid_ends(grid)
            pl.when(first)(lambda: _exchange_start(*ex))

        @pl.when(j == 0)
        def _():
            m_s[...] = jnp.full(m_s.shape, -jnp.inf, F32)
            acc_s[...] = jnp.zeros(acc_s.shape, F32)

        lane = lax.broadcasted_iota(jnp.int32, (1, LANE), 1)
        vv = jnp.where(lane == VDIM, jnp.ones((), BF16), v_ref[...])
        t = lax.dot_general(q_ref[...], k_ref[...], _NT, preferred_element_type=F32) * c2
        m_prev = m_s[...]
        m_new = jnp.maximum(m_prev, jnp.max(t, axis=-1, keepdims=True))
        alpha = jnp.exp2(m_prev - m_new)
        e = jnp.exp2(t - jnp.tile(m_new, (1, tk // LANE)))
        acc_s[...] = alpha * acc_s[...] + jnp.dot(e.astype(BF16), vv, preferred_element_type=F32)
        m_s[...] = m_new

        @pl.when(j == nk - 1)
        def _():
            acc = acc_s[...]
            l = acc[:, VDIM:VDIM + 1]
            o_ref[...] = jnp.where(lane < VDIM, acc / l, 0.0)
            lse_ref[...] = (m_s[...] + jnp.log2(l)) * math.log(2.0)

        if n:
            pl.when(last)(lambda: _exchange_wait(*ex))

    qs = pl.BlockSpec((tq, LANE), lambda h, i, j: (i, h))
    ks = pl.BlockSpec((tk, LANE), lambda h, i, j: (j, h))
    hbm = pl.BlockSpec(memory_space=pltpu.HBM)
    outs = pl.pallas_call(
        body, name=name, grid=grid,
        out_shape=(jax.ShapeDtypeStruct((S, HP), F32), jax.ShapeDtypeStruct((S, HP), F32)) + _exchange_shapes(srcs, same_src),
        in_specs=[qs, ks, ks] + [hbm] * n, out_specs=(qs, qs) + (hbm,) * n,
        scratch_shapes=[pltpu.VMEM((tq, LANE), F32), pltpu.VMEM((tq, LANE), F32)] + (_exchange_sems(n) if n else []),
        compiler_params=_params(3),
    )(q, k, v, *srcs)
    return outs[0], outs[1], list(outs[2:])


def _flash_bwd(q, k, v, o, do, lse, name, comm=None):
    S = q.shape[0]
    tq = tk = _pick(S, (1024, 512))
    nq = S // tq
    scale = (NOPE + ROPE) ** -0.5
    srcs, same_src = comm if comm is not None else ([], True)
    n = len(srcs)
    grid = (HEADS, S // tk, nq)

    def body(*refs):
        q_ref, k_ref, v_ref, o_ref, do_ref, lse_ref = refs[:6]
        dq_ref, dk_ref, dv_ref = refs[6 + n:9 + n]
        dk_s, dv_s = refs[9 + 2 * n:11 + 2 * n]
        ex = (refs[6:6 + n], refs[9 + n:9 + 2 * n], *refs[11 + 2 * n:], same_src)
        j = pl.program_id(1)
        i = pl.program_id(2)
        if n:
            first, last = _grid_ends(grid)
            pl.when(first)(lambda: _exchange_start(*ex))
        qv = q_ref[...]
        kv = k_ref[...]
        do = do_ref[...]
        dob = do.astype(BF16)
        s = lax.dot_general(qv, kv, _NT, preferred_element_type=F32) * scale
        p = jnp.exp(s - lse_ref[:, 0:1])
        dp = lax.dot_general(dob, v_ref[...], _NT, preferred_element_type=F32)
        delta = jnp.sum(do * o_ref[...], axis=-1, keepdims=True)
        ds = (p * (dp - delta)).astype(BF16)
        dv_part = lax.dot_general(p.astype(BF16), dob, _TN, preferred_element_type=F32)
        dk_part = lax.dot_general(ds, qv, _TN, preferred_element_type=F32)
        dq_part = jnp.dot(ds, kv, preferred_element_type=F32) * scale

        @pl.when(i == 0)
        def _():
            dk_s[...] = dk_part
            dv_s[...] = dv_part

        @pl.when(i > 0)
        def _():
            dk_s[...] += dk_part
            dv_s[...] += dv_part

        rows = pl.ds(pl.multiple_of(i * tq, tq), tq)

        @pl.when(j == 0)
        def _():
            dq_ref[rows, :] = dq_part

        @pl.when(j > 0)
        def _():
            dq_ref[rows, :] += dq_part

        @pl.when(i == nq - 1)
        def _():
            dk_ref[...] = dk_s[...] * scale
            dv_ref[...] = dv_s[...].astype(BF16)

        if n:
            pl.when(last)(lambda: _exchange_wait(*ex))

    qs = pl.BlockSpec((tq, LANE), lambda h, j, i: (i, h))
    ks = pl.BlockSpec((tk, LANE), lambda h, j, i: (j, h))
    hbm = pl.BlockSpec(memory_space=pltpu.HBM)
    outs = pl.pallas_call(
        body, name=name, grid=grid,
        out_shape=(jax.ShapeDtypeStruct((S, HP), F32), jax.ShapeDtypeStruct((S, HP), F32), jax.ShapeDtypeStruct((S, HP), BF16))
        + _exchange_shapes(srcs, same_src),
        in_specs=[qs, ks, ks, qs, qs, qs] + [hbm] * n,
        out_specs=(pl.BlockSpec((S, LANE), lambda h, j, i: (0, h)), ks, ks) + (hbm,) * n,
        scratch_shapes=[pltpu.VMEM((tk, LANE), F32), pltpu.VMEM((tk, LANE), F32)] + (_exchange_sems(n) if n else []),
        compiler_params=_params(3),
    )(q, k, v, o, do, lse, *srcs)
    return outs[0], outs[1], outs[2], list(outs[3:])


def _band_tiles(L):
    tq = min(512, L)
    return tq, tq // SUB, tq // HALF, L // HALF


def _halo_specs(tq, rpb, n64, col):
    prev = pl.BlockSpec((HALF, DIL_W), lambda r, i: (jnp.maximum(rpb * i - 1, 0), col(r)))
    cur = pl.BlockSpec((tq, DIL_W), lambda r, i: (i, col(r)))
    nxt = pl.BlockSpec((HALF, DIL_W), lambda r, i: (jnp.minimum(rpb * i + rpb, n64 - 1), col(r)))
    return [prev, cur, nxt]


def _fill(buf, prev_ref, cur_ref, next_ref, tq):
    buf[0:HALF, :] = prev_ref[...]
    buf[HALF:HALF + tq, :] = cur_ref[...]
    buf[HALF + tq:HALF + tq + HALF, :] = next_ref[...]


def _lo_lanes():
    return lax.broadcasted_iota(jnp.int32, (1, LANE), 1) < DIL_HD


def _stack_heads(x, lo):
    zero = jnp.zeros_like(x)
    return jnp.concatenate([jnp.where(lo, x, zero), jnp.where(lo, zero, x)], axis=0)


def _stack_cols(x):
    return jnp.concatenate([x[:, 0:1], x[:, DIL_HD:DIL_HD + 1]], axis=0)


def _band_valid(q0, k0, nq, nk, L, bound_q):
    qpos = q0 + lax.broadcasted_iota(jnp.int32, (nq, 1), 0)
    kpos = k0 + lax.broadcasted_iota(jnp.int32, (1, nk), 1)
    side = qpos if bound_q else kpos
    return (jnp.abs(qpos - kpos) <= HALF) & (side >= 0) & (side < L)


def _band_fwd(dilr, d, name):
    S = dilr.shape[0]
    L = S // d
    tq, nsub, rpb, n64 = _band_tiles(L)
    view = dilr.reshape(L, d * dilr.shape[1])
    win = SUB + 2 * HALF

    def body(q_ref, kp_ref, kc_ref, kn_ref, vp_ref, vc_ref, vn_ref, o_ref, lse_ref, kbuf, vbuf):
        i = pl.program_id(1)
        _fill(kbuf, kp_ref, kc_ref, kn_ref, tq)
        _fill(vbuf, vp_ref, vc_ref, vn_ref, tq)
        lo = _lo_lanes()
        for a in range(nsub):
            r0 = a * SUB
            rows = slice(r0, r0 + SUB)
            valid = _band_valid(i * tq + r0, i * tq + r0 - HALF, SUB, win, L, False)
            valid2 = jnp.concatenate([valid, valid], axis=0)
            for hp in range(4):
                cs = slice(hp * LANE, (hp + 1) * LANE)
                q = q_ref[rows, cs]
                kw = kbuf[r0:r0 + win, cs]
                vw = vbuf[r0:r0 + win, cs]
                q2 = _stack_heads(q, lo)
                s = lax.dot_general(q2, kw, _NT, preferred_element_type=F32)
                s = jnp.where(valid2, s, -jnp.inf)
                m = jnp.max(s, axis=-1, keepdims=True)
                e = jnp.exp(s - m)
                l = jnp.sum(e, axis=-1, keepdims=True)
                o2 = jnp.dot(e.astype(BF16), vw, preferred_element_type=F32) / l
                lse2 = m + jnp.log(l)
                o_ref[rows, cs] = jnp.where(lo, o2[:SUB], o2[SUB:])
                lse_ref[rows, cs] = jnp.where(lo, lse2[:SUB], lse2[SUB:])

    out_spec = pl.BlockSpec((tq, DIL_W), lambda r, i: (i, r))
    o, lse = pl.pallas_call(
        body, name=name, grid=(d, L // tq),
        out_shape=(jax.ShapeDtypeStruct((L, d * DIL_W), F32), jax.ShapeDtypeStruct((L, d * DIL_W), F32)),
        in_specs=[pl.BlockSpec((tq, DIL_W), lambda r, i: (i, r * 3))]
        + _halo_specs(tq, rpb, n64, lambda r: r * 3 + 1) + _halo_specs(tq, rpb, n64, lambda r: r * 3 + 2),
        out_specs=(out_spec, out_spec),
        scratch_shapes=[pltpu.VMEM((tq + 2 * HALF, DIL_W), BF16), pltpu.VMEM((tq + 2 * HALF, DIL_W), BF16)],
        compiler_params=_params(2),
    )(view, view, view, view, view, view, view)
    return o.reshape(S, DIL_W), lse.reshape(S, DIL_W)


def _band_bwd_q(dilr, d, do, lse, dlt, dil, name):
    S = dilr.shape[0]
    L = S // d
    tq, nsub, rpb, n64 = _band_tiles(L)
    view = dilr.reshape(L, d * dilr.shape[1])
    v4 = lambda t: t.reshape(L, d * DIL_W)
    tv = lambda t: t.reshape(L, d * LANE)
    win = SUB + 2 * HALF

    def body(q_ref, kp_ref, kc_ref, kn_ref, vp_ref, vc_ref, vn_ref, do_ref, lse_ref, dlt_ref, c_ref, s1_ref, s2_ref, dq_ref,
             kbuf, vbuf):
        i = pl.program_id(1)
        _fill(kbuf, kp_ref, kc_ref, kn_ref, tq)
        _fill(vbuf, vp_ref, vc_ref, vn_ref, tq)
        lo = _lo_lanes()
        for a in range(nsub):
            r0 = a * SUB
            rows = slice(r0, r0 + SUB)
            valid = _band_valid(i * tq + r0, i * tq + r0 - HALF, SUB, win, L, False)
            valid2 = jnp.concatenate([valid, valid], axis=0)
            cv, s1v, s2v = c_ref[rows, :], s1_ref[rows, :], s2_ref[rows, :]
            for hp in range(4):
                cs = slice(hp * LANE, (hp + 1) * LANE)
                kw = kbuf[r0:r0 + win, cs]
                vw = vbuf[r0:r0 + win, cs]
                q2 = _stack_heads(q_ref[rows, cs], lo)
                do2 = _stack_heads(do_ref[rows, cs], lo).astype(BF16)
                s = lax.dot_general(q2, kw, _NT, preferred_element_type=F32)
                p = jnp.where(valid2, jnp.exp(s - _stack_cols(lse_ref[rows, cs])), 0.0)
                dp = lax.dot_general(do2, vw, _NT, preferred_element_type=F32)
                ds = (p * (dp - _stack_cols(dlt_ref[rows, cs]))).astype(BF16)
                dq2 = jnp.dot(ds, kw, preferred_element_type=F32)
                dq = jnp.where(lo, dq2[:SUB], dq2[SUB:])
                dq_ref[rows, cs] = (_rot(dq, cv, -s1v, -s2v, ROT // 2) * (DIL_HD ** -0.5)).astype(BF16)

    row = pl.BlockSpec((tq, DIL_W), lambda r, i: (i, r))
    tab = pl.BlockSpec((tq, LANE), lambda r, i: (i, r))
    dq = pl.pallas_call(
        body, name=name, grid=(d, L // tq), out_shape=jax.ShapeDtypeStruct((L, d * DIL_W), BF16),
        in_specs=[pl.BlockSpec((tq, DIL_W), lambda r, i: (i, r * 3))]
        + _halo_specs(tq, rpb, n64, lambda r: r * 3 + 1) + _halo_specs(tq, rpb, n64, lambda r: r * 3 + 2)
        + [row, row, row, tab, tab, tab],
        out_specs=row,
        scratch_shapes=[pltpu.VMEM((tq + 2 * HALF, DIL_W), BF16), pltpu.VMEM((tq + 2 * HALF, DIL_W), BF16)],
        compiler_params=_params(2),
    )(view, view, view, view, view, view, view, v4(do), v4(lse), v4(dlt), tv(dil["c"]), tv(dil["s1"]), tv(dil["s2"]))
    return dq.reshape(S, DIL_W)


def _band_bwd_kv(dilr, d, do, lse, dlt, dil, name):
    S = dilr.shape[0]
    L = S // d
    tq, nsub, rpb, n64 = _band_tiles(L)
    view = dilr.reshape(L, d * dilr.shape[1])
    v4 = lambda t: t.reshape(L, d * DIL_W)
    tv = lambda t: t.reshape(L, d * LANE)
    win = SUB + 2 * HALF

    def body(k_ref, v_ref, qp_ref, qc_ref, qn_ref, dop_ref, doc_ref, don_ref, lp_ref, lc_ref, ln_ref, tp_ref, tc_ref, tn_ref,
             c_ref, s1_ref, s2_ref, dk_ref, dv_ref, qbuf, dobuf, lbuf, tbuf):
        j = pl.program_id(1)
        _fill(qbuf, qp_ref, qc_ref, qn_ref, tq)
        _fill(dobuf, dop_ref, doc_ref, don_ref, tq)
        _fill(lbuf, lp_ref, lc_ref, ln_ref, tq)
        _fill(tbuf, tp_ref, tc_ref, tn_ref, tq)
        lo = _lo_lanes()
        for a in range(nsub):
            r0 = a * SUB
            rows = slice(r0, r0 + SUB)
            wrows = slice(r0, r0 + win)
            valid = _band_valid(j * tq + r0 - HALF, j * tq + r0, win, SUB, L, True)
            valid2 = jnp.concatenate([valid, valid], axis=0)
            cv, s1v, s2v = c_ref[rows, :], s1_ref[rows, :], s2_ref[rows, :]
            for hp in range(4):
                cs = slice(hp * LANE, (hp + 1) * LANE)
                k = k_ref[rows, cs]
                v = v_ref[rows, cs]
                q2 = _stack_heads(qbuf[wrows, cs], lo)
                do2 = _stack_heads(dobuf[wrows, cs], lo).astype(BF16)
                s = lax.dot_general(q2, k, _NT, preferred_element_type=F32)
                p = jnp.where(valid2, jnp.exp(s - _stack_cols(lbuf[wrows, cs])), 0.0)
                dv = lax.dot_general(p.astype(BF16), do2, _TN, preferred_element_type=F32)
                dp = lax.dot_general(do2, v, _NT, preferred_element_type=F32)
                ds = (p * (dp - _stack_cols(tbuf[wrows, cs]))).astype(BF16)
                dk = lax.dot_general(ds, q2, _TN, preferred_element_type=F32)
                dk_ref[rows, cs] = _rot(dk, cv, -s1v, -s2v, ROT // 2).astype(BF16)
                dv_ref[rows, cs] = dv.astype(BF16)

    row = pl.BlockSpec((tq, DIL_W), lambda r, i: (i, r))
    tab = pl.BlockSpec((tq, LANE), lambda r, i: (i, r))
    halo = _halo_specs(tq, rpb, n64, lambda r: r)
    hb = tq + 2 * HALF
    dk, dv = pl.pallas_call(
        body, name=name, grid=(d, L // tq),
        out_shape=(jax.ShapeDtypeStruct((L, d * DIL_W), BF16), jax.ShapeDtypeStruct((L, d * DIL_W), BF16)),
        in_specs=[pl.BlockSpec((tq, DIL_W), lambda r, i: (i, r * 3 + 1)), pl.BlockSpec((tq, DIL_W), lambda r, i: (i, r * 3 + 2))]
        + _halo_specs(tq, rpb, n64, lambda r: r * 3) + halo + halo + halo + [tab, tab, tab],
        out_specs=(row, row),
        scratch_shapes=[pltpu.VMEM((hb, DIL_W), BF16), pltpu.VMEM((hb, DIL_W), F32), pltpu.VMEM((hb, DIL_W), F32), pltpu.VMEM((hb, DIL_W), F32)],
        compiler_params=_params(2),
    )(view, view, view, view, view, v4(do), v4(do), v4(do), v4(lse), v4(lse), v4(lse), v4(dlt), v4(dlt), v4(dlt),
      tv(dil["c"]), tv(dil["s1"]), tv(dil["s2"]))
    return dk.reshape(S, DIL_W), dv.reshape(S, DIL_W)


def _sigmoid(x):
    return 1.0 / (1.0 + jnp.exp(-x))


def _gate_a(o, p, name):
    S = o.shape[0]
    tm = ROW_TILE

    def body(o_ref, g_ref, a_ref):
        g = g_ref[...]
        a_ref[...] = (o_ref[...] * (g * _sigmoid(g))).astype(BF16)

    blk = pl.BlockSpec((tm, HP), lambda i: (i, 0))
    return pl.pallas_call(
        body, name=name, grid=(S // tm,), out_shape=jax.ShapeDtypeStruct((S, HP), BF16),
        in_specs=[blk, pl.BlockSpec((tm, HP), lambda i: (i, CB_GA * LANE // HP))], out_specs=blk, compiler_params=_params(1),
    )(o, p)


def _gate_a_bwd(da, o, p, name):
    S = o.shape[0]
    tm = ROW_TILE

    def body(da_ref, o_ref, g_ref, do_ref, dg_ref):
        g = g_ref[...]
        da = da_ref[...]
        sg = _sigmoid(g)
        do_ref[...] = da * (g * sg)
        dg_ref[...] = (da * o_ref[...] * (sg * (1.0 + g * (1.0 - sg)))).astype(BF16)

    blk = pl.BlockSpec((tm, HP), lambda i: (i, 0))
    return pl.pallas_call(
        body, name=name, grid=(S // tm,),
        out_shape=(jax.ShapeDtypeStruct((S, HP), F32), jax.ShapeDtypeStruct((S, HP), BF16)),
        in_specs=[blk, blk, pl.BlockSpec((tm, HP), lambda i: (i, CB_GA * LANE // HP))], out_specs=(blk, blk), compiler_params=_params(1),
    )(da, o, p)


def _merge_weights(l0, l1, l2):
    mx = jnp.maximum(jnp.maximum(l0, l1), l2)
    e0, e1, e2 = jnp.exp(l0 - mx), jnp.exp(l1 - mx), jnp.exp(l2 - mx)
    den = e0 + e1 + e2
    return e0 / den, e1 / den, e2 / den


def _gate_b(os_, ls_, p, name):
    S = p.shape[0]
    tm = ROW_TILE

    def body(o0, o1, o2, l0, l1, l2, g_ref, b_ref):
        a0, a1, a2 = _merge_weights(l0[...], l1[...], l2[...])
        bm = a0 * o0[...] + a1 * o1[...] + a2 * o2[...]
        g = g_ref[...]
        b_ref[...] = (bm * (g * _sigmoid(g))).astype(BF16)

    blk = pl.BlockSpec((tm, DIL_W), lambda i: (i, 0))
    return pl.pallas_call(
        body, name=name, grid=(S // tm,), out_shape=jax.ShapeDtypeStruct((S, DIL_W), BF16),
        in_specs=[blk] * 6 + [pl.BlockSpec((tm, DIL_W), lambda i: (i, CB_GB * LANE // DIL_W))], out_specs=blk, compiler_params=_params(1),
    )(*os_, *ls_, p)


def _gate_b_bwd(db, os_, ls_, p, name):
    S = p.shape[0]
    tm = ROW_TILE

    def body(db_ref, o0, o1, o2, l0, l1, l2, g_ref, dg_ref, d0, d1, d2, t0, t1, t2):
        a0, a1, a2 = _merge_weights(l0[...], l1[...], l2[...])
        bm = a0 * o0[...] + a1 * o1[...] + a2 * o2[...]
        g = g_ref[...]
        db = db_ref[...]
        sg = _sigmoid(g)
        dbm = db * (g * sg)
        dg_ref[...] = (db * bm * (sg * (1.0 + g * (1.0 - sg)))).astype(BF16)
        prod = dbm * bm
        lo = _lo_lanes()
        parts = []
        for hp in range(DIL_W // LANE):
            pc = prod[:, hp * LANE:(hp + 1) * LANE]
            tl = jnp.sum(jnp.where(lo, pc, 0.0), axis=-1, keepdims=True)
            th = jnp.sum(jnp.where(lo, 0.0, pc), axis=-1, keepdims=True)
            parts.append(jnp.where(lo, tl, th))
        t = jnp.concatenate(parts, axis=1)
        d0[...] = a0 * dbm
        d1[...] = a1 * dbm
        d2[...] = a2 * dbm
        t0[...] = a0 * t
        t1[...] = a1 * t
        t2[...] = a2 * t

    blk = pl.BlockSpec((tm, DIL_W), lambda i: (i, 0))
    f = jax.ShapeDtypeStruct((S, DIL_W), F32)
    outs = pl.pallas_call(
        body, name=name, grid=(S // tm,),
        out_shape=(jax.ShapeDtypeStruct((S, DIL_W), BF16), f, f, f, f, f, f),
        in_specs=[blk] * 7 + [pl.BlockSpec((tm, DIL_W), lambda i: (i, CB_GB * LANE // DIL_W))], out_specs=(blk,) * 7, compiler_params=_params(1),
    )(db, *os_, *ls_, p)
    return outs[0], outs[1:4], outs[4:7]


def _loss_head(x, target, g, name):
    S, D = x.shape
    tm = ROW_TILE

    def body(x_ref, t_ref, g_ref, dx_ref, dg_ref, loss_ref):
        xv = x_ref[...]
        gv = g_ref[...]
        r = lax.rsqrt(jnp.mean(xv * xv, axis=-1, keepdims=True) + EPS)
        xr = xv * r
        err = xr * gv - t_ref[...]
        lpart = 0.5 * jnp.sum(jnp.mean(err * err, axis=-1, keepdims=True), axis=0, keepdims=True)
        dy = err / D
        dyg = dy * gv
        c = jnp.mean(dyg * xv, axis=-1, keepdims=True)
        dx_ref[...] = r * dyg - xv * (r * r * r) * c
        gpart = jnp.sum(dy * xr, axis=0, keepdims=True)

        @pl.when(pl.program_id(0) == 0)
        def _():
            dg_ref[...] = gpart
            loss_ref[...] = jnp.broadcast_to(lpart, loss_ref.shape)

        @pl.when(pl.program_id(0) > 0)
        def _():
            dg_ref[...] += gpart
            loss_ref[...] += jnp.broadcast_to(lpart, loss_ref.shape)

    row = pl.BlockSpec((tm, D), lambda i: (i, 0))
    vec = pl.BlockSpec((1, D), lambda i: (0, 0))
    return pl.pallas_call(
        body, name=name, grid=(S // tm,),
        out_shape=(jax.ShapeDtypeStruct((S, D), F32), jax.ShapeDtypeStruct((1, D), F32), jax.ShapeDtypeStruct((1, D), F32)),
        in_specs=[row, row, vec], out_specs=(row, vec, vec), compiler_params=_params(1),
    )(x, target, g.reshape(1, D))


def _adamw(parts, w, m, v, name):
    R, C = w.shape
    tr = _pick(R, (128, 64, 32, 16, 8))

    def body(p_ref, w_ref, m_ref, v_ref, g_ref, d_ref, nm_ref, nv_ref):
        g = p_ref[0].astype(F32)
        for k in range(1, N_DEV):
            g = g + p_ref[k].astype(F32)
        m2 = ADAM_B1 * m_ref[...] + (1.0 - ADAM_B1) * g
        v2 = ADAM_B2 * v_ref[...] + (1.0 - ADAM_B2) * (g * g)
        m_hat = m2 / (1.0 - ADAM_B1 ** ADAM_STEP)
        v_hat = v2 / (1.0 - ADAM_B2 ** ADAM_STEP)
        g_ref[...] = g
        d_ref[...] = -ADAM_LR * (m_hat / (jnp.sqrt(v_hat) + ADAM_EPS) + ADAM_WD * w_ref[...])
        nm_ref[...] = m2
        nv_ref[...] = v2

    blk = pl.BlockSpec((tr, C), lambda i: (i, 0))
    f = jax.ShapeDtypeStruct((R, C), F32)
    return pl.pallas_call(
        body, name=name, grid=(R // tr,), out_shape=(f, f, f, f),
        in_specs=[pl.BlockSpec((N_DEV, tr, C), lambda i: (0, i, 0)), blk, blk, blk], out_specs=(blk,) * 4,
        compiler_params=_params(1),
    )(parts, w, m, v)


def _exchange_copies(src_refs, out_refs, send_sems, recv_sems, local_sems, same_src):
    n = len(src_refs)
    x, y, c = lax.axis_index("x"), lax.axis_index("y"), lax.axis_index("c")
    me = 4 * x + 2 * y + c

    def block(t, j):
        return src_refs[t] if same_src else src_refs[t].at[j]

    local = [pltpu.make_async_copy(block(t, me), out_refs[t].at[me], local_sems.at[t]) for t in range(n)]
    remote = []
    for k in range(1, N_DEV):
        px = 1 - x if (k >> 2) & 1 else x
        py = 1 - y if (k >> 1) & 1 else y
        pc = 1 - c if k & 1 else c
        for t in range(n):
            remote.append(pltpu.make_async_remote_copy(
                src_ref=block(t, 4 * px + 2 * py + pc), dst_ref=out_refs[t].at[me],
                send_sem=send_sems.at[(k - 1) * n + t], recv_sem=recv_sems.at[(k - 1) * n + t],
                device_id=(px, py, pc), device_id_type=pl.DeviceIdType.MESH))
    return local, remote


def _exchange_start(*args):
    local, remote = _exchange_copies(*args)
    for cp in local + remote:
        cp.start()


def _exchange_wait(*args):
    local, remote = _exchange_copies(*args)
    for cp in remote:
        cp.wait()
    for cp in local:
        cp.wait()


def _exchange_shapes(srcs, same_src):
    return tuple(jax.ShapeDtypeStruct((N_DEV,) + (tuple(s.shape) if same_src else tuple(s.shape[1:])), s.dtype) for s in srcs)


def _exchange_sems(n):
    return [pltpu.SemaphoreType.DMA(((N_DEV - 1) * n,)), pltpu.SemaphoreType.DMA(((N_DEV - 1) * n,)), pltpu.SemaphoreType.DMA((n,))]


def _exchange(srcs, same_src, name):
    n = len(srcs)

    def body(*refs):
        args = (refs[:n], refs[n:2 * n], *refs[2 * n:], same_src)
        _exchange_start(*args)
        _exchange_wait(*args)

    hbm = pl.BlockSpec(memory_space=pltpu.HBM)
    outs = pl.pallas_call(
        body, name=name, out_shape=_exchange_shapes(srcs, same_src),
        in_specs=[hbm] * n, out_specs=(hbm,) * n, scratch_shapes=_exchange_sems(n),
    )(*srcs)
    return list(outs)


def _full_weights(g_in, g_uq, g_ukv, g_out):
    cat_cols = lambda t: jnp.moveaxis(t, 0, 2).reshape(t.shape[1], t.shape[2], -1)
    return cat_cols(g_in), cat_cols(g_uq), cat_cols(g_ukv), jnp.moveaxis(g_out, 0, 1).reshape(g_out.shape[1], D_MODEL, D_MODEL)


def _grad_blocks(g_in, g_uq, g_ukv, g_out):
    split_cols = lambda t: jnp.moveaxis(t.astype(BF16).reshape(t.shape[0], N_DEV, -1), 1, 0)
    return [split_cols(g_in), split_cols(g_uq), split_cols(g_ukv), g_out.astype(BF16).reshape(N_DEV, D_MODEL // N_DEV, D_MODEL)]


def _layer_weights(w_in, w_uq, w_ukv, w_out):
    z = lambda r, n: jnp.zeros((r, n), BF16)
    c_q, c_kv, k_r = w_in[:, 0:384], w_in[:, 384:640], w_in[:, 640:672]
    gate_a, dil, gate_b = w_in[:, 672:1184], w_in[:, 1184:5792], w_in[:, 5792:6304]
    ga_pad = jnp.pad(gate_a.reshape(D_MODEL, HEADS, VDIM), ((0, 0), (0, 0), (0, LANE - VDIM))).reshape(D_MODEL, HP)
    w_p = jnp.concatenate([ga_pad, dil, gate_b, c_kv, z(D_MODEL, 64), k_r, z(D_MODEL, 32), c_q], axis=1)
    uq = jnp.pad(w_uq.reshape(Q_LORA, HEADS, NOPE + ROPE), ((0, 0), (0, 0), (0, LANE - NOPE - ROPE))).reshape(Q_LORA, HP)
    ukv = w_ukv.reshape(KV_LORA, HEADS, NOPE + VDIM)
    pad64 = lambda t: jnp.pad(t, ((0, 0), (0, 0), (0, LANE - 64))).reshape(KV_LORA, HP)
    uk, uv = pad64(ukv[..., :NOPE]), pad64(ukv[..., NOPE:])
    wa = jnp.pad(w_out[:HEADS * VDIM].reshape(HEADS, VDIM, D_MODEL), ((0, 0), (0, LANE - VDIM), (0, 0))).reshape(HP, D_MODEL)
    wb = w_out[HEADS * VDIM:]
    return dict(p=w_p, uq=uq, uk=uk, uv=uv, wa=wa, wb=wb)


def _unpad_grads(g):
    gp = g["p"]
    seg = lambda cb, n: gp[:, cb * LANE:cb * LANE + n]
    ga = seg(CB_GA, HP).reshape(D_MODEL, HEADS, LANE)[..., :VDIM].reshape(D_MODEL, HEADS * VDIM)
    k_r = gp[:, CB_KR * LANE + NOPE:CB_KR * LANE + NOPE + ROPE]
    g_in = jnp.concatenate([seg(CB_CQ, Q_LORA), seg(CB_CKV, KV_LORA), k_r, ga, seg(CB_DIL, 9 * DIL_W), seg(CB_GB, DIL_W)], axis=1)
    g_uq = g["uq"].reshape(Q_LORA, HEADS, LANE)[..., :NOPE + ROPE].reshape(Q_LORA, -1)
    uk = g["uk"].reshape(KV_LORA, HEADS, LANE)[..., :NOPE]
    uv = g["uv"].reshape(KV_LORA, HEADS, LANE)[..., :VDIM]
    g_ukv = jnp.concatenate([uk, uv], axis=-1).reshape(KV_LORA, -1)
    wa = g["wa"].reshape(HEADS, LANE, D_MODEL)[:, :VDIM].reshape(HEADS * VDIM, D_MODEL)
    g_out = jnp.concatenate([wa, g["wb"]], axis=0)
    return g_in, g_uq, g_ukv, g_out


def _layer_fwd(x, w, norm_g, q_norm_g, kv_norm_g, mla, dil, l, comm=None):
    n = lambda s: f"l{l}_{s}"
    h = _rms_fwd(x, 0, D_MODEL, norm_g, n("norm"))
    p = _mm(h, w["p"], "nn", n("in_proj"))
    cqn = _rms_fwd(p, CB_CQ * LANE // Q_LORA, Q_LORA, q_norm_g, n("q_norm"))
    ckvn = _rms_fwd(p, CB_CKV * LANE // KV_LORA, KV_LORA, kv_norm_g, n("kv_norm"))
    qp = _mm(cqn, w["uq"], "nn", n("q_up"))
    kpre = _mm(ckvn, w["uk"], "nn", n("k_up"))
    v = _mm(ckvn, w["uv"], "nn", n("v_up"), out_dtype=BF16)
    q = _rope_heads(qp, mla["c_q"], mla["s1"], mla["s2"], n("q_rope"), BF16)
    k = _k_assemble(kpre, p, mla, n("k_asm"))
    o, lse, received = _flash_fwd(q, k, v, n("mla_fwd"), comm)
    a = _gate_a(o, p, n("gate_a"))
    dilr, os_, ls_ = [], [], []
    for g, (_, d) in enumerate(DIL_PAIRS):
        dilr.append(_dil_prep(p, dil, g, n(f"dil_prep{g}")))
        og, lg = _band_fwd(dilr[g], d, n(f"band{g}_fwd"))
        os_.append(og)
        ls_.append(lg)
    b = _gate_b(os_, ls_, p, n("gate_b"))
    x1 = _mm(a, w["wa"], "nn", n("out_a"), res=x)
    x2 = _mm(b, w["wb"], "nn", n("out_b"), res=x1)
    saved = dict(x=x, h=h, p=p, cqn=cqn, ckvn=ckvn, q=q, k=k, v=v, o=o, lse=lse, a=a, dilr=dilr, os=os_, ls=ls_, b=b)
    return x2, saved, received


def _layer_bwd(dx, sv, w, norm_g, q_norm_g, kv_norm_g, mla, dil, l, comm=None):
    n = lambda s: f"l{l}_{s}"
    g = {}
    da = _mm(dx, w["wa"], "nt", n("d_a"))
    db = _mm(dx, w["wb"], "nt", n("d_b"))
    g["wa"] = _mm(sv["a"], dx, "tn", n("dw_a"))
    g["wb"] = _mm(sv["b"], dx, "tn", n("dw_b"))
    do, dga = _gate_a_bwd(da, sv["o"], sv["p"], n("gate_a_bwd"))
    dgb, dos, dts = _gate_b_bwd(db, sv["os"], sv["ls"], sv["p"], n("gate_b_bwd"))
    ddil = []
    for gi, (_, d) in enumerate(DIL_PAIRS):
        dq = _band_bwd_q(sv["dilr"][gi], d, dos[gi], sv["ls"][gi], dts[gi], dil, n(f"band{gi}_bwd_q"))
        dk, dv = _band_bwd_kv(sv["dilr"][gi], d, dos[gi], sv["ls"][gi], dts[gi], dil, n(f"band{gi}_bwd_kv"))
        ddil += [dq, dk, dv]
    dq, dk, dv, received = _flash_bwd(sv["q"], sv["k"], sv["v"], sv["o"], do, sv["lse"], n("mla_bwd"), comm)
    dqp = _rope_heads(dq, mla["c_q"], -mla["s1"], -mla["s2"], n("q_rope_bwd"), BF16)
    dkr = _kr_bwd(dk, mla, n("kr_bwd"))
    g["uq"] = _mm(sv["cqn"], dqp, "tn", n("dw_uq"))
    g["uk"] = _mm(sv["ckvn"], dk, "tn", n("dw_uk"))
    g["uv"] = _mm(sv["ckvn"], dv, "tn", n("dw_uv"))
    dcqn = _mm(dqp, w["uq"], "nt", n("d_cqn"))
    dckvn = _mm(dk, w["uk"], "nt", n("d_ckvn_k"))
    dckvn = _mm(dv, w["uv"], "nt", n("d_ckvn_v"), res=dckvn)
    dcq, g_qn = _rms_bwd(sv["p"], CB_CQ * LANE // Q_LORA, Q_LORA, dcqn, q_norm_g, n("q_norm_bwd"), BF16)
    dckv, g_kvn = _rms_bwd(sv["p"], CB_CKV * LANE // KV_LORA, KV_LORA, dckvn, kv_norm_g, n("kv_norm_bwd"), BF16)
    dp = jnp.concatenate([dga] + ddil + [dgb, dckv, dkr, dcq], axis=1)
    g["p"] = _mm(sv["h"], dp, "tn", n("dw_in"))
    dh = _mm(dp, w["p"], "nt", n("d_h"))
    dx_in, g_n = _rms_bwd(sv["x"], 0, D_MODEL, dh, norm_g, n("norm_bwd"), F32, res=dx)
    return dx_in, g, g_n, g_qn, g_kvn, received


_SMALL_ROWS = 16


def _pack_small(norm, qn, kvn, fin, loss_row=None):
    padc = lambda t: jnp.pad(t, ((0, 0), (0, D_MODEL - t.shape[1])))
    extra = jnp.zeros((1, D_MODEL), F32) if loss_row is None else loss_row
    return jnp.concatenate([norm, padc(qn), padc(kvn), fin.reshape(1, D_MODEL), extra, jnp.zeros((2, D_MODEL), F32)], axis=0)


def _unpack_small(p):
    return (p[0:4], p[4:8, :Q_LORA], p[8:12, :KV_LORA], p[12]), p[13, 0]


def kernel(x, norm_g, w_in, q_norm_g, kv_norm_g, w_uq, w_ukv, w_out, final_g, loss_target, m_norm_g, m_w_in, m_q_norm_g, m_kv_norm_g, m_w_uq, m_w_ukv, m_w_out, m_final_g, v_norm_g, v_w_in, v_q_norm_g, v_kv_norm_g, v_w_uq, v_w_ukv, v_w_out, v_final_g):
    S = x.shape[1]
    xs = x.reshape(S, D_MODEL)
    target = loss_target.reshape(S, D_MODEL)

    wb = [t.astype(BF16) for t in (w_in, w_uq, w_ukv, w_out)]
    first = _full_weights(*_exchange([t[0:1] for t in wb], True, "gather_weights0"))
    ws = [_layer_weights(*(t[0] for t in first))]
    mla, dil = _rope_tables(S)

    saved = []
    h = xs
    for l in range(DEPTH):
        comm = ([t[1:] for t in wb], True) if l == 0 else None
        h, sv, received = _layer_fwd(h, ws[l], norm_g[l], q_norm_g[l], kv_norm_g[l], mla, dil, l, comm)
        saved.append(sv)
        if l == 0:
            rest = _full_weights(*received)
            ws += [_layer_weights(*(t[i] for t in rest)) for i in range(DEPTH - 1)]
    dx, g_final, loss_row = _loss_head(h, target, final_g, "loss_head")
    g_norm, g_qn, g_kvn, parts_l = [None] * DEPTH, [None] * DEPTH, [None] * DEPTH, [None] * DEPTH
    blocks = None
    for l in reversed(range(DEPTH)):
        comm = (blocks, False) if blocks is not None else None
        dx, g, g_norm[l], g_qn[l], g_kvn[l], received = _layer_bwd(dx, saved[l], ws[l], norm_g[l], q_norm_g[l], kv_norm_g[l], mla, dil, l, comm)
        if blocks is not None:
            parts_l[l + 1] = received
        blocks = _grad_blocks(*_unpad_grads(g))
    parts_l[0] = _exchange(blocks, False, "exchange_grads0")

    parts = [jnp.stack([parts_l[l][t] for l in range(DEPTH)], axis=1) for t in range(4)]
    sh = []
    for t, (pt, w, m, v) in enumerate(zip(parts, (w_in, w_uq, w_ukv, w_out), (m_w_in, m_w_uq, m_w_ukv, m_w_out), (v_w_in, v_w_uq, v_w_ukv, v_w_out))):
        two = lambda a: a.reshape(-1, a.shape[-1])
        outs = _adamw(pt.reshape(N_DEV, -1, pt.shape[-1]), two(w), two(m), two(v), f"adamw_{t}")
        sh.append([o.reshape(w.shape) for o in outs])

    small = _pack_small(jnp.concatenate(g_norm, 0), jnp.concatenate(g_qn, 0), jnp.concatenate(g_kvn, 0), g_final, loss_row)
    (small_parts,) = _exchange([small], True, "gather_small")
    souts = _adamw(small_parts, _pack_small(norm_g, q_norm_g, kv_norm_g, final_g), _pack_small(m_norm_g, m_q_norm_g, m_kv_norm_g, m_final_g),
                   _pack_small(v_norm_g, v_q_norm_g, v_kv_norm_g, v_final_g), "adamw_small")
    (g_sm, loss), (d_sm, _), (m_sm, _), (v_sm, _) = (_unpack_small(t) for t in souts)

    def order(sm, k):
        return (sm[0], sh[0][k], sm[1], sm[2], sh[1][k], sh[2][k], sh[3][k], sm[3])

    return (loss, dx.reshape(1, S, D_MODEL), *order(g_sm, 0), *order(d_sm, 1), *order(m_sm, 2), *order(v_sm, 3))
```

```python
import functools
import math

import jax
import jax.numpy as jnp
from jax import lax
from jax.experimental import pallas as pl
from jax.experimental.pallas import tpu as pltpu

F32 = jnp.float32
BF16 = jnp.bfloat16

D_MODEL = 1024
DEPTH = 4
HEADS = 8
NOPE = 64
ROPE = 32
VDIM = 64
Q_LORA = 384
KV_LORA = 256
DIL_PAIRS = ((128, 1), (512, 4), (2048, 16))
DIL_HD = 64
DIL_W = 512
ROT = 16
HALF = 64
THETA = 500000.0
EPS = 1e-6
IN_WIDTH = 6304
N_DEV = 8

LANE = 128
CB_GA, CB_DIL, CB_GB, CB_CKV, CB_KR, CB_CQ = 0, 8, 44, 48, 50, 51
NP = 54 * LANE
HP = HEADS * LANE

ADAM_LR = 0.001
ADAM_B1 = 0.9
ADAM_B2 = 0.999
ADAM_EPS = 1e-08
ADAM_WD = 0.01
ADAM_STEP = 10

VMEM_LIMIT = 48 * 1024 * 1024
ROW_TILE = 512
SUB = 128

_NT = (((1,), (1,)), ((), ()))
_NN = (((1,), (0,)), ((), ()))
_TN = (((0,), (0,)), ((), ()))


def _params(n_axes):
    return pltpu.CompilerParams(dimension_semantics=("arbitrary",) * n_axes, vmem_limit_bytes=VMEM_LIMIT)


def _pick(n, cands):
    for c in cands:
        if n % c == 0:
            return c
    raise ValueError(f"no tile for {n}")


def _mm(a, b, mode, name, out_dtype=F32, res=None):
    if mode == "nn":
        (M, K), (K2, N) = a.shape, b.shape
    elif mode == "nt":
        (M, K), (N, K2) = a.shape, b.shape
    else:
        (K, M), (K2, N) = a.shape, b.shape
    assert K == K2, (a.shape, b.shape, mode)
    tm = _pick(M, (1024, 512, 384, 256, 128))
    tn = _pick(N, (1152, 1024, 768, 640, 512, 384, 256, 128))
    tk = _pick(K, (1152, 1024, 768, 640, 512, 384, 256, 128))
    nk = K // tk
    dims = {"nn": _NN, "nt": _NT, "tn": _TN}[mode]

    def body(*refs):
        if res is not None:
            a_ref, b_ref, r_ref, o_ref = refs[:4]
        else:
            a_ref, b_ref, o_ref = refs[:3]
            r_ref = None
        part = lax.dot_general(a_ref[...].astype(BF16), b_ref[...].astype(BF16), dims, preferred_element_type=F32)

        def finish(acc):
            if r_ref is not None:
                acc = acc + r_ref[...]
            o_ref[...] = acc.astype(out_dtype)

        if nk == 1:
            finish(part)
        else:
            acc_ref = refs[-1]
            k = pl.program_id(2)

            @pl.when(k == 0)
            def _():
                acc_ref[...] = part

            @pl.when(k > 0)
            def _():
                acc_ref[...] += part

            @pl.when(k == nk - 1)
            def _():
                finish(acc_ref[...])

    if mode == "nn":
        a_spec = pl.BlockSpec((tm, tk), lambda i, j, k: (i, k))
        b_spec = pl.BlockSpec((tk, tn), lambda i, j, k: (k, j))
    elif mode == "nt":
        a_spec = pl.BlockSpec((tm, tk), lambda i, j, k: (i, k))
        b_spec = pl.BlockSpec((tn, tk), lambda i, j, k: (j, k))
    else:
        a_spec = pl.BlockSpec((tk, tm), lambda i, j, k: (k, i))
        b_spec = pl.BlockSpec((tk, tn), lambda i, j, k: (k, j))
    o_spec = pl.BlockSpec((tm, tn), lambda i, j, k: (i, j))
    in_specs = [a_spec, b_spec] + ([o_spec] if res is not None else [])
    args = (a, b) + ((res,) if res is not None else ())
    return pl.pallas_call(
        body, name=name, grid=(M // tm, N // tn, nk), out_shape=jax.ShapeDtypeStruct((M, N), out_dtype),
        in_specs=in_specs, out_specs=o_spec,
        scratch_shapes=[pltpu.VMEM((tm, tn), F32)] if nk > 1 else [],
        compiler_params=_params(3),
    )(*args)


def _rms_fwd(src, cb, width, g, name):
    S = src.shape[0]
    tm = ROW_TILE

    def body(x_ref, g_ref, o_ref):
        x = x_ref[...]
        r = lax.rsqrt(jnp.mean(x * x, axis=-1, keepdims=True) + EPS)
        o_ref[...] = (x * r * g_ref[...]).astype(BF16)

    return pl.pallas_call(
        body, name=name, grid=(S // tm,), out_shape=jax.ShapeDtypeStruct((S, width), BF16),
        in_specs=[pl.BlockSpec((tm, width), lambda i: (i, cb)), pl.BlockSpec((1, width), lambda i: (0, 0))],
        out_specs=pl.BlockSpec((tm, width), lambda i: (i, 0)), compiler_params=_params(1),
    )(src, g.reshape(1, width))


def _rms_bwd(src, cb, width, dy, g, name, out_dtype, res=None):
    S = src.shape[0]
    tm = ROW_TILE

    def body(*refs):
        if res is not None:
            x_ref, dy_ref, g_ref, r_ref, dx_ref, dg_ref = refs
        else:
            x_ref, dy_ref, g_ref, dx_ref, dg_ref = refs
            r_ref = None
        x = x_ref[...]
        dy = dy_ref[...]
        r = lax.rsqrt(jnp.mean(x * x, axis=-1, keepdims=True) + EPS)
        dyg = dy * g_ref[...]
        c = jnp.mean(dyg * x, axis=-1, keepdims=True)
        dx = r * dyg - x * (r * r * r) * c
        if r_ref is not None:
            dx = dx + r_ref[...]
        dx_ref[...] = dx.astype(out_dtype)
        part = jnp.sum(dy * x * r, axis=0, keepdims=True)

        @pl.when(pl.program_id(0) == 0)
        def _():
            dg_ref[...] = part

        @pl.when(pl.program_id(0) > 0)
        def _():
            dg_ref[...] += part

    row = pl.BlockSpec((tm, width), lambda i: (i, 0))
    in_specs = [pl.BlockSpec((tm, width), lambda i: (i, cb)), row, pl.BlockSpec((1, width), lambda i: (0, 0))]
    args = [src, dy, g.reshape(1, width)]
    if res is not None:
        in_specs.append(row)
        args.append(res)
    return pl.pallas_call(
        body, name=name, grid=(S // tm,),
        out_shape=(jax.ShapeDtypeStruct((S, width), out_dtype), jax.ShapeDtypeStruct((1, width), F32)),
        in_specs=in_specs, out_specs=(row, pl.BlockSpec((1, width), lambda i: (0, 0))),
        compiler_params=_params(1),
    )(*args)


def _rot(x, c, s1, s2, h):
    return x * c + pltpu.roll(x, x.shape[1] - h, 1) * s1 + pltpu.roll(x, h, 1) * s2


def _rope_tables(S):
    def tables(dim):
        inv = 1.0 / (THETA ** (jnp.arange(0, dim, 2, dtype=F32) / dim))
        ang = jnp.arange(S, dtype=F32)[:, None] * inv[None, :]
        return jnp.cos(ang), jnp.sin(ang)

    cm, sm = tables(ROPE)
    cd, sd = tables(ROT)
    z = lambda n: jnp.zeros((S, n), F32)
    o = lambda n: jnp.ones((S, n), F32)
    mla = dict(
        c_q=jnp.concatenate([o(64), cm, cm, z(32)], 1),
        c_kr=jnp.concatenate([z(64), cm, cm, z(32)], 1),
        s1=jnp.concatenate([z(64), -sm, z(16), z(32)], 1),
        s2=jnp.concatenate([z(64), z(16), sm, z(32)], 1),
    )
    one = lambda a, b, c: jnp.concatenate([a, b, c, a, b, c], 1)
    dil = dict(c=one(cd, cd, o(48)), s1=one(-sd, z(8), z(48)), s2=one(z(8), sd, z(48)))
    return mla, dil


def _rope_heads(src, c, s1, s2, name, out_dtype):
    S = src.shape[0]
    tm = ROW_TILE

    def body(x_ref, c_ref, s1_ref, s2_ref, o_ref):
        cv, s1v, s2v = c_ref[...], s1_ref[...], s2_ref[...]
        for h in range(HEADS):
            sl = slice(h * LANE, (h + 1) * LANE)
            o_ref[:, sl] = _rot(x_ref[:, sl], cv, s1v, s2v, ROPE // 2).astype(out_dtype)

    tab = pl.BlockSpec((tm, LANE), lambda i: (i, 0))
    wide = pl.BlockSpec((tm, HP), lambda i: (i, 0))
    return pl.pallas_call(
        body, name=name, grid=(S // tm,), out_shape=jax.ShapeDtypeStruct((S, HP), out_dtype),
        in_specs=[wide, tab, tab, tab], out_specs=wide, compiler_params=_params(1),
    )(src, c, s1, s2)


def _k_assemble(kpre, p, mla, name):
    S = kpre.shape[0]
    tm = ROW_TILE

    def body(k_ref, kr_ref, c_ref, s1_ref, s2_ref, o_ref):
        r = _rot(kr_ref[...], c_ref[...], s1_ref[...], s2_ref[...], ROPE // 2)
        for h in range(HEADS):
            sl = slice(h * LANE, (h + 1) * LANE)
            o_ref[:, sl] = (k_ref[:, sl] + r).astype(BF16)

    tab = pl.BlockSpec((tm, LANE), lambda i: (i, 0))
    wide = pl.BlockSpec((tm, HP), lambda i: (i, 0))
    return pl.pallas_call(
        body, name=name, grid=(S // tm,), out_shape=jax.ShapeDtypeStruct((S, HP), BF16),
        in_specs=[wide, pl.BlockSpec((tm, LANE), lambda i: (i, CB_KR)), tab, tab, tab],
        out_specs=wide, compiler_params=_params(1),
    )(kpre, p, mla["c_kr"], mla["s1"], mla["s2"])


def _kr_bwd(dk, mla, name):
    S = dk.shape[0]
    tm = ROW_TILE

    def body(dk_ref, c_ref, s1_ref, s2_ref, o_ref):
        t = dk_ref[:, 0:LANE]
        for h in range(1, HEADS):
            t = t + dk_ref[:, h * LANE:(h + 1) * LANE]
        lane = lax.broadcasted_iota(jnp.int32, (1, LANE), 1)
        t = jnp.where((lane >= NOPE) & (lane < NOPE + ROPE), t, 0.0)
        o_ref[...] = _rot(t, c_ref[...], -s1_ref[...], -s2_ref[...], ROPE // 2).astype(BF16)

    tab = pl.BlockSpec((tm, LANE), lambda i: (i, 0))
    return pl.pallas_call(
        body, name=name, grid=(S // tm,), out_shape=jax.ShapeDtypeStruct((S, LANE), BF16),
        in_specs=[pl.BlockSpec((tm, HP), lambda i: (i, 0)), tab, tab, tab],
        out_specs=tab, compiler_params=_params(1),
    )(dk, mla["c_kr"], mla["s1"], mla["s2"])


def _dil_prep(p, dil, g, name):
    S = p.shape[0]
    tm = ROW_TILE
    first = CB_DIL * LANE // DIL_W + 3 * g

    def body(x_ref, c_ref, s1_ref, s2_ref, o_ref):
        t = pl.program_id(1)
        rep = DIL_W // LANE

        def roped():
            return _rot(x_ref[...], jnp.tile(c_ref[...], (1, rep)), jnp.tile(s1_ref[...], (1, rep)),
                        jnp.tile(s2_ref[...], (1, rep)), ROT // 2)

        @pl.when(t == 0)
        def _():
            o_ref[...] = (roped() * (DIL_HD ** -0.5)).astype(BF16)

        @pl.when(t == 1)
        def _():
            o_ref[...] = roped().astype(BF16)

        @pl.when(t == 2)
        def _():
            o_ref[...] = x_ref[...].astype(BF16)

    tab = pl.BlockSpec((tm, LANE), lambda i, j: (i, 0))
    return pl.pallas_call(
        body, name=name, grid=(S // tm, 3), out_shape=jax.ShapeDtypeStruct((S, 3 * DIL_W), BF16),
        in_specs=[pl.BlockSpec((tm, DIL_W), lambda i, j: (i, first + j)), tab, tab, tab],
        out_specs=pl.BlockSpec((tm, DIL_W), lambda i, j: (i, j)), compiler_params=_params(2),
    )(p, dil["c"], dil["s1"], dil["s2"])


def _grid_ends(dims):
    ids = [pl.program_id(a) for a in range(len(dims))]
    first = functools.reduce(jnp.logical_and, [i == 0 for i in ids])
    last = functools.reduce(jnp.logical_and, [i == n - 1 for i, n in zip(ids, dims)])
    return first, last


def _flash_fwd(q, k, v, name, comm=None):
    S = q.shape[0]
    tq = tk = _pick(S, (1024, 512))
    nk = S // tk
    c2 = (NOPE + ROPE) ** -0.5 * math.log2(math.e)
    srcs, same_src = comm if comm is not None else ([], True)
    n = len(srcs)
    grid = (HEADS, S // tq, nk)

    def body(*refs):
        q_ref, k_ref, v_ref = refs[:3]
        o_ref, lse_ref = refs[3 + n:5 + n]
        m_s, acc_s = refs[5 + 2 * n:7 + 2 * n]
        ex = (refs[3:3 + n], refs[5 + n:5 + 2 * n], *refs[7 + 2 * n:], same_src)
        j = pl.program_id(2)
        if n:
            first, last = _grid_ends(grid)
            pl.when(first)(lambda: _exchange_start(*ex))

        @pl.when(j == 0)
        def _():
            m_s[...] = jnp.full(m_s.shape, -jnp.inf, F32)
            acc_s[...] = jnp.zeros(acc_s.shape, F32)

        lane = lax.broadcasted_iota(jnp.int32, (1, LANE), 1)
        vv = jnp.where(lane == VDIM, jnp.ones((), BF16), v_ref[...])
        t = lax.dot_general(q_ref[...], k_ref[...], _NT, preferred_element_type=F32) * c2
        m_prev = m_s[...]
        m_new = jnp.maximum(m_prev, jnp.max(t, axis=-1, keepdims=True))
        alpha = jnp.exp2(m_prev - m_new)
        e = jnp.exp2(t - jnp.tile(m_new, (1, tk // LANE)))
        acc_s[...] = alpha * acc_s[...] + jnp.dot(e.astype(BF16), vv, preferred_element_type=F32)
        m_s[...] = m_new

        @pl.when(j == nk - 1)
        def _():
            acc = acc_s[...]
            l = acc[:, VDIM:VDIM + 1]
            o_ref[...] = jnp.where(lane < VDIM, acc / l, 0.0)
            lse_ref[...] = (m_s[...] + jnp.log2(l)) * math.log(2.0)

        if n:
            pl.when(last)(lambda: _exchange_wait(*ex))

    qs = pl.BlockSpec((tq, LANE), lambda h, i, j: (i, h))
    ks = pl.BlockSpec((tk, LANE), lambda h, i, j: (j, h))
    hbm = pl.BlockSpec(memory_space=pltpu.HBM)
    outs = pl.pallas_call(
        body, name=name, grid=grid,
        out_shape=(jax.ShapeDtypeStruct((S, HP), F32), jax.ShapeDtypeStruct((S, HP), F32)) + _exchange_shapes(srcs, same_src),
        in_specs=[qs, ks, ks] + [hbm] * n, out_specs=(qs, qs) + (hbm,) * n,
        scratch_shapes=[pltpu.VMEM((tq, LANE), F32), pltpu.VMEM((tq, LANE), F32)] + (_exchange_sems(n) if n else []),
        compiler_params=_params(3),
    )(q, k, v, *srcs)
    return outs[0], outs[1], list(outs[2:])


def _flash_bwd(q, k, v, o, do, lse, name, comm=None):
    S = q.shape[0]
    tq = tk = _pick(S, (1024, 512))
    nq = S // tq
    scale = (NOPE + ROPE) ** -0.5
    srcs, same_src = comm if comm is not None else ([], True)
    n = len(srcs)
    grid = (HEADS, S // tk, nq)

    def body(*refs):
        q_ref, k_ref, v_ref, o_ref, do_ref, lse_ref = refs[:6]
        dq_ref, dk_ref, dv_ref = refs[6 + n:9 + n]
        dk_s, dv_s = refs[9 + 2 * n:11 + 2 * n]
        ex = (refs[6:6 + n], refs[9 + n:9 + 2 * n], *refs[11 + 2 * n:], same_src)
        j = pl.program_id(1)
        i = pl.program_id(2)
        if n:
            first, last = _grid_ends(grid)
            pl.when(first)(lambda: _exchange_start(*ex))
        qv = q_ref[...]
        kv = k_ref[...]
        do = do_ref[...]
        dob = do.astype(BF16)
        lse_row = jnp.transpose(lse_ref[...])[0:1, :]
        delta_row = jnp.sum(jnp.transpose(do * o_ref[...]), axis=0, keepdims=True)
        st = lax.dot_general(kv, qv, _NT, preferred_element_type=F32) * scale
        pt = jnp.exp(st - lse_row)
        dpt = lax.dot_general(v_ref[...], dob, _NT, preferred_element_type=F32)
        dst = (pt * (dpt - delta_row)).astype(BF16)
        dv_part = jnp.dot(pt.astype(BF16), dob, preferred_element_type=F32)
        dk_part = jnp.dot(dst, qv, preferred_element_type=F32)
        dq_part = jnp.transpose(jnp.dot(jnp.transpose(kv), dst, preferred_element_type=F32)) * scale

        @pl.when(i == 0)
        def _():
            dk_s[...] = dk_part
            dv_s[...] = dv_part

        @pl.when(i > 0)
        def _():
            dk_s[...] += dk_part
            dv_s[...] += dv_part

        rows = pl.ds(pl.multiple_of(i * tq, tq), tq)

        @pl.when(j == 0)
        def _():
            dq_ref[rows, :] = dq_part

        @pl.when(j > 0)
        def _():
            dq_ref[rows, :] += dq_part

        @pl.when(i == nq - 1)
        def _():
            dk_ref[...] = dk_s[...] * scale
            dv_ref[...] = dv_s[...].astype(BF16)

        if n:
            pl.when(last)(lambda: _exchange_wait(*ex))

    qs = pl.BlockSpec((tq, LANE), lambda h, j, i: (i, h))
    ks = pl.BlockSpec((tk, LANE), lambda h, j, i: (j, h))
    hbm = pl.BlockSpec(memory_space=pltpu.HBM)
    outs = pl.pallas_call(
        body, name=name, grid=grid,
        out_shape=(jax.ShapeDtypeStruct((S, HP), F32), jax.ShapeDtypeStruct((S, HP), F32), jax.ShapeDtypeStruct((S, HP), BF16))
        + _exchange_shapes(srcs, same_src),
        in_specs=[qs, ks, ks, qs, qs, qs] + [hbm] * n,
        out_specs=(pl.BlockSpec((S, LANE), lambda h, j, i: (0, h)), ks, ks) + (hbm,) * n,
        scratch_shapes=[pltpu.VMEM((tk, LANE), F32), pltpu.VMEM((tk, LANE), F32)] + (_exchange_sems(n) if n else []),
        compiler_params=_params(3),
    )(q, k, v, o, do, lse, *srcs)
    return outs[0], outs[1], outs[2], list(outs[3:])


def _band_tiles(L):
    tq = min(512, L)
    return tq, tq // SUB, tq // HALF, L // HALF


def _halo_specs(tq, rpb, n64, col):
    prev = pl.BlockSpec((HALF, DIL_W), lambda r, i: (jnp.maximum(rpb * i - 1, 0), col(r)))
    cur = pl.BlockSpec((tq, DIL_W), lambda r, i: (i, col(r)))
    nxt = pl.BlockSpec((HALF, DIL_W), lambda r, i: (jnp.minimum(rpb * i + rpb, n64 - 1), col(r)))
    return [prev, cur, nxt]


def _fill(buf, prev_ref, cur_ref, next_ref, tq):
    buf[0:HALF, :] = prev_ref[...]
    buf[HALF:HALF + tq, :] = cur_ref[...]
    buf[HALF + tq:HALF + tq + HALF, :] = next_ref[...]


def _lo_lanes():
    return lax.broadcasted_iota(jnp.int32, (1, LANE), 1) < DIL_HD


def _stack_heads(x, lo):
    zero = jnp.zeros_like(x)
    return jnp.concatenate([jnp.where(lo, x, zero), jnp.where(lo, zero, x)], axis=0)


def _stack_cols(x):
    return jnp.concatenate([x[:, 0:1], x[:, DIL_HD:DIL_HD + 1]], axis=0)


def _band_valid(q0, k0, nq, nk, L, bound_q):
    qpos = q0 + lax.broadcasted_iota(jnp.int32, (nq, 1), 0)
    kpos = k0 + lax.broadcasted_iota(jnp.int32, (1, nk), 1)
    side = qpos if bound_q else kpos
    return (jnp.abs(qpos - kpos) <= HALF) & (side >= 0) & (side < L)


def _band_fwd(dilr, d, name):
    S = dilr.shape[0]
    L = S // d
    tq, nsub, rpb, n64 = _band_tiles(L)
    view = dilr.reshape(L, d * dilr.shape[1])
    win = SUB + 2 * HALF

    def body(q_ref, kp_ref, kc_ref, kn_ref, vp_ref, vc_ref, vn_ref, o_ref, lse_ref, kbuf, vbuf):
        i = pl.program_id(1)
        _fill(kbuf, kp_ref, kc_ref, kn_ref, tq)
        _fill(vbuf, vp_ref, vc_ref, vn_ref, tq)
        lo = _lo_lanes()
        for a in range(nsub):
            r0 = a * SUB
            rows = slice(r0, r0 + SUB)
            valid = _band_valid(i * tq + r0, i * tq + r0 - HALF, SUB, win, L, False)
            valid2 = jnp.concatenate([valid, valid], axis=0)
            for hp in range(4):
                cs = slice(hp * LANE, (hp + 1) * LANE)
                q = q_ref[rows, cs]
                kw = kbuf[r0:r0 + win, cs]
                vw = vbuf[r0:r0 + win, cs]
                q2 = _stack_heads(q, lo)
                s = lax.dot_general(q2, kw, _NT, preferred_element_type=F32)
                s = jnp.where(valid2, s, -jnp.inf)
                m = jnp.max(s, axis=-1, keepdims=True)
                e = jnp.exp(s - m)
                l = jnp.sum(e, axis=-1, keepdims=True)
                o2 = jnp.dot(e.astype(BF16), vw, preferred_element_type=F32) / l
                lse2 = m + jnp.log(l)
                o_ref[rows, cs] = jnp.where(lo, o2[:SUB], o2[SUB:])
                lse_ref[rows, cs] = jnp.where(lo, lse2[:SUB], lse2[SUB:])

    out_spec = pl.BlockSpec((tq, DIL_W), lambda r, i: (i, r))
    o, lse = pl.pallas_call(
        body, name=name, grid=(d, L // tq),
        out_shape=(jax.ShapeDtypeStruct((L, d * DIL_W), F32), jax.ShapeDtypeStruct((L, d * DIL_W), F32)),
        in_specs=[pl.BlockSpec((tq, DIL_W), lambda r, i: (i, r * 3))]
        + _halo_specs(tq, rpb, n64, lambda r: r * 3 + 1) + _halo_specs(tq, rpb, n64, lambda r: r * 3 + 2),
        out_specs=(out_spec, out_spec),
        scratch_shapes=[pltpu.VMEM((tq + 2 * HALF, DIL_W), BF16), pltpu.VMEM((tq + 2 * HALF, DIL_W), BF16)],
        compiler_params=_params(2),
    )(view, view, view, view, view, view, view)
    return o.reshape(S, DIL_W), lse.reshape(S, DIL_W)


def _band_bwd_q(dilr, d, do, lse, dlt, dil, name):
    S = dilr.shape[0]
    L = S // d
    tq, nsub, rpb, n64 = _band_tiles(L)
    view = dilr.reshape(L, d * dilr.shape[1])
    v4 = lambda t: t.reshape(L, d * DIL_W)
    tv = lambda t: t.reshape(L, d * LANE)
    win = SUB + 2 * HALF

    def body(q_ref, kp_ref, kc_ref, kn_ref, vp_ref, vc_ref, vn_ref, do_ref, lse_ref, dlt_ref, c_ref, s1_ref, s2_ref, dq_ref,
             kbuf, vbuf):
        i = pl.program_id(1)
        _fill(kbuf, kp_ref, kc_ref, kn_ref, tq)
        _fill(vbuf, vp_ref, vc_ref, vn_ref, tq)
        lo = _lo_lanes()
        for a in range(nsub):
            r0 = a * SUB
            rows = slice(r0, r0 + SUB)
            valid = _band_valid(i * tq + r0, i * tq + r0 - HALF, SUB, win, L, False)
            valid2 = jnp.concatenate([valid, valid], axis=0)
            cv, s1v, s2v = c_ref[rows, :], s1_ref[rows, :], s2_ref[rows, :]
            for hp in range(4):
                cs = slice(hp * LANE, (hp + 1) * LANE)
                kw = kbuf[r0:r0 + win, cs]
                vw = vbuf[r0:r0 + win, cs]
                q2 = _stack_heads(q_ref[rows, cs], lo)
                do2 = _stack_heads(do_ref[rows, cs], lo).astype(BF16)
                s = lax.dot_general(q2, kw, _NT, preferred_element_type=F32)
                p = jnp.where(valid2, jnp.exp(s - _stack_cols(lse_ref[rows, cs])), 0.0)
                dp = lax.dot_general(do2, vw, _NT, preferred_element_type=F32)
                ds = (p * (dp - _stack_cols(dlt_ref[rows, cs]))).astype(BF16)
                dq2 = jnp.dot(ds, kw, preferred_element_type=F32)
                dq = jnp.where(lo, dq2[:SUB], dq2[SUB:])
                dq_ref[rows, cs] = (_rot(dq, cv, -s1v, -s2v, ROT // 2) * (DIL_HD ** -0.5)).astype(BF16)

    row = pl.BlockSpec((tq, DIL_W), lambda r, i: (i, r))
    tab = pl.BlockSpec((tq, LANE), lambda r, i: (i, r))
    dq = pl.pallas_call(
        body, name=name, grid=(d, L // tq), out_shape=jax.ShapeDtypeStruct((L, d * DIL_W), BF16),
        in_specs=[pl.BlockSpec((tq, DIL_W), lambda r, i: (i, r * 3))]
        + _halo_specs(tq, rpb, n64, lambda r: r * 3 + 1) + _halo_specs(tq, rpb, n64, lambda r: r * 3 + 2)
        + [row, row, row, tab, tab, tab],
        out_specs=row,
        scratch_shapes=[pltpu.VMEM((tq + 2 * HALF, DIL_W), BF16), pltpu.VMEM((tq + 2 * HALF, DIL_W), BF16)],
        compiler_params=_params(2),
    )(view, view, view, view, view, view, view, v4(do), v4(lse), v4(dlt), tv(dil["c"]), tv(dil["s1"]), tv(dil["s2"]))
    return dq.reshape(S, DIL_W)


def _band_bwd_kv(dilr, d, do, lse, dlt, dil, name):
    S = dilr.shape[0]
    L = S // d
    tq, nsub, rpb, n64 = _band_tiles(L)
    view = dilr.reshape(L, d * dilr.shape[1])
    v4 = lambda t: t.reshape(L, d * DIL_W)
    tv = lambda t: t.reshape(L, d * LANE)
    win = SUB + 2 * HALF

    def body(k_ref, v_ref, qp_ref, qc_ref, qn_ref, dop_ref, doc_ref, don_ref, lp_ref, lc_ref, ln_ref, tp_ref, tc_ref, tn_ref,
             c_ref, s1_ref, s2_ref, dk_ref, dv_ref, qbuf, dobuf, lbuf, tbuf):
        j = pl.program_id(1)
        _fill(qbuf, qp_ref, qc_ref, qn_ref, tq)
        _fill(dobuf, dop_ref, doc_ref, don_ref, tq)
        _fill(lbuf, lp_ref, lc_ref, ln_ref, tq)
        _fill(tbuf, tp_ref, tc_ref, tn_ref, tq)
        lo = _lo_lanes()
        for a in range(nsub):
            r0 = a * SUB
            rows = slice(r0, r0 + SUB)
            wrows = slice(r0, r0 + win)
            valid = _band_valid(j * tq + r0 - HALF, j * tq + r0, win, SUB, L, True)
            valid2 = jnp.concatenate([valid, valid], axis=0)
            cv, s1v, s2v = c_ref[rows, :], s1_ref[rows, :], s2_ref[rows, :]
            for hp in range(4):
                cs = slice(hp * LANE, (hp + 1) * LANE)
                k = k_ref[rows, cs]
                v = v_ref[rows, cs]
                q2 = _stack_heads(qbuf[wrows, cs], lo)
                do2 = _stack_heads(dobuf[wrows, cs], lo).astype(BF16)
                s = lax.dot_general(q2, k, _NT, preferred_element_type=F32)
                p = jnp.where(valid2, jnp.exp(s - _stack_cols(lbuf[wrows, cs])), 0.0)
                dv = lax.dot_general(p.astype(BF16), do2, _TN, preferred_element_type=F32)
                dp = lax.dot_general(do2, v, _NT, preferred_element_type=F32)
                ds = (p * (dp - _stack_cols(tbuf[wrows, cs]))).astype(BF16)
                dk = lax.dot_general(ds, q2, _TN, preferred_element_type=F32)
                dk_ref[rows, cs] = _rot(dk, cv, -s1v, -s2v, ROT // 2).astype(BF16)
                dv_ref[rows, cs] = dv.astype(BF16)

    row = pl.BlockSpec((tq, DIL_W), lambda r, i: (i, r))
    tab = pl.BlockSpec((tq, LANE), lambda r, i: (i, r))
    halo = _halo_specs(tq, rpb, n64, lambda r: r)
    hb = tq + 2 * HALF
    dk, dv = pl.pallas_call(
        body, name=name, grid=(d, L // tq),
        out_shape=(jax.ShapeDtypeStruct((L, d * DIL_W), BF16), jax.ShapeDtypeStruct((L, d * DIL_W), BF16)),
        in_specs=[pl.BlockSpec((tq, DIL_W), lambda r, i: (i, r * 3 + 1)), pl.BlockSpec((tq, DIL_W), lambda r, i: (i, r * 3 + 2))]
        + _halo_specs(tq, rpb, n64, lambda r: r * 3) + halo + halo + halo + [tab, tab, tab],
        out_specs=(row, row),
        scratch_shapes=[pltpu.VMEM((hb, DIL_W), BF16), pltpu.VMEM((hb, DIL_W), F32), pltpu.VMEM((hb, DIL_W), F32), pltpu.VMEM((hb, DIL_W), F32)],
        compiler_params=_params(2),
    )(view, view, view, view, view, v4(do), v4(do), v4(do), v4(lse), v4(lse), v4(lse), v4(dlt), v4(dlt), v4(dlt),
      tv(dil["c"]), tv(dil["s1"]), tv(dil["s2"]))
    return dk.reshape(S, DIL_W), dv.reshape(S, DIL_W)


def _sigmoid(x):
    return 1.0 / (1.0 + jnp.exp(-x))


def _gate_a(o, p, name):
    S = o.shape[0]
    tm = ROW_TILE

    def body(o_ref, g_ref, a_ref):
        g = g_ref[...]
        a_ref[...] = (o_ref[...] * (g * _sigmoid(g))).astype(BF16)

    blk = pl.BlockSpec((tm, HP), lambda i: (i, 0))
    return pl.pallas_call(
        body, name=name, grid=(S // tm,), out_shape=jax.ShapeDtypeStruct((S, HP), BF16),
        in_specs=[blk, pl.BlockSpec((tm, HP), lambda i: (i, CB_GA * LANE // HP))], out_specs=blk, compiler_params=_params(1),
    )(o, p)


def _gate_a_bwd(da, o, p, name):
    S = o.shape[0]
    tm = ROW_TILE

    def body(da_ref, o_ref, g_ref, do_ref, dg_ref):
        g = g_ref[...]
        da = da_ref[...]
        sg = _sigmoid(g)
        do_ref[...] = da * (g * sg)
        dg_ref[...] = (da * o_ref[...] * (sg * (1.0 + g * (1.0 - sg)))).astype(BF16)

    blk = pl.BlockSpec((tm, HP), lambda i: (i, 0))
    return pl.pallas_call(
        body, name=name, grid=(S // tm,),
        out_shape=(jax.ShapeDtypeStruct((S, HP), F32), jax.ShapeDtypeStruct((S, HP), BF16)),
        in_specs=[blk, blk, pl.BlockSpec((tm, HP), lambda i: (i, CB_GA * LANE // HP))], out_specs=(blk, blk), compiler_params=_params(1),
    )(da, o, p)


def _merge_weights(l0, l1, l2):
    mx = jnp.maximum(jnp.maximum(l0, l1), l2)
    e0, e1, e2 = jnp.exp(l0 - mx), jnp.exp(l1 - mx), jnp.exp(l2 - mx)
    den = e0 + e1 + e2
    return e0 / den, e1 / den, e2 / den


def _gate_b(os_, ls_, p, name):
    S = p.shape[0]
    tm = ROW_TILE

    def body(o0, o1, o2, l0, l1, l2, g_ref, b_ref):
        a0, a1, a2 = _merge_weights(l0[...], l1[...], l2[...])
        bm = a0 * o0[...] + a1 * o1[...] + a2 * o2[...]
        g = g_ref[...]
        b_ref[...] = (bm * (g * _sigmoid(g))).astype(BF16)

    blk = pl.BlockSpec((tm, DIL_W), lambda i: (i, 0))
    return pl.pallas_call(
        body, name=name, grid=(S // tm,), out_shape=jax.ShapeDtypeStruct((S, DIL_W), BF16),
        in_specs=[blk] * 6 + [pl.BlockSpec((tm, DIL_W), lambda i: (i, CB_GB * LANE // DIL_W))], out_specs=blk, compiler_params=_params(1),
    )(*os_, *ls_, p)


def _gate_b_bwd(db, os_, ls_, p, name):
    S = p.shape[0]
    tm = ROW_TILE

    def body(db_ref, o0, o1, o2, l0, l1, l2, g_ref, dg_ref, d0, d1, d2, t0, t1, t2):
        a0, a1, a2 = _merge_weights(l0[...], l1[...], l2[...])
        bm = a0 * o0[...] + a1 * o1[...] + a2 * o2[...]
        g = g_ref[...]
        db = db_ref[...]
        sg = _sigmoid(g)
        dbm = db * (g * sg)
        dg_ref[...] = (db * bm * (sg * (1.0 + g * (1.0 - sg)))).astype(BF16)
        prod = dbm * bm
        lo = _lo_lanes()
        parts = []
        for hp in range(DIL_W // LANE):
            pc = prod[:, hp * LANE:(hp + 1) * LANE]
            tl = jnp.sum(jnp.where(lo, pc, 0.0), axis=-1, keepdims=True)
            th = jnp.sum(jnp.where(lo, 0.0, pc), axis=-1, keepdims=True)
            parts.append(jnp.where(lo, tl, th))
        t = jnp.concatenate(parts, axis=1)
        d0[...] = a0 * dbm
        d1[...] = a1 * dbm
        d2[...] = a2 * dbm
        t0[...] = a0 * t
        t1[...] = a1 * t
        t2[...] = a2 * t

    blk = pl.BlockSpec((tm, DIL_W), lambda i: (i, 0))
    f = jax.ShapeDtypeStruct((S, DIL_W), F32)
    outs = pl.pallas_call(
        body, name=name, grid=(S // tm,),
        out_shape=(jax.ShapeDtypeStruct((S, DIL_W), BF16), f, f, f, f, f, f),
        in_specs=[blk] * 7 + [pl.BlockSpec((tm, DIL_W), lambda i: (i, CB_GB * LANE // DIL_W))], out_specs=(blk,) * 7, compiler_params=_params(1),
    )(db, *os_, *ls_, p)
    return outs[0], outs[1:4], outs[4:7]


def _loss_head(x, target, g, name):
    S, D = x.shape
    tm = ROW_TILE

    def body(x_ref, t_ref, g_ref, dx_ref, dg_ref, loss_ref):
        xv = x_ref[...]
        gv = g_ref[...]
        r = lax.rsqrt(jnp.mean(xv * xv, axis=-1, keepdims=True) + EPS)
        xr = xv * r
        err = xr * gv - t_ref[...]
        lpart = 0.5 * jnp.sum(jnp.mean(err * err, axis=-1, keepdims=True), axis=0, keepdims=True)
        dy = err / D
        dyg = dy * gv
        c = jnp.mean(dyg * xv, axis=-1, keepdims=True)
        dx_ref[...] = r * dyg - xv * (r * r * r) * c
        gpart = jnp.sum(dy * xr, axis=0, keepdims=True)

        @pl.when(pl.program_id(0) == 0)
        def _():
            dg_ref[...] = gpart
            loss_ref[...] = jnp.broadcast_to(lpart, loss_ref.shape)

        @pl.when(pl.program_id(0) > 0)
        def _():
            dg_ref[...] += gpart
            loss_ref[...] += jnp.broadcast_to(lpart, loss_ref.shape)

    row = pl.BlockSpec((tm, D), lambda i: (i, 0))
    vec = pl.BlockSpec((1, D), lambda i: (0, 0))
    return pl.pallas_call(
        body, name=name, grid=(S // tm,),
        out_shape=(jax.ShapeDtypeStruct((S, D), F32), jax.ShapeDtypeStruct((1, D), F32), jax.ShapeDtypeStruct((1, D), F32)),
        in_specs=[row, row, vec], out_specs=(row, vec, vec), compiler_params=_params(1),
    )(x, target, g.reshape(1, D))


def _adamw(parts, w, m, v, name):
    R, C = w.shape
    tr = _pick(R, (128, 64, 32, 16, 8))

    def body(p_ref, w_ref, m_ref, v_ref, g_ref, d_ref, nm_ref, nv_ref):
        g = p_ref[0].astype(F32)
        for k in range(1, N_DEV):
            g = g + p_ref[k].astype(F32)
        m2 = ADAM_B1 * m_ref[...] + (1.0 - ADAM_B1) * g
        v2 = ADAM_B2 * v_ref[...] + (1.0 - ADAM_B2) * (g * g)
        m_hat = m2 / (1.0 - ADAM_B1 ** ADAM_STEP)
        v_hat = v2 / (1.0 - ADAM_B2 ** ADAM_STEP)
        g_ref[...] = g
        d_ref[...] = -ADAM_LR * (m_hat / (jnp.sqrt(v_hat) + ADAM_EPS) + ADAM_WD * w_ref[...])
        nm_ref[...] = m2
        nv_ref[...] = v2

    blk = pl.BlockSpec((tr, C), lambda i: (i, 0))
    f = jax.ShapeDtypeStruct((R, C), F32)
    return pl.pallas_call(
        body, name=name, grid=(R // tr,), out_shape=(f, f, f, f),
        in_specs=[pl.BlockSpec((N_DEV, tr, C), lambda i: (0, i, 0)), blk, blk, blk], out_specs=(blk,) * 4,
        compiler_params=_params(1),
    )(parts, w, m, v)


def _exchange_copies(src_refs, out_refs, send_sems, recv_sems, local_sems, same_src):
    n = len(src_refs)
    x, y, c = lax.axis_index("x"), lax.axis_index("y"), lax.axis_index("c")
    me = 4 * x + 2 * y + c

    def block(t, j):
        return src_refs[t] if same_src else src_refs[t].at[j]

    local = [pltpu.make_async_copy(block(t, me), out_refs[t].at[me], local_sems.at[t]) for t in range(n)]
    remote = []
    for k in range(1, N_DEV):
        px = 1 - x if (k >> 2) & 1 else x
        py = 1 - y if (k >> 1) & 1 else y
        pc = 1 - c if k & 1 else c
        for t in range(n):
            remote.append(pltpu.make_async_remote_copy(
                src_ref=block(t, 4 * px + 2 * py + pc), dst_ref=out_refs[t].at[me],
                send_sem=send_sems.at[(k - 1) * n + t], recv_sem=recv_sems.at[(k - 1) * n + t],
                device_id=(px, py, pc), device_id_type=pl.DeviceIdType.MESH))
    return local, remote


def _exchange_start(*args):
    local, remote = _exchange_copies(*args)
    for cp in local + remote:
        cp.start()


def _exchange_wait(*args):
    local, remote = _exchange_copies(*args)
    for cp in remote:
        cp.wait()
    for cp in local:
        cp.wait()


def _exchange_shapes(srcs, same_src):
    return tuple(jax.ShapeDtypeStruct((N_DEV,) + (tuple(s.shape) if same_src else tuple(s.shape[1:])), s.dtype) for s in srcs)


def _exchange_sems(n):
    return [pltpu.SemaphoreType.DMA(((N_DEV - 1) * n,)), pltpu.SemaphoreType.DMA(((N_DEV - 1) * n,)), pltpu.SemaphoreType.DMA((n,))]


def _exchange(srcs, same_src, name):
    n = len(srcs)

    def body(*refs):
        args = (refs[:n], refs[n:2 * n], *refs[2 * n:], same_src)
        _exchange_start(*args)
        _exchange_wait(*args)

    hbm = pl.BlockSpec(memory_space=pltpu.HBM)
    outs = pl.pallas_call(
        body, name=name, out_shape=_exchange_shapes(srcs, same_src),
        in_specs=[hbm] * n, out_specs=(hbm,) * n, scratch_shapes=_exchange_sems(n),
    )(*srcs)
    return list(outs)


def _full_weights(g_in, g_uq, g_ukv, g_out):
    cat_cols = lambda t: jnp.moveaxis(t, 0, 2).reshape(t.shape[1], t.shape[2], -1)
    return cat_cols(g_in), cat_cols(g_uq), cat_cols(g_ukv), jnp.moveaxis(g_out, 0, 1).reshape(g_out.shape[1], D_MODEL, D_MODEL)


def _grad_blocks(g_in, g_uq, g_ukv, g_out):
    split_cols = lambda t: jnp.moveaxis(t.astype(BF16).reshape(t.shape[0], N_DEV, -1), 1, 0)
    return [split_cols(g_in), split_cols(g_uq), split_cols(g_ukv), g_out.astype(BF16).reshape(N_DEV, D_MODEL // N_DEV, D_MODEL)]


def _layer_weights(w_in, w_uq, w_ukv, w_out):
    z = lambda r, n: jnp.zeros((r, n), BF16)
    c_q, c_kv, k_r = w_in[:, 0:384], w_in[:, 384:640], w_in[:, 640:672]
    gate_a, dil, gate_b = w_in[:, 672:1184], w_in[:, 1184:5792], w_in[:, 5792:6304]
    ga_pad = jnp.pad(gate_a.reshape(D_MODEL, HEADS, VDIM), ((0, 0), (0, 0), (0, LANE - VDIM))).reshape(D_MODEL, HP)
    w_p = jnp.concatenate([ga_pad, dil, gate_b, c_kv, z(D_MODEL, 64), k_r, z(D_MODEL, 32), c_q], axis=1)
    uq = jnp.pad(w_uq.reshape(Q_LORA, HEADS, NOPE + ROPE), ((0, 0), (0, 0), (0, LANE - NOPE - ROPE))).reshape(Q_LORA, HP)
    ukv = w_ukv.reshape(KV_LORA, HEADS, NOPE + VDIM)
    pad64 = lambda t: jnp.pad(t, ((0, 0), (0, 0), (0, LANE - 64))).reshape(KV_LORA, HP)
    uk, uv = pad64(ukv[..., :NOPE]), pad64(ukv[..., NOPE:])
    wa = jnp.pad(w_out[:HEADS * VDIM].reshape(HEADS, VDIM, D_MODEL), ((0, 0), (0, LANE - VDIM), (0, 0))).reshape(HP, D_MODEL)
    wb = w_out[HEADS * VDIM:]
    return dict(p=w_p, uq=uq, uk=uk, uv=uv, wa=wa, wb=wb)


def _unpad_grads(g):
    gp = g["p"]
    seg = lambda cb, n: gp[:, cb * LANE:cb * LANE + n]
    ga = seg(CB_GA, HP).reshape(D_MODEL, HEADS, LANE)[..., :VDIM].reshape(D_MODEL, HEADS * VDIM)
    k_r = gp[:, CB_KR * LANE + NOPE:CB_KR * LANE + NOPE + ROPE]
    g_in = jnp.concatenate([seg(CB_CQ, Q_LORA), seg(CB_CKV, KV_LORA), k_r, ga, seg(CB_DIL, 9 * DIL_W), seg(CB_GB, DIL_W)], axis=1)
    g_uq = g["uq"].reshape(Q_LORA, HEADS, LANE)[..., :NOPE + ROPE].reshape(Q_LORA, -1)
    uk = g["uk"].reshape(KV_LORA, HEADS, LANE)[..., :NOPE]
    uv = g["uv"].reshape(KV_LORA, HEADS, LANE)[..., :VDIM]
    g_ukv = jnp.concatenate([uk, uv], axis=-1).reshape(KV_LORA, -1)
    wa = g["wa"].reshape(HEADS, LANE, D_MODEL)[:, :VDIM].reshape(HEADS * VDIM, D_MODEL)
    g_out = jnp.concatenate([wa, g["wb"]], axis=0)
    return g_in, g_uq, g_ukv, g_out


def _layer_fwd(x, w, norm_g, q_norm_g, kv_norm_g, mla, dil, l, comm=None):
    n = lambda s: f"l{l}_{s}"
    h = _rms_fwd(x, 0, D_MODEL, norm_g, n("norm"))
    p = _mm(h, w["p"], "nn", n("in_proj"))
    cqn = _rms_fwd(p, CB_CQ * LANE // Q_LORA, Q_LORA, q_norm_g, n("q_norm"))
    ckvn = _rms_fwd(p, CB_CKV * LANE // KV_LORA, KV_LORA, kv_norm_g, n("kv_norm"))
    qp = _mm(cqn, w["uq"], "nn", n("q_up"))
    kpre = _mm(ckvn, w["uk"], "nn", n("k_up"))
    v = _mm(ckvn, w["uv"], "nn", n("v_up"), out_dtype=BF16)
    q = _rope_heads(qp, mla["c_q"], mla["s1"], mla["s2"], n("q_rope"), BF16)
    k = _k_assemble(kpre, p, mla, n("k_asm"))
    o, lse, received = _flash_fwd(q, k, v, n("mla_fwd"), comm)
    a = _gate_a(o, p, n("gate_a"))
    dilr, os_, ls_ = [], [], []
    for g, (_, d) in enumerate(DIL_PAIRS):
        dilr.append(_dil_prep(p, dil, g, n(f"dil_prep{g}")))
        og, lg = _band_fwd(dilr[g], d, n(f"band{g}_fwd"))
        os_.append(og)
        ls_.append(lg)
    b = _gate_b(os_, ls_, p, n("gate_b"))
    x1 = _mm(a, w["wa"], "nn", n("out_a"), res=x)
    x2 = _mm(b, w["wb"], "nn", n("out_b"), res=x1)
    saved = dict(x=x, h=h, p=p, cqn=cqn, ckvn=ckvn, q=q, k=k, v=v, o=o, lse=lse, a=a, dilr=dilr, os=os_, ls=ls_, b=b)
    return x2, saved, received


def _layer_bwd(dx, sv, w, norm_g, q_norm_g, kv_norm_g, mla, dil, l, comm=None):
    n = lambda s: f"l{l}_{s}"
    g = {}
    da = _mm(dx, w["wa"], "nt", n("d_a"))
    db = _mm(dx, w["wb"], "nt", n("d_b"))
    g["wa"] = _mm(sv["a"], dx, "tn", n("dw_a"))
    g["wb"] = _mm(sv["b"], dx, "tn", n("dw_b"))
    do, dga = _gate_a_bwd(da, sv["o"], sv["p"], n("gate_a_bwd"))
    dgb, dos, dts = _gate_b_bwd(db, sv["os"], sv["ls"], sv["p"], n("gate_b_bwd"))
    ddil = []
    for gi, (_, d) in enumerate(DIL_PAIRS):
        dq = _band_bwd_q(sv["dilr"][gi], d, dos[gi], sv["ls"][gi], dts[gi], dil, n(f"band{gi}_bwd_q"))
        dk, dv = _band_bwd_kv(sv["dilr"][gi], d, dos[gi], sv["ls"][gi], dts[gi], dil, n(f"band{gi}_bwd_kv"))
        ddil += [dq, dk, dv]
    dq, dk, dv, received = _flash_bwd(sv["q"], sv["k"], sv["v"], sv["o"], do, sv["lse"], n("mla_bwd"), comm)
    dqp = _rope_heads(dq, mla["c_q"], -mla["s1"], -mla["s2"], n("q_rope_bwd"), BF16)
    dkr = _kr_bwd(dk, mla, n("kr_bwd"))
    g["uq"] = _mm(sv["cqn"], dqp, "tn", n("dw_uq"))
    g["uk"] = _mm(sv["ckvn"], dk, "tn", n("dw_uk"))
    g["uv"] = _mm(sv["ckvn"], dv, "tn", n("dw_uv"))
    dcqn = _mm(dqp, w["uq"], "nt", n("d_cqn"))
    dckvn = _mm(dk, w["uk"], "nt", n("d_ckvn_k"))
    dckvn = _mm(dv, w["uv"], "nt", n("d_ckvn_v"), res=dckvn)
    dcq, g_qn = _rms_bwd(sv["p"], CB_CQ * LANE // Q_LORA, Q_LORA, dcqn, q_norm_g, n("q_norm_bwd"), BF16)
    dckv, g_kvn = _rms_bwd(sv["p"], CB_CKV * LANE // KV_LORA, KV_LORA, dckvn, kv_norm_g, n("kv_norm_bwd"), BF16)
    dp = jnp.concatenate([dga] + ddil + [dgb, dckv, dkr, dcq], axis=1)
    g["p"] = _mm(sv["h"], dp, "tn", n("dw_in"))
    dh = _mm(dp, w["p"], "nt", n("d_h"))
    dx_in, g_n = _rms_bwd(sv["x"], 0, D_MODEL, dh, norm_g, n("norm_bwd"), F32, res=dx)
    return dx_in, g, g_n, g_qn, g_kvn, received


_SMALL_ROWS = 16


def _pack_small(norm, qn, kvn, fin, loss_row=None):
    padc = lambda t: jnp.pad(t, ((0, 0), (0, D_MODEL - t.shape[1])))
    extra = jnp.zeros((1, D_MODEL), F32) if loss_row is None else loss_row
    return jnp.concatenate([norm, padc(qn), padc(kvn), fin.reshape(1, D_MODEL), extra, jnp.zeros((2, D_MODEL), F32)], axis=0)


def _unpack_small(p):
    return (p[0:4], p[4:8, :Q_LORA], p[8:12, :KV_LORA], p[12]), p[13, 0]


def kernel(x, norm_g, w_in, q_norm_g, kv_norm_g, w_uq, w_ukv, w_out, final_g, loss_target, m_norm_g, m_w_in, m_q_norm_g, m_kv_norm_g, m_w_uq, m_w_ukv, m_w_out, m_final_g, v_norm_g, v_w_in, v_q_norm_g, v_kv_norm_g, v_w_uq, v_w_ukv, v_w_out, v_final_g):
    S = x.shape[1]
    xs = x.reshape(S, D_MODEL)
    target = loss_target.reshape(S, D_MODEL)

    wb = [t.astype(BF16) for t in (w_in, w_uq, w_ukv, w_out)]
    first = _full_weights(*_exchange([t[0:1] for t in wb], True, "gather_weights0"))
    ws = [_layer_weights(*(t[0] for t in first))]
    mla, dil = _rope_tables(S)

    saved = []
    h = xs
    for l in range(DEPTH):
        comm = ([t[1:] for t in wb], True) if l == 0 else None
        h, sv, received = _layer_fwd(h, ws[l], norm_g[l], q_norm_g[l], kv_norm_g[l], mla, dil, l, comm)
        saved.append(sv)
        if l == 0:
            rest = _full_weights(*received)
            ws += [_layer_weights(*(t[i] for t in rest)) for i in range(DEPTH - 1)]
    dx, g_final, loss_row = _loss_head(h, target, final_g, "loss_head")
    g_norm, g_qn, g_kvn, parts_l = [None] * DEPTH, [None] * DEPTH, [None] * DEPTH, [None] * DEPTH
    blocks = None
    for l in reversed(range(DEPTH)):
        comm = (blocks, False) if blocks is not None else None
        dx, g, g_norm[l], g_qn[l], g_kvn[l], received = _layer_bwd(dx, saved[l], ws[l], norm_g[l], q_norm_g[l], kv_norm_g[l], mla, dil, l, comm)
        if blocks is not None:
            parts_l[l + 1] = received
        blocks = _grad_blocks(*_unpad_grads(g))
    parts_l[0] = _exchange(blocks, False, "exchange_grads0")

    parts = [jnp.stack([parts_l[l][t] for l in range(DEPTH)], axis=1) for t in range(4)]
    sh = []
    for t, (pt, w, m, v) in enumerate(zip(parts, (w_in, w_uq, w_ukv, w_out), (m_w_in, m_w_uq, m_w_ukv, m_w_out), (v_w_in, v_w_uq, v_w_ukv, v_w_out))):
        two = lambda a: a.reshape(-1, a.shape[-1])
        outs = _adamw(pt.reshape(N_DEV, -1, pt.shape[-1]), two(w), two(m), two(v), f"adamw_{t}")
        sh.append([o.reshape(w.shape) for o in outs])

    small = _pack_small(jnp.concatenate(g_norm, 0), jnp.concatenate(g_qn, 0), jnp.concatenate(g_kvn, 0), g_final, loss_row)
    (small_parts,) = _exchange([small], True, "gather_small")
    souts = _adamw(small_parts, _pack_small(norm_g, q_norm_g, kv_norm_g, final_g), _pack_small(m_norm_g, m_q_norm_g, m_kv_norm_g, m_final_g),
                   _pack_small(v_norm_g, v_q_norm_g, v_kv_norm_g, v_final_g), "adamw_small")
    (g_sm, loss), (d_sm, _), (m_sm, _), (v_sm, _) = (_unpack_small(t) for t in souts)

    def order(sm, k):
        return (sm[0], sh[0][k], sm[1], sm[2], sh[1][k], sh[2][k], sh[3][k], sm[3])

    return (loss, dx.reshape(1, S, D_MODEL), *order(g_sm, 0), *order(d_sm, 1), *order(m_sm, 2), *order(v_sm, 3))
```

```python
import functools
import math

import jax
import jax.numpy as jnp
from jax import lax
from jax.experimental import pallas as pl
from jax.experimental.pallas import tpu as pltpu

F32 = jnp.float32
BF16 = jnp.bfloat16

D_MODEL = 1024
DEPTH = 4
HEADS = 8
NOPE = 64
ROPE = 32
VDIM = 64
Q_LORA = 384
KV_LORA = 256
DIL_PAIRS = ((128, 1), (512, 4), (2048, 16))
DIL_HD = 64
DIL_W = 512
ROT = 16
HALF = 64
THETA = 500000.0
EPS = 1e-6
IN_WIDTH = 6304
N_DEV = 8

LANE = 128
CB_GA, CB_DIL, CB_GB, CB_CKV, CB_KR, CB_CQ = 0, 8, 44, 48, 50, 51
NP = 54 * LANE
HP = HEADS * LANE

ADAM_LR = 0.001
ADAM_B1 = 0.9
ADAM_B2 = 0.999
ADAM_EPS = 1e-08
ADAM_WD = 0.01
ADAM_STEP = 10

VMEM_LIMIT = 48 * 1024 * 1024
ROW_TILE = 512
SUB = 128

_NT = (((1,), (1,)), ((), ()))
_NN = (((1,), (0,)), ((), ()))
_TN = (((0,), (0,)), ((), ()))


def _params(n_axes):
    return pltpu.CompilerParams(dimension_semantics=("arbitrary",) * n_axes, vmem_limit_bytes=VMEM_LIMIT)


def _pick(n, cands):
    for c in cands:
        if n % c == 0:
            return c
    raise ValueError(f"no tile for {n}")


def _mm(a, b, mode, name, out_dtype=F32, res=None):
    if mode == "nn":
        (M, K), (K2, N) = a.shape, b.shape
    elif mode == "nt":
        (M, K), (N, K2) = a.shape, b.shape
    else:
        (K, M), (K2, N) = a.shape, b.shape
    assert K == K2, (a.shape, b.shape, mode)
    tm = _pick(M, (1024, 512, 384, 256, 128))
    tn = _pick(N, (1152, 1024, 768, 640, 512, 384, 256, 128))
    tk = _pick(K, (1152, 1024, 768, 640, 512, 384, 256, 128))
    nk = K // tk
    dims = {"nn": _NN, "nt": _NT, "tn": _TN}[mode]

    def body(*refs):
        if res is not None:
            a_ref, b_ref, r_ref, o_ref = refs[:4]
        else:
            a_ref, b_ref, o_ref = refs[:3]
            r_ref = None
        part = lax.dot_general(a_ref[...].astype(BF16), b_ref[...].astype(BF16), dims, preferred_element_type=F32)

        def finish(acc):
            if r_ref is not None:
                acc = acc + r_ref[...]
            o_ref[...] = acc.astype(out_dtype)

        if nk == 1:
            finish(part)
        else:
            acc_ref = refs[-1]
            k = pl.program_id(2)

            @pl.when(k == 0)
            def _():
                acc_ref[...] = part

            @pl.when(k > 0)
            def _():
                acc_ref[...] += part

            @pl.when(k == nk - 1)
            def _():
                finish(acc_ref[...])

    if mode == "nn":
        a_spec = pl.BlockSpec((tm, tk), lambda i, j, k: (i, k))
        b_spec = pl.BlockSpec((tk, tn), lambda i, j, k: (k, j))
    elif mode == "nt":
        a_spec = pl.BlockSpec((tm, tk), lambda i, j, k: (i, k))
        b_spec = pl.BlockSpec((tn, tk), lambda i, j, k: (j, k))
    else:
        a_spec = pl.BlockSpec((tk, tm), lambda i, j, k: (k, i))
        b_spec = pl.BlockSpec((tk, tn), lambda i, j, k: (k, j))
    o_spec = pl.BlockSpec((tm, tn), lambda i, j, k: (i, j))
    in_specs = [a_spec, b_spec] + ([o_spec] if res is not None else [])
    args = (a, b) + ((res,) if res is not None else ())
    return pl.pallas_call(
        body, name=name, grid=(M // tm, N // tn, nk), out_shape=jax.ShapeDtypeStruct((M, N), out_dtype),
        in_specs=in_specs, out_specs=o_spec,
        scratch_shapes=[pltpu.VMEM((tm, tn), F32)] if nk > 1 else [],
        compiler_params=_params(3),
    )(*args)


def _rms_fwd(src, cb, width, g, name):
    S = src.shape[0]
    tm = ROW_TILE

    def body(x_ref, g_ref, o_ref):
        x = x_ref[...]
        r = lax.rsqrt(jnp.mean(x * x, axis=-1, keepdims=True) + EPS)
        o_ref[...] = (x * r * g_ref[...]).astype(BF16)

    return pl.pallas_call(
        body, name=name, grid=(S // tm,), out_shape=jax.ShapeDtypeStruct((S, width), BF16),
        in_specs=[pl.BlockSpec((tm, width), lambda i: (i, cb)), pl.BlockSpec((1, width), lambda i: (0, 0))],
        out_specs=pl.BlockSpec((tm, width), lambda i: (i, 0)), compiler_params=_params(1),
    )(src, g.reshape(1, width))


def _rms_bwd(src, cb, width, dy, g, name, out_dtype, res=None):
    S = src.shape[0]
    tm = ROW_TILE

    def body(*refs):
        if res is not None:
            x_ref, dy_ref, g_ref, r_ref, dx_ref, dg_ref = refs
        else:
            x_ref, dy_ref, g_ref, dx_ref, dg_ref = refs
            r_ref = None
        x = x_ref[...]
        dy = dy_ref[...]
        r = lax.rsqrt(jnp.mean(x * x, axis=-1, keepdims=True) + EPS)
        dyg = dy * g_ref[...]
        c = jnp.mean(dyg * x, axis=-1, keepdims=True)
        dx = r * dyg - x * (r * r * r) * c
        if r_ref is not None:
            dx = dx + r_ref[...]
        dx_ref[...] = dx.astype(out_dtype)
        part = jnp.sum(dy * x * r, axis=0, keepdims=True)

        @pl.when(pl.program_id(0) == 0)
        def _():
            dg_ref[...] = part

        @pl.when(pl.program_id(0) > 0)
        def _():
            dg_ref[...] += part

    row = pl.BlockSpec((tm, width), lambda i: (i, 0))
    in_specs = [pl.BlockSpec((tm, width), lambda i: (i, cb)), row, pl.BlockSpec((1, width), lambda i: (0, 0))]
    args = [src, dy, g.reshape(1, width)]
    if res is not None:
        in_specs.append(row)
        args.append(res)
    return pl.pallas_call(
        body, name=name, grid=(S // tm,),
        out_shape=(jax.ShapeDtypeStruct((S, width), out_dtype), jax.ShapeDtypeStruct((1, width), F32)),
        in_specs=in_specs, out_specs=(row, pl.BlockSpec((1, width), lambda i: (0, 0))),
        compiler_params=_params(1),
    )(*args)


def _rot(x, c, s1, s2, h):
    return x * c + pltpu.roll(x, x.shape[1] - h, 1) * s1 + pltpu.roll(x, h, 1) * s2


def _rope_tables(S):
    def tables(dim):
        inv = 1.0 / (THETA ** (jnp.arange(0, dim, 2, dtype=F32) / dim))
        ang = jnp.arange(S, dtype=F32)[:, None] * inv[None, :]
        return jnp.cos(ang), jnp.sin(ang)

    cm, sm = tables(ROPE)
    cd, sd = tables(ROT)
    z = lambda n: jnp.zeros((S, n), F32)
    o = lambda n: jnp.ones((S, n), F32)
    mla = dict(
        c_q=jnp.concatenate([o(64), cm, cm, z(32)], 1),
        c_kr=jnp.concatenate([z(64), cm, cm, z(32)], 1),
        s1=jnp.concatenate([z(64), -sm, z(16), z(32)], 1),
        s2=jnp.concatenate([z(64), z(16), sm, z(32)], 1),
    )
    one = lambda a, b, c: jnp.concatenate([a, b, c, a, b, c], 1)
    dil = dict(c=one(cd, cd, o(48)), s1=one(-sd, z(8), z(48)), s2=one(z(8), sd, z(48)))
    return mla, dil


def _rope_heads(src, c, s1, s2, name, out_dtype):
    S = src.shape[0]
    tm = ROW_TILE

    def body(x_ref, c_ref, s1_ref, s2_ref, o_ref):
        cv, s1v, s2v = c_ref[...], s1_ref[...], s2_ref[...]
        for h in range(HEADS):
            sl = slice(h * LANE, (h + 1) * LANE)
            o_ref[:, sl] = _rot(x_ref[:, sl], cv, s1v, s2v, ROPE // 2).astype(out_dtype)

    tab = pl.BlockSpec((tm, LANE), lambda i: (i, 0))
    wide = pl.BlockSpec((tm, HP), lambda i: (i, 0))
    return pl.pallas_call(
        body, name=name, grid=(S // tm,), out_shape=jax.ShapeDtypeStruct((S, HP), out_dtype),
        in_specs=[wide, tab, tab, tab], out_specs=wide, compiler_params=_params(1),
    )(src, c, s1, s2)


def _k_assemble(kpre, p, mla, name):
    S = kpre.shape[0]
    tm = ROW_TILE

    def body(k_ref, kr_ref, c_ref, s1_ref, s2_ref, o_ref):
        r = _rot(kr_ref[...], c_ref[...], s1_ref[...], s2_ref[...], ROPE // 2)
        for h in range(HEADS):
            sl = slice(h * LANE, (h + 1) * LANE)
            o_ref[:, sl] = (k_ref[:, sl] + r).astype(BF16)

    tab = pl.BlockSpec((tm, LANE), lambda i: (i, 0))
    wide = pl.BlockSpec((tm, HP), lambda i: (i, 0))
    return pl.pallas_call(
        body, name=name, grid=(S // tm,), out_shape=jax.ShapeDtypeStruct((S, HP), BF16),
        in_specs=[wide, pl.BlockSpec((tm, LANE), lambda i: (i, CB_KR)), tab, tab, tab],
        out_specs=wide, compiler_params=_params(1),
    )(kpre, p, mla["c_kr"], mla["s1"], mla["s2"])


def _kr_bwd(dk, mla, name):
    S = dk.shape[0]
    tm = ROW_TILE

    def body(dk_ref, c_ref, s1_ref, s2_ref, o_ref):
        t = dk_ref[:, 0:LANE]
        for h in range(1, HEADS):
            t = t + dk_ref[:, h * LANE:(h + 1) * LANE]
        lane = lax.broadcasted_iota(jnp.int32, (1, LANE), 1)
        t = jnp.where((lane >= NOPE) & (lane < NOPE + ROPE), t, 0.0)
        o_ref[...] = _rot(t, c_ref[...], -s1_ref[...], -s2_ref[...], ROPE // 2).astype(BF16)

    tab = pl.BlockSpec((tm, LANE), lambda i: (i, 0))
    return pl.pallas_call(
        body, name=name, grid=(S // tm,), out_shape=jax.ShapeDtypeStruct((S, LANE), BF16),
        in_specs=[pl.BlockSpec((tm, HP), lambda i: (i, 0)), tab, tab, tab],
        out_specs=tab, compiler_params=_params(1),
    )(dk, mla["c_kr"], mla["s1"], mla["s2"])


def _dil_prep(p, dil, g, d, name):
    S = p.shape[0]
    tm = ROW_TILE
    first = CB_DIL * LANE // DIL_W + 3 * g
    nc = DIL_W // LANE
    n = tm // d

    def body(q_ref, k_ref, v_ref, c_ref, s1_ref, s2_ref, o_ref, scr):
        cv, s1v, s2v = (jnp.tile(t[...], (1, nc)) for t in (c_ref, s1_ref, s2_ref))
        ys = (_rot(q_ref[...], cv, s1v, s2v, ROT // 2) * (DIL_HD ** -0.5), _rot(k_ref[...], cv, s1v, s2v, ROT // 2), v_ref[...])
        if d == 1:
            for t, y in enumerate(ys):
                o_ref[:, t * DIL_W:(t + 1) * DIL_W] = y.astype(BF16)
        else:
            for t, y in enumerate(ys):
                for c in range(nc):
                    scr[t * nc + c] = y[:, c * LANE:(c + 1) * LANE]
            for r in range(d):
                for tc in range(3 * nc):
                    col = r * 3 * DIL_W + tc * LANE
                    o_ref[:, col:col + LANE] = scr.at[tc][pl.ds(r, n, stride=d), :].astype(BF16)

    tab = pl.BlockSpec((tm, LANE), lambda i: (i, 0))
    chunk = lambda t: pl.BlockSpec((tm, DIL_W), lambda i: (i, first + t))
    return pl.pallas_call(
        body, name=name, grid=(S // tm,), out_shape=jax.ShapeDtypeStruct((S // d, d * 3 * DIL_W), BF16),
        in_specs=[chunk(0), chunk(1), chunk(2), tab, tab, tab], out_specs=_view_spec(d, 3 * DIL_W),
        scratch_shapes=[pltpu.VMEM((3 * nc, tm, LANE), F32)], compiler_params=_params(1),
    )(p, p, p, dil["c"], dil["s1"], dil["s2"])


def _grid_ends(dims):
    ids = [pl.program_id(a) for a in range(len(dims))]
    first = functools.reduce(jnp.logical_and, [i == 0 for i in ids])
    last = functools.reduce(jnp.logical_and, [i == n - 1 for i, n in zip(ids, dims)])
    return first, last


def _flash_fwd(q, k, v, name, comm=None):
    S = q.shape[0]
    tq = tk = _pick(S, (1024, 512))
    nk = S // tk
    c2 = (NOPE + ROPE) ** -0.5 * math.log2(math.e)
    srcs, same_src = comm if comm is not None else ([], True)
    n = len(srcs)
    grid = (HEADS, S // tq, nk)

    def body(*refs):
        q_ref, k_ref, v_ref = refs[:3]
        o_ref, lse_ref = refs[3 + n:5 + n]
        m_s, acc_s = refs[5 + 2 * n:7 + 2 * n]
        ex = (refs[3:3 + n], refs[5 + n:5 + 2 * n], *refs[7 + 2 * n:], same_src)
        j = pl.program_id(2)
        if n:
            first, last = _grid_ends(grid)
            pl.when(first)(lambda: _exchange_start(*ex))

        @pl.when(j == 0)
        def _():
            m_s[...] = jnp.full(m_s.shape, -jnp.inf, F32)
            acc_s[...] = jnp.zeros(acc_s.shape, F32)

        lane = lax.broadcasted_iota(jnp.int32, (1, LANE), 1)
        vv = jnp.where(lane == VDIM, jnp.ones((), BF16), v_ref[...])
        t = lax.dot_general(q_ref[...], k_ref[...], _NT, preferred_element_type=F32) * c2
        m_prev = m_s[...]
        m_new = jnp.maximum(m_prev, jnp.max(t, axis=-1, keepdims=True))
        alpha = jnp.exp2(m_prev - m_new)
        e = jnp.exp2(t - jnp.tile(m_new, (1, tk // LANE)))
        acc_s[...] = alpha * acc_s[...] + jnp.dot(e.astype(BF16), vv, preferred_element_type=F32)
        m_s[...] = m_new

        @pl.when(j == nk - 1)
        def _():
            acc = acc_s[...]
            l = acc[:, VDIM:VDIM + 1]
            o_ref[...] = jnp.where(lane < VDIM, acc / l, 0.0)
            lse_ref[...] = (m_s[...] + jnp.log2(l)) * math.log(2.0)

        if n:
            pl.when(last)(lambda: _exchange_wait(*ex))

    qs = pl.BlockSpec((tq, LANE), lambda h, i, j: (i, h))
    ks = pl.BlockSpec((tk, LANE), lambda h, i, j: (j, h))
    hbm = pl.BlockSpec(memory_space=pltpu.HBM)
    outs = pl.pallas_call(
        body, name=name, grid=grid,
        out_shape=(jax.ShapeDtypeStruct((S, HP), F32), jax.ShapeDtypeStruct((S, HP), F32)) + _exchange_shapes(srcs, same_src),
        in_specs=[qs, ks, ks] + [hbm] * n, out_specs=(qs, qs) + (hbm,) * n,
        scratch_shapes=[pltpu.VMEM((tq, LANE), F32), pltpu.VMEM((tq, LANE), F32)] + (_exchange_sems(n) if n else []),
        compiler_params=_params(3),
    )(q, k, v, *srcs)
    return outs[0], outs[1], list(outs[2:])


def _flash_bwd(q, k, v, o, do, lse, name, comm=None):
    S = q.shape[0]
    tq = tk = _pick(S, (1024, 512))
    nq = S // tq
    scale = (NOPE + ROPE) ** -0.5
    srcs, same_src = comm if comm is not None else ([], True)
    n = len(srcs)
    grid = (HEADS, S // tk, nq)

    def body(*refs):
        q_ref, k_ref, v_ref, o_ref, do_ref, lse_ref = refs[:6]
        dq_ref, dk_ref, dv_ref = refs[6 + n:9 + n]
        dk_s, dv_s = refs[9 + 2 * n:11 + 2 * n]
        ex = (refs[6:6 + n], refs[9 + n:9 + 2 * n], *refs[11 + 2 * n:], same_src)
        j = pl.program_id(1)
        i = pl.program_id(2)
        if n:
            first, last = _grid_ends(grid)
            pl.when(first)(lambda: _exchange_start(*ex))
        qv = q_ref[...]
        kv = k_ref[...]
        do = do_ref[...]
        dob = do.astype(BF16)
        lse_row = jnp.transpose(lse_ref[...])[0:1, :]
        delta_row = jnp.sum(jnp.transpose(do * o_ref[...]), axis=0, keepdims=True)
        st = lax.dot_general(kv, qv, _NT, preferred_element_type=F32) * scale
        pt = jnp.exp(st - lse_row)
        dpt = lax.dot_general(v_ref[...], dob, _NT, preferred_element_type=F32)
        dst = (pt * (dpt - delta_row)).astype(BF16)
        dv_part = jnp.dot(pt.astype(BF16), dob, preferred_element_type=F32)
        dk_part = jnp.dot(dst, qv, preferred_element_type=F32)
        dq_part = jnp.transpose(jnp.dot(jnp.transpose(kv), dst, preferred_element_type=F32)) * scale

        @pl.when(i == 0)
        def _():
            dk_s[...] = dk_part
            dv_s[...] = dv_part

        @pl.when(i > 0)
        def _():
            dk_s[...] += dk_part
            dv_s[...] += dv_part

        rows = pl.ds(pl.multiple_of(i * tq, tq), tq)

        @pl.when(j == 0)
        def _():
            dq_ref[rows, :] = dq_part

        @pl.when(j > 0)
        def _():
            dq_ref[rows, :] += dq_part

        @pl.when(i == nq - 1)
        def _():
            dk_ref[...] = dk_s[...] * scale
            dv_ref[...] = dv_s[...].astype(BF16)

        if n:
            pl.when(last)(lambda: _exchange_wait(*ex))

    qs = pl.BlockSpec((tq, LANE), lambda h, j, i: (i, h))
    ks = pl.BlockSpec((tk, LANE), lambda h, j, i: (j, h))
    hbm = pl.BlockSpec(memory_space=pltpu.HBM)
    outs = pl.pallas_call(
        body, name=name, grid=grid,
        out_shape=(jax.ShapeDtypeStruct((S, HP), F32), jax.ShapeDtypeStruct((S, HP), F32), jax.ShapeDtypeStruct((S, HP), BF16))
        + _exchange_shapes(srcs, same_src),
        in_specs=[qs, ks, ks, qs, qs, qs] + [hbm] * n,
        out_specs=(pl.BlockSpec((S, LANE), lambda h, j, i: (0, h)), ks, ks) + (hbm,) * n,
        scratch_shapes=[pltpu.VMEM((tk, LANE), F32), pltpu.VMEM((tk, LANE), F32)] + (_exchange_sems(n) if n else []),
        compiler_params=_params(3),
    )(q, k, v, o, do, lse, *srcs)
    return outs[0], outs[1], outs[2], list(outs[3:])


def _band_tiles(L):
    tq = min(512, L)
    return tq, tq // SUB, tq // HALF, L // HALF


def _halo_specs(tq, rpb, n64, col):
    prev = pl.BlockSpec((HALF, DIL_W), lambda r, i: (jnp.maximum(rpb * i - 1, 0), col(r)))
    cur = pl.BlockSpec((tq, DIL_W), lambda r, i: (i, col(r)))
    nxt = pl.BlockSpec((HALF, DIL_W), lambda r, i: (jnp.minimum(rpb * i + rpb, n64 - 1), col(r)))
    return [prev, cur, nxt]


def _fill(buf, prev_ref, cur_ref, next_ref, tq):
    buf[0:HALF, :] = prev_ref[...]
    buf[HALF:HALF + tq, :] = cur_ref[...]
    buf[HALF + tq:HALF + tq + HALF, :] = next_ref[...]


def _lo_lanes():
    return lax.broadcasted_iota(jnp.int32, (1, LANE), 1) < DIL_HD


def _stack_heads(x, lo):
    zero = jnp.zeros_like(x)
    return jnp.concatenate([jnp.where(lo, x, zero), jnp.where(lo, zero, x)], axis=0)


def _stack_cols(x):
    return jnp.concatenate([x[:, 0:1], x[:, DIL_HD:DIL_HD + 1]], axis=0)


def _band_valid(q0, k0, nq, nk, L, bound_q):
    qpos = q0 + lax.broadcasted_iota(jnp.int32, (nq, 1), 0)
    kpos = k0 + lax.broadcasted_iota(jnp.int32, (1, nk), 1)
    side = qpos if bound_q else kpos
    return (jnp.abs(qpos - kpos) <= HALF) & (side >= 0) & (side < L)


def _band_fwd(dilr, d, name):
    L = dilr.shape[0]
    S = L * d
    tq, nsub, rpb, n64 = _band_tiles(L)
    view = dilr
    win = SUB + 2 * HALF

    def body(q_ref, kp_ref, kc_ref, kn_ref, vp_ref, vc_ref, vn_ref, o_ref, lse_ref, kbuf, vbuf):
        i = pl.program_id(1)
        _fill(kbuf, kp_ref, kc_ref, kn_ref, tq)
        _fill(vbuf, vp_ref, vc_ref, vn_ref, tq)
        lo = _lo_lanes()
        for a in range(nsub):
            r0 = a * SUB
            rows = slice(r0, r0 + SUB)
            valid = _band_valid(i * tq + r0, i * tq + r0 - HALF, SUB, win, L, False)
            valid2 = jnp.concatenate([valid, valid], axis=0)
            for hp in range(4):
                cs = slice(hp * LANE, (hp + 1) * LANE)
                q = q_ref[rows, cs]
                kw = kbuf[r0:r0 + win, cs]
                vw = vbuf[r0:r0 + win, cs]
                q2 = _stack_heads(q, lo)
                s = lax.dot_general(q2, kw, _NT, preferred_element_type=F32)
                s = jnp.where(valid2, s, -jnp.inf)
                m = jnp.max(s, axis=-1, keepdims=True)
                e = jnp.exp(s - m)
                l = jnp.sum(e, axis=-1, keepdims=True)
                o2 = jnp.dot(e.astype(BF16), vw, preferred_element_type=F32) / l
                lse2 = m + jnp.log(l)
                o_ref[rows, cs] = jnp.where(lo, o2[:SUB], o2[SUB:])
                lse_ref[rows, cs] = jnp.where(lo, lse2[:SUB], lse2[SUB:])

    out_spec = pl.BlockSpec((tq, DIL_W), lambda r, i: (i, r))
    o, lse = pl.pallas_call(
        body, name=name, grid=(d, L // tq),
        out_shape=(jax.ShapeDtypeStruct((L, d * DIL_W), F32), jax.ShapeDtypeStruct((L, d * DIL_W), F32)),
        in_specs=[pl.BlockSpec((tq, DIL_W), lambda r, i: (i, r * 3))]
        + _halo_specs(tq, rpb, n64, lambda r: r * 3 + 1) + _halo_specs(tq, rpb, n64, lambda r: r * 3 + 2),
        out_specs=(out_spec, out_spec),
        scratch_shapes=[pltpu.VMEM((tq + 2 * HALF, DIL_W), BF16), pltpu.VMEM((tq + 2 * HALF, DIL_W), BF16)],
        compiler_params=_params(2),
    )(view, view, view, view, view, view, view)
    return o, lse


def _band_bwd_q(dilr, d, do, lse, dlt, dil, name):
    L = dilr.shape[0]
    S = L * d
    tq, nsub, rpb, n64 = _band_tiles(L)
    view = dilr
    v4 = lambda t: t.reshape(L, d * DIL_W)
    tv = lambda t: t.reshape(L, d * LANE)
    win = SUB + 2 * HALF

    def body(q_ref, kp_ref, kc_ref, kn_ref, vp_ref, vc_ref, vn_ref, do_ref, lse_ref, dlt_ref, c_ref, s1_ref, s2_ref, dq_ref,
             kbuf, vbuf):
        i = pl.program_id(1)
        _fill(kbuf, kp_ref, kc_ref, kn_ref, tq)
        _fill(vbuf, vp_ref, vc_ref, vn_ref, tq)
        lo = _lo_lanes()
        for a in range(nsub):
            r0 = a * SUB
            rows = slice(r0, r0 + SUB)
            valid = _band_valid(i * tq + r0, i * tq + r0 - HALF, SUB, win, L, False)
            valid2 = jnp.concatenate([valid, valid], axis=0)
            cv, s1v, s2v = c_ref[rows, :], s1_ref[rows, :], s2_ref[rows, :]
            for hp in range(4):
                cs = slice(hp * LANE, (hp + 1) * LANE)
                kw = kbuf[r0:r0 + win, cs]
                vw = vbuf[r0:r0 + win, cs]
                q2 = _stack_heads(q_ref[rows, cs], lo)
                do2 = _stack_heads(do_ref[rows, cs], lo).astype(BF16)
                s = lax.dot_general(q2, kw, _NT, preferred_element_type=F32)
                p = jnp.where(valid2, jnp.exp(s - _stack_cols(lse_ref[rows, cs])), 0.0)
                dp = lax.dot_general(do2, vw, _NT, preferred_element_type=F32)
                ds = (p * (dp - _stack_cols(dlt_ref[rows, cs]))).astype(BF16)
                dq2 = jnp.dot(ds, kw, preferred_element_type=F32)
                dq = jnp.where(lo, dq2[:SUB], dq2[SUB:])
                dq_ref[rows, cs] = (_rot(dq, cv, -s1v, -s2v, ROT // 2) * (DIL_HD ** -0.5)).astype(BF16)

    row = pl.BlockSpec((tq, DIL_W), lambda r, i: (i, r))
    tab = pl.BlockSpec((tq, LANE), lambda r, i: (i, r))
    dq = pl.pallas_call(
        body, name=name, grid=(d, L // tq), out_shape=jax.ShapeDtypeStruct((L, d * DIL_W), BF16),
        in_specs=[pl.BlockSpec((tq, DIL_W), lambda r, i: (i, r * 3))]
        + _halo_specs(tq, rpb, n64, lambda r: r * 3 + 1) + _halo_specs(tq, rpb, n64, lambda r: r * 3 + 2)
        + [row, row, row, tab, tab, tab],
        out_specs=row,
        scratch_shapes=[pltpu.VMEM((tq + 2 * HALF, DIL_W), BF16), pltpu.VMEM((tq + 2 * HALF, DIL_W), BF16)],
        compiler_params=_params(2),
    )(view, view, view, view, view, view, view, v4(do), v4(lse), v4(dlt), tv(dil["c"]), tv(dil["s1"]), tv(dil["s2"]))
    return dq.reshape(S, DIL_W)


def _band_bwd_kv(dilr, d, do, lse, dlt, dil, name):
    L = dilr.shape[0]
    S = L * d
    tq, nsub, rpb, n64 = _band_tiles(L)
    view = dilr
    v4 = lambda t: t.reshape(L, d * DIL_W)
    tv = lambda t: t.reshape(L, d * LANE)
    win = SUB + 2 * HALF

    def body(k_ref, v_ref, qp_ref, qc_ref, qn_ref, dop_ref, doc_ref, don_ref, lp_ref, lc_ref, ln_ref, tp_ref, tc_ref, tn_ref,
             c_ref, s1_ref, s2_ref, dk_ref, dv_ref, qbuf, dobuf, lbuf, tbuf):
        j = pl.program_id(1)
        _fill(qbuf, qp_ref, qc_ref, qn_ref, tq)
        _fill(dobuf, dop_ref, doc_ref, don_ref, tq)
        _fill(lbuf, lp_ref, lc_ref, ln_ref, tq)
        _fill(tbuf, tp_ref, tc_ref, tn_ref, tq)
        lo = _lo_lanes()
        quarter = (lax.broadcasted_iota(jnp.int32, (1, LANE), 1) & (DIL_HD - 1)) < DIL_HD // 2
        for a in range(nsub):
            r0 = a * SUB
            rows = slice(r0, r0 + SUB)
            wrows = slice(r0, r0 + win)
            kpos = j * tq + r0 + lax.broadcasted_iota(jnp.int32, (SUB, 1), 0)
            qpos = j * tq + r0 - HALF + lax.broadcasted_iota(jnp.int32, (1, win), 1)
            valid = (jnp.abs(qpos - kpos) <= HALF) & (qpos >= 0) & (qpos < L)
            valid2 = jnp.concatenate([valid, valid], axis=1)
            cv, s1v, s2v = c_ref[rows, :], s1_ref[rows, :], s2_ref[rows, :]
            for hp in range(4):
                cs = slice(hp * LANE, (hp + 1) * LANE)
                k = k_ref[rows, cs]
                v = v_ref[rows, cs]
                q2 = _stack_heads(qbuf[wrows, cs], lo)
                do2 = _stack_heads(dobuf[wrows, cs], lo).astype(BF16)
                zt = jnp.transpose(jnp.where(quarter, lbuf[wrows, cs], tbuf[wrows, cs]))
                lse_row = jnp.concatenate([zt[0:1, :], zt[DIL_HD:DIL_HD + 1, :]], axis=1)
                dlt_row = jnp.concatenate([zt[DIL_HD // 2:DIL_HD // 2 + 1, :], zt[3 * DIL_HD // 2:3 * DIL_HD // 2 + 1, :]], axis=1)
                st = lax.dot_general(k, q2, _NT, preferred_element_type=F32)
                pt = jnp.where(valid2, jnp.exp(st - lse_row), 0.0)
                dv = jnp.dot(pt.astype(BF16), do2, preferred_element_type=F32)
                dpt = lax.dot_general(v, do2, _NT, preferred_element_type=F32)
                dst = (pt * (dpt - dlt_row)).astype(BF16)
                dk = jnp.dot(dst, q2, preferred_element_type=F32)
                dk_ref[rows, cs] = _rot(dk, cv, -s1v, -s2v, ROT // 2).astype(BF16)
                dv_ref[rows, cs] = dv.astype(BF16)

    row = pl.BlockSpec((tq, DIL_W), lambda r, i: (i, r))
    tab = pl.BlockSpec((tq, LANE), lambda r, i: (i, r))
    halo = _halo_specs(tq, rpb, n64, lambda r: r)
    hb = tq + 2 * HALF
    dk, dv = pl.pallas_call(
        body, name=name, grid=(d, L // tq),
        out_shape=(jax.ShapeDtypeStruct((L, d * DIL_W), BF16), jax.ShapeDtypeStruct((L, d * DIL_W), BF16)),
        in_specs=[pl.BlockSpec((tq, DIL_W), lambda r, i: (i, r * 3 + 1)), pl.BlockSpec((tq, DIL_W), lambda r, i: (i, r * 3 + 2))]
        + _halo_specs(tq, rpb, n64, lambda r: r * 3) + halo + halo + halo + [tab, tab, tab],
        out_specs=(row, row),
        scratch_shapes=[pltpu.VMEM((hb, DIL_W), BF16), pltpu.VMEM((hb, DIL_W), F32), pltpu.VMEM((hb, DIL_W), F32), pltpu.VMEM((hb, DIL_W), F32)],
        compiler_params=_params(2),
    )(view, view, view, view, view, v4(do), v4(do), v4(do), v4(lse), v4(lse), v4(lse), v4(dlt), v4(dlt), v4(dlt),
      tv(dil["c"]), tv(dil["s1"]), tv(dil["s2"]))
    return dk.reshape(S, DIL_W), dv.reshape(S, DIL_W)


def _sigmoid(x):
    return 1.0 / (1.0 + jnp.exp(-x))


def _gate_a(o, p, name):
    S = o.shape[0]
    tm = ROW_TILE

    def body(o_ref, g_ref, a_ref):
        g = g_ref[...]
        a_ref[...] = (o_ref[...] * (g * _sigmoid(g))).astype(BF16)

    blk = pl.BlockSpec((tm, HP), lambda i: (i, 0))
    return pl.pallas_call(
        body, name=name, grid=(S // tm,), out_shape=jax.ShapeDtypeStruct((S, HP), BF16),
        in_specs=[blk, pl.BlockSpec((tm, HP), lambda i: (i, CB_GA * LANE // HP))], out_specs=blk, compiler_params=_params(1),
    )(o, p)


def _gate_a_bwd(da, o, p, name):
    S = o.shape[0]
    tm = ROW_TILE

    def body(da_ref, o_ref, g_ref, do_ref, dg_ref):
        g = g_ref[...]
        da = da_ref[...]
        sg = _sigmoid(g)
        do_ref[...] = da * (g * sg)
        dg_ref[...] = (da * o_ref[...] * (sg * (1.0 + g * (1.0 - sg)))).astype(BF16)

    blk = pl.BlockSpec((tm, HP), lambda i: (i, 0))
    return pl.pallas_call(
        body, name=name, grid=(S // tm,),
        out_shape=(jax.ShapeDtypeStruct((S, HP), F32), jax.ShapeDtypeStruct((S, HP), BF16)),
        in_specs=[blk, blk, pl.BlockSpec((tm, HP), lambda i: (i, CB_GA * LANE // HP))], out_specs=(blk, blk), compiler_params=_params(1),
    )(da, o, p)


def _merge_weights(l0, l1, l2):
    mx = jnp.maximum(jnp.maximum(l0, l1), l2)
    e0, e1, e2 = jnp.exp(l0 - mx), jnp.exp(l1 - mx), jnp.exp(l2 - mx)
    den = e0 + e1 + e2
    return e0 / den, e1 / den, e2 / den


def _view_spec(d, width):
    return pl.BlockSpec((ROW_TILE // d, d * width), lambda i: (i, 0))


def _to_tokens(src_ref, scr, base, d):
    n = ROW_TILE // d
    for r in range(d):
        for c in range(DIL_W // LANE):
            scr.at[base + c][pl.ds(r, n, stride=d), :] = src_ref[:, r * DIL_W + c * LANE:r * DIL_W + (c + 1) * LANE]


def _from_tokens(scr, base, dst_ref, d):
    n = ROW_TILE // d
    for r in range(d):
        for c in range(DIL_W // LANE):
            dst_ref[:, r * DIL_W + c * LANE:r * DIL_W + (c + 1) * LANE] = scr.at[base + c][pl.ds(r, n, stride=d), :]


def _gate_b(os_, ls_, p, name):
    S = p.shape[0]
    tm = ROW_TILE
    nc = DIL_W // LANE
    dils = [d for _, d in DIL_PAIRS]

    def body(o0, o1, o2, l0, l1, l2, g_ref, b_ref, scr):
        for gi, (o_ref, l_ref) in enumerate(((o1, l1), (o2, l2))):
            _to_tokens(o_ref, scr, (2 * gi) * nc, dils[gi + 1])
            _to_tokens(l_ref, scr, (2 * gi + 1) * nc, dils[gi + 1])
        for c in range(nc):
            cs = slice(c * LANE, (c + 1) * LANE)
            a0, a1, a2 = _merge_weights(l0[:, cs], scr[nc + c], scr[3 * nc + c])
            bm = a0 * o0[:, cs] + a1 * scr[c] + a2 * scr[2 * nc + c]
            g = g_ref[:, cs]
            b_ref[:, cs] = (bm * (g * _sigmoid(g))).astype(BF16)

    blk = pl.BlockSpec((tm, DIL_W), lambda i: (i, 0))
    views = [_view_spec(d, DIL_W) for d in dils]
    return pl.pallas_call(
        body, name=name, grid=(S // tm,), out_shape=jax.ShapeDtypeStruct((S, DIL_W), BF16),
        in_specs=views + views + [pl.BlockSpec((tm, DIL_W), lambda i: (i, CB_GB * LANE // DIL_W))], out_specs=blk,
        scratch_shapes=[pltpu.VMEM((4 * nc, tm, LANE), F32)], compiler_params=_params(1),
    )(*os_, *ls_, p)


def _gate_b_bwd(db, os_, ls_, p, name):
    S = p.shape[0]
    tm = ROW_TILE
    nc = DIL_W // LANE
    dils = [d for _, d in DIL_PAIRS]

    def body(db_ref, o0, o1, o2, l0, l1, l2, g_ref, dg_ref, d0, d1, d2, t0, t1, t2, scr, out_scr):
        for gi, (o_ref, l_ref) in enumerate(((o1, l1), (o2, l2))):
            _to_tokens(o_ref, scr, (2 * gi) * nc, dils[gi + 1])
            _to_tokens(l_ref, scr, (2 * gi + 1) * nc, dils[gi + 1])
        lo = _lo_lanes()
        for c in range(nc):
            cs = slice(c * LANE, (c + 1) * LANE)
            a0, a1, a2 = _merge_weights(l0[:, cs], scr[nc + c], scr[3 * nc + c])
            bm = a0 * o0[:, cs] + a1 * scr[c] + a2 * scr[2 * nc + c]
            g = g_ref[:, cs]
            db = db_ref[:, cs]
            sg = _sigmoid(g)
            dbm = db * (g * sg)
            dg_ref[:, cs] = (db * bm * (sg * (1.0 + g * (1.0 - sg)))).astype(BF16)
            prod = dbm * bm
            tl = jnp.sum(jnp.where(lo, prod, 0.0), axis=-1, keepdims=True)
            th = jnp.sum(jnp.where(lo, 0.0, prod), axis=-1, keepdims=True)
            t = jnp.where(lo, tl, th)
            d0[:, cs] = a0 * dbm
            t0[:, cs] = a0 * t
            out_scr[c] = a1 * dbm
            out_scr[nc + c] = a1 * t
            out_scr[2 * nc + c] = a2 * dbm
            out_scr[3 * nc + c] = a2 * t
        _from_tokens(out_scr, 0, d1, dils[1])
        _from_tokens(out_scr, nc, t1, dils[1])
        _from_tokens(out_scr, 2 * nc, d2, dils[2])
        _from_tokens(out_scr, 3 * nc, t2, dils[2])

    blk = pl.BlockSpec((tm, DIL_W), lambda i: (i, 0))
    views = [_view_spec(d, DIL_W) for d in dils]
    fs = [jax.ShapeDtypeStruct((S // d, d * DIL_W), F32) for d in dils]
    outs = pl.pallas_call(
        body, name=name, grid=(S // tm,),
        out_shape=(jax.ShapeDtypeStruct((S, DIL_W), BF16), *fs, *fs),
        in_specs=[blk] + views + views + [pl.BlockSpec((tm, DIL_W), lambda i: (i, CB_GB * LANE // DIL_W))],
        out_specs=(blk, *views, *views),
        scratch_shapes=[pltpu.VMEM((4 * nc, tm, LANE), F32), pltpu.VMEM((4 * nc, tm, LANE), F32)], compiler_params=_params(1),
    )(db, *os_, *ls_, p)
    return outs[0], outs[1:4], outs[4:7]


def _loss_head(x, target, g, name):
    S, D = x.shape
    tm = ROW_TILE

    def body(x_ref, t_ref, g_ref, dx_ref, dg_ref, loss_ref):
        xv = x_ref[...]
        gv = g_ref[...]
        r = lax.rsqrt(jnp.mean(xv * xv, axis=-1, keepdims=True) + EPS)
        xr = xv * r
        err = xr * gv - t_ref[...]
        lpart = 0.5 * jnp.sum(jnp.mean(err * err, axis=-1, keepdims=True), axis=0, keepdims=True)
        dy = err / D
        dyg = dy * gv
        c = jnp.mean(dyg * xv, axis=-1, keepdims=True)
        dx_ref[...] = r * dyg - xv * (r * r * r) * c
        gpart = jnp.sum(dy * xr, axis=0, keepdims=True)

        @pl.when(pl.program_id(0) == 0)
        def _():
            dg_ref[...] = gpart
            loss_ref[...] = jnp.broadcast_to(lpart, loss_ref.shape)

        @pl.when(pl.program_id(0) > 0)
        def _():
            dg_ref[...] += gpart
            loss_ref[...] += jnp.broadcast_to(lpart, loss_ref.shape)

    row = pl.BlockSpec((tm, D), lambda i: (i, 0))
    vec = pl.BlockSpec((1, D), lambda i: (0, 0))
    return pl.pallas_call(
        body, name=name, grid=(S // tm,),
        out_shape=(jax.ShapeDtypeStruct((S, D), F32), jax.ShapeDtypeStruct((1, D), F32), jax.ShapeDtypeStruct((1, D), F32)),
        in_specs=[row, row, vec], out_specs=(row, vec, vec), compiler_params=_params(1),
    )(x, target, g.reshape(1, D))


def _adamw(parts, w, m, v, name):
    R, C = w.shape
    tr = _pick(R, (128, 64, 32, 16, 8))

    def body(p_ref, w_ref, m_ref, v_ref, g_ref, d_ref, nm_ref, nv_ref):
        g = p_ref[0].astype(F32)
        for k in range(1, N_DEV):
            g = g + p_ref[k].astype(F32)
        m2 = ADAM_B1 * m_ref[...] + (1.0 - ADAM_B1) * g
        v2 = ADAM_B2 * v_ref[...] + (1.0 - ADAM_B2) * (g * g)
        m_hat = m2 / (1.0 - ADAM_B1 ** ADAM_STEP)
        v_hat = v2 / (1.0 - ADAM_B2 ** ADAM_STEP)
        g_ref[...] = g
        d_ref[...] = -ADAM_LR * (m_hat / (jnp.sqrt(v_hat) + ADAM_EPS) + ADAM_WD * w_ref[...])
        nm_ref[...] = m2
        nv_ref[...] = v2

    blk = pl.BlockSpec((tr, C), lambda i: (i, 0))
    f = jax.ShapeDtypeStruct((R, C), F32)
    return pl.pallas_call(
        body, name=name, grid=(R // tr,), out_shape=(f, f, f, f),
        in_specs=[pl.BlockSpec((N_DEV, tr, C), lambda i: (0, i, 0)), blk, blk, blk], out_specs=(blk,) * 4,
        compiler_params=_params(1),
    )(parts, w, m, v)


def _exchange_copies(src_refs, out_refs, send_sems, recv_sems, local_sems, same_src):
    n = len(src_refs)
    x, y, c = lax.axis_index("x"), lax.axis_index("y"), lax.axis_index("c")
    me = 4 * x + 2 * y + c

    def block(t, j):
        return src_refs[t] if same_src else src_refs[t].at[j]

    local = [pltpu.make_async_copy(block(t, me), out_refs[t].at[me], local_sems.at[t]) for t in range(n)]
    remote = []
    for k in range(1, N_DEV):
        px = 1 - x if (k >> 2) & 1 else x
        py = 1 - y if (k >> 1) & 1 else y
        pc = 1 - c if k & 1 else c
        for t in range(n):
            remote.append(pltpu.make_async_remote_copy(
                src_ref=block(t, 4 * px + 2 * py + pc), dst_ref=out_refs[t].at[me],
                send_sem=send_sems.at[(k - 1) * n + t], recv_sem=recv_sems.at[(k - 1) * n + t],
                device_id=(px, py, pc), device_id_type=pl.DeviceIdType.MESH))
    return local, remote


def _exchange_start(*args):
    local, remote = _exchange_copies(*args)
    for cp in local + remote:
        cp.start()


def _exchange_wait(*args):
    local, remote = _exchange_copies(*args)
    for cp in remote:
        cp.wait()
    for cp in local:
        cp.wait()


def _exchange_shapes(srcs, same_src):
    return tuple(jax.ShapeDtypeStruct((N_DEV,) + (tuple(s.shape) if same_src else tuple(s.shape[1:])), s.dtype) for s in srcs)


def _exchange_sems(n):
    return [pltpu.SemaphoreType.DMA(((N_DEV - 1) * n,)), pltpu.SemaphoreType.DMA(((N_DEV - 1) * n,)), pltpu.SemaphoreType.DMA((n,))]


def _exchange(srcs, same_src, name):
    n = len(srcs)

    def body(*refs):
        args = (refs[:n], refs[n:2 * n], *refs[2 * n:], same_src)
        _exchange_start(*args)
        _exchange_wait(*args)

    hbm = pl.BlockSpec(memory_space=pltpu.HBM)
    outs = pl.pallas_call(
        body, name=name, out_shape=_exchange_shapes(srcs, same_src),
        in_specs=[hbm] * n, out_specs=(hbm,) * n, scratch_shapes=_exchange_sems(n),
    )(*srcs)
    return list(outs)


def _full_weights(g_in, g_uq, g_ukv, g_out):
    cat_cols = lambda t: jnp.moveaxis(t, 0, 2).reshape(t.shape[1], t.shape[2], -1)
    return cat_cols(g_in), cat_cols(g_uq), cat_cols(g_ukv), jnp.moveaxis(g_out, 0, 1).reshape(g_out.shape[1], D_MODEL, D_MODEL)


def _grad_blocks(g_in, g_uq, g_ukv, g_out):
    split_cols = lambda t: jnp.moveaxis(t.astype(BF16).reshape(t.shape[0], N_DEV, -1), 1, 0)
    return [split_cols(g_in), split_cols(g_uq), split_cols(g_ukv), g_out.astype(BF16).reshape(N_DEV, D_MODEL // N_DEV, D_MODEL)]


def _layer_weights(w_in, w_uq, w_ukv, w_out):
    z = lambda r, n: jnp.zeros((r, n), BF16)
    c_q, c_kv, k_r = w_in[:, 0:384], w_in[:, 384:640], w_in[:, 640:672]
    gate_a, dil, gate_b = w_in[:, 672:1184], w_in[:, 1184:5792], w_in[:, 5792:6304]
    ga_pad = jnp.pad(gate_a.reshape(D_MODEL, HEADS, VDIM), ((0, 0), (0, 0), (0, LANE - VDIM))).reshape(D_MODEL, HP)
    w_p = jnp.concatenate([ga_pad, dil, gate_b, c_kv, z(D_MODEL, 64), k_r, z(D_MODEL, 32), c_q], axis=1)
    uq = jnp.pad(w_uq.reshape(Q_LORA, HEADS, NOPE + ROPE), ((0, 0), (0, 0), (0, LANE - NOPE - ROPE))).reshape(Q_LORA, HP)
    ukv = w_ukv.reshape(KV_LORA, HEADS, NOPE + VDIM)
    pad64 = lambda t: jnp.pad(t, ((0, 0), (0, 0), (0, LANE - 64))).reshape(KV_LORA, HP)
    uk, uv = pad64(ukv[..., :NOPE]), pad64(ukv[..., NOPE:])
    wa = jnp.pad(w_out[:HEADS * VDIM].reshape(HEADS, VDIM, D_MODEL), ((0, 0), (0, LANE - VDIM), (0, 0))).reshape(HP, D_MODEL)
    wb = w_out[HEADS * VDIM:]
    return dict(p=w_p, uq=uq, uk=uk, uv=uv, wa=wa, wb=wb)


def _unpad_grads(g):
    gp = g["p"]
    seg = lambda cb, n: gp[:, cb * LANE:cb * LANE + n]
    ga = seg(CB_GA, HP).reshape(D_MODEL, HEADS, LANE)[..., :VDIM].reshape(D_MODEL, HEADS * VDIM)
    k_r = gp[:, CB_KR * LANE + NOPE:CB_KR * LANE + NOPE + ROPE]
    g_in = jnp.concatenate([seg(CB_CQ, Q_LORA), seg(CB_CKV, KV_LORA), k_r, ga, seg(CB_DIL, 9 * DIL_W), seg(CB_GB, DIL_W)], axis=1)
    g_uq = g["uq"].reshape(Q_LORA, HEADS, LANE)[..., :NOPE + ROPE].reshape(Q_LORA, -1)
    uk = g["uk"].reshape(KV_LORA, HEADS, LANE)[..., :NOPE]
    uv = g["uv"].reshape(KV_LORA, HEADS, LANE)[..., :VDIM]
    g_ukv = jnp.concatenate([uk, uv], axis=-1).reshape(KV_LORA, -1)
    wa = g["wa"].reshape(HEADS, LANE, D_MODEL)[:, :VDIM].reshape(HEADS * VDIM, D_MODEL)
    g_out = jnp.concatenate([wa, g["wb"]], axis=0)
    return g_in, g_uq, g_ukv, g_out


def _layer_fwd(x, w, norm_g, q_norm_g, kv_norm_g, mla, dil, l, comm=None):
    n = lambda s: f"l{l}_{s}"
    h = _rms_fwd(x, 0, D_MODEL, norm_g, n("norm"))
    p = _mm(h, w["p"], "nn", n("in_proj"))
    cqn = _rms_fwd(p, CB_CQ * LANE // Q_LORA, Q_LORA, q_norm_g, n("q_norm"))
    ckvn = _rms_fwd(p, CB_CKV * LANE // KV_LORA, KV_LORA, kv_norm_g, n("kv_norm"))
    qp = _mm(cqn, w["uq"], "nn", n("q_up"))
    kpre = _mm(ckvn, w["uk"], "nn", n("k_up"))
    v = _mm(ckvn, w["uv"], "nn", n("v_up"), out_dtype=BF16)
    q = _rope_heads(qp, mla["c_q"], mla["s1"], mla["s2"], n("q_rope"), BF16)
    k = _k_assemble(kpre, p, mla, n("k_asm"))
    o, lse, received = _flash_fwd(q, k, v, n("mla_fwd"), comm)
    a = _gate_a(o, p, n("gate_a"))
    dilr, os_, ls_ = [], [], []
    for g, (_, d) in enumerate(DIL_PAIRS):
        dilr.append(_dil_prep(p, dil, g, d, n(f"dil_prep{g}")))
        og, lg = _band_fwd(dilr[g], d, n(f"band{g}_fwd"))
        os_.append(og)
        ls_.append(lg)
    b = _gate_b(os_, ls_, p, n("gate_b"))
    x1 = _mm(a, w["wa"], "nn", n("out_a"), res=x)
    x2 = _mm(b, w["wb"], "nn", n("out_b"), res=x1)
    saved = dict(x=x, h=h, p=p, cqn=cqn, ckvn=ckvn, q=q, k=k, v=v, o=o, lse=lse, a=a, dilr=dilr, os=os_, ls=ls_, b=b)
    return x2, saved, received


def _layer_bwd(dx, sv, w, norm_g, q_norm_g, kv_norm_g, mla, dil, l, comm=None):
    n = lambda s: f"l{l}_{s}"
    g = {}
    da = _mm(dx, w["wa"], "nt", n("d_a"))
    db = _mm(dx, w["wb"], "nt", n("d_b"))
    g["wa"] = _mm(sv["a"], dx, "tn", n("dw_a"))
    g["wb"] = _mm(sv["b"], dx, "tn", n("dw_b"))
    do, dga = _gate_a_bwd(da, sv["o"], sv["p"], n("gate_a_bwd"))
    dgb, dos, dts = _gate_b_bwd(db, sv["os"], sv["ls"], sv["p"], n("gate_b_bwd"))
    ddil = []
    for gi, (_, d) in enumerate(DIL_PAIRS):
        dq = _band_bwd_q(sv["dilr"][gi], d, dos[gi], sv["ls"][gi], dts[gi], dil, n(f"band{gi}_bwd_q"))
        dk, dv = _band_bwd_kv(sv["dilr"][gi], d, dos[gi], sv["ls"][gi], dts[gi], dil, n(f"band{gi}_bwd_kv"))
        ddil += [dq, dk, dv]
    dq, dk, dv, received = _flash_bwd(sv["q"], sv["k"], sv["v"], sv["o"], do, sv["lse"], n("mla_bwd"), comm)
    dqp = _rope_heads(dq, mla["c_q"], -mla["s1"], -mla["s2"], n("q_rope_bwd"), BF16)
    dkr = _kr_bwd(dk, mla, n("kr_bwd"))
    g["uq"] = _mm(sv["cqn"], dqp, "tn", n("dw_uq"))
    g["uk"] = _mm(sv["ckvn"], dk, "tn", n("dw_uk"))
    g["uv"] = _mm(sv["ckvn"], dv, "tn", n("dw_uv"))
    dcqn = _mm(dqp, w["uq"], "nt", n("d_cqn"))
    dckvn = _mm(dk, w["uk"], "nt", n("d_ckvn_k"))
    dckvn = _mm(dv, w["uv"], "nt", n("d_ckvn_v"), res=dckvn)
    dcq, g_qn = _rms_bwd(sv["p"], CB_CQ * LANE // Q_LORA, Q_LORA, dcqn, q_norm_g, n("q_norm_bwd"), BF16)
    dckv, g_kvn = _rms_bwd(sv["p"], CB_CKV * LANE // KV_LORA, KV_LORA, dckvn, kv_norm_g, n("kv_norm_bwd"), BF16)
    dp = jnp.concatenate([dga] + ddil + [dgb, dckv, dkr, dcq], axis=1)
    g["p"] = _mm(sv["h"], dp, "tn", n("dw_in"))
    dh = _mm(dp, w["p"], "nt", n("d_h"))
    dx_in, g_n = _rms_bwd(sv["x"], 0, D_MODEL, dh, norm_g, n("norm_bwd"), F32, res=dx)
    return dx_in, g, g_n, g_qn, g_kvn, received


_SMALL_ROWS = 16


def _pack_small(norm, qn, kvn, fin, loss_row=None):
    padc = lambda t: jnp.pad(t, ((0, 0), (0, D_MODEL - t.shape[1])))
    extra = jnp.zeros((1, D_MODEL), F32) if loss_row is None else loss_row
    return jnp.concatenate([norm, padc(qn), padc(kvn), fin.reshape(1, D_MODEL), extra, jnp.zeros((2, D_MODEL), F32)], axis=0)


def _unpack_small(p):
    return (p[0:4], p[4:8, :Q_LORA], p[8:12, :KV_LORA], p[12]), p[13, 0]


def kernel(x, norm_g, w_in, q_norm_g, kv_norm_g, w_uq, w_ukv, w_out, final_g, loss_target, m_norm_g, m_w_in, m_q_norm_g, m_kv_norm_g, m_w_uq, m_w_ukv, m_w_out, m_final_g, v_norm_g, v_w_in, v_q_norm_g, v_kv_norm_g, v_w_uq, v_w_ukv, v_w_out, v_final_g):
    S = x.shape[1]
    xs = x.reshape(S, D_MODEL)
    target = loss_target.reshape(S, D_MODEL)

    wb = [t.astype(BF16) for t in (w_in, w_uq, w_ukv, w_out)]
    first = _full_weights(*_exchange([t[0:1] for t in wb], True, "gather_weights0"))
    ws = [_layer_weights(*(t[0] for t in first))]
    mla, dil = _rope_tables(S)

    saved = []
    h = xs
    for l in range(DEPTH):
        comm = ([t[1:] for t in wb], True) if l == 0 else None
        h, sv, received = _layer_fwd(h, ws[l], norm_g[l], q_norm_g[l], kv_norm_g[l], mla, dil, l, comm)
        saved.append(sv)
        if l == 0:
            rest = _full_weights(*received)
            ws += [_layer_weights(*(t[i] for t in rest)) for i in range(DEPTH - 1)]
    dx, g_final, loss_row = _loss_head(h, target, final_g, "loss_head")
    g_norm, g_qn, g_kvn, parts_l = [None] * DEPTH, [None] * DEPTH, [None] * DEPTH, [None] * DEPTH
    blocks = None
    for l in reversed(range(DEPTH)):
        comm = (blocks, False) if blocks is not None else None
        dx, g, g_norm[l], g_qn[l], g_kvn[l], received = _layer_bwd(dx, saved[l], ws[l], norm_g[l], q_norm_g[l], kv_norm_g[l], mla, dil, l, comm)
        if blocks is not None:
            parts_l[l + 1] = received
        blocks = _grad_blocks(*_unpad_grads(g))
    parts_l[0] = _exchange(blocks, False, "exchange_grads0")

    parts = [jnp.stack([parts_l[l][t] for l in range(DEPTH)], axis=1) for t in range(4)]
    sh = []
    for t, (pt, w, m, v) in enumerate(zip(parts, (w_in, w_uq, w_ukv, w_out), (m_w_in, m_w_uq, m_w_ukv, m_w_out), (v_w_in, v_w_uq, v_w_ukv, v_w_out))):
        two = lambda a: a.reshape(-1, a.shape[-1])
        outs = _adamw(pt.reshape(N_DEV, -1, pt.shape[-1]), two(w), two(m), two(v), f"adamw_{t}")
        sh.append([o.reshape(w.shape) for o in outs])

    small = _pack_small(jnp.concatenate(g_norm, 0), jnp.concatenate(g_qn, 0), jnp.concatenate(g_kvn, 0), g_final, loss_row)
    (small_parts,) = _exchange([small], True, "gather_small")
    souts = _adamw(small_parts, _pack_small(norm_g, q_norm_g, kv_norm_g, final_g), _pack_small(m_norm_g, m_q_norm_g, m_kv_norm_g, m_final_g),
                   _pack_small(v_norm_g, v_q_norm_g, v_kv_norm_g, v_final_g), "adamw_small")
    (g_sm, loss), (d_sm, _), (m_sm, _), (v_sm, _) = (_unpack_small(t) for t in souts)

    def order(sm, k):
        return (sm[0], sh[0][k], sm[1], sm[2], sh[1][k], sh[2][k], sh[3][k], sm[3])

    return (loss, dx.reshape(1, S, D_MODEL), *order(g_sm, 0), *order(d_sm, 1), *order(m_sm, 2), *order(v_sm, 3))
```

```python
import functools
import math

import jax
import jax.numpy as jnp
from jax import lax
from jax.experimental import pallas as pl
from jax.experimental.pallas import tpu as pltpu

F32 = jnp.float32
BF16 = jnp.bfloat16

D_MODEL = 1024
DEPTH = 4
HEADS = 8
NOPE = 64
ROPE = 32
VDIM = 64
Q_LORA = 384
KV_LORA = 256
DIL_PAIRS = ((128, 1), (512, 4), (2048, 16))
DIL_HD = 64
DIL_W = 512
ROT = 16
HALF = 64
THETA = 500000.0
EPS = 1e-6
IN_WIDTH = 6304
N_DEV = 8

LANE = 128
CB_GA, CB_DIL, CB_GB, CB_CKV, CB_KR, CB_CQ = 0, 8, 44, 48, 50, 51
NP = 54 * LANE
HP = HEADS * LANE

ADAM_LR = 0.001
ADAM_B1 = 0.9
ADAM_B2 = 0.999
ADAM_EPS = 1e-08
ADAM_WD = 0.01
ADAM_STEP = 10

VMEM_LIMIT = 48 * 1024 * 1024
ROW_TILE = 512
SUB = 128

_NT = (((1,), (1,)), ((), ()))
_NN = (((1,), (0,)), ((), ()))
_TN = (((0,), (0,)), ((), ()))


def _params(n_axes):
    return pltpu.CompilerParams(dimension_semantics=("arbitrary",) * n_axes, vmem_limit_bytes=VMEM_LIMIT)


def _pick(n, cands):
    for c in cands:
        if n % c == 0:
            return c
    raise ValueError(f"no tile for {n}")


def _mm(a, b, mode, name, out_dtype=F32, res=None):
    if mode == "nn":
        (M, K), (K2, N) = a.shape, b.shape
    elif mode == "nt":
        (M, K), (N, K2) = a.shape, b.shape
    else:
        (K, M), (K2, N) = a.shape, b.shape
    assert K == K2, (a.shape, b.shape, mode)
    tm = _pick(M, (1024, 512, 384, 256, 128))
    tn = _pick(N, (1152, 1024, 768, 640, 512, 384, 256, 128))
    tk = _pick(K, (1152, 1024, 768, 640, 512, 384, 256, 128))
    nk = K // tk
    dims = {"nn": _NN, "nt": _NT, "tn": _TN}[mode]

    def body(*refs):
        if res is not None:
            a_ref, b_ref, r_ref, o_ref = refs[:4]
        else:
            a_ref, b_ref, o_ref = refs[:3]
            r_ref = None
        part = lax.dot_general(a_ref[...].astype(BF16), b_ref[...].astype(BF16), dims, preferred_element_type=F32)

        def finish(acc):
            if r_ref is not None:
                acc = acc + r_ref[...]
            o_ref[...] = acc.astype(out_dtype)

        if nk == 1:
            finish(part)
        else:
            acc_ref = refs[-1]
            k = pl.program_id(2)

            @pl.when(k == 0)
            def _():
                acc_ref[...] = part

            @pl.when(k > 0)
            def _():
                acc_ref[...] += part

            @pl.when(k == nk - 1)
            def _():
                finish(acc_ref[...])

    if mode == "nn":
        a_spec = pl.BlockSpec((tm, tk), lambda i, j, k: (i, k))
        b_spec = pl.BlockSpec((tk, tn), lambda i, j, k: (k, j))
    elif mode == "nt":
        a_spec = pl.BlockSpec((tm, tk), lambda i, j, k: (i, k))
        b_spec = pl.BlockSpec((tn, tk), lambda i, j, k: (j, k))
    else:
        a_spec = pl.BlockSpec((tk, tm), lambda i, j, k: (k, i))
        b_spec = pl.BlockSpec((tk, tn), lambda i, j, k: (k, j))
    o_spec = pl.BlockSpec((tm, tn), lambda i, j, k: (i, j))
    in_specs = [a_spec, b_spec] + ([o_spec] if res is not None else [])
    args = (a, b) + ((res,) if res is not None else ())
    return pl.pallas_call(
        body, name=name, grid=(M // tm, N // tn, nk), out_shape=jax.ShapeDtypeStruct((M, N), out_dtype),
        in_specs=in_specs, out_specs=o_spec,
        scratch_shapes=[pltpu.VMEM((tm, tn), F32)] if nk > 1 else [],
        compiler_params=_params(3),
    )(*args)


def _rms_fwd(src, cb, width, g, name):
    S = src.shape[0]
    tm = ROW_TILE

    def body(x_ref, g_ref, o_ref):
        x = x_ref[...]
        r = lax.rsqrt(jnp.mean(x * x, axis=-1, keepdims=True) + EPS)
        o_ref[...] = (x * r * g_ref[...]).astype(BF16)

    return pl.pallas_call(
        body, name=name, grid=(S // tm,), out_shape=jax.ShapeDtypeStruct((S, width), BF16),
        in_specs=[pl.BlockSpec((tm, width), lambda i: (i, cb)), pl.BlockSpec((1, width), lambda i: (0, 0))],
        out_specs=pl.BlockSpec((tm, width), lambda i: (i, 0)), compiler_params=_params(1),
    )(src, g.reshape(1, width))


def _rms_bwd(src, cb, width, dy, g, name, out_dtype, res=None):
    S = src.shape[0]
    tm = ROW_TILE

    def body(*refs):
        if res is not None:
            x_ref, dy_ref, g_ref, r_ref, dx_ref, dg_ref = refs
        else:
            x_ref, dy_ref, g_ref, dx_ref, dg_ref = refs
            r_ref = None
        x = x_ref[...]
        dy = dy_ref[...]
        r = lax.rsqrt(jnp.mean(x * x, axis=-1, keepdims=True) + EPS)
        dyg = dy * g_ref[...]
        c = jnp.mean(dyg * x, axis=-1, keepdims=True)
        dx = r * dyg - x * (r * r * r) * c
        if r_ref is not None:
            dx = dx + r_ref[...]
        dx_ref[...] = dx.astype(out_dtype)
        part = jnp.sum(dy * x * r, axis=0, keepdims=True)

        @pl.when(pl.program_id(0) == 0)
        def _():
            dg_ref[...] = part

        @pl.when(pl.program_id(0) > 0)
        def _():
            dg_ref[...] += part

    row = pl.BlockSpec((tm, width), lambda i: (i, 0))
    in_specs = [pl.BlockSpec((tm, width), lambda i: (i, cb)), row, pl.BlockSpec((1, width), lambda i: (0, 0))]
    args = [src, dy, g.reshape(1, width)]
    if res is not None:
        in_specs.append(row)
        args.append(res)
    return pl.pallas_call(
        body, name=name, grid=(S // tm,),
        out_shape=(jax.ShapeDtypeStruct((S, width), out_dtype), jax.ShapeDtypeStruct((1, width), F32)),
        in_specs=in_specs, out_specs=(row, pl.BlockSpec((1, width), lambda i: (0, 0))),
        compiler_params=_params(1),
    )(*args)


def _rot(x, c, s1, s2, h):
    return x * c + pltpu.roll(x, x.shape[1] - h, 1) * s1 + pltpu.roll(x, h, 1) * s2


def _rope_tables(S):
    def tables(dim):
        inv = 1.0 / (THETA ** (jnp.arange(0, dim, 2, dtype=F32) / dim))
        ang = jnp.arange(S, dtype=F32)[:, None] * inv[None, :]
        return jnp.cos(ang), jnp.sin(ang)

    cm, sm = tables(ROPE)
    cd, sd = tables(ROT)
    z = lambda n: jnp.zeros((S, n), F32)
    o = lambda n: jnp.ones((S, n), F32)
    mla = dict(
        c_q=jnp.concatenate([o(64), cm, cm, z(32)], 1),
        c_kr=jnp.concatenate([z(64), cm, cm, z(32)], 1),
        s1=jnp.concatenate([z(64), -sm, z(16), z(32)], 1),
        s2=jnp.concatenate([z(64), z(16), sm, z(32)], 1),
    )
    one = lambda a, b, c: jnp.concatenate([a, b, c, a, b, c], 1)
    dil = dict(c=one(cd, cd, o(48)), s1=one(-sd, z(8), z(48)), s2=one(z(8), sd, z(48)))
    return mla, dil


def _rope_heads(src, c, s1, s2, name, out_dtype):
    S = src.shape[0]
    tm = ROW_TILE

    def body(x_ref, c_ref, s1_ref, s2_ref, o_ref):
        cv, s1v, s2v = c_ref[...], s1_ref[...], s2_ref[...]
        for h in range(HEADS):
            sl = slice(h * LANE, (h + 1) * LANE)
            o_ref[:, sl] = _rot(x_ref[:, sl], cv, s1v, s2v, ROPE // 2).astype(out_dtype)

    tab = pl.BlockSpec((tm, LANE), lambda i: (i, 0))
    wide = pl.BlockSpec((tm, HP), lambda i: (i, 0))
    return pl.pallas_call(
        body, name=name, grid=(S // tm,), out_shape=jax.ShapeDtypeStruct((S, HP), out_dtype),
        in_specs=[wide, tab, tab, tab], out_specs=wide, compiler_params=_params(1),
    )(src, c, s1, s2)


def _k_assemble(kpre, p, mla, name):
    S = kpre.shape[0]
    tm = ROW_TILE

    def body(k_ref, kr_ref, c_ref, s1_ref, s2_ref, o_ref):
        r = _rot(kr_ref[...], c_ref[...], s1_ref[...], s2_ref[...], ROPE // 2)
        for h in range(HEADS):
            sl = slice(h * LANE, (h + 1) * LANE)
            o_ref[:, sl] = (k_ref[:, sl] + r).astype(BF16)

    tab = pl.BlockSpec((tm, LANE), lambda i: (i, 0))
    wide = pl.BlockSpec((tm, HP), lambda i: (i, 0))
    return pl.pallas_call(
        body, name=name, grid=(S // tm,), out_shape=jax.ShapeDtypeStruct((S, HP), BF16),
        in_specs=[wide, pl.BlockSpec((tm, LANE), lambda i: (i, CB_KR)), tab, tab, tab],
        out_specs=wide, compiler_params=_params(1),
    )(kpre, p, mla["c_kr"], mla["s1"], mla["s2"])


def _kr_bwd(dk, mla, name):
    S = dk.shape[0]
    tm = ROW_TILE

    def body(dk_ref, c_ref, s1_ref, s2_ref, o_ref):
        t = dk_ref[:, 0:LANE]
        for h in range(1, HEADS):
            t = t + dk_ref[:, h * LANE:(h + 1) * LANE]
        lane = lax.broadcasted_iota(jnp.int32, (1, LANE), 1)
        t = jnp.where((lane >= NOPE) & (lane < NOPE + ROPE), t, 0.0)
        o_ref[...] = _rot(t, c_ref[...], -s1_ref[...], -s2_ref[...], ROPE // 2).astype(BF16)

    tab = pl.BlockSpec((tm, LANE), lambda i: (i, 0))
    return pl.pallas_call(
        body, name=name, grid=(S // tm,), out_shape=jax.ShapeDtypeStruct((S, LANE), BF16),
        in_specs=[pl.BlockSpec((tm, HP), lambda i: (i, 0)), tab, tab, tab],
        out_specs=tab, compiler_params=_params(1),
    )(dk, mla["c_kr"], mla["s1"], mla["s2"])


def _dil_prep(p, dil, g, d, name):
    S = p.shape[0]
    tm = ROW_TILE
    first = CB_DIL * LANE // DIL_W + 3 * g
    nc = DIL_W // LANE
    n = tm // d

    def body(q_ref, k_ref, v_ref, c_ref, s1_ref, s2_ref, o_ref, scr):
        cv, s1v, s2v = (jnp.tile(t[...], (1, nc)) for t in (c_ref, s1_ref, s2_ref))
        ys = (_rot(q_ref[...], cv, s1v, s2v, ROT // 2) * (DIL_HD ** -0.5), _rot(k_ref[...], cv, s1v, s2v, ROT // 2), v_ref[...])
        if d == 1:
            for t, y in enumerate(ys):
                o_ref[:, t * DIL_W:(t + 1) * DIL_W] = y.astype(BF16)
        else:
            for t, y in enumerate(ys):
                for c in range(nc):
                    scr[t * nc + c] = y[:, c * LANE:(c + 1) * LANE]
            for r in range(d):
                for tc in range(3 * nc):
                    col = r * 3 * DIL_W + tc * LANE
                    o_ref[:, col:col + LANE] = scr.at[tc][pl.ds(r, n, stride=d), :].astype(BF16)

    tab = pl.BlockSpec((tm, LANE), lambda i: (i, 0))
    chunk = lambda t: pl.BlockSpec((tm, DIL_W), lambda i: (i, first + t))
    return pl.pallas_call(
        body, name=name, grid=(S // tm,), out_shape=jax.ShapeDtypeStruct((S // d, d * 3 * DIL_W), BF16),
        in_specs=[chunk(0), chunk(1), chunk(2), tab, tab, tab], out_specs=_view_spec(d, 3 * DIL_W),
        scratch_shapes=[pltpu.VMEM((3 * nc, tm, LANE), F32)], compiler_params=_params(1),
    )(p, p, p, dil["c"], dil["s1"], dil["s2"])


def _grid_ends(dims):
    ids = [pl.program_id(a) for a in range(len(dims))]
    first = functools.reduce(jnp.logical_and, [i == 0 for i in ids])
    last = functools.reduce(jnp.logical_and, [i == n - 1 for i, n in zip(ids, dims)])
    return first, last


def _flash_fwd(q, k, v, name, comm=None):
    S = q.shape[0]
    tq = _pick(S, (1024, 512))
    tk = _pick(S, (2048, 1024, 512))
    nk = S // tk
    c2 = (NOPE + ROPE) ** -0.5 * math.log2(math.e)
    srcs, same_src = comm if comm is not None else ([], True)
    n = len(srcs)
    grid = (HEADS, S // tq, nk)

    def body(*refs):
        q_ref, k_ref, v_ref = refs[:3]
        o_ref, lse_ref = refs[3 + n:5 + n]
        m_s, acc_s = refs[5 + 2 * n:7 + 2 * n]
        ex = (refs[3:3 + n], refs[5 + n:5 + 2 * n], *refs[7 + 2 * n:], same_src)
        j = pl.program_id(2)
        if n:
            first, last = _grid_ends(grid)
            pl.when(first)(lambda: _exchange_start(*ex))

        @pl.when(j == 0)
        def _():
            m_s[...] = jnp.full(m_s.shape, -jnp.inf, F32)
            acc_s[...] = jnp.zeros(acc_s.shape, F32)

        lane = lax.broadcasted_iota(jnp.int32, (1, LANE), 1)
        vv = jnp.where(lane == VDIM, jnp.ones((), BF16), v_ref[...])
        t = lax.dot_general(q_ref[...], k_ref[...], _NT, preferred_element_type=F32) * c2
        m_prev = m_s[...]
        m_new = jnp.maximum(m_prev, jnp.max(t, axis=-1, keepdims=True))
        alpha = jnp.exp2(m_prev - m_new)
        e = jnp.exp2(t - jnp.tile(m_new, (1, tk // LANE)))
        acc_s[...] = alpha * acc_s[...] + jnp.dot(e.astype(BF16), vv, preferred_element_type=F32)
        m_s[...] = m_new

        @pl.when(j == nk - 1)
        def _():
            acc = acc_s[...]
            l = acc[:, VDIM:VDIM + 1]
            o_ref[...] = jnp.where(lane < VDIM, acc / l, 0.0)
            lse_ref[...] = (m_s[...] + jnp.log2(l)) * math.log(2.0)

        if n:
            pl.when(last)(lambda: _exchange_wait(*ex))

    qs = pl.BlockSpec((tq, LANE), lambda h, i, j: (i, h))
    ks = pl.BlockSpec((tk, LANE), lambda h, i, j: (j, h))
    hbm = pl.BlockSpec(memory_space=pltpu.HBM)
    outs = pl.pallas_call(
        body, name=name, grid=grid,
        out_shape=(jax.ShapeDtypeStruct((S, HP), F32), jax.ShapeDtypeStruct((S, HP), F32)) + _exchange_shapes(srcs, same_src),
        in_specs=[qs, ks, ks] + [hbm] * n, out_specs=(qs, qs) + (hbm,) * n,
        scratch_shapes=[pltpu.VMEM((tq, LANE), F32), pltpu.VMEM((tq, LANE), F32)] + (_exchange_sems(n) if n else []),
        compiler_params=_params(3),
    )(q, k, v, *srcs)
    return outs[0], outs[1], list(outs[2:])


def _flash_bwd(q, k, v, o, do, lse, name, comm=None):
    S = q.shape[0]
    tq = _pick(S, (1024, 512))
    tk = _pick(S, (2048, 1024, 512))
    nq = S // tq
    scale = (NOPE + ROPE) ** -0.5
    srcs, same_src = comm if comm is not None else ([], True)
    n = len(srcs)
    grid = (HEADS, S // tk, nq)

    def body(*refs):
        q_ref, k_ref, v_ref, o_ref, do_ref, lse_ref = refs[:6]
        dq_ref, dk_ref, dv_ref = refs[6 + n:9 + n]
        dk_s, dv_s = refs[9 + 2 * n:11 + 2 * n]
        ex = (refs[6:6 + n], refs[9 + n:9 + 2 * n], *refs[11 + 2 * n:], same_src)
        j = pl.program_id(1)
        i = pl.program_id(2)
        if n:
            first, last = _grid_ends(grid)
            pl.when(first)(lambda: _exchange_start(*ex))
        qv = q_ref[...]
        kv = k_ref[...]
        do = do_ref[...]
        dob = do.astype(BF16)
        lse_row = jnp.transpose(lse_ref[...])[0:1, :]
        delta_row = jnp.sum(jnp.transpose(do * o_ref[...]), axis=0, keepdims=True)
        st = lax.dot_general(kv, qv, _NT, preferred_element_type=F32) * scale
        pt = jnp.exp(st - lse_row)
        dpt = lax.dot_general(v_ref[...], dob, _NT, preferred_element_type=F32)
        dst = (pt * (dpt - delta_row)).astype(BF16)
        dv_part = jnp.dot(pt.astype(BF16), dob, preferred_element_type=F32)
        dk_part = jnp.dot(dst, qv, preferred_element_type=F32)
        dq_part = jnp.transpose(jnp.dot(jnp.transpose(kv), dst, preferred_element_type=F32)) * scale

        @pl.when(i == 0)
        def _():
            dk_s[...] = dk_part
            dv_s[...] = dv_part

        @pl.when(i > 0)
        def _():
            dk_s[...] += dk_part
            dv_s[...] += dv_part

        rows = pl.ds(pl.multiple_of(i * tq, tq), tq)

        @pl.when(j == 0)
        def _():
            dq_ref[rows, :] = dq_part

        @pl.when(j > 0)
        def _():
            dq_ref[rows, :] += dq_part

        @pl.when(i == nq - 1)
        def _():
            dk_ref[...] = dk_s[...] * scale
            dv_ref[...] = dv_s[...].astype(BF16)

        if n:
            pl.when(last)(lambda: _exchange_wait(*ex))

    qs = pl.BlockSpec((tq, LANE), lambda h, j, i: (i, h))
    ks = pl.BlockSpec((tk, LANE), lambda h, j, i: (j, h))
    hbm = pl.BlockSpec(memory_space=pltpu.HBM)
    outs = pl.pallas_call(
        body, name=name, grid=grid,
        out_shape=(jax.ShapeDtypeStruct((S, HP), F32), jax.ShapeDtypeStruct((S, HP), F32), jax.ShapeDtypeStruct((S, HP), BF16))
        + _exchange_shapes(srcs, same_src),
        in_specs=[qs, ks, ks, qs, qs, qs] + [hbm] * n,
        out_specs=(pl.BlockSpec((S, LANE), lambda h, j, i: (0, h)), ks, ks) + (hbm,) * n,
        scratch_shapes=[pltpu.VMEM((tk, LANE), F32), pltpu.VMEM((tk, LANE), F32)] + (_exchange_sems(n) if n else []),
        compiler_params=_params(3),
    )(q, k, v, o, do, lse, *srcs)
    return outs[0], outs[1], outs[2], list(outs[3:])


def _band_tiles(L):
    tq = min(512, L)
    return tq, tq // SUB, tq // HALF, L // HALF


def _halo_specs(tq, rpb, n64, col):
    prev = pl.BlockSpec((HALF, DIL_W), lambda r, i: (jnp.maximum(rpb * i - 1, 0), col(r)))
    cur = pl.BlockSpec((tq, DIL_W), lambda r, i: (i, col(r)))
    nxt = pl.BlockSpec((HALF, DIL_W), lambda r, i: (jnp.minimum(rpb * i + rpb, n64 - 1), col(r)))
    return [prev, cur, nxt]


def _fill(buf, prev_ref, cur_ref, next_ref, tq):
    buf[0:HALF, :] = prev_ref[...]
    buf[HALF:HALF + tq, :] = cur_ref[...]
    buf[HALF + tq:HALF + tq + HALF, :] = next_ref[...]


def _lo_lanes():
    return lax.broadcasted_iota(jnp.int32, (1, LANE), 1) < DIL_HD


def _stack_heads(x, lo):
    zero = jnp.zeros_like(x)
    return jnp.concatenate([jnp.where(lo, x, zero), jnp.where(lo, zero, x)], axis=0)


def _stack_cols(x):
    return jnp.concatenate([x[:, 0:1], x[:, DIL_HD:DIL_HD + 1]], axis=0)


def _band_valid(q0, k0, nq, nk, L, bound_q):
    qpos = q0 + lax.broadcasted_iota(jnp.int32, (nq, 1), 0)
    kpos = k0 + lax.broadcasted_iota(jnp.int32, (1, nk), 1)
    side = qpos if bound_q else kpos
    return (jnp.abs(qpos - kpos) <= HALF) & (side >= 0) & (side < L)


def _band_fwd(dilr, d, name):
    L = dilr.shape[0]
    S = L * d
    tq, nsub, rpb, n64 = _band_tiles(L)
    view = dilr
    win = SUB + 2 * HALF

    def body(q_ref, kp_ref, kc_ref, kn_ref, vp_ref, vc_ref, vn_ref, o_ref, lse_ref, kbuf, vbuf):
        i = pl.program_id(1)
        _fill(kbuf, kp_ref, kc_ref, kn_ref, tq)
        _fill(vbuf, vp_ref, vc_ref, vn_ref, tq)
        lo = _lo_lanes()
        for a in range(nsub):
            r0 = a * SUB
            rows = slice(r0, r0 + SUB)
            valid = _band_valid(i * tq + r0, i * tq + r0 - HALF, SUB, win, L, False)
            valid2 = jnp.concatenate([valid, valid], axis=0)
            for hp in range(4):
                cs = slice(hp * LANE, (hp + 1) * LANE)
                q = q_ref[rows, cs]
                kw = kbuf[r0:r0 + win, cs]
                vw = vbuf[r0:r0 + win, cs]
                q2 = _stack_heads(q, lo)
                s = lax.dot_general(q2, kw, _NT, preferred_element_type=F32)
                s = jnp.where(valid2, s, -jnp.inf)
                m = jnp.max(s, axis=-1, keepdims=True)
                e = jnp.exp(s - m)
                l = jnp.sum(e, axis=-1, keepdims=True)
                o2 = jnp.dot(e.astype(BF16), vw, preferred_element_type=F32) / l
                lse2 = m + jnp.log(l)
                o_ref[rows, cs] = jnp.where(lo, o2[:SUB], o2[SUB:])
                lse_ref[rows, cs] = jnp.where(lo, lse2[:SUB], lse2[SUB:])

    out_spec = pl.BlockSpec((tq, DIL_W), lambda r, i: (i, r))
    o, lse = pl.pallas_call(
        body, name=name, grid=(d, L // tq),
        out_shape=(jax.ShapeDtypeStruct((L, d * DIL_W), F32), jax.ShapeDtypeStruct((L, d * DIL_W), F32)),
        in_specs=[pl.BlockSpec((tq, DIL_W), lambda r, i: (i, r * 3))]
        + _halo_specs(tq, rpb, n64, lambda r: r * 3 + 1) + _halo_specs(tq, rpb, n64, lambda r: r * 3 + 2),
        out_specs=(out_spec, out_spec),
        scratch_shapes=[pltpu.VMEM((tq + 2 * HALF, DIL_W), BF16), pltpu.VMEM((tq + 2 * HALF, DIL_W), BF16)],
        compiler_params=_params(2),
    )(view, view, view, view, view, view, view)
    return o, lse


def _band_bwd_q(dilr, d, do, lse, dlt, dil, name):
    L = dilr.shape[0]
    S = L * d
    tq, nsub, rpb, n64 = _band_tiles(L)
    view = dilr
    v4 = lambda t: t.reshape(L, d * DIL_W)
    tv = lambda t: t.reshape(L, d * LANE)
    win = SUB + 2 * HALF

    def body(q_ref, kp_ref, kc_ref, kn_ref, vp_ref, vc_ref, vn_ref, do_ref, lse_ref, dlt_ref, c_ref, s1_ref, s2_ref, dq_ref,
             kbuf, vbuf):
        i = pl.program_id(1)
        _fill(kbuf, kp_ref, kc_ref, kn_ref, tq)
        _fill(vbuf, vp_ref, vc_ref, vn_ref, tq)
        lo = _lo_lanes()
        for a in range(nsub):
            r0 = a * SUB
            rows = slice(r0, r0 + SUB)
            valid = _band_valid(i * tq + r0, i * tq + r0 - HALF, SUB, win, L, False)
            valid2 = jnp.concatenate([valid, valid], axis=0)
            cv, s1v, s2v = c_ref[rows, :], s1_ref[rows, :], s2_ref[rows, :]
            for hp in range(4):
                cs = slice(hp * LANE, (hp + 1) * LANE)
                kw = kbuf[r0:r0 + win, cs]
                vw = vbuf[r0:r0 + win, cs]
                q2 = _stack_heads(q_ref[rows, cs], lo)
                do2 = _stack_heads(do_ref[rows, cs], lo).astype(BF16)
                s = lax.dot_general(q2, kw, _NT, preferred_element_type=F32)
                p = jnp.where(valid2, jnp.exp(s - _stack_cols(lse_ref[rows, cs])), 0.0)
                dp = lax.dot_general(do2, vw, _NT, preferred_element_type=F32)
                ds = (p * (dp - _stack_cols(dlt_ref[rows, cs]))).astype(BF16)
                dq2 = jnp.dot(ds, kw, preferred_element_type=F32)
                dq = jnp.where(lo, dq2[:SUB], dq2[SUB:])
                dq_ref[rows, cs] = (_rot(dq, cv, -s1v, -s2v, ROT // 2) * (DIL_HD ** -0.5)).astype(BF16)

    row = pl.BlockSpec((tq, DIL_W), lambda r, i: (i, r))
    tab = pl.BlockSpec((tq, LANE), lambda r, i: (i, r))
    dq = pl.pallas_call(
        body, name=name, grid=(d, L // tq), out_shape=jax.ShapeDtypeStruct((L, d * DIL_W), BF16),
        in_specs=[pl.BlockSpec((tq, DIL_W), lambda r, i: (i, r * 3))]
        + _halo_specs(tq, rpb, n64, lambda r: r * 3 + 1) + _halo_specs(tq, rpb, n64, lambda r: r * 3 + 2)
        + [row, row, row, tab, tab, tab],
        out_specs=row,
        scratch_shapes=[pltpu.VMEM((tq + 2 * HALF, DIL_W), BF16), pltpu.VMEM((tq + 2 * HALF, DIL_W), BF16)],
        compiler_params=_params(2),
    )(view, view, view, view, view, view, view, v4(do), v4(lse), v4(dlt), tv(dil["c"]), tv(dil["s1"]), tv(dil["s2"]))
    return dq


def _band_bwd_kv(dilr, d, do, lse, dlt, dil, name):
    L = dilr.shape[0]
    S = L * d
    tq, nsub, rpb, n64 = _band_tiles(L)
    view = dilr
    v4 = lambda t: t.reshape(L, d * DIL_W)
    tv = lambda t: t.reshape(L, d * LANE)
    win = SUB + 2 * HALF

    def body(k_ref, v_ref, qp_ref, qc_ref, qn_ref, dop_ref, doc_ref, don_ref, lp_ref, lc_ref, ln_ref, tp_ref, tc_ref, tn_ref,
             c_ref, s1_ref, s2_ref, dk_ref, dv_ref, qbuf, dobuf, lbuf, tbuf):
        j = pl.program_id(1)
        _fill(qbuf, qp_ref, qc_ref, qn_ref, tq)
        _fill(dobuf, dop_ref, doc_ref, don_ref, tq)
        _fill(lbuf, lp_ref, lc_ref, ln_ref, tq)
        _fill(tbuf, tp_ref, tc_ref, tn_ref, tq)
        lo = _lo_lanes()
        quarter = (lax.broadcasted_iota(jnp.int32, (1, LANE), 1) & (DIL_HD - 1)) < DIL_HD // 2
        for a in range(nsub):
            r0 = a * SUB
            rows = slice(r0, r0 + SUB)
            wrows = slice(r0, r0 + win)
            kpos = j * tq + r0 + lax.broadcasted_iota(jnp.int32, (SUB, 1), 0)
            qpos = j * tq + r0 - HALF + lax.broadcasted_iota(jnp.int32, (1, win), 1)
            valid = (jnp.abs(qpos - kpos) <= HALF) & (qpos >= 0) & (qpos < L)
            valid2 = jnp.concatenate([valid, valid], axis=1)
            cv, s1v, s2v = c_ref[rows, :], s1_ref[rows, :], s2_ref[rows, :]
            for hp in range(4):
                cs = slice(hp * LANE, (hp + 1) * LANE)
                k = k_ref[rows, cs]
                v = v_ref[rows, cs]
                q2 = _stack_heads(qbuf[wrows, cs], lo)
                do2 = _stack_heads(dobuf[wrows, cs], lo).astype(BF16)
                zt = jnp.transpose(jnp.where(quarter, lbuf[wrows, cs], tbuf[wrows, cs]))
                lse_row = jnp.concatenate([zt[0:1, :], zt[DIL_HD:DIL_HD + 1, :]], axis=1)
                dlt_row = jnp.concatenate([zt[DIL_HD // 2:DIL_HD // 2 + 1, :], zt[3 * DIL_HD // 2:3 * DIL_HD // 2 + 1, :]], axis=1)
                st = lax.dot_general(k, q2, _NT, preferred_element_type=F32)
                pt = jnp.where(valid2, jnp.exp(st - lse_row), 0.0)
                dv = jnp.dot(pt.astype(BF16), do2, preferred_element_type=F32)
                dpt = lax.dot_general(v, do2, _NT, preferred_element_type=F32)
                dst = (pt * (dpt - dlt_row)).astype(BF16)
                dk = jnp.dot(dst, q2, preferred_element_type=F32)
                dk_ref[rows, cs] = _rot(dk, cv, -s1v, -s2v, ROT // 2).astype(BF16)
                dv_ref[rows, cs] = dv.astype(BF16)

    row = pl.BlockSpec((tq, DIL_W), lambda r, i: (i, r))
    tab = pl.BlockSpec((tq, LANE), lambda r, i: (i, r))
    halo = _halo_specs(tq, rpb, n64, lambda r: r)
    hb = tq + 2 * HALF
    dk, dv = pl.pallas_call(
        body, name=name, grid=(d, L // tq),
        out_shape=(jax.ShapeDtypeStruct((L, d * DIL_W), BF16), jax.ShapeDtypeStruct((L, d * DIL_W), BF16)),
        in_specs=[pl.BlockSpec((tq, DIL_W), lambda r, i: (i, r * 3 + 1)), pl.BlockSpec((tq, DIL_W), lambda r, i: (i, r * 3 + 2))]
        + _halo_specs(tq, rpb, n64, lambda r: r * 3) + halo + halo + halo + [tab, tab, tab],
        out_specs=(row, row),
        scratch_shapes=[pltpu.VMEM((hb, DIL_W), BF16), pltpu.VMEM((hb, DIL_W), F32), pltpu.VMEM((hb, DIL_W), F32), pltpu.VMEM((hb, DIL_W), F32)],
        compiler_params=_params(2),
    )(view, view, view, view, view, v4(do), v4(do), v4(do), v4(lse), v4(lse), v4(lse), v4(dlt), v4(dlt), v4(dlt),
      tv(dil["c"]), tv(dil["s1"]), tv(dil["s2"]))
    return dk, dv


def _sigmoid(x):
    return 1.0 / (1.0 + jnp.exp(-x))


def _gate_a(o, p, name):
    S = o.shape[0]
    tm = ROW_TILE

    def body(o_ref, g_ref, a_ref):
        g = g_ref[...]
        a_ref[...] = (o_ref[...] * (g * _sigmoid(g))).astype(BF16)

    blk = pl.BlockSpec((tm, HP), lambda i: (i, 0))
    return pl.pallas_call(
        body, name=name, grid=(S // tm,), out_shape=jax.ShapeDtypeStruct((S, HP), BF16),
        in_specs=[blk, pl.BlockSpec((tm, HP), lambda i: (i, CB_GA * LANE // HP))], out_specs=blk, compiler_params=_params(1),
    )(o, p)


def _gate_a_bwd(da, o, p, name):
    S = o.shape[0]
    tm = ROW_TILE

    def body(da_ref, o_ref, g_ref, do_ref, dg_ref):
        g = g_ref[...]
        da = da_ref[...]
        sg = _sigmoid(g)
        do_ref[...] = da * (g * sg)
        dg_ref[...] = (da * o_ref[...] * (sg * (1.0 + g * (1.0 - sg)))).astype(BF16)

    blk = pl.BlockSpec((tm, HP), lambda i: (i, 0))
    return pl.pallas_call(
        body, name=name, grid=(S // tm,),
        out_shape=(jax.ShapeDtypeStruct((S, HP), F32), jax.ShapeDtypeStruct((S, HP), BF16)),
        in_specs=[blk, blk, pl.BlockSpec((tm, HP), lambda i: (i, CB_GA * LANE // HP))], out_specs=(blk, blk), compiler_params=_params(1),
    )(da, o, p)


def _assemble_dp(dga, ddil, dgb, dckv, dkr, dcq, name):
    S = dga.shape[0]
    tm = ROW_TILE
    nc = DIL_W // LANE
    dils = [d for _, d in DIL_PAIRS for _ in range(3)]

    def body(*refs):
        ga_ref, dil_refs, (gb_ref, ckv_ref, kr_ref, cq_ref, o_ref, scr) = refs[0], refs[1:10], refs[10:]
        o_ref[:, CB_GA * LANE:CB_GA * LANE + HP] = ga_ref[...]
        for t, (x_ref, d) in enumerate(zip(dil_refs, dils)):
            off = CB_DIL * LANE + t * DIL_W
            if d == 1:
                o_ref[:, off:off + DIL_W] = x_ref[...]
            else:
                n = tm // d
                for r in range(d):
                    for c in range(nc):
                        scr.at[c][pl.ds(r, n, stride=d), :] = x_ref[:, r * DIL_W + c * LANE:r * DIL_W + (c + 1) * LANE].astype(F32)
                for c in range(nc):
                    o_ref[:, off + c * LANE:off + (c + 1) * LANE] = scr[c].astype(BF16)
        o_ref[:, CB_GB * LANE:CB_GB * LANE + DIL_W] = gb_ref[...]
        o_ref[:, CB_CKV * LANE:CB_CKV * LANE + KV_LORA] = ckv_ref[...]
        o_ref[:, CB_KR * LANE:(CB_KR + 1) * LANE] = kr_ref[...]
        o_ref[:, CB_CQ * LANE:CB_CQ * LANE + Q_LORA] = cq_ref[...]

    row = lambda w: pl.BlockSpec((tm, w), lambda i: (i, 0))
    return pl.pallas_call(
        body, name=name, grid=(S // tm,), out_shape=jax.ShapeDtypeStruct((S, NP), BF16),
        in_specs=[row(HP)] + [_view_spec(d, DIL_W) for d in dils] + [row(DIL_W), row(KV_LORA), row(LANE), row(Q_LORA)],
        out_specs=row(NP), scratch_shapes=[pltpu.VMEM((nc, tm, LANE), F32)], compiler_params=_params(1),
    )(dga, *ddil, dgb, dckv, dkr, dcq)


def _merge_weights(l0, l1, l2):
    mx = jnp.maximum(jnp.maximum(l0, l1), l2)
    e0, e1, e2 = jnp.exp(l0 - mx), jnp.exp(l1 - mx), jnp.exp(l2 - mx)
    den = e0 + e1 + e2
    return e0 / den, e1 / den, e2 / den


def _view_spec(d, width):
    return pl.BlockSpec((ROW_TILE // d, d * width), lambda i: (i, 0))


def _to_tokens(src_ref, scr, base, d):
    n = ROW_TILE // d
    for r in range(d):
        for c in range(DIL_W // LANE):
            scr.at[base + c][pl.ds(r, n, stride=d), :] = src_ref[:, r * DIL_W + c * LANE:r * DIL_W + (c + 1) * LANE]


def _from_tokens(scr, base, dst_ref, d):
    n = ROW_TILE // d
    for r in range(d):
        for c in range(DIL_W // LANE):
            dst_ref[:, r * DIL_W + c * LANE:r * DIL_W + (c + 1) * LANE] = scr.at[base + c][pl.ds(r, n, stride=d), :]


def _gate_b(os_, ls_, p, name):
    S = p.shape[0]
    tm = ROW_TILE
    nc = DIL_W // LANE
    dils = [d for _, d in DIL_PAIRS]

    def body(o0, o1, o2, l0, l1, l2, g_ref, b_ref, scr):
        for gi, (o_ref, l_ref) in enumerate(((o1, l1), (o2, l2))):
            _to_tokens(o_ref, scr, (2 * gi) * nc, dils[gi + 1])
            _to_tokens(l_ref, scr, (2 * gi + 1) * nc, dils[gi + 1])
        for c in range(nc):
            cs = slice(c * LANE, (c + 1) * LANE)
            a0, a1, a2 = _merge_weights(l0[:, cs], scr[nc + c], scr[3 * nc + c])
            bm = a0 * o0[:, cs] + a1 * scr[c] + a2 * scr[2 * nc + c]
            g = g_ref[:, cs]
            b_ref[:, cs] = (bm * (g * _sigmoid(g))).astype(BF16)

    blk = pl.BlockSpec((tm, DIL_W), lambda i: (i, 0))
    views = [_view_spec(d, DIL_W) for d in dils]
    return pl.pallas_call(
        body, name=name, grid=(S // tm,), out_shape=jax.ShapeDtypeStruct((S, DIL_W), BF16),
        in_specs=views + views + [pl.BlockSpec((tm, DIL_W), lambda i: (i, CB_GB * LANE // DIL_W))], out_specs=blk,
        scratch_shapes=[pltpu.VMEM((4 * nc, tm, LANE), F32)], compiler_params=_params(1),
    )(*os_, *ls_, p)


def _gate_b_bwd(db, os_, ls_, p, name):
    S = p.shape[0]
    tm = ROW_TILE
    nc = DIL_W // LANE
    dils = [d for _, d in DIL_PAIRS]

    def body(db_ref, o0, o1, o2, l0, l1, l2, g_ref, dg_ref, d0, d1, d2, t0, t1, t2, scr, out_scr):
        for gi, (o_ref, l_ref) in enumerate(((o1, l1), (o2, l2))):
            _to_tokens(o_ref, scr, (2 * gi) * nc, dils[gi + 1])
            _to_tokens(l_ref, scr, (2 * gi + 1) * nc, dils[gi + 1])
        lo = _lo_lanes()
        for c in range(nc):
            cs = slice(c * LANE, (c + 1) * LANE)
            a0, a1, a2 = _merge_weights(l0[:, cs], scr[nc + c], scr[3 * nc + c])
            bm = a0 * o0[:, cs] + a1 * scr[c] + a2 * scr[2 * nc + c]
            g = g_ref[:, cs]
            db = db_ref[:, cs]
            sg = _sigmoid(g)
            dbm = db * (g * sg)
            dg_ref[:, cs] = (db * bm * (sg * (1.0 + g * (1.0 - sg)))).astype(BF16)
            prod = dbm * bm
            tl = jnp.sum(jnp.where(lo, prod, 0.0), axis=-1, keepdims=True)
            th = jnp.sum(jnp.where(lo, 0.0, prod), axis=-1, keepdims=True)
            t = jnp.where(lo, tl, th)
            d0[:, cs] = a0 * dbm
            t0[:, cs] = a0 * t
            out_scr[c] = a1 * dbm
            out_scr[nc + c] = a1 * t
            out_scr[2 * nc + c] = a2 * dbm
            out_scr[3 * nc + c] = a2 * t
        _from_tokens(out_scr, 0, d1, dils[1])
        _from_tokens(out_scr, nc, t1, dils[1])
        _from_tokens(out_scr, 2 * nc, d2, dils[2])
        _from_tokens(out_scr, 3 * nc, t2, dils[2])

    blk = pl.BlockSpec((tm, DIL_W), lambda i: (i, 0))
    views = [_view_spec(d, DIL_W) for d in dils]
    fs = [jax.ShapeDtypeStruct((S // d, d * DIL_W), F32) for d in dils]
    outs = pl.pallas_call(
        body, name=name, grid=(S // tm,),
        out_shape=(jax.ShapeDtypeStruct((S, DIL_W), BF16), *fs, *fs),
        in_specs=[blk] + views + views + [pl.BlockSpec((tm, DIL_W), lambda i: (i, CB_GB * LANE // DIL_W))],
        out_specs=(blk, *views, *views),
        scratch_shapes=[pltpu.VMEM((4 * nc, tm, LANE), F32), pltpu.VMEM((4 * nc, tm, LANE), F32)], compiler_params=_params(1),
    )(db, *os_, *ls_, p)
    return outs[0], outs[1:4], outs[4:7]


def _loss_head(x, target, g, name):
    S, D = x.shape
    tm = ROW_TILE

    def body(x_ref, t_ref, g_ref, dx_ref, dg_ref, loss_ref):
        xv = x_ref[...]
        gv = g_ref[...]
        r = lax.rsqrt(jnp.mean(xv * xv, axis=-1, keepdims=True) + EPS)
        xr = xv * r
        err = xr * gv - t_ref[...]
        lpart = 0.5 * jnp.sum(jnp.mean(err * err, axis=-1, keepdims=True), axis=0, keepdims=True)
        dy = err / D
        dyg = dy * gv
        c = jnp.mean(dyg * xv, axis=-1, keepdims=True)
        dx_ref[...] = r * dyg - xv * (r * r * r) * c
        gpart = jnp.sum(dy * xr, axis=0, keepdims=True)

        @pl.when(pl.program_id(0) == 0)
        def _():
            dg_ref[...] = gpart
            loss_ref[...] = jnp.broadcast_to(lpart, loss_ref.shape)

        @pl.when(pl.program_id(0) > 0)
        def _():
            dg_ref[...] += gpart
            loss_ref[...] += jnp.broadcast_to(lpart, loss_ref.shape)

    row = pl.BlockSpec((tm, D), lambda i: (i, 0))
    vec = pl.BlockSpec((1, D), lambda i: (0, 0))
    return pl.pallas_call(
        body, name=name, grid=(S // tm,),
        out_shape=(jax.ShapeDtypeStruct((S, D), F32), jax.ShapeDtypeStruct((1, D), F32), jax.ShapeDtypeStruct((1, D), F32)),
        in_specs=[row, row, vec], out_specs=(row, vec, vec), compiler_params=_params(1),
    )(x, target, g.reshape(1, D))


def _adamw(parts, w, m, v, name):
    R, C = w.shape
    tr = _pick(R, (128, 64, 32, 16, 8))

    def body(p_ref, w_ref, m_ref, v_ref, g_ref, d_ref, nm_ref, nv_ref):
        g = p_ref[0].astype(F32)
        for k in range(1, N_DEV):
            g = g + p_ref[k].astype(F32)
        m2 = ADAM_B1 * m_ref[...] + (1.0 - ADAM_B1) * g
        v2 = ADAM_B2 * v_ref[...] + (1.0 - ADAM_B2) * (g * g)
        m_hat = m2 / (1.0 - ADAM_B1 ** ADAM_STEP)
        v_hat = v2 / (1.0 - ADAM_B2 ** ADAM_STEP)
        g_ref[...] = g
        d_ref[...] = -ADAM_LR * (m_hat / (jnp.sqrt(v_hat) + ADAM_EPS) + ADAM_WD * w_ref[...])
        nm_ref[...] = m2
        nv_ref[...] = v2

    blk = pl.BlockSpec((tr, C), lambda i: (i, 0))
    f = jax.ShapeDtypeStruct((R, C), F32)
    return pl.pallas_call(
        body, name=name, grid=(R // tr,), out_shape=(f, f, f, f),
        in_specs=[pl.BlockSpec((N_DEV, tr, C), lambda i: (0, i, 0)), blk, blk, blk], out_specs=(blk,) * 4,
        compiler_params=_params(1),
    )(parts, w, m, v)


def _exchange_copies(src_refs, out_refs, send_sems, recv_sems, local_sems, same_src):
    n = len(src_refs)
    x, y, c = lax.axis_index("x"), lax.axis_index("y"), lax.axis_index("c")
    me = 4 * x + 2 * y + c

    def block(t, j):
        return src_refs[t] if same_src else src_refs[t].at[j]

    local = [pltpu.make_async_copy(block(t, me), out_refs[t].at[me], local_sems.at[t]) for t in range(n)]
    remote = []
    for k in range(1, N_DEV):
        px = 1 - x if (k >> 2) & 1 else x
        py = 1 - y if (k >> 1) & 1 else y
        pc = 1 - c if k & 1 else c
        for t in range(n):
            remote.append(pltpu.make_async_remote_copy(
                src_ref=block(t, 4 * px + 2 * py + pc), dst_ref=out_refs[t].at[me],
                send_sem=send_sems.at[(k - 1) * n + t], recv_sem=recv_sems.at[(k - 1) * n + t],
                device_id=(px, py, pc), device_id_type=pl.DeviceIdType.MESH))
    return local, remote


def _exchange_start(*args):
    local, remote = _exchange_copies(*args)
    for cp in local + remote:
        cp.start()


def _exchange_wait(*args):
    local, remote = _exchange_copies(*args)
    for cp in remote:
        cp.wait()
    for cp in local:
        cp.wait()


def _exchange_shapes(srcs, same_src):
    return tuple(jax.ShapeDtypeStruct((N_DEV,) + (tuple(s.shape) if same_src else tuple(s.shape[1:])), s.dtype) for s in srcs)


def _exchange_sems(n):
    return [pltpu.SemaphoreType.DMA(((N_DEV - 1) * n,)), pltpu.SemaphoreType.DMA(((N_DEV - 1) * n,)), pltpu.SemaphoreType.DMA((n,))]


def _exchange(srcs, same_src, name):
    n = len(srcs)

    def body(*refs):
        args = (refs[:n], refs[n:2 * n], *refs[2 * n:], same_src)
        _exchange_start(*args)
        _exchange_wait(*args)

    hbm = pl.BlockSpec(memory_space=pltpu.HBM)
    outs = pl.pallas_call(
        body, name=name, out_shape=_exchange_shapes(srcs, same_src),
        in_specs=[hbm] * n, out_specs=(hbm,) * n, scratch_shapes=_exchange_sems(n),
    )(*srcs)
    return list(outs)


def _full_weights(g_in, g_uq, g_ukv, g_out):
    cat_cols = lambda t: jnp.moveaxis(t, 0, 2).reshape(t.shape[1], t.shape[2], -1)
    return cat_cols(g_in), cat_cols(g_uq), cat_cols(g_ukv), jnp.moveaxis(g_out, 0, 1).reshape(g_out.shape[1], D_MODEL, D_MODEL)


def _grad_blocks(g_in, g_uq, g_ukv, g_out):
    split_cols = lambda t: jnp.moveaxis(t.astype(BF16).reshape(t.shape[0], N_DEV, -1), 1, 0)
    return [split_cols(g_in), split_cols(g_uq), split_cols(g_ukv), g_out.astype(BF16).reshape(N_DEV, D_MODEL // N_DEV, D_MODEL)]


def _layer_weights(w_in, w_uq, w_ukv, w_out):
    z = lambda r, n: jnp.zeros((r, n), BF16)
    c_q, c_kv, k_r = w_in[:, 0:384], w_in[:, 384:640], w_in[:, 640:672]
    gate_a, dil, gate_b = w_in[:, 672:1184], w_in[:, 1184:5792], w_in[:, 5792:6304]
    ga_pad = jnp.pad(gate_a.reshape(D_MODEL, HEADS, VDIM), ((0, 0), (0, 0), (0, LANE - VDIM))).reshape(D_MODEL, HP)
    w_p = jnp.concatenate([ga_pad, dil, gate_b, c_kv, z(D_MODEL, 64), k_r, z(D_MODEL, 32), c_q], axis=1)
    uq = jnp.pad(w_uq.reshape(Q_LORA, HEADS, NOPE + ROPE), ((0, 0), (0, 0), (0, LANE - NOPE - ROPE))).reshape(Q_LORA, HP)
    ukv = w_ukv.reshape(KV_LORA, HEADS, NOPE + VDIM)
    pad64 = lambda t: jnp.pad(t, ((0, 0), (0, 0), (0, LANE - 64))).reshape(KV_LORA, HP)
    uk, uv = pad64(ukv[..., :NOPE]), pad64(ukv[..., NOPE:])
    wa = jnp.pad(w_out[:HEADS * VDIM].reshape(HEADS, VDIM, D_MODEL), ((0, 0), (0, LANE - VDIM), (0, 0))).reshape(HP, D_MODEL)
    wb = w_out[HEADS * VDIM:]
    return dict(p=w_p, uq=uq, uk=uk, uv=uv, wa=wa, wb=wb)


def _unpad_grads(g):
    gp = g["p"]
    seg = lambda cb, n: gp[:, cb * LANE:cb * LANE + n]
    ga = seg(CB_GA, HP).reshape(D_MODEL, HEADS, LANE)[..., :VDIM].reshape(D_MODEL, HEADS * VDIM)
    k_r = gp[:, CB_KR * LANE + NOPE:CB_KR * LANE + NOPE + ROPE]
    g_in = jnp.concatenate([seg(CB_CQ, Q_LORA), seg(CB_CKV, KV_LORA), k_r, ga, seg(CB_DIL, 9 * DIL_W), seg(CB_GB, DIL_W)], axis=1)
    g_uq = g["uq"].reshape(Q_LORA, HEADS, LANE)[..., :NOPE + ROPE].reshape(Q_LORA, -1)
    uk = g["uk"].reshape(KV_LORA, HEADS, LANE)[..., :NOPE]
    uv = g["uv"].reshape(KV_LORA, HEADS, LANE)[..., :VDIM]
    g_ukv = jnp.concatenate([uk, uv], axis=-1).reshape(KV_LORA, -1)
    wa = g["wa"].reshape(HEADS, LANE, D_MODEL)[:, :VDIM].reshape(HEADS * VDIM, D_MODEL)
    g_out = jnp.concatenate([wa, g["wb"]], axis=0)
    return g_in, g_uq, g_ukv, g_out


def _layer_fwd(x, w, norm_g, q_norm_g, kv_norm_g, mla, dil, l, comm=None):
    n = lambda s: f"l{l}_{s}"
    h = _rms_fwd(x, 0, D_MODEL, norm_g, n("norm"))
    p = _mm(h, w["p"], "nn", n("in_proj"))
    cqn = _rms_fwd(p, CB_CQ * LANE // Q_LORA, Q_LORA, q_norm_g, n("q_norm"))
    ckvn = _rms_fwd(p, CB_CKV * LANE // KV_LORA, KV_LORA, kv_norm_g, n("kv_norm"))
    qp = _mm(cqn, w["uq"], "nn", n("q_up"))
    kpre = _mm(ckvn, w["uk"], "nn", n("k_up"))
    v = _mm(ckvn, w["uv"], "nn", n("v_up"), out_dtype=BF16)
    q = _rope_heads(qp, mla["c_q"], mla["s1"], mla["s2"], n("q_rope"), BF16)
    k = _k_assemble(kpre, p, mla, n("k_asm"))
    o, lse, received = _flash_fwd(q, k, v, n("mla_fwd"), comm)
    a = _gate_a(o, p, n("gate_a"))
    dilr, os_, ls_ = [], [], []
    for g, (_, d) in enumerate(DIL_PAIRS):
        dilr.append(_dil_prep(p, dil, g, d, n(f"dil_prep{g}")))
        og, lg = _band_fwd(dilr[g], d, n(f"band{g}_fwd"))
        os_.append(og)
        ls_.append(lg)
    b = _gate_b(os_, ls_, p, n("gate_b"))
    x1 = _mm(a, w["wa"], "nn", n("out_a"), res=x)
    x2 = _mm(b, w["wb"], "nn", n("out_b"), res=x1)
    saved = dict(x=x, h=h, p=p, cqn=cqn, ckvn=ckvn, q=q, k=k, v=v, o=o, lse=lse, a=a, dilr=dilr, os=os_, ls=ls_, b=b)
    return x2, saved, received


def _layer_bwd(dx, sv, w, norm_g, q_norm_g, kv_norm_g, mla, dil, l, comm=None):
    n = lambda s: f"l{l}_{s}"
    g = {}
    da = _mm(dx, w["wa"], "nt", n("d_a"))
    db = _mm(dx, w["wb"], "nt", n("d_b"))
    g["wa"] = _mm(sv["a"], dx, "tn", n("dw_a"))
    g["wb"] = _mm(sv["b"], dx, "tn", n("dw_b"))
    do, dga = _gate_a_bwd(da, sv["o"], sv["p"], n("gate_a_bwd"))
    dgb, dos, dts = _gate_b_bwd(db, sv["os"], sv["ls"], sv["p"], n("gate_b_bwd"))
    ddil = []
    for gi, (_, d) in enumerate(DIL_PAIRS):
        dq = _band_bwd_q(sv["dilr"][gi], d, dos[gi], sv["ls"][gi], dts[gi], dil, n(f"band{gi}_bwd_q"))
        dk, dv = _band_bwd_kv(sv["dilr"][gi], d, dos[gi], sv["ls"][gi], dts[gi], dil, n(f"band{gi}_bwd_kv"))
        ddil += [dq, dk, dv]
    dq, dk, dv, received = _flash_bwd(sv["q"], sv["k"], sv["v"], sv["o"], do, sv["lse"], n("mla_bwd"), comm)
    dqp = _rope_heads(dq, mla["c_q"], -mla["s1"], -mla["s2"], n("q_rope_bwd"), BF16)
    dkr = _kr_bwd(dk, mla, n("kr_bwd"))
    g["uq"] = _mm(sv["cqn"], dqp, "tn", n("dw_uq"))
    g["uk"] = _mm(sv["ckvn"], dk, "tn", n("dw_uk"))
    g["uv"] = _mm(sv["ckvn"], dv, "tn", n("dw_uv"))
    dcqn = _mm(dqp, w["uq"], "nt", n("d_cqn"))
    dckvn = _mm(dk, w["uk"], "nt", n("d_ckvn_k"))
    dckvn = _mm(dv, w["uv"], "nt", n("d_ckvn_v"), res=dckvn)
    dcq, g_qn = _rms_bwd(sv["p"], CB_CQ * LANE // Q_LORA, Q_LORA, dcqn, q_norm_g, n("q_norm_bwd"), BF16)
    dckv, g_kvn = _rms_bwd(sv["p"], CB_CKV * LANE // KV_LORA, KV_LORA, dckvn, kv_norm_g, n("kv_norm_bwd"), BF16)
    dp = _assemble_dp(dga, ddil, dgb, dckv, dkr, dcq, n("dp"))
    g["p"] = _mm(sv["h"], dp, "tn", n("dw_in"))
    dh = _mm(dp, w["p"], "nt", n("d_h"))
    dx_in, g_n = _rms_bwd(sv["x"], 0, D_MODEL, dh, norm_g, n("norm_bwd"), F32, res=dx)
    return dx_in, g, g_n, g_qn, g_kvn, received


_SMALL_ROWS = 16


def _pack_small(norm, qn, kvn, fin, loss_row=None):
    padc = lambda t: jnp.pad(t, ((0, 0), (0, D_MODEL - t.shape[1])))
    extra = jnp.zeros((1, D_MODEL), F32) if loss_row is None else loss_row
    return jnp.concatenate([norm, padc(qn), padc(kvn), fin.reshape(1, D_MODEL), extra, jnp.zeros((2, D_MODEL), F32)], axis=0)


def _unpack_small(p):
    return (p[0:4], p[4:8, :Q_LORA], p[8:12, :KV_LORA], p[12]), p[13, 0]


def kernel(x, norm_g, w_in, q_norm_g, kv_norm_g, w_uq, w_ukv, w_out, final_g, loss_target, m_norm_g, m_w_in, m_q_norm_g, m_kv_norm_g, m_w_uq, m_w_ukv, m_w_out, m_final_g, v_norm_g, v_w_in, v_q_norm_g, v_kv_norm_g, v_w_uq, v_w_ukv, v_w_out, v_final_g):
    S = x.shape[1]
    xs = x.reshape(S, D_MODEL)
    target = loss_target.reshape(S, D_MODEL)

    wb = [t.astype(BF16) for t in (w_in, w_uq, w_ukv, w_out)]
    first = _full_weights(*_exchange([t[0:1] for t in wb], True, "gather_weights0"))
    ws = [_layer_weights(*(t[0] for t in first))]
    mla, dil = _rope_tables(S)

    saved = []
    h = xs
    for l in range(DEPTH):
        comm = ([t[1:] for t in wb], True) if l == 0 else None
        h, sv, received = _layer_fwd(h, ws[l], norm_g[l], q_norm_g[l], kv_norm_g[l], mla, dil, l, comm)
        saved.append(sv)
        if l == 0:
            rest = _full_weights(*received)
            ws += [_layer_weights(*(t[i] for t in rest)) for i in range(DEPTH - 1)]
    dx, g_final, loss_row = _loss_head(h, target, final_g, "loss_head")
    g_norm, g_qn, g_kvn, parts_l = [None] * DEPTH, [None] * DEPTH, [None] * DEPTH, [None] * DEPTH
    blocks = None
    for l in reversed(range(DEPTH)):
        comm = (blocks, False) if blocks is not None else None
        dx, g, g_norm[l], g_qn[l], g_kvn[l], received = _layer_bwd(dx, saved[l], ws[l], norm_g[l], q_norm_g[l], kv_norm_g[l], mla, dil, l, comm)
        if blocks is not None:
            parts_l[l + 1] = received
        blocks = _grad_blocks(*_unpad_grads(g))
    parts_l[0] = _exchange(blocks, False, "exchange_grads0")

    parts = [jnp.stack([parts_l[l][t] for l in range(DEPTH)], axis=1) for t in range(4)]
    sh = []
    for t, (pt, w, m, v) in enumerate(zip(parts, (w_in, w_uq, w_ukv, w_out), (m_w_in, m_w_uq, m_w_ukv, m_w_out), (v_w_in, v_w_uq, v_w_ukv, v_w_out))):
        two = lambda a: a.reshape(-1, a.shape[-1])
        outs = _adamw(pt.reshape(N_DEV, -1, pt.shape[-1]), two(w), two(m), two(v), f"adamw_{t}")
        sh.append([o.reshape(w.shape) for o in outs])

    small = _pack_small(jnp.concatenate(g_norm, 0), jnp.concatenate(g_qn, 0), jnp.concatenate(g_kvn, 0), g_final, loss_row)
    (small_parts,) = _exchange([small], True, "gather_small")
    souts = _adamw(small_parts, _pack_small(norm_g, q_norm_g, kv_norm_g, final_g), _pack_small(m_norm_g, m_q_norm_g, m_kv_norm_g, m_final_g),
                   _pack_small(v_norm_g, v_q_norm_g, v_kv_norm_g, v_final_g), "adamw_small")
    (g_sm, loss), (d_sm, _), (m_sm, _), (v_sm, _) = (_unpack_small(t) for t in souts)

    def order(sm, k):
        return (sm[0], sh[0][k], sm[1], sm[2], sh[1][k], sh[2][k], sh[3][k], sm[3])

    return (loss, dx.reshape(1, S, D_MODEL), *order(g_sm, 0), *order(d_sm, 1), *order(m_sm, 2), *order(v_sm, 3))
```

```python
import functools
import math

import jax
import jax.numpy as jnp
from jax import lax
from jax.experimental import pallas as pl
from jax.experimental.pallas import tpu as pltpu

F32 = jnp.float32
BF16 = jnp.bfloat16

D_MODEL = 1024
DEPTH = 4
HEADS = 8
NOPE = 64
ROPE = 32
VDIM = 64
Q_LORA = 384
KV_LORA = 256
DIL_PAIRS = ((128, 1), (512, 4), (2048, 16))
DIL_HD = 64
DIL_W = 512
ROT = 16
HALF = 64
THETA = 500000.0
EPS = 1e-6
IN_WIDTH = 6304
N_DEV = 8

LANE = 128
CB_GA, CB_DIL, CB_GB, CB_CKV, CB_KR, CB_CQ = 0, 8, 44, 48, 50, 51
NP = 54 * LANE
HP = HEADS * LANE

ADAM_LR = 0.001
ADAM_B1 = 0.9
ADAM_B2 = 0.999
ADAM_EPS = 1e-08
ADAM_WD = 0.01
ADAM_STEP = 10

VMEM_LIMIT = 48 * 1024 * 1024
ROW_TILE = 512
SUB = 128

_NT = (((1,), (1,)), ((), ()))
_NN = (((1,), (0,)), ((), ()))
_TN = (((0,), (0,)), ((), ()))


def _params(n_axes):
    return pltpu.CompilerParams(dimension_semantics=("arbitrary",) * n_axes, vmem_limit_bytes=VMEM_LIMIT)


def _pick(n, cands):
    for c in cands:
        if n % c == 0:
            return c
    raise ValueError(f"no tile for {n}")


def _mm(a, b, mode, name, out_dtype=F32, res=None):
    if mode == "nn":
        (M, K), (K2, N) = a.shape, b.shape
    elif mode == "nt":
        (M, K), (N, K2) = a.shape, b.shape
    else:
        (K, M), (K2, N) = a.shape, b.shape
    assert K == K2, (a.shape, b.shape, mode)
    tm = _pick(M, (1024, 512, 384, 256, 128))
    tn = _pick(N, (1152, 1024, 768, 640, 512, 384, 256, 128))
    tk = _pick(K, (1152, 1024, 768, 640, 512, 384, 256, 128))
    nk = K // tk
    dims = {"nn": _NN, "nt": _NT, "tn": _TN}[mode]

    def body(*refs):
        if res is not None:
            a_ref, b_ref, r_ref, o_ref = refs[:4]
        else:
            a_ref, b_ref, o_ref = refs[:3]
            r_ref = None
        part = lax.dot_general(a_ref[...].astype(BF16), b_ref[...].astype(BF16), dims, preferred_element_type=F32)

        def finish(acc):
            if r_ref is not None:
                acc = acc + r_ref[...]
            o_ref[...] = acc.astype(out_dtype)

        if nk == 1:
            finish(part)
        else:
            acc_ref = refs[-1]
            k = pl.program_id(2)

            @pl.when(k == 0)
            def _():
                acc_ref[...] = part

            @pl.when(k > 0)
            def _():
                acc_ref[...] += part

            @pl.when(k == nk - 1)
            def _():
                finish(acc_ref[...])

    if mode == "nn":
        a_spec = pl.BlockSpec((tm, tk), lambda i, j, k: (i, k))
        b_spec = pl.BlockSpec((tk, tn), lambda i, j, k: (k, j))
    elif mode == "nt":
        a_spec = pl.BlockSpec((tm, tk), lambda i, j, k: (i, k))
        b_spec = pl.BlockSpec((tn, tk), lambda i, j, k: (j, k))
    else:
        a_spec = pl.BlockSpec((tk, tm), lambda i, j, k: (k, i))
        b_spec = pl.BlockSpec((tk, tn), lambda i, j, k: (k, j))
    o_spec = pl.BlockSpec((tm, tn), lambda i, j, k: (i, j))
    in_specs = [a_spec, b_spec] + ([o_spec] if res is not None else [])
    args = (a, b) + ((res,) if res is not None else ())
    return pl.pallas_call(
        body, name=name, grid=(M // tm, N // tn, nk), out_shape=jax.ShapeDtypeStruct((M, N), out_dtype),
        in_specs=in_specs, out_specs=o_spec,
        scratch_shapes=[pltpu.VMEM((tm, tn), F32)] if nk > 1 else [],
        compiler_params=_params(3),
    )(*args)


def _rms_fwd(src, cb, width, g, name):
    S = src.shape[0]
    tm = ROW_TILE

    def body(x_ref, g_ref, o_ref):
        x = x_ref[...]
        r = lax.rsqrt(jnp.mean(x * x, axis=-1, keepdims=True) + EPS)
        o_ref[...] = (x * r * g_ref[...]).astype(BF16)

    return pl.pallas_call(
        body, name=name, grid=(S // tm,), out_shape=jax.ShapeDtypeStruct((S, width), BF16),
        in_specs=[pl.BlockSpec((tm, width), lambda i: (i, cb)), pl.BlockSpec((1, width), lambda i: (0, 0))],
        out_specs=pl.BlockSpec((tm, width), lambda i: (i, 0)), compiler_params=_params(1),
    )(src, g.reshape(1, width))


def _rms_bwd(src, cb, width, dy, g, name, out_dtype, res=None):
    S = src.shape[0]
    tm = ROW_TILE

    def body(*refs):
        if res is not None:
            x_ref, dy_ref, g_ref, r_ref, dx_ref, dg_ref = refs
        else:
            x_ref, dy_ref, g_ref, dx_ref, dg_ref = refs
            r_ref = None
        x = x_ref[...]
        dy = dy_ref[...]
        r = lax.rsqrt(jnp.mean(x * x, axis=-1, keepdims=True) + EPS)
        dyg = dy * g_ref[...]
        c = jnp.mean(dyg * x, axis=-1, keepdims=True)
        dx = r * dyg - x * (r * r * r) * c
        if r_ref is not None:
            dx = dx + r_ref[...]
        dx_ref[...] = dx.astype(out_dtype)
        part = jnp.sum(dy * x * r, axis=0, keepdims=True)

        @pl.when(pl.program_id(0) == 0)
        def _():
            dg_ref[...] = part

        @pl.when(pl.program_id(0) > 0)
        def _():
            dg_ref[...] += part

    row = pl.BlockSpec((tm, width), lambda i: (i, 0))
    in_specs = [pl.BlockSpec((tm, width), lambda i: (i, cb)), row, pl.BlockSpec((1, width), lambda i: (0, 0))]
    args = [src, dy, g.reshape(1, width)]
    if res is not None:
        in_specs.append(row)
        args.append(res)
    return pl.pallas_call(
        body, name=name, grid=(S // tm,),
        out_shape=(jax.ShapeDtypeStruct((S, width), out_dtype), jax.ShapeDtypeStruct((1, width), F32)),
        in_specs=in_specs, out_specs=(row, pl.BlockSpec((1, width), lambda i: (0, 0))),
        compiler_params=_params(1),
    )(*args)


def _rot(x, c, s1, s2, h):
    return x * c + pltpu.roll(x, x.shape[1] - h, 1) * s1 + pltpu.roll(x, h, 1) * s2


def _rope_tables(S):
    def tables(dim):
        inv = 1.0 / (THETA ** (jnp.arange(0, dim, 2, dtype=F32) / dim))
        ang = jnp.arange(S, dtype=F32)[:, None] * inv[None, :]
        return jnp.cos(ang), jnp.sin(ang)

    cm, sm = tables(ROPE)
    cd, sd = tables(ROT)
    z = lambda n: jnp.zeros((S, n), F32)
    o = lambda n: jnp.ones((S, n), F32)
    mla = dict(
        c_q=jnp.concatenate([o(64), cm, cm, z(32)], 1),
        c_kr=jnp.concatenate([z(64), cm, cm, z(32)], 1),
        s1=jnp.concatenate([z(64), -sm, z(16), z(32)], 1),
        s2=jnp.concatenate([z(64), z(16), sm, z(32)], 1),
    )
    one = lambda a, b, c: jnp.concatenate([a, b, c, a, b, c], 1)
    dil = dict(c=one(cd, cd, o(48)), s1=one(-sd, z(8), z(48)), s2=one(z(8), sd, z(48)))
    return mla, dil


def _rope_heads(src, c, s1, s2, name, out_dtype):
    S = src.shape[0]
    tm = ROW_TILE

    def body(x_ref, c_ref, s1_ref, s2_ref, o_ref):
        cv, s1v, s2v = c_ref[...], s1_ref[...], s2_ref[...]
        for h in range(HEADS):
            sl = slice(h * LANE, (h + 1) * LANE)
            o_ref[:, sl] = _rot(x_ref[:, sl], cv, s1v, s2v, ROPE // 2).astype(out_dtype)

    tab = pl.BlockSpec((tm, LANE), lambda i: (i, 0))
    wide = pl.BlockSpec((tm, HP), lambda i: (i, 0))
    return pl.pallas_call(
        body, name=name, grid=(S // tm,), out_shape=jax.ShapeDtypeStruct((S, HP), out_dtype),
        in_specs=[wide, tab, tab, tab], out_specs=wide, compiler_params=_params(1),
    )(src, c, s1, s2)


def _k_assemble(kpre, p, mla, name):
    S = kpre.shape[0]
    tm = ROW_TILE

    def body(k_ref, kr_ref, c_ref, s1_ref, s2_ref, o_ref):
        r = _rot(kr_ref[...], c_ref[...], s1_ref[...], s2_ref[...], ROPE // 2)
        for h in range(HEADS):
            sl = slice(h * LANE, (h + 1) * LANE)
            o_ref[:, sl] = (k_ref[:, sl] + r).astype(BF16)

    tab = pl.BlockSpec((tm, LANE), lambda i: (i, 0))
    wide = pl.BlockSpec((tm, HP), lambda i: (i, 0))
    return pl.pallas_call(
        body, name=name, grid=(S // tm,), out_shape=jax.ShapeDtypeStruct((S, HP), BF16),
        in_specs=[wide, pl.BlockSpec((tm, LANE), lambda i: (i, CB_KR)), tab, tab, tab],
        out_specs=wide, compiler_params=_params(1),
    )(kpre, p, mla["c_kr"], mla["s1"], mla["s2"])


def _kr_bwd(dk, mla, name):
    S = dk.shape[0]
    tm = ROW_TILE

    def body(dk_ref, c_ref, s1_ref, s2_ref, o_ref):
        t = dk_ref[:, 0:LANE]
        for h in range(1, HEADS):
            t = t + dk_ref[:, h * LANE:(h + 1) * LANE]
        lane = lax.broadcasted_iota(jnp.int32, (1, LANE), 1)
        t = jnp.where((lane >= NOPE) & (lane < NOPE + ROPE), t, 0.0)
        o_ref[...] = _rot(t, c_ref[...], -s1_ref[...], -s2_ref[...], ROPE // 2).astype(BF16)

    tab = pl.BlockSpec((tm, LANE), lambda i: (i, 0))
    return pl.pallas_call(
        body, name=name, grid=(S // tm,), out_shape=jax.ShapeDtypeStruct((S, LANE), BF16),
        in_specs=[pl.BlockSpec((tm, HP), lambda i: (i, 0)), tab, tab, tab],
        out_specs=tab, compiler_params=_params(1),
    )(dk, mla["c_kr"], mla["s1"], mla["s2"])


def _dil_prep(p, dil, g, d, name):
    S = p.shape[0]
    tm = ROW_TILE
    first = CB_DIL * LANE // DIL_W + 3 * g
    nc = DIL_W // LANE
    n = tm // d

    def body(q_ref, k_ref, v_ref, c_ref, s1_ref, s2_ref, o_ref, scr):
        cv, s1v, s2v = (jnp.tile(t[...], (1, nc)) for t in (c_ref, s1_ref, s2_ref))
        ys = (_rot(q_ref[...], cv, s1v, s2v, ROT // 2) * (DIL_HD ** -0.5), _rot(k_ref[...], cv, s1v, s2v, ROT // 2), v_ref[...])
        if d == 1:
            for t, y in enumerate(ys):
                o_ref[:, t * DIL_W:(t + 1) * DIL_W] = y.astype(BF16)
        else:
            for t, y in enumerate(ys):
                for c in range(nc):
                    scr[t * nc + c] = y[:, c * LANE:(c + 1) * LANE]
            for r in range(d):
                for tc in range(3 * nc):
                    col = r * 3 * DIL_W + tc * LANE
                    o_ref[:, col:col + LANE] = scr.at[tc][pl.ds(r, n, stride=d), :].astype(BF16)

    tab = pl.BlockSpec((tm, LANE), lambda i: (i, 0))
    chunk = lambda t: pl.BlockSpec((tm, DIL_W), lambda i: (i, first + t))
    return pl.pallas_call(
        body, name=name, grid=(S // tm,), out_shape=jax.ShapeDtypeStruct((S // d, d * 3 * DIL_W), BF16),
        in_specs=[chunk(0), chunk(1), chunk(2), tab, tab, tab], out_specs=_view_spec(d, 3 * DIL_W),
        scratch_shapes=[pltpu.VMEM((3 * nc, tm, LANE), F32)], compiler_params=_params(1),
    )(p, p, p, dil["c"], dil["s1"], dil["s2"])


def _grid_ends(dims):
    ids = [pl.program_id(a) for a in range(len(dims))]
    first = functools.reduce(jnp.logical_and, [i == 0 for i in ids])
    last = functools.reduce(jnp.logical_and, [i == n - 1 for i, n in zip(ids, dims)])
    return first, last


def _flash_fwd(q, k, v, name, comm=None):
    S = q.shape[0]
    tq = _pick(S, (1024, 512))
    tk = _pick(S, (2048, 1024, 512))
    nk = S // tk
    c2 = (NOPE + ROPE) ** -0.5 * math.log2(math.e)
    srcs, same_src = comm if comm is not None else ([], True)
    n = len(srcs)
    grid = (HEADS, S // tq, nk)

    def body(*refs):
        q_ref, k_ref, v_ref = refs[:3]
        o_ref, lse_ref = refs[3 + n:5 + n]
        m_s, acc_s = refs[5 + 2 * n:7 + 2 * n]
        ex = (refs[3:3 + n], refs[5 + n:5 + 2 * n], *refs[7 + 2 * n:], same_src)
        j = pl.program_id(2)
        if n:
            first, last = _grid_ends(grid)
            pl.when(first)(lambda: _exchange_start(*ex))

        @pl.when(j == 0)
        def _():
            m_s[...] = jnp.full(m_s.shape, -jnp.inf, F32)
            acc_s[...] = jnp.zeros(acc_s.shape, F32)

        lane = lax.broadcasted_iota(jnp.int32, (1, LANE), 1)
        vv = jnp.where(lane == VDIM, jnp.ones((), BF16), v_ref[...])
        t = lax.dot_general(q_ref[...], k_ref[...], _NT, preferred_element_type=F32) * c2
        m_prev = m_s[...]
        m_new = jnp.maximum(m_prev, jnp.max(t, axis=-1, keepdims=True))
        alpha = jnp.exp2(m_prev - m_new)
        e = jnp.exp2(t - jnp.tile(m_new, (1, tk // LANE)))
        acc_s[...] = alpha * acc_s[...] + jnp.dot(e.astype(BF16), vv, preferred_element_type=F32)
        m_s[...] = m_new

        @pl.when(j == nk - 1)
        def _():
            acc = acc_s[...]
            l = acc[:, VDIM:VDIM + 1]
            o_ref[...] = jnp.where(lane < VDIM, acc / l, 0.0)
            lse_ref[...] = (m_s[...] + jnp.log2(l)) * math.log(2.0)

        if n:
            pl.when(last)(lambda: _exchange_wait(*ex))

    qs = pl.BlockSpec((tq, LANE), lambda h, i, j: (i, h))
    ks = pl.BlockSpec((tk, LANE), lambda h, i, j: (j, h))
    hbm = pl.BlockSpec(memory_space=pltpu.HBM)
    outs = pl.pallas_call(
        body, name=name, grid=grid,
        out_shape=(jax.ShapeDtypeStruct((S, HP), F32), jax.ShapeDtypeStruct((S, HP), F32)) + _exchange_shapes(srcs, same_src),
        in_specs=[qs, ks, ks] + [hbm] * n, out_specs=(qs, qs) + (hbm,) * n,
        scratch_shapes=[pltpu.VMEM((tq, LANE), F32), pltpu.VMEM((tq, LANE), F32)] + (_exchange_sems(n) if n else []),
        compiler_params=_params(3),
    )(q, k, v, *srcs)
    return outs[0], outs[1], list(outs[2:])


def _flash_bwd(q, k, v, o, do, lse, name, comm=None):
    S = q.shape[0]
    tq = _pick(S, (1024, 512))
    tk = _pick(S, (2048, 1024, 512))
    nq = S // tq
    scale = (NOPE + ROPE) ** -0.5
    srcs, same_src = comm if comm is not None else ([], True)
    n = len(srcs)
    grid = (HEADS, S // tk, nq)

    def body(*refs):
        q_ref, k_ref, v_ref, o_ref, do_ref, lse_ref = refs[:6]
        dq_ref, dk_ref, dv_ref = refs[6 + n:9 + n]
        dk_s, dv_s = refs[9 + 2 * n:11 + 2 * n]
        ex = (refs[6:6 + n], refs[9 + n:9 + 2 * n], *refs[11 + 2 * n:], same_src)
        j = pl.program_id(1)
        i = pl.program_id(2)
        if n:
            first, last = _grid_ends(grid)
            pl.when(first)(lambda: _exchange_start(*ex))
        qv = q_ref[...]
        kv = k_ref[...]
        do = do_ref[...]
        dob = do.astype(BF16)
        lse_row = jnp.transpose(lse_ref[...])[0:1, :]
        delta_row = jnp.sum(jnp.transpose(do * o_ref[...]), axis=0, keepdims=True)
        st = lax.dot_general(kv, qv, _NT, preferred_element_type=F32) * scale
        pt = jnp.exp(st - lse_row)
        dpt = lax.dot_general(v_ref[...], dob, _NT, preferred_element_type=F32)
        dst = (pt * (dpt - delta_row)).astype(BF16)
        dv_part = jnp.dot(pt.astype(BF16), dob, preferred_element_type=F32)
        dk_part = jnp.dot(dst, qv, preferred_element_type=F32)
        dq_part = jnp.transpose(jnp.dot(jnp.transpose(kv), dst, preferred_element_type=F32)) * scale

        @pl.when(i == 0)
        def _():
            dk_s[...] = dk_part
            dv_s[...] = dv_part

        @pl.when(i > 0)
        def _():
            dk_s[...] += dk_part
            dv_s[...] += dv_part

        rows = pl.ds(pl.multiple_of(i * tq, tq), tq)

        @pl.when(j == 0)
        def _():
            dq_ref[rows, :] = dq_part

        @pl.when(j > 0)
        def _():
            dq_ref[rows, :] += dq_part

        @pl.when(i == nq - 1)
        def _():
            dk_ref[...] = dk_s[...] * scale
            dv_ref[...] = dv_s[...].astype(BF16)

        if n:
            pl.when(last)(lambda: _exchange_wait(*ex))

    qs = pl.BlockSpec((tq, LANE), lambda h, j, i: (i, h))
    ks = pl.BlockSpec((tk, LANE), lambda h, j, i: (j, h))
    hbm = pl.BlockSpec(memory_space=pltpu.HBM)
    outs = pl.pallas_call(
        body, name=name, grid=grid,
        out_shape=(jax.ShapeDtypeStruct((S, HP), F32), jax.ShapeDtypeStruct((S, HP), F32), jax.ShapeDtypeStruct((S, HP), BF16))
        + _exchange_shapes(srcs, same_src),
        in_specs=[qs, ks, ks, qs, qs, qs] + [hbm] * n,
        out_specs=(pl.BlockSpec((S, LANE), lambda h, j, i: (0, h)), ks, ks) + (hbm,) * n,
        scratch_shapes=[pltpu.VMEM((tk, LANE), F32), pltpu.VMEM((tk, LANE), F32)] + (_exchange_sems(n) if n else []),
        compiler_params=_params(3),
    )(q, k, v, o, do, lse, *srcs)
    return outs[0], outs[1], outs[2], list(outs[3:])


def _band_tiles(L):
    tq = min(512, L)
    return tq, tq // SUB, tq // HALF, L // HALF


def _halo_specs(tq, rpb, n64, col):
    prev = pl.BlockSpec((HALF, DIL_W), lambda r, i: (jnp.maximum(rpb * i - 1, 0), col(r)))
    cur = pl.BlockSpec((tq, DIL_W), lambda r, i: (i, col(r)))
    nxt = pl.BlockSpec((HALF, DIL_W), lambda r, i: (jnp.minimum(rpb * i + rpb, n64 - 1), col(r)))
    return [prev, cur, nxt]


def _fill(buf, prev_ref, cur_ref, next_ref, tq):
    buf[0:HALF, :] = prev_ref[...]
    buf[HALF:HALF + tq, :] = cur_ref[...]
    buf[HALF + tq:HALF + tq + HALF, :] = next_ref[...]


def _lo_lanes():
    return lax.broadcasted_iota(jnp.int32, (1, LANE), 1) < DIL_HD


def _stack_heads(x, lo):
    zero = jnp.zeros_like(x)
    return jnp.concatenate([jnp.where(lo, x, zero), jnp.where(lo, zero, x)], axis=0)


def _stack_cols(x):
    return jnp.concatenate([x[:, 0:1], x[:, DIL_HD:DIL_HD + 1]], axis=0)


def _band_valid(q0, k0, nq, nk, L, bound_q):
    qpos = q0 + lax.broadcasted_iota(jnp.int32, (nq, 1), 0)
    kpos = k0 + lax.broadcasted_iota(jnp.int32, (1, nk), 1)
    side = qpos if bound_q else kpos
    return (jnp.abs(qpos - kpos) <= HALF) & (side >= 0) & (side < L)


def _band_fwd(dilr, d, name):
    L = dilr.shape[0]
    S = L * d
    tq, nsub, rpb, n64 = _band_tiles(L)
    view = dilr
    win = SUB + 2 * HALF

    def body(q_ref, kp_ref, kc_ref, kn_ref, vp_ref, vc_ref, vn_ref, o_ref, lse_ref, kbuf, vbuf):
        i = pl.program_id(1)
        _fill(kbuf, kp_ref, kc_ref, kn_ref, tq)
        _fill(vbuf, vp_ref, vc_ref, vn_ref, tq)
        lo = _lo_lanes()
        for a in range(nsub):
            r0 = a * SUB
            rows = slice(r0, r0 + SUB)
            valid = _band_valid(i * tq + r0, i * tq + r0 - HALF, SUB, win, L, False)
            valid2 = jnp.concatenate([valid, valid], axis=0)
            for hp in range(4):
                cs = slice(hp * LANE, (hp + 1) * LANE)
                q = q_ref[rows, cs]
                kw = kbuf[r0:r0 + win, cs]
                vw = vbuf[r0:r0 + win, cs]
                q2 = _stack_heads(q, lo)
                s = lax.dot_general(q2, kw, _NT, preferred_element_type=F32)
                s = jnp.where(valid2, s, -jnp.inf)
                m = jnp.max(s, axis=-1, keepdims=True)
                e = jnp.exp(s - m)
                l = jnp.sum(e, axis=-1, keepdims=True)
                o2 = jnp.dot(e.astype(BF16), vw, preferred_element_type=F32) / l
                lse2 = m + jnp.log(l)
                o_ref[rows, cs] = jnp.where(lo, o2[:SUB], o2[SUB:])
                lse_ref[rows, cs] = jnp.where(lo, lse2[:SUB], lse2[SUB:])

    out_spec = pl.BlockSpec((tq, DIL_W), lambda r, i: (i, r))
    o, lse = pl.pallas_call(
        body, name=name, grid=(d, L // tq),
        out_shape=(jax.ShapeDtypeStruct((L, d * DIL_W), F32), jax.ShapeDtypeStruct((L, d * DIL_W), F32)),
        in_specs=[pl.BlockSpec((tq, DIL_W), lambda r, i: (i, r * 3))]
        + _halo_specs(tq, rpb, n64, lambda r: r * 3 + 1) + _halo_specs(tq, rpb, n64, lambda r: r * 3 + 2),
        out_specs=(out_spec, out_spec),
        scratch_shapes=[pltpu.VMEM((tq + 2 * HALF, DIL_W), BF16), pltpu.VMEM((tq + 2 * HALF, DIL_W), BF16)],
        compiler_params=_params(2),
    )(view, view, view, view, view, view, view)
    return o, lse


def _band_bwd_q(dilr, d, do, lse, dlt, dil, name):
    L = dilr.shape[0]
    S = L * d
    tq, nsub, rpb, n64 = _band_tiles(L)
    view = dilr
    v4 = lambda t: t.reshape(L, d * DIL_W)
    tv = lambda t: t.reshape(L, d * LANE)
    win = SUB + 2 * HALF

    def body(q_ref, kp_ref, kc_ref, kn_ref, vp_ref, vc_ref, vn_ref, do_ref, lse_ref, dlt_ref, c_ref, s1_ref, s2_ref, dq_ref,
             kbuf, vbuf):
        i = pl.program_id(1)
        _fill(kbuf, kp_ref, kc_ref, kn_ref, tq)
        _fill(vbuf, vp_ref, vc_ref, vn_ref, tq)
        lo = _lo_lanes()
        for a in range(nsub):
            r0 = a * SUB
            rows = slice(r0, r0 + SUB)
            valid = _band_valid(i * tq + r0, i * tq + r0 - HALF, SUB, win, L, False)
            valid2 = jnp.concatenate([valid, valid], axis=0)
            cv, s1v, s2v = c_ref[rows, :], s1_ref[rows, :], s2_ref[rows, :]
            for hp in range(4):
                cs = slice(hp * LANE, (hp + 1) * LANE)
                kw = kbuf[r0:r0 + win, cs]
                vw = vbuf[r0:r0 + win, cs]
                q2 = _stack_heads(q_ref[rows, cs], lo)
                do2 = _stack_heads(do_ref[rows, cs], lo).astype(BF16)
                s = lax.dot_general(q2, kw, _NT, preferred_element_type=F32)
                p = jnp.where(valid2, jnp.exp(s - _stack_cols(lse_ref[rows, cs])), 0.0)
                dp = lax.dot_general(do2, vw, _NT, preferred_element_type=F32)
                ds = (p * (dp - _stack_cols(dlt_ref[rows, cs]))).astype(BF16)
                dq2 = jnp.dot(ds, kw, preferred_element_type=F32)
                dq = jnp.where(lo, dq2[:SUB], dq2[SUB:])
                dq_ref[rows, cs] = (_rot(dq, cv, -s1v, -s2v, ROT // 2) * (DIL_HD ** -0.5)).astype(BF16)

    row = pl.BlockSpec((tq, DIL_W), lambda r, i: (i, r))
    tab = pl.BlockSpec((tq, LANE), lambda r, i: (i, r))
    dq = pl.pallas_call(
        body, name=name, grid=(d, L // tq), out_shape=jax.ShapeDtypeStruct((L, d * DIL_W), BF16),
        in_specs=[pl.BlockSpec((tq, DIL_W), lambda r, i: (i, r * 3))]
        + _halo_specs(tq, rpb, n64, lambda r: r * 3 + 1) + _halo_specs(tq, rpb, n64, lambda r: r * 3 + 2)
        + [row, row, row, tab, tab, tab],
        out_specs=row,
        scratch_shapes=[pltpu.VMEM((tq + 2 * HALF, DIL_W), BF16), pltpu.VMEM((tq + 2 * HALF, DIL_W), BF16)],
        compiler_params=_params(2),
    )(view, view, view, view, view, view, view, v4(do), v4(lse), v4(dlt), tv(dil["c"]), tv(dil["s1"]), tv(dil["s2"]))
    return dq


def _band_bwd_kv(dilr, d, do, lse, dlt, dil, name):
    L = dilr.shape[0]
    S = L * d
    tq, nsub, rpb, n64 = _band_tiles(L)
    view = dilr
    v4 = lambda t: t.reshape(L, d * DIL_W)
    tv = lambda t: t.reshape(L, d * LANE)
    win = SUB + 2 * HALF

    def body(k_ref, v_ref, qp_ref, qc_ref, qn_ref, dop_ref, doc_ref, don_ref, lp_ref, lc_ref, ln_ref, tp_ref, tc_ref, tn_ref,
             c_ref, s1_ref, s2_ref, dk_ref, dv_ref, qbuf, dobuf, lbuf, tbuf):
        j = pl.program_id(1)
        _fill(qbuf, qp_ref, qc_ref, qn_ref, tq)
        _fill(dobuf, dop_ref, doc_ref, don_ref, tq)
        _fill(lbuf, lp_ref, lc_ref, ln_ref, tq)
        _fill(tbuf, tp_ref, tc_ref, tn_ref, tq)
        lo = _lo_lanes()
        quarter = (lax.broadcasted_iota(jnp.int32, (1, LANE), 1) & (DIL_HD - 1)) < DIL_HD // 2
        for a in range(nsub):
            r0 = a * SUB
            rows = slice(r0, r0 + SUB)
            wrows = slice(r0, r0 + win)
            kpos = j * tq + r0 + lax.broadcasted_iota(jnp.int32, (SUB, 1), 0)
            qpos = j * tq + r0 - HALF + lax.broadcasted_iota(jnp.int32, (1, win), 1)
            valid = (jnp.abs(qpos - kpos) <= HALF) & (qpos >= 0) & (qpos < L)
            valid2 = jnp.concatenate([valid, valid], axis=1)
            cv, s1v, s2v = c_ref[rows, :], s1_ref[rows, :], s2_ref[rows, :]
            for hp in range(4):
                cs = slice(hp * LANE, (hp + 1) * LANE)
                k = k_ref[rows, cs]
                v = v_ref[rows, cs]
                q2 = _stack_heads(qbuf[wrows, cs], lo)
                do2 = _stack_heads(dobuf[wrows, cs], lo).astype(BF16)
                zt = jnp.transpose(jnp.where(quarter, lbuf[wrows, cs], tbuf[wrows, cs]))
                lse_row = jnp.concatenate([zt[0:1, :], zt[DIL_HD:DIL_HD + 1, :]], axis=1)
                dlt_row = jnp.concatenate([zt[DIL_HD // 2:DIL_HD // 2 + 1, :], zt[3 * DIL_HD // 2:3 * DIL_HD // 2 + 1, :]], axis=1)
                st = lax.dot_general(k, q2, _NT, preferred_element_type=F32)
                pt = jnp.where(valid2, jnp.exp(st - lse_row), 0.0)
                dv = jnp.dot(pt.astype(BF16), do2, preferred_element_type=F32)
                dpt = lax.dot_general(v, do2, _NT, preferred_element_type=F32)
                dst = (pt * (dpt - dlt_row)).astype(BF16)
                dk = jnp.dot(dst, q2, preferred_element_type=F32)
                dk_ref[rows, cs] = _rot(dk, cv, -s1v, -s2v, ROT // 2).astype(BF16)
                dv_ref[rows, cs] = dv.astype(BF16)

    row = pl.BlockSpec((tq, DIL_W), lambda r, i: (i, r))
    tab = pl.BlockSpec((tq, LANE), lambda r, i: (i, r))
    halo = _halo_specs(tq, rpb, n64, lambda r: r)
    hb = tq + 2 * HALF
    dk, dv = pl.pallas_call(
        body, name=name, grid=(d, L // tq),
        out_shape=(jax.ShapeDtypeStruct((L, d * DIL_W), BF16), jax.ShapeDtypeStruct((L, d * DIL_W), BF16)),
        in_specs=[pl.BlockSpec((tq, DIL_W), lambda r, i: (i, r * 3 + 1)), pl.BlockSpec((tq, DIL_W), lambda r, i: (i, r * 3 + 2))]
        + _halo_specs(tq, rpb, n64, lambda r: r * 3) + halo + halo + halo + [tab, tab, tab],
        out_specs=(row, row),
        scratch_shapes=[pltpu.VMEM((hb, DIL_W), BF16), pltpu.VMEM((hb, DIL_W), F32), pltpu.VMEM((hb, DIL_W), F32), pltpu.VMEM((hb, DIL_W), F32)],
        compiler_params=_params(2),
    )(view, view, view, view, view, v4(do), v4(do), v4(do), v4(lse), v4(lse), v4(lse), v4(dlt), v4(dlt), v4(dlt),
      tv(dil["c"]), tv(dil["s1"]), tv(dil["s2"]))
    return dk, dv


def _sigmoid(x):
    return 1.0 / (1.0 + jnp.exp(-x))


def _gate_a(o, p, name):
    S = o.shape[0]
    tm = ROW_TILE

    def body(o_ref, g_ref, a_ref):
        lo = _lo_lanes()
        for pr in range(HEADS // 2):
            halves = []
            for h in (2 * pr, 2 * pr + 1):
                hs = slice(h * LANE, (h + 1) * LANE)
                g = g_ref[:, hs]
                halves.append(o_ref[:, hs] * (g * _sigmoid(g)))
            a_ref[:, pr * LANE:(pr + 1) * LANE] = jnp.where(lo, halves[0], pltpu.roll(halves[1], VDIM, 1)).astype(BF16)

    blk = pl.BlockSpec((tm, HP), lambda i: (i, 0))
    return pl.pallas_call(
        body, name=name, grid=(S // tm,), out_shape=jax.ShapeDtypeStruct((S, HEADS * VDIM), BF16),
        in_specs=[blk, pl.BlockSpec((tm, HP), lambda i: (i, CB_GA * LANE // HP))],
        out_specs=pl.BlockSpec((tm, HEADS * VDIM), lambda i: (i, 0)), compiler_params=_params(1),
    )(o, p)


def _gate_a_bwd(da, o, p, name):
    S = o.shape[0]
    tm = ROW_TILE

    def body(da_ref, o_ref, g_ref, do_ref, dg_ref):
        lo = _lo_lanes()
        for pr in range(HEADS // 2):
            pair = da_ref[:, pr * LANE:(pr + 1) * LANE]
            das = (jnp.where(lo, pair, 0.0), jnp.where(lo, pltpu.roll(pair, VDIM, 1), 0.0))
            for da, h in zip(das, (2 * pr, 2 * pr + 1)):
                hs = slice(h * LANE, (h + 1) * LANE)
                g = g_ref[:, hs]
                sg = _sigmoid(g)
                do_ref[:, hs] = da * (g * sg)
                dg_ref[:, hs] = (da * o_ref[:, hs] * (sg * (1.0 + g * (1.0 - sg)))).astype(BF16)

    blk = pl.BlockSpec((tm, HP), lambda i: (i, 0))
    return pl.pallas_call(
        body, name=name, grid=(S // tm,),
        out_shape=(jax.ShapeDtypeStruct((S, HP), F32), jax.ShapeDtypeStruct((S, HP), BF16)),
        in_specs=[pl.BlockSpec((tm, HEADS * VDIM), lambda i: (i, 0)), blk, pl.BlockSpec((tm, HP), lambda i: (i, CB_GA * LANE // HP))],
        out_specs=(blk, blk), compiler_params=_params(1),
    )(da, o, p)


def _assemble_dp(dga, ddil, dgb, dckv, dkr, dcq, name):
    S = dga.shape[0]
    tm = ROW_TILE
    nc = DIL_W // LANE
    dils = [d for _, d in DIL_PAIRS for _ in range(3)]

    def body(*refs):
        ga_ref, dil_refs, (gb_ref, ckv_ref, kr_ref, cq_ref, o_ref, scr) = refs[0], refs[1:10], refs[10:]
        o_ref[:, CB_GA * LANE:CB_GA * LANE + HP] = ga_ref[...]
        for t, (x_ref, d) in enumerate(zip(dil_refs, dils)):
            off = CB_DIL * LANE + t * DIL_W
            if d == 1:
                o_ref[:, off:off + DIL_W] = x_ref[...]
            else:
                n = tm // d
                for r in range(d):
                    for c in range(nc):
                        scr.at[c][pl.ds(r, n, stride=d), :] = x_ref[:, r * DIL_W + c * LANE:r * DIL_W + (c + 1) * LANE].astype(F32)
                for c in range(nc):
                    o_ref[:, off + c * LANE:off + (c + 1) * LANE] = scr[c].astype(BF16)
        o_ref[:, CB_GB * LANE:CB_GB * LANE + DIL_W] = gb_ref[...]
        o_ref[:, CB_CKV * LANE:CB_CKV * LANE + KV_LORA] = ckv_ref[...]
        o_ref[:, CB_KR * LANE:(CB_KR + 1) * LANE] = kr_ref[...]
        o_ref[:, CB_CQ * LANE:CB_CQ * LANE + Q_LORA] = cq_ref[...]

    row = lambda w: pl.BlockSpec((tm, w), lambda i: (i, 0))
    return pl.pallas_call(
        body, name=name, grid=(S // tm,), out_shape=jax.ShapeDtypeStruct((S, NP), BF16),
        in_specs=[row(HP)] + [_view_spec(d, DIL_W) for d in dils] + [row(DIL_W), row(KV_LORA), row(LANE), row(Q_LORA)],
        out_specs=row(NP), scratch_shapes=[pltpu.VMEM((nc, tm, LANE), F32)], compiler_params=_params(1),
    )(dga, *ddil, dgb, dckv, dkr, dcq)


def _merge_weights(l0, l1, l2):
    mx = jnp.maximum(jnp.maximum(l0, l1), l2)
    e0, e1, e2 = jnp.exp(l0 - mx), jnp.exp(l1 - mx), jnp.exp(l2 - mx)
    den = e0 + e1 + e2
    return e0 / den, e1 / den, e2 / den


def _view_spec(d, width):
    return pl.BlockSpec((ROW_TILE // d, d * width), lambda i: (i, 0))


def _to_tokens(src_ref, scr, base, d):
    n = ROW_TILE // d
    for r in range(d):
        for c in range(DIL_W // LANE):
            scr.at[base + c][pl.ds(r, n, stride=d), :] = src_ref[:, r * DIL_W + c * LANE:r * DIL_W + (c + 1) * LANE]


def _from_tokens(scr, base, dst_ref, d):
    n = ROW_TILE // d
    for r in range(d):
        for c in range(DIL_W // LANE):
            dst_ref[:, r * DIL_W + c * LANE:r * DIL_W + (c + 1) * LANE] = scr.at[base + c][pl.ds(r, n, stride=d), :]


def _gate_b(os_, ls_, p, name):
    S = p.shape[0]
    tm = ROW_TILE
    nc = DIL_W // LANE
    dils = [d for _, d in DIL_PAIRS]

    def body(o0, o1, o2, l0, l1, l2, g_ref, b_ref, scr):
        for gi, (o_ref, l_ref) in enumerate(((o1, l1), (o2, l2))):
            _to_tokens(o_ref, scr, (2 * gi) * nc, dils[gi + 1])
            _to_tokens(l_ref, scr, (2 * gi + 1) * nc, dils[gi + 1])
        for c in range(nc):
            cs = slice(c * LANE, (c + 1) * LANE)
            a0, a1, a2 = _merge_weights(l0[:, cs], scr[nc + c], scr[3 * nc + c])
            bm = a0 * o0[:, cs] + a1 * scr[c] + a2 * scr[2 * nc + c]
            g = g_ref[:, cs]
            b_ref[:, cs] = (bm * (g * _sigmoid(g))).astype(BF16)

    blk = pl.BlockSpec((tm, DIL_W), lambda i: (i, 0))
    views = [_view_spec(d, DIL_W) for d in dils]
    return pl.pallas_call(
        body, name=name, grid=(S // tm,), out_shape=jax.ShapeDtypeStruct((S, DIL_W), BF16),
        in_specs=views + views + [pl.BlockSpec((tm, DIL_W), lambda i: (i, CB_GB * LANE // DIL_W))], out_specs=blk,
        scratch_shapes=[pltpu.VMEM((4 * nc, tm, LANE), F32)], compiler_params=_params(1),
    )(*os_, *ls_, p)


def _gate_b_bwd(db, os_, ls_, p, name):
    S = p.shape[0]
    tm = ROW_TILE
    nc = DIL_W // LANE
    dils = [d for _, d in DIL_PAIRS]

    def body(db_ref, o0, o1, o2, l0, l1, l2, g_ref, dg_ref, d0, d1, d2, t0, t1, t2, scr, out_scr):
        for gi, (o_ref, l_ref) in enumerate(((o1, l1), (o2, l2))):
            _to_tokens(o_ref, scr, (2 * gi) * nc, dils[gi + 1])
            _to_tokens(l_ref, scr, (2 * gi + 1) * nc, dils[gi + 1])
        lo = _lo_lanes()
        for c in range(nc):
            cs = slice(c * LANE, (c + 1) * LANE)
            a0, a1, a2 = _merge_weights(l0[:, cs], scr[nc + c], scr[3 * nc + c])
            bm = a0 * o0[:, cs] + a1 * scr[c] + a2 * scr[2 * nc + c]
            g = g_ref[:, cs]
            db = db_ref[:, cs]
            sg = _sigmoid(g)
            dbm = db * (g * sg)
            dg_ref[:, cs] = (db * bm * (sg * (1.0 + g * (1.0 - sg)))).astype(BF16)
            prod = dbm * bm
            tl = jnp.sum(jnp.where(lo, prod, 0.0), axis=-1, keepdims=True)
            th = jnp.sum(jnp.where(lo, 0.0, prod), axis=-1, keepdims=True)
            t = jnp.where(lo, tl, th)
            d0[:, cs] = a0 * dbm
            t0[:, cs] = a0 * t
            out_scr[c] = a1 * dbm
            out_scr[nc + c] = a1 * t
            out_scr[2 * nc + c] = a2 * dbm
            out_scr[3 * nc + c] = a2 * t
        _from_tokens(out_scr, 0, d1, dils[1])
        _from_tokens(out_scr, nc, t1, dils[1])
        _from_tokens(out_scr, 2 * nc, d2, dils[2])
        _from_tokens(out_scr, 3 * nc, t2, dils[2])

    blk = pl.BlockSpec((tm, DIL_W), lambda i: (i, 0))
    views = [_view_spec(d, DIL_W) for d in dils]
    fs = [jax.ShapeDtypeStruct((S // d, d * DIL_W), F32) for d in dils]
    outs = pl.pallas_call(
        body, name=name, grid=(S // tm,),
        out_shape=(jax.ShapeDtypeStruct((S, DIL_W), BF16), *fs, *fs),
        in_specs=[blk] + views + views + [pl.BlockSpec((tm, DIL_W), lambda i: (i, CB_GB * LANE // DIL_W))],
        out_specs=(blk, *views, *views),
        scratch_shapes=[pltpu.VMEM((4 * nc, tm, LANE), F32), pltpu.VMEM((4 * nc, tm, LANE), F32)], compiler_params=_params(1),
    )(db, *os_, *ls_, p)
    return outs[0], outs[1:4], outs[4:7]


def _loss_head(x, target, g, name):
    S, D = x.shape
    tm = ROW_TILE

    def body(x_ref, t_ref, g_ref, dx_ref, dg_ref, loss_ref):
        xv = x_ref[...]
        gv = g_ref[...]
        r = lax.rsqrt(jnp.mean(xv * xv, axis=-1, keepdims=True) + EPS)
        xr = xv * r
        err = xr * gv - t_ref[...]
        lpart = 0.5 * jnp.sum(jnp.mean(err * err, axis=-1, keepdims=True), axis=0, keepdims=True)
        dy = err / D
        dyg = dy * gv
        c = jnp.mean(dyg * xv, axis=-1, keepdims=True)
        dx_ref[...] = r * dyg - xv * (r * r * r) * c
        gpart = jnp.sum(dy * xr, axis=0, keepdims=True)

        @pl.when(pl.program_id(0) == 0)
        def _():
            dg_ref[...] = gpart
            loss_ref[...] = jnp.broadcast_to(lpart, loss_ref.shape)

        @pl.when(pl.program_id(0) > 0)
        def _():
            dg_ref[...] += gpart
            loss_ref[...] += jnp.broadcast_to(lpart, loss_ref.shape)

    row = pl.BlockSpec((tm, D), lambda i: (i, 0))
    vec = pl.BlockSpec((1, D), lambda i: (0, 0))
    return pl.pallas_call(
        body, name=name, grid=(S // tm,),
        out_shape=(jax.ShapeDtypeStruct((S, D), F32), jax.ShapeDtypeStruct((1, D), F32), jax.ShapeDtypeStruct((1, D), F32)),
        in_specs=[row, row, vec], out_specs=(row, vec, vec), compiler_params=_params(1),
    )(x, target, g.reshape(1, D))


def _adamw(parts, w, m, v, name):
    R, C = w.shape
    tr = _pick(R, (128, 64, 32, 16, 8))

    def body(p_ref, w_ref, m_ref, v_ref, g_ref, d_ref, nm_ref, nv_ref):
        g = p_ref[0].astype(F32)
        for k in range(1, N_DEV):
            g = g + p_ref[k].astype(F32)
        m2 = ADAM_B1 * m_ref[...] + (1.0 - ADAM_B1) * g
        v2 = ADAM_B2 * v_ref[...] + (1.0 - ADAM_B2) * (g * g)
        m_hat = m2 / (1.0 - ADAM_B1 ** ADAM_STEP)
        v_hat = v2 / (1.0 - ADAM_B2 ** ADAM_STEP)
        g_ref[...] = g
        d_ref[...] = -ADAM_LR * (m_hat / (jnp.sqrt(v_hat) + ADAM_EPS) + ADAM_WD * w_ref[...])
        nm_ref[...] = m2
        nv_ref[...] = v2

    blk = pl.BlockSpec((tr, C), lambda i: (i, 0))
    f = jax.ShapeDtypeStruct((R, C), F32)
    return pl.pallas_call(
        body, name=name, grid=(R // tr,), out_shape=(f, f, f, f),
        in_specs=[pl.BlockSpec((N_DEV, tr, C), lambda i: (0, i, 0)), blk, blk, blk], out_specs=(blk,) * 4,
        compiler_params=_params(1),
    )(parts, w, m, v)


def _exchange_copies(src_refs, out_refs, send_sems, recv_sems, local_sems, same_src):
    n = len(src_refs)
    x, y, c = lax.axis_index("x"), lax.axis_index("y"), lax.axis_index("c")
    me = 4 * x + 2 * y + c

    def block(t, j):
        return src_refs[t] if same_src else src_refs[t].at[j]

    local = [pltpu.make_async_copy(block(t, me), out_refs[t].at[me], local_sems.at[t]) for t in range(n)]
    remote = []
    for k in range(1, N_DEV):
        px = 1 - x if (k >> 2) & 1 else x
        py = 1 - y if (k >> 1) & 1 else y
        pc = 1 - c if k & 1 else c
        for t in range(n):
            remote.append(pltpu.make_async_remote_copy(
                src_ref=block(t, 4 * px + 2 * py + pc), dst_ref=out_refs[t].at[me],
                send_sem=send_sems.at[(k - 1) * n + t], recv_sem=recv_sems.at[(k - 1) * n + t],
                device_id=(px, py, pc), device_id_type=pl.DeviceIdType.MESH))
    return local, remote


def _exchange_start(*args):
    local, remote = _exchange_copies(*args)
    for cp in local + remote:
        cp.start()


def _exchange_wait(*args):
    local, remote = _exchange_copies(*args)
    for cp in remote:
        cp.wait()
    for cp in local:
        cp.wait()


def _exchange_shapes(srcs, same_src):
    return tuple(jax.ShapeDtypeStruct((N_DEV,) + (tuple(s.shape) if same_src else tuple(s.shape[1:])), s.dtype) for s in srcs)


def _exchange_sems(n):
    return [pltpu.SemaphoreType.DMA(((N_DEV - 1) * n,)), pltpu.SemaphoreType.DMA(((N_DEV - 1) * n,)), pltpu.SemaphoreType.DMA((n,))]


def _exchange(srcs, same_src, name):
    n = len(srcs)

    def body(*refs):
        args = (refs[:n], refs[n:2 * n], *refs[2 * n:], same_src)
        _exchange_start(*args)
        _exchange_wait(*args)

    hbm = pl.BlockSpec(memory_space=pltpu.HBM)
    outs = pl.pallas_call(
        body, name=name, out_shape=_exchange_shapes(srcs, same_src),
        in_specs=[hbm] * n, out_specs=(hbm,) * n, scratch_shapes=_exchange_sems(n),
    )(*srcs)
    return list(outs)


def _full_weights(g_in, g_uq, g_ukv, g_out):
    cat_cols = lambda t: jnp.moveaxis(t, 0, 2).reshape(t.shape[1], t.shape[2], -1)
    return cat_cols(g_in), cat_cols(g_uq), cat_cols(g_ukv), jnp.moveaxis(g_out, 0, 1).reshape(g_out.shape[1], D_MODEL, D_MODEL)


def _grad_blocks(g_in, g_uq, g_ukv, g_out):
    split_cols = lambda t: jnp.moveaxis(t.astype(BF16).reshape(t.shape[0], N_DEV, -1), 1, 0)
    return [split_cols(g_in), split_cols(g_uq), split_cols(g_ukv), g_out.astype(BF16).reshape(N_DEV, D_MODEL // N_DEV, D_MODEL)]


def _layer_weights(w_in, w_uq, w_ukv, w_out):
    z = lambda r, n: jnp.zeros((r, n), BF16)
    c_q, c_kv, k_r = w_in[:, 0:384], w_in[:, 384:640], w_in[:, 640:672]
    gate_a, dil, gate_b = w_in[:, 672:1184], w_in[:, 1184:5792], w_in[:, 5792:6304]
    ga_pad = jnp.pad(gate_a.reshape(D_MODEL, HEADS, VDIM), ((0, 0), (0, 0), (0, LANE - VDIM))).reshape(D_MODEL, HP)
    w_p = jnp.concatenate([ga_pad, dil, gate_b, c_kv, z(D_MODEL, 64), k_r, z(D_MODEL, 32), c_q], axis=1)
    uq = jnp.pad(w_uq.reshape(Q_LORA, HEADS, NOPE + ROPE), ((0, 0), (0, 0), (0, LANE - NOPE - ROPE))).reshape(Q_LORA, HP)
    ukv = w_ukv.reshape(KV_LORA, HEADS, NOPE + VDIM)
    pad64 = lambda t: jnp.pad(t, ((0, 0), (0, 0), (0, LANE - 64))).reshape(KV_LORA, HP)
    uk, uv = pad64(ukv[..., :NOPE]), pad64(ukv[..., NOPE:])
    wa = w_out[:HEADS * VDIM]
    wb = w_out[HEADS * VDIM:]
    return dict(p=w_p, uq=uq, uk=uk, uv=uv, wa=wa, wb=wb)


def _unpad_grads(g):
    gp = g["p"]
    seg = lambda cb, n: gp[:, cb * LANE:cb * LANE + n]
    ga = seg(CB_GA, HP).reshape(D_MODEL, HEADS, LANE)[..., :VDIM].reshape(D_MODEL, HEADS * VDIM)
    k_r = gp[:, CB_KR * LANE + NOPE:CB_KR * LANE + NOPE + ROPE]
    g_in = jnp.concatenate([seg(CB_CQ, Q_LORA), seg(CB_CKV, KV_LORA), k_r, ga, seg(CB_DIL, 9 * DIL_W), seg(CB_GB, DIL_W)], axis=1)
    g_uq = g["uq"].reshape(Q_LORA, HEADS, LANE)[..., :NOPE + ROPE].reshape(Q_LORA, -1)
    uk = g["uk"].reshape(KV_LORA, HEADS, LANE)[..., :NOPE]
    uv = g["uv"].reshape(KV_LORA, HEADS, LANE)[..., :VDIM]
    g_ukv = jnp.concatenate([uk, uv], axis=-1).reshape(KV_LORA, -1)
    g_out = jnp.concatenate([g["wa"], g["wb"]], axis=0)
    return g_in, g_uq, g_ukv, g_out


def _layer_fwd(x, w, norm_g, q_norm_g, kv_norm_g, mla, dil, l, comm=None):
    n = lambda s: f"l{l}_{s}"
    h = _rms_fwd(x, 0, D_MODEL, norm_g, n("norm"))
    p = _mm(h, w["p"], "nn", n("in_proj"))
    cqn = _rms_fwd(p, CB_CQ * LANE // Q_LORA, Q_LORA, q_norm_g, n("q_norm"))
    ckvn = _rms_fwd(p, CB_CKV * LANE // KV_LORA, KV_LORA, kv_norm_g, n("kv_norm"))
    qp = _mm(cqn, w["uq"], "nn", n("q_up"))
    kpre = _mm(ckvn, w["uk"], "nn", n("k_up"))
    v = _mm(ckvn, w["uv"], "nn", n("v_up"), out_dtype=BF16)
    q = _rope_heads(qp, mla["c_q"], mla["s1"], mla["s2"], n("q_rope"), BF16)
    k = _k_assemble(kpre, p, mla, n("k_asm"))
    o, lse, received = _flash_fwd(q, k, v, n("mla_fwd"), comm)
    a = _gate_a(o, p, n("gate_a"))
    dilr, os_, ls_ = [], [], []
    for g, (_, d) in enumerate(DIL_PAIRS):
        dilr.append(_dil_prep(p, dil, g, d, n(f"dil_prep{g}")))
        og, lg = _band_fwd(dilr[g], d, n(f"band{g}_fwd"))
        os_.append(og)
        ls_.append(lg)
    b = _gate_b(os_, ls_, p, n("gate_b"))
    x1 = _mm(a, w["wa"], "nn", n("out_a"), res=x)
    x2 = _mm(b, w["wb"], "nn", n("out_b"), res=x1)
    saved = dict(x=x, h=h, p=p, cqn=cqn, ckvn=ckvn, q=q, k=k, v=v, o=o, lse=lse, a=a, dilr=dilr, os=os_, ls=ls_, b=b)
    return x2, saved, received


def _layer_bwd(dx, sv, w, norm_g, q_norm_g, kv_norm_g, mla, dil, l, comm=None):
    n = lambda s: f"l{l}_{s}"
    g = {}
    da = _mm(dx, w["wa"], "nt", n("d_a"))
    db = _mm(dx, w["wb"], "nt", n("d_b"))
    g["wa"] = _mm(sv["a"], dx, "tn", n("dw_a"))
    g["wb"] = _mm(sv["b"], dx, "tn", n("dw_b"))
    do, dga = _gate_a_bwd(da, sv["o"], sv["p"], n("gate_a_bwd"))
    dgb, dos, dts = _gate_b_bwd(db, sv["os"], sv["ls"], sv["p"], n("gate_b_bwd"))
    ddil = []
    for gi, (_, d) in enumerate(DIL_PAIRS):
        dq = _band_bwd_q(sv["dilr"][gi], d, dos[gi], sv["ls"][gi], dts[gi], dil, n(f"band{gi}_bwd_q"))
        dk, dv = _band_bwd_kv(sv["dilr"][gi], d, dos[gi], sv["ls"][gi], dts[gi], dil, n(f"band{gi}_bwd_kv"))
        ddil += [dq, dk, dv]
    dq, dk, dv, received = _flash_bwd(sv["q"], sv["k"], sv["v"], sv["o"], do, sv["lse"], n("mla_bwd"), comm)
    dqp = _rope_heads(dq, mla["c_q"], -mla["s1"], -mla["s2"], n("q_rope_bwd"), BF16)
    dkr = _kr_bwd(dk, mla, n("kr_bwd"))
    g["uq"] = _mm(sv["cqn"], dqp, "tn", n("dw_uq"))
    g["uk"] = _mm(sv["ckvn"], dk, "tn", n("dw_uk"))
    g["uv"] = _mm(sv["ckvn"], dv, "tn", n("dw_uv"))
    dcqn = _mm(dqp, w["uq"], "nt", n("d_cqn"))
    dckvn = _mm(dk, w["uk"], "nt", n("d_ckvn_k"))
    dckvn = _mm(dv, w["uv"], "nt", n("d_ckvn_v"), res=dckvn)
    dcq, g_qn = _rms_bwd(sv["p"], CB_CQ * LANE // Q_LORA, Q_LORA, dcqn, q_norm_g, n("q_norm_bwd"), BF16)
    dckv, g_kvn = _rms_bwd(sv["p"], CB_CKV * LANE // KV_LORA, KV_LORA, dckvn, kv_norm_g, n("kv_norm_bwd"), BF16)
    dp = _assemble_dp(dga, ddil, dgb, dckv, dkr, dcq, n("dp"))
    g["p"] = _mm(sv["h"], dp, "tn", n("dw_in"))
    dh = _mm(dp, w["p"], "nt", n("d_h"))
    dx_in, g_n = _rms_bwd(sv["x"], 0, D_MODEL, dh, norm_g, n("norm_bwd"), F32, res=dx)
    return dx_in, g, g_n, g_qn, g_kvn, received


_SMALL_ROWS = 16


def _pack_small(norm, qn, kvn, fin, loss_row=None):
    padc = lambda t: jnp.pad(t, ((0, 0), (0, D_MODEL - t.shape[1])))
    extra = jnp.zeros((1, D_MODEL), F32) if loss_row is None else loss_row
    return jnp.concatenate([norm, padc(qn), padc(kvn), fin.reshape(1, D_MODEL), extra, jnp.zeros((2, D_MODEL), F32)], axis=0)


def _unpack_small(p):
    return (p[0:4], p[4:8, :Q_LORA], p[8:12, :KV_LORA], p[12]), p[13, 0]


def kernel(x, norm_g, w_in, q_norm_g, kv_norm_g, w_uq, w_ukv, w_out, final_g, loss_target, m_norm_g, m_w_in, m_q_norm_g, m_kv_norm_g, m_w_uq, m_w_ukv, m_w_out, m_final_g, v_norm_g, v_w_in, v_q_norm_g, v_kv_norm_g, v_w_uq, v_w_ukv, v_w_out, v_final_g):
    S = x.shape[1]
    xs = x.reshape(S, D_MODEL)
    target = loss_target.reshape(S, D_MODEL)

    wb = [t.astype(BF16) for t in (w_in, w_uq, w_ukv, w_out)]
    first = _full_weights(*_exchange([t[0:1] for t in wb], True, "gather_weights0"))
    ws = [_layer_weights(*(t[0] for t in first))]
    mla, dil = _rope_tables(S)

    saved = []
    h = xs
    for l in range(DEPTH):
        comm = ([t[1:] for t in wb], True) if l == 0 else None
        h, sv, received = _layer_fwd(h, ws[l], norm_g[l], q_norm_g[l], kv_norm_g[l], mla, dil, l, comm)
        saved.append(sv)
        if l == 0:
            rest = _full_weights(*received)
            ws += [_layer_weights(*(t[i] for t in rest)) for i in range(DEPTH - 1)]
    dx, g_final, loss_row = _loss_head(h, target, final_g, "loss_head")
    g_norm, g_qn, g_kvn, parts_l = [None] * DEPTH, [None] * DEPTH, [None] * DEPTH, [None] * DEPTH
    blocks = None
    for l in reversed(range(DEPTH)):
        comm = (blocks, False) if blocks is not None else None
        dx, g, g_norm[l], g_qn[l], g_kvn[l], received = _layer_bwd(dx, saved[l], ws[l], norm_g[l], q_norm_g[l], kv_norm_g[l], mla, dil, l, comm)
        if blocks is not None:
            parts_l[l + 1] = received
        blocks = _grad_blocks(*_unpad_grads(g))
    parts_l[0] = _exchange(blocks, False, "exchange_grads0")

    parts = [jnp.stack([parts_l[l][t] for l in range(DEPTH)], axis=1) for t in range(4)]
    sh = []
    for t, (pt, w, m, v) in enumerate(zip(parts, (w_in, w_uq, w_ukv, w_out), (m_w_in, m_w_uq, m_w_ukv, m_w_out), (v_w_in, v_w_uq, v_w_ukv, v_w_out))):
        two = lambda a: a.reshape(-1, a.shape[-1])
        outs = _adamw(pt.reshape(N_DEV, -1, pt.shape[-1]), two(w), two(m), two(v), f"adamw_{t}")
        sh.append([o.reshape(w.shape) for o in outs])

    small = _pack_small(jnp.concatenate(g_norm, 0), jnp.concatenate(g_qn, 0), jnp.concatenate(g_kvn, 0), g_final, loss_row)
    (small_parts,) = _exchange([small], True, "gather_small")
    souts = _adamw(small_parts, _pack_small(norm_g, q_norm_g, kv_norm_g, final_g), _pack_small(m_norm_g, m_q_norm_g, m_kv_norm_g, m_final_g),
                   _pack_small(v_norm_g, v_q_norm_g, v_kv_norm_g, v_final_g), "adamw_small")
    (g_sm, loss), (d_sm, _), (m_sm, _), (v_sm, _) = (_unpack_small(t) for t in souts)

    def order(sm, k):
        return (sm[0], sh[0][k], sm[1], sm[2], sh[1][k], sh[2][k], sh[3][k], sm[3])

    return (loss, dx.reshape(1, S, D_MODEL), *order(g_sm, 0), *order(d_sm, 1), *order(m_sm, 2), *order(v_sm, 3))
```

```python
import functools
import math

import jax
import jax.numpy as jnp
from jax import lax
from jax.experimental import pallas as pl
from jax.experimental.pallas import tpu as pltpu

F32 = jnp.float32
BF16 = jnp.bfloat16

D_MODEL = 1024
DEPTH = 4
HEADS = 8
NOPE = 64
ROPE = 32
VDIM = 64
Q_LORA = 384
KV_LORA = 256
DIL_PAIRS = ((128, 1), (512, 4), (2048, 16))
DIL_HD = 64
DIL_W = 512
ROT = 16
HALF = 64
THETA = 500000.0
EPS = 1e-6
IN_WIDTH = 6304
N_DEV = 8

LANE = 128
CB_GA, CB_DIL, CB_GB, CB_CKV, CB_KR, CB_CQ = 0, 8, 44, 48, 50, 51
NP = 54 * LANE
HP = HEADS * LANE

ADAM_LR = 0.001
ADAM_B1 = 0.9
ADAM_B2 = 0.999
ADAM_EPS = 1e-08
ADAM_WD = 0.01
ADAM_STEP = 10

VMEM_LIMIT = 48 * 1024 * 1024
ROW_TILE = 512
SUB = 128

_NT = (((1,), (1,)), ((), ()))
_NN = (((1,), (0,)), ((), ()))
_TN = (((0,), (0,)), ((), ()))


def _params(n_axes):
    return pltpu.CompilerParams(dimension_semantics=("arbitrary",) * n_axes, vmem_limit_bytes=VMEM_LIMIT)


def _pick(n, cands):
    for c in cands:
        if n % c == 0:
            return c
    raise ValueError(f"no tile for {n}")


def _mm(a, b, mode, name, out_dtype=F32, res=None):
    if mode == "nn":
        (M, K), (K2, N) = a.shape, b.shape
    elif mode == "nt":
        (M, K), (N, K2) = a.shape, b.shape
    else:
        (K, M), (K2, N) = a.shape, b.shape
    assert K == K2, (a.shape, b.shape, mode)
    tm = _pick(M, (1024, 512, 384, 256, 128))
    tn = _pick(N, (1152, 1024, 768, 640, 512, 384, 256, 128))
    tk = _pick(K, (2304, 2048, 1152, 1024, 768, 640, 512, 384, 256, 128))
    nk = K // tk
    dims = {"nn": _NN, "nt": _NT, "tn": _TN}[mode]

    def body(*refs):
        if res is not None:
            a_ref, b_ref, r_ref, o_ref = refs[:4]
        else:
            a_ref, b_ref, o_ref = refs[:3]
            r_ref = None
        part = lax.dot_general(a_ref[...].astype(BF16), b_ref[...].astype(BF16), dims, preferred_element_type=F32)

        def finish(acc):
            if r_ref is not None:
                acc = acc + r_ref[...]
            o_ref[...] = acc.astype(out_dtype)

        if nk == 1:
            finish(part)
        else:
            acc_ref = refs[-1]
            k = pl.program_id(2)

            @pl.when(k == 0)
            def _():
                acc_ref[...] = part

            @pl.when(k > 0)
            def _():
                acc_ref[...] += part

            @pl.when(k == nk - 1)
            def _():
                finish(acc_ref[...])

    if mode == "nn":
        a_spec = pl.BlockSpec((tm, tk), lambda i, j, k: (i, k))
        b_spec = pl.BlockSpec((tk, tn), lambda i, j, k: (k, j))
    elif mode == "nt":
        a_spec = pl.BlockSpec((tm, tk), lambda i, j, k: (i, k))
        b_spec = pl.BlockSpec((tn, tk), lambda i, j, k: (j, k))
    else:
        a_spec = pl.BlockSpec((tk, tm), lambda i, j, k: (k, i))
        b_spec = pl.BlockSpec((tk, tn), lambda i, j, k: (k, j))
    o_spec = pl.BlockSpec((tm, tn), lambda i, j, k: (i, j))
    in_specs = [a_spec, b_spec] + ([o_spec] if res is not None else [])
    args = (a, b) + ((res,) if res is not None else ())
    return pl.pallas_call(
        body, name=name, grid=(M // tm, N // tn, nk), out_shape=jax.ShapeDtypeStruct((M, N), out_dtype),
        in_specs=in_specs, out_specs=o_spec,
        scratch_shapes=[pltpu.VMEM((tm, tn), F32)] if nk > 1 else [],
        compiler_params=_params(3),
    )(*args)


def _rms_fwd(src, cb, width, g, name):
    S = src.shape[0]
    tm = ROW_TILE

    def body(x_ref, g_ref, o_ref):
        x = x_ref[...]
        r = lax.rsqrt(jnp.mean(x * x, axis=-1, keepdims=True) + EPS)
        o_ref[...] = (x * r * g_ref[...]).astype(BF16)

    return pl.pallas_call(
        body, name=name, grid=(S // tm,), out_shape=jax.ShapeDtypeStruct((S, width), BF16),
        in_specs=[pl.BlockSpec((tm, width), lambda i: (i, cb)), pl.BlockSpec((1, width), lambda i: (0, 0))],
        out_specs=pl.BlockSpec((tm, width), lambda i: (i, 0)), compiler_params=_params(1),
    )(src, g.reshape(1, width))


def _rms_bwd(src, cb, width, dy, g, name, out_dtype, res=None):
    S = src.shape[0]
    tm = ROW_TILE

    def body(*refs):
        if res is not None:
            x_ref, dy_ref, g_ref, r_ref, dx_ref, dg_ref = refs
        else:
            x_ref, dy_ref, g_ref, dx_ref, dg_ref = refs
            r_ref = None
        x = x_ref[...]
        dy = dy_ref[...]
        r = lax.rsqrt(jnp.mean(x * x, axis=-1, keepdims=True) + EPS)
        dyg = dy * g_ref[...]
        c = jnp.mean(dyg * x, axis=-1, keepdims=True)
        dx = r * dyg - x * (r * r * r) * c
        if r_ref is not None:
            dx = dx + r_ref[...]
        dx_ref[...] = dx.astype(out_dtype)
        part = jnp.sum(dy * x * r, axis=0, keepdims=True)

        @pl.when(pl.program_id(0) == 0)
        def _():
            dg_ref[...] = part

        @pl.when(pl.program_id(0) > 0)
        def _():
            dg_ref[...] += part

    row = pl.BlockSpec((tm, width), lambda i: (i, 0))
    in_specs = [pl.BlockSpec((tm, width), lambda i: (i, cb)), row, pl.BlockSpec((1, width), lambda i: (0, 0))]
    args = [src, dy, g.reshape(1, width)]
    if res is not None:
        in_specs.append(row)
        args.append(res)
    return pl.pallas_call(
        body, name=name, grid=(S // tm,),
        out_shape=(jax.ShapeDtypeStruct((S, width), out_dtype), jax.ShapeDtypeStruct((1, width), F32)),
        in_specs=in_specs, out_specs=(row, pl.BlockSpec((1, width), lambda i: (0, 0))),
        compiler_params=_params(1),
    )(*args)


def _rot(x, c, s1, s2, h):
    return x * c + pltpu.roll(x, x.shape[1] - h, 1) * s1 + pltpu.roll(x, h, 1) * s2


def _rope_tables(S):
    def tables(dim):
        inv = 1.0 / (THETA ** (jnp.arange(0, dim, 2, dtype=F32) / dim))
        ang = jnp.arange(S, dtype=F32)[:, None] * inv[None, :]
        return jnp.cos(ang), jnp.sin(ang)

    cm, sm = tables(ROPE)
    cd, sd = tables(ROT)
    z = lambda n: jnp.zeros((S, n), F32)
    o = lambda n: jnp.ones((S, n), F32)
    mla = dict(
        c_q=jnp.concatenate([o(64), cm, cm, z(32)], 1),
        c_kr=jnp.concatenate([z(64), cm, cm, z(32)], 1),
        s1=jnp.concatenate([z(64), -sm, z(16), z(32)], 1),
        s2=jnp.concatenate([z(64), z(16), sm, z(32)], 1),
    )
    one = lambda a, b, c: jnp.concatenate([a, b, c, a, b, c], 1)
    dil = dict(c=one(cd, cd, o(48)), s1=one(-sd, z(8), z(48)), s2=one(z(8), sd, z(48)))
    return mla, dil


def _rope_heads(src, c, s1, s2, name, out_dtype):
    S = src.shape[0]
    tm = ROW_TILE

    def body(x_ref, c_ref, s1_ref, s2_ref, o_ref):
        cv, s1v, s2v = c_ref[...], s1_ref[...], s2_ref[...]
        for h in range(HEADS):
            sl = slice(h * LANE, (h + 1) * LANE)
            o_ref[:, sl] = _rot(x_ref[:, sl], cv, s1v, s2v, ROPE // 2).astype(out_dtype)

    tab = pl.BlockSpec((tm, LANE), lambda i: (i, 0))
    wide = pl.BlockSpec((tm, HP), lambda i: (i, 0))
    return pl.pallas_call(
        body, name=name, grid=(S // tm,), out_shape=jax.ShapeDtypeStruct((S, HP), out_dtype),
        in_specs=[wide, tab, tab, tab], out_specs=wide, compiler_params=_params(1),
    )(src, c, s1, s2)


def _k_assemble(kpre, p, mla, name):
    S = kpre.shape[0]
    tm = ROW_TILE

    def body(k_ref, kr_ref, c_ref, s1_ref, s2_ref, o_ref):
        r = _rot(kr_ref[...], c_ref[...], s1_ref[...], s2_ref[...], ROPE // 2)
        for h in range(HEADS):
            sl = slice(h * LANE, (h + 1) * LANE)
            o_ref[:, sl] = (k_ref[:, sl] + r).astype(BF16)

    tab = pl.BlockSpec((tm, LANE), lambda i: (i, 0))
    wide = pl.BlockSpec((tm, HP), lambda i: (i, 0))
    return pl.pallas_call(
        body, name=name, grid=(S // tm,), out_shape=jax.ShapeDtypeStruct((S, HP), BF16),
        in_specs=[wide, pl.BlockSpec((tm, LANE), lambda i: (i, CB_KR)), tab, tab, tab],
        out_specs=wide, compiler_params=_params(1),
    )(kpre, p, mla["c_kr"], mla["s1"], mla["s2"])


def _kr_bwd(dk, mla, name):
    S = dk.shape[0]
    tm = ROW_TILE

    def body(dk_ref, c_ref, s1_ref, s2_ref, o_ref):
        t = dk_ref[:, 0:LANE]
        for h in range(1, HEADS):
            t = t + dk_ref[:, h * LANE:(h + 1) * LANE]
        lane = lax.broadcasted_iota(jnp.int32, (1, LANE), 1)
        t = jnp.where((lane >= NOPE) & (lane < NOPE + ROPE), t, 0.0)
        o_ref[...] = _rot(t, c_ref[...], -s1_ref[...], -s2_ref[...], ROPE // 2).astype(BF16)

    tab = pl.BlockSpec((tm, LANE), lambda i: (i, 0))
    return pl.pallas_call(
        body, name=name, grid=(S // tm,), out_shape=jax.ShapeDtypeStruct((S, LANE), BF16),
        in_specs=[pl.BlockSpec((tm, HP), lambda i: (i, 0)), tab, tab, tab],
        out_specs=tab, compiler_params=_params(1),
    )(dk, mla["c_kr"], mla["s1"], mla["s2"])


def _dil_prep(p, dil, g, d, name):
    S = p.shape[0]
    tm = ROW_TILE
    first = CB_DIL * LANE // DIL_W + 3 * g
    nc = DIL_W // LANE
    n = tm // d

    def body(q_ref, k_ref, v_ref, c_ref, s1_ref, s2_ref, o_ref, scr):
        cv, s1v, s2v = (jnp.tile(t[...], (1, nc)) for t in (c_ref, s1_ref, s2_ref))
        ys = (_rot(q_ref[...], cv, s1v, s2v, ROT // 2) * (DIL_HD ** -0.5), _rot(k_ref[...], cv, s1v, s2v, ROT // 2), v_ref[...])
        if d == 1:
            for t, y in enumerate(ys):
                o_ref[:, t * DIL_W:(t + 1) * DIL_W] = y.astype(BF16)
        else:
            for t, y in enumerate(ys):
                for c in range(nc):
                    scr[t * nc + c] = y[:, c * LANE:(c + 1) * LANE]
            for r in range(d):
                for tc in range(3 * nc):
                    col = r * 3 * DIL_W + tc * LANE
                    o_ref[:, col:col + LANE] = scr.at[tc][pl.ds(r, n, stride=d), :].astype(BF16)

    tab = pl.BlockSpec((tm, LANE), lambda i: (i, 0))
    chunk = lambda t: pl.BlockSpec((tm, DIL_W), lambda i: (i, first + t))
    return pl.pallas_call(
        body, name=name, grid=(S // tm,), out_shape=jax.ShapeDtypeStruct((S // d, d * 3 * DIL_W), BF16),
        in_specs=[chunk(0), chunk(1), chunk(2), tab, tab, tab], out_specs=_view_spec(d, 3 * DIL_W),
        scratch_shapes=[pltpu.VMEM((3 * nc, tm, LANE), F32)], compiler_params=_params(1),
    )(p, p, p, dil["c"], dil["s1"], dil["s2"])


def _grid_ends(dims):
    ids = [pl.program_id(a) for a in range(len(dims))]
    first = functools.reduce(jnp.logical_and, [i == 0 for i in ids])
    last = functools.reduce(jnp.logical_and, [i == n - 1 for i, n in zip(ids, dims)])
    return first, last


def _flash_fwd(q, k, v, name, comm=None):
    S = q.shape[0]
    tq = _pick(S, (1024, 512))
    tk = _pick(S, (2048, 1024, 512))
    nk = S // tk
    c2 = (NOPE + ROPE) ** -0.5 * math.log2(math.e)
    srcs, same_src = comm if comm is not None else ([], True)
    n = len(srcs)
    grid = (HEADS, S // tq, nk)

    def body(*refs):
        q_ref, k_ref, v_ref = refs[:3]
        o_ref, lse_ref = refs[3 + n:5 + n]
        m_s, acc_s = refs[5 + 2 * n:7 + 2 * n]
        ex = (refs[3:3 + n], refs[5 + n:5 + 2 * n], *refs[7 + 2 * n:], same_src)
        j = pl.program_id(2)
        if n:
            first, last = _grid_ends(grid)
            pl.when(first)(lambda: _exchange_start(*ex))

        @pl.when(j == 0)
        def _():
            m_s[...] = jnp.full(m_s.shape, -jnp.inf, F32)
            acc_s[...] = jnp.zeros(acc_s.shape, F32)

        lane = lax.broadcasted_iota(jnp.int32, (1, LANE), 1)
        vv = jnp.where(lane == VDIM, jnp.ones((), BF16), v_ref[...])
        t = lax.dot_general(q_ref[...], k_ref[...], _NT, preferred_element_type=F32) * c2
        m_prev = m_s[...]
        m_new = jnp.maximum(m_prev, jnp.max(t, axis=-1, keepdims=True))
        alpha = jnp.exp2(m_prev - m_new)
        e = jnp.exp2(t - jnp.tile(m_new, (1, tk // LANE)))
        acc_s[...] = alpha * acc_s[...] + jnp.dot(e.astype(BF16), vv, preferred_element_type=F32)
        m_s[...] = m_new

        @pl.when(j == nk - 1)
        def _():
            acc = acc_s[...]
            l = acc[:, VDIM:VDIM + 1]
            o_ref[...] = jnp.where(lane < VDIM, acc / l, 0.0)
            lse_ref[...] = (m_s[...] + jnp.log2(l)) * math.log(2.0)

        if n:
            pl.when(last)(lambda: _exchange_wait(*ex))

    qs = pl.BlockSpec((tq, LANE), lambda h, i, j: (i, h))
    ks = pl.BlockSpec((tk, LANE), lambda h, i, j: (j, h))
    hbm = pl.BlockSpec(memory_space=pltpu.HBM)
    outs = pl.pallas_call(
        body, name=name, grid=grid,
        out_shape=(jax.ShapeDtypeStruct((S, HP), F32), jax.ShapeDtypeStruct((S, HP), F32)) + _exchange_shapes(srcs, same_src),
        in_specs=[qs, ks, ks] + [hbm] * n, out_specs=(qs, qs) + (hbm,) * n,
        scratch_shapes=[pltpu.VMEM((tq, LANE), F32), pltpu.VMEM((tq, LANE), F32)] + (_exchange_sems(n) if n else []),
        compiler_params=_params(3),
    )(q, k, v, *srcs)
    return outs[0], outs[1], list(outs[2:])


def _flash_bwd(q, k, v, o, do, lse, name, comm=None):
    S = q.shape[0]
    tq = _pick(S, (1024, 512))
    tk = _pick(S, (2048, 1024, 512))
    nq = S // tq
    scale = (NOPE + ROPE) ** -0.5
    srcs, same_src = comm if comm is not None else ([], True)
    n = len(srcs)
    grid = (HEADS, S // tk, nq)

    def body(*refs):
        q_ref, k_ref, v_ref, o_ref, do_ref, lse_ref = refs[:6]
        dq_ref, dk_ref, dv_ref = refs[6 + n:9 + n]
        dk_s, dv_s = refs[9 + 2 * n:11 + 2 * n]
        ex = (refs[6:6 + n], refs[9 + n:9 + 2 * n], *refs[11 + 2 * n:], same_src)
        j = pl.program_id(1)
        i = pl.program_id(2)
        if n:
            first, last = _grid_ends(grid)
            pl.when(first)(lambda: _exchange_start(*ex))
        qv = q_ref[...]
        kv = k_ref[...]
        do = do_ref[...]
        dob = do.astype(BF16)
        lse_row = jnp.transpose(lse_ref[...])[0:1, :]
        delta_row = jnp.sum(jnp.transpose(do * o_ref[...]), axis=0, keepdims=True)
        st = lax.dot_general(kv, qv, _NT, preferred_element_type=F32) * scale
        pt = jnp.exp(st - lse_row)
        dpt = lax.dot_general(v_ref[...], dob, _NT, preferred_element_type=F32)
        dst = (pt * (dpt - delta_row)).astype(BF16)
        dv_part = jnp.dot(pt.astype(BF16), dob, preferred_element_type=F32)
        dk_part = jnp.dot(dst, qv, preferred_element_type=F32)
        dq_part = jnp.transpose(jnp.dot(jnp.transpose(kv), dst, preferred_element_type=F32)) * scale

        @pl.when(i == 0)
        def _():
            dk_s[...] = dk_part
            dv_s[...] = dv_part

        @pl.when(i > 0)
        def _():
            dk_s[...] += dk_part
            dv_s[...] += dv_part

        rows = pl.ds(pl.multiple_of(i * tq, tq), tq)

        @pl.when(j == 0)
        def _():
            dq_ref[rows, :] = dq_part

        @pl.when(j > 0)
        def _():
            dq_ref[rows, :] += dq_part

        @pl.when(i == nq - 1)
        def _():
            dk_ref[...] = dk_s[...] * scale
            dv_ref[...] = dv_s[...].astype(BF16)

        if n:
            pl.when(last)(lambda: _exchange_wait(*ex))

    qs = pl.BlockSpec((tq, LANE), lambda h, j, i: (i, h))
    ks = pl.BlockSpec((tk, LANE), lambda h, j, i: (j, h))
    hbm = pl.BlockSpec(memory_space=pltpu.HBM)
    outs = pl.pallas_call(
        body, name=name, grid=grid,
        out_shape=(jax.ShapeDtypeStruct((S, HP), F32), jax.ShapeDtypeStruct((S, HP), F32), jax.ShapeDtypeStruct((S, HP), BF16))
        + _exchange_shapes(srcs, same_src),
        in_specs=[qs, ks, ks, qs, qs, qs] + [hbm] * n,
        out_specs=(pl.BlockSpec((S, LANE), lambda h, j, i: (0, h)), ks, ks) + (hbm,) * n,
        scratch_shapes=[pltpu.VMEM((tk, LANE), F32), pltpu.VMEM((tk, LANE), F32)] + (_exchange_sems(n) if n else []),
        compiler_params=_params(3),
    )(q, k, v, o, do, lse, *srcs)
    return outs[0], outs[1], outs[2], list(outs[3:])


def _band_tiles(L):
    tq = min(512, L)
    return tq, tq // SUB, tq // HALF, L // HALF


def _halo_specs(tq, rpb, n64, col):
    prev = pl.BlockSpec((HALF, DIL_W), lambda r, i: (jnp.maximum(rpb * i - 1, 0), col(r)))
    cur = pl.BlockSpec((tq, DIL_W), lambda r, i: (i, col(r)))
    nxt = pl.BlockSpec((HALF, DIL_W), lambda r, i: (jnp.minimum(rpb * i + rpb, n64 - 1), col(r)))
    return [prev, cur, nxt]


def _fill(buf, prev_ref, cur_ref, next_ref, tq):
    buf[0:HALF, :] = prev_ref[...]
    buf[HALF:HALF + tq, :] = cur_ref[...]
    buf[HALF + tq:HALF + tq + HALF, :] = next_ref[...]


def _lo_lanes():
    return lax.broadcasted_iota(jnp.int32, (1, LANE), 1) < DIL_HD


def _stack_heads(x, lo):
    zero = jnp.zeros_like(x)
    return jnp.concatenate([jnp.where(lo, x, zero), jnp.where(lo, zero, x)], axis=0)


def _stack_cols(x):
    return jnp.concatenate([x[:, 0:1], x[:, DIL_HD:DIL_HD + 1]], axis=0)


def _band_valid(q0, k0, nq, nk, L, bound_q):
    qpos = q0 + lax.broadcasted_iota(jnp.int32, (nq, 1), 0)
    kpos = k0 + lax.broadcasted_iota(jnp.int32, (1, nk), 1)
    side = qpos if bound_q else kpos
    return (jnp.abs(qpos - kpos) <= HALF) & (side >= 0) & (side < L)


def _band_fwd(dilr, d, name):
    L = dilr.shape[0]
    S = L * d
    tq, nsub, rpb, n64 = _band_tiles(L)
    view = dilr
    win = SUB + 2 * HALF

    def body(q_ref, kp_ref, kc_ref, kn_ref, vp_ref, vc_ref, vn_ref, o_ref, lse_ref, kbuf, vbuf):
        i = pl.program_id(1)
        _fill(kbuf, kp_ref, kc_ref, kn_ref, tq)
        _fill(vbuf, vp_ref, vc_ref, vn_ref, tq)
        lo = _lo_lanes()
        for a in range(nsub):
            r0 = a * SUB
            rows = slice(r0, r0 + SUB)
            valid = _band_valid(i * tq + r0, i * tq + r0 - HALF, SUB, win, L, False)
            valid2 = jnp.concatenate([valid, valid], axis=0)
            for hp in range(4):
                cs = slice(hp * LANE, (hp + 1) * LANE)
                q = q_ref[rows, cs]
                kw = kbuf[r0:r0 + win, cs]
                vw = vbuf[r0:r0 + win, cs]
                q2 = _stack_heads(q, lo)
                s = lax.dot_general(q2, kw, _NT, preferred_element_type=F32)
                s = jnp.where(valid2, s, -jnp.inf)
                m = jnp.max(s, axis=-1, keepdims=True)
                e = jnp.exp(s - m)
                l = jnp.sum(e, axis=-1, keepdims=True)
                o2 = jnp.dot(e.astype(BF16), vw, preferred_element_type=F32) / l
                lse2 = m + jnp.log(l)
                o_ref[rows, cs] = jnp.where(lo, o2[:SUB], o2[SUB:])
                lse_ref[rows, cs] = jnp.where(lo, lse2[:SUB], lse2[SUB:])

    out_spec = pl.BlockSpec((tq, DIL_W), lambda r, i: (i, r))
    o, lse = pl.pallas_call(
        body, name=name, grid=(d, L // tq),
        out_shape=(jax.ShapeDtypeStruct((L, d * DIL_W), F32), jax.ShapeDtypeStruct((L, d * DIL_W), F32)),
        in_specs=[pl.BlockSpec((tq, DIL_W), lambda r, i: (i, r * 3))]
        + _halo_specs(tq, rpb, n64, lambda r: r * 3 + 1) + _halo_specs(tq, rpb, n64, lambda r: r * 3 + 2),
        out_specs=(out_spec, out_spec),
        scratch_shapes=[pltpu.VMEM((tq + 2 * HALF, DIL_W), BF16), pltpu.VMEM((tq + 2 * HALF, DIL_W), BF16)],
        compiler_params=_params(2),
    )(view, view, view, view, view, view, view)
    return o, lse


def _band_bwd_q(dilr, d, do, lse, dlt, dil, name):
    L = dilr.shape[0]
    S = L * d
    tq, nsub, rpb, n64 = _band_tiles(L)
    view = dilr
    v4 = lambda t: t.reshape(L, d * DIL_W)
    tv = lambda t: t.reshape(L, d * LANE)
    win = SUB + 2 * HALF

    def body(q_ref, kp_ref, kc_ref, kn_ref, vp_ref, vc_ref, vn_ref, do_ref, lse_ref, dlt_ref, c_ref, s1_ref, s2_ref, dq_ref,
             kbuf, vbuf):
        i = pl.program_id(1)
        _fill(kbuf, kp_ref, kc_ref, kn_ref, tq)
        _fill(vbuf, vp_ref, vc_ref, vn_ref, tq)
        lo = _lo_lanes()
        for a in range(nsub):
            r0 = a * SUB
            rows = slice(r0, r0 + SUB)
            valid = _band_valid(i * tq + r0, i * tq + r0 - HALF, SUB, win, L, False)
            valid2 = jnp.concatenate([valid, valid], axis=0)
            cv, s1v, s2v = c_ref[rows, :], s1_ref[rows, :], s2_ref[rows, :]
            for hp in range(4):
                cs = slice(hp * LANE, (hp + 1) * LANE)
                kw = kbuf[r0:r0 + win, cs]
                vw = vbuf[r0:r0 + win, cs]
                q2 = _stack_heads(q_ref[rows, cs], lo)
                do2 = _stack_heads(do_ref[rows, cs], lo).astype(BF16)
                s = lax.dot_general(q2, kw, _NT, preferred_element_type=F32)
                p = jnp.where(valid2, jnp.exp(s - _stack_cols(lse_ref[rows, cs])), 0.0)
                dp = lax.dot_general(do2, vw, _NT, preferred_element_type=F32)
                ds = (p * (dp - _stack_cols(dlt_ref[rows, cs]))).astype(BF16)
                dq2 = jnp.dot(ds, kw, preferred_element_type=F32)
                dq = jnp.where(lo, dq2[:SUB], dq2[SUB:])
                dq_ref[rows, cs] = (_rot(dq, cv, -s1v, -s2v, ROT // 2) * (DIL_HD ** -0.5)).astype(BF16)

    row = pl.BlockSpec((tq, DIL_W), lambda r, i: (i, r))
    tab = pl.BlockSpec((tq, LANE), lambda r, i: (i, r))
    dq = pl.pallas_call(
        body, name=name, grid=(d, L // tq), out_shape=jax.ShapeDtypeStruct((L, d * DIL_W), BF16),
        in_specs=[pl.BlockSpec((tq, DIL_W), lambda r, i: (i, r * 3))]
        + _halo_specs(tq, rpb, n64, lambda r: r * 3 + 1) + _halo_specs(tq, rpb, n64, lambda r: r * 3 + 2)
        + [row, row, row, tab, tab, tab],
        out_specs=row,
        scratch_shapes=[pltpu.VMEM((tq + 2 * HALF, DIL_W), BF16), pltpu.VMEM((tq + 2 * HALF, DIL_W), BF16)],
        compiler_params=_params(2),
    )(view, view, view, view, view, view, view, v4(do), v4(lse), v4(dlt), tv(dil["c"]), tv(dil["s1"]), tv(dil["s2"]))
    return dq


def _band_bwd_kv(dilr, d, do, lse, dlt, dil, name):
    L = dilr.shape[0]
    S = L * d
    tq, nsub, rpb, n64 = _band_tiles(L)
    view = dilr
    v4 = lambda t: t.reshape(L, d * DIL_W)
    tv = lambda t: t.reshape(L, d * LANE)
    win = SUB + 2 * HALF

    def body(k_ref, v_ref, qp_ref, qc_ref, qn_ref, dop_ref, doc_ref, don_ref, lp_ref, lc_ref, ln_ref, tp_ref, tc_ref, tn_ref,
             c_ref, s1_ref, s2_ref, dk_ref, dv_ref, qbuf, dobuf, lbuf, tbuf):
        j = pl.program_id(1)
        _fill(qbuf, qp_ref, qc_ref, qn_ref, tq)
        _fill(dobuf, dop_ref, doc_ref, don_ref, tq)
        _fill(lbuf, lp_ref, lc_ref, ln_ref, tq)
        _fill(tbuf, tp_ref, tc_ref, tn_ref, tq)
        lo = _lo_lanes()
        quarter = (lax.broadcasted_iota(jnp.int32, (1, LANE), 1) & (DIL_HD - 1)) < DIL_HD // 2
        for a in range(nsub):
            r0 = a * SUB
            rows = slice(r0, r0 + SUB)
            wrows = slice(r0, r0 + win)
            kpos = j * tq + r0 + lax.broadcasted_iota(jnp.int32, (SUB, 1), 0)
            qpos = j * tq + r0 - HALF + lax.broadcasted_iota(jnp.int32, (1, win), 1)
            valid = (jnp.abs(qpos - kpos) <= HALF) & (qpos >= 0) & (qpos < L)
            valid2 = jnp.concatenate([valid, valid], axis=1)
            cv, s1v, s2v = c_ref[rows, :], s1_ref[rows, :], s2_ref[rows, :]
            for hp in range(4):
                cs = slice(hp * LANE, (hp + 1) * LANE)
                k = k_ref[rows, cs]
                v = v_ref[rows, cs]
                q2 = _stack_heads(qbuf[wrows, cs], lo)
                do2 = _stack_heads(dobuf[wrows, cs], lo).astype(BF16)
                zt = jnp.transpose(jnp.where(quarter, lbuf[wrows, cs], tbuf[wrows, cs]))
                lse_row = jnp.concatenate([zt[0:1, :], zt[DIL_HD:DIL_HD + 1, :]], axis=1)
                dlt_row = jnp.concatenate([zt[DIL_HD // 2:DIL_HD // 2 + 1, :], zt[3 * DIL_HD // 2:3 * DIL_HD // 2 + 1, :]], axis=1)
                st = lax.dot_general(k, q2, _NT, preferred_element_type=F32)
                pt = jnp.where(valid2, jnp.exp(st - lse_row), 0.0)
                dv = jnp.dot(pt.astype(BF16), do2, preferred_element_type=F32)
                dpt = lax.dot_general(v, do2, _NT, preferred_element_type=F32)
                dst = (pt * (dpt - dlt_row)).astype(BF16)
                dk = jnp.dot(dst, q2, preferred_element_type=F32)
                dk_ref[rows, cs] = _rot(dk, cv, -s1v, -s2v, ROT // 2).astype(BF16)
                dv_ref[rows, cs] = dv.astype(BF16)

    row = pl.BlockSpec((tq, DIL_W), lambda r, i: (i, r))
    tab = pl.BlockSpec((tq, LANE), lambda r, i: (i, r))
    halo = _halo_specs(tq, rpb, n64, lambda r: r)
    hb = tq + 2 * HALF
    dk, dv = pl.pallas_call(
        body, name=name, grid=(d, L // tq),
        out_shape=(jax.ShapeDtypeStruct((L, d * DIL_W), BF16), jax.ShapeDtypeStruct((L, d * DIL_W), BF16)),
        in_specs=[pl.BlockSpec((tq, DIL_W), lambda r, i: (i, r * 3 + 1)), pl.BlockSpec((tq, DIL_W), lambda r, i: (i, r * 3 + 2))]
        + _halo_specs(tq, rpb, n64, lambda r: r * 3) + halo + halo + halo + [tab, tab, tab],
        out_specs=(row, row),
        scratch_shapes=[pltpu.VMEM((hb, DIL_W), BF16), pltpu.VMEM((hb, DIL_W), F32), pltpu.VMEM((hb, DIL_W), F32), pltpu.VMEM((hb, DIL_W), F32)],
        compiler_params=_params(2),
    )(view, view, view, view, view, v4(do), v4(do), v4(do), v4(lse), v4(lse), v4(lse), v4(dlt), v4(dlt), v4(dlt),
      tv(dil["c"]), tv(dil["s1"]), tv(dil["s2"]))
    return dk, dv


def _sigmoid(x):
    return 1.0 / (1.0 + jnp.exp(-x))


def _gate_a(o, p, name):
    S = o.shape[0]
    tm = ROW_TILE

    def body(o_ref, g_ref, a_ref):
        lo = _lo_lanes()
        for pr in range(HEADS // 2):
            halves = []
            for h in (2 * pr, 2 * pr + 1):
                hs = slice(h * LANE, (h + 1) * LANE)
                g = g_ref[:, hs]
                halves.append(o_ref[:, hs] * (g * _sigmoid(g)))
            a_ref[:, pr * LANE:(pr + 1) * LANE] = jnp.where(lo, halves[0], pltpu.roll(halves[1], VDIM, 1)).astype(BF16)

    blk = pl.BlockSpec((tm, HP), lambda i: (i, 0))
    return pl.pallas_call(
        body, name=name, grid=(S // tm,), out_shape=jax.ShapeDtypeStruct((S, HEADS * VDIM), BF16),
        in_specs=[blk, pl.BlockSpec((tm, HP), lambda i: (i, CB_GA * LANE // HP))],
        out_specs=pl.BlockSpec((tm, HEADS * VDIM), lambda i: (i, 0)), compiler_params=_params(1),
    )(o, p)


def _gate_a_bwd(da, o, p, name):
    S = o.shape[0]
    tm = ROW_TILE

    def body(da_ref, o_ref, g_ref, do_ref, dg_ref):
        lo = _lo_lanes()
        for pr in range(HEADS // 2):
            pair = da_ref[:, pr * LANE:(pr + 1) * LANE]
            das = (jnp.where(lo, pair, 0.0), jnp.where(lo, pltpu.roll(pair, VDIM, 1), 0.0))
            for da, h in zip(das, (2 * pr, 2 * pr + 1)):
                hs = slice(h * LANE, (h + 1) * LANE)
                g = g_ref[:, hs]
                sg = _sigmoid(g)
                do_ref[:, hs] = da * (g * sg)
                dg_ref[:, hs] = (da * o_ref[:, hs] * (sg * (1.0 + g * (1.0 - sg)))).astype(BF16)

    blk = pl.BlockSpec((tm, HP), lambda i: (i, 0))
    return pl.pallas_call(
        body, name=name, grid=(S // tm,),
        out_shape=(jax.ShapeDtypeStruct((S, HP), F32), jax.ShapeDtypeStruct((S, HP), BF16)),
        in_specs=[pl.BlockSpec((tm, HEADS * VDIM), lambda i: (i, 0)), blk, pl.BlockSpec((tm, HP), lambda i: (i, CB_GA * LANE // HP))],
        out_specs=(blk, blk), compiler_params=_params(1),
    )(da, o, p)


def _assemble_dp(dga, ddil, dgb, dckv, dkr, dcq, name):
    S = dga.shape[0]
    tm = ROW_TILE
    nc = DIL_W // LANE
    dils = [d for _, d in DIL_PAIRS for _ in range(3)]

    def body(*refs):
        ga_ref, dil_refs, (gb_ref, ckv_ref, kr_ref, cq_ref, o_ref, scr) = refs[0], refs[1:10], refs[10:]
        o_ref[:, CB_GA * LANE:CB_GA * LANE + HP] = ga_ref[...]
        for t, (x_ref, d) in enumerate(zip(dil_refs, dils)):
            off = CB_DIL * LANE + t * DIL_W
            if d == 1:
                o_ref[:, off:off + DIL_W] = x_ref[...]
            else:
                n = tm // d
                for r in range(d):
                    for c in range(nc):
                        scr.at[c][pl.ds(r, n, stride=d), :] = x_ref[:, r * DIL_W + c * LANE:r * DIL_W + (c + 1) * LANE].astype(F32)
                for c in range(nc):
                    o_ref[:, off + c * LANE:off + (c + 1) * LANE] = scr[c].astype(BF16)
        o_ref[:, CB_GB * LANE:CB_GB * LANE + DIL_W] = gb_ref[...]
        o_ref[:, CB_CKV * LANE:CB_CKV * LANE + KV_LORA] = ckv_ref[...]
        o_ref[:, CB_KR * LANE:(CB_KR + 1) * LANE] = kr_ref[...]
        o_ref[:, CB_CQ * LANE:CB_CQ * LANE + Q_LORA] = cq_ref[...]

    row = lambda w: pl.BlockSpec((tm, w), lambda i: (i, 0))
    return pl.pallas_call(
        body, name=name, grid=(S // tm,), out_shape=jax.ShapeDtypeStruct((S, NP), BF16),
        in_specs=[row(HP)] + [_view_spec(d, DIL_W) for d in dils] + [row(DIL_W), row(KV_LORA), row(LANE), row(Q_LORA)],
        out_specs=row(NP), scratch_shapes=[pltpu.VMEM((nc, tm, LANE), F32)], compiler_params=_params(1),
    )(dga, *ddil, dgb, dckv, dkr, dcq)


def _merge_weights(l0, l1, l2):
    mx = jnp.maximum(jnp.maximum(l0, l1), l2)
    e0, e1, e2 = jnp.exp(l0 - mx), jnp.exp(l1 - mx), jnp.exp(l2 - mx)
    den = e0 + e1 + e2
    return e0 / den, e1 / den, e2 / den


def _view_spec(d, width):
    return pl.BlockSpec((ROW_TILE // d, d * width), lambda i: (i, 0))


def _to_tokens(src_ref, scr, base, d):
    n = ROW_TILE // d
    for r in range(d):
        for c in range(DIL_W // LANE):
            scr.at[base + c][pl.ds(r, n, stride=d), :] = src_ref[:, r * DIL_W + c * LANE:r * DIL_W + (c + 1) * LANE]


def _from_tokens(scr, base, dst_ref, d):
    n = ROW_TILE // d
    for r in range(d):
        for c in range(DIL_W // LANE):
            dst_ref[:, r * DIL_W + c * LANE:r * DIL_W + (c + 1) * LANE] = scr.at[base + c][pl.ds(r, n, stride=d), :]


def _gate_b(os_, ls_, p, name):
    S = p.shape[0]
    tm = ROW_TILE
    nc = DIL_W // LANE
    dils = [d for _, d in DIL_PAIRS]

    def body(o0, o1, o2, l0, l1, l2, g_ref, b_ref, scr):
        for gi, (o_ref, l_ref) in enumerate(((o1, l1), (o2, l2))):
            _to_tokens(o_ref, scr, (2 * gi) * nc, dils[gi + 1])
            _to_tokens(l_ref, scr, (2 * gi + 1) * nc, dils[gi + 1])
        for c in range(nc):
            cs = slice(c * LANE, (c + 1) * LANE)
            a0, a1, a2 = _merge_weights(l0[:, cs], scr[nc + c], scr[3 * nc + c])
            bm = a0 * o0[:, cs] + a1 * scr[c] + a2 * scr[2 * nc + c]
            g = g_ref[:, cs]
            b_ref[:, cs] = (bm * (g * _sigmoid(g))).astype(BF16)

    blk = pl.BlockSpec((tm, DIL_W), lambda i: (i, 0))
    views = [_view_spec(d, DIL_W) for d in dils]
    return pl.pallas_call(
        body, name=name, grid=(S // tm,), out_shape=jax.ShapeDtypeStruct((S, DIL_W), BF16),
        in_specs=views + views + [pl.BlockSpec((tm, DIL_W), lambda i: (i, CB_GB * LANE // DIL_W))], out_specs=blk,
        scratch_shapes=[pltpu.VMEM((4 * nc, tm, LANE), F32)], compiler_params=_params(1),
    )(*os_, *ls_, p)


def _gate_b_bwd(db, os_, ls_, p, name):
    S = p.shape[0]
    tm = ROW_TILE
    nc = DIL_W // LANE
    dils = [d for _, d in DIL_PAIRS]

    def body(db_ref, o0, o1, o2, l0, l1, l2, g_ref, dg_ref, d0, d1, d2, t0, t1, t2, scr, out_scr):
        for gi, (o_ref, l_ref) in enumerate(((o1, l1), (o2, l2))):
            _to_tokens(o_ref, scr, (2 * gi) * nc, dils[gi + 1])
            _to_tokens(l_ref, scr, (2 * gi + 1) * nc, dils[gi + 1])
        lo = _lo_lanes()
        for c in range(nc):
            cs = slice(c * LANE, (c + 1) * LANE)
            a0, a1, a2 = _merge_weights(l0[:, cs], scr[nc + c], scr[3 * nc + c])
            bm = a0 * o0[:, cs] + a1 * scr[c] + a2 * scr[2 * nc + c]
            g = g_ref[:, cs]
            db = db_ref[:, cs]
            sg = _sigmoid(g)
            dbm = db * (g * sg)
            dg_ref[:, cs] = (db * bm * (sg * (1.0 + g * (1.0 - sg)))).astype(BF16)
            prod = dbm * bm
            tl = jnp.sum(jnp.where(lo, prod, 0.0), axis=-1, keepdims=True)
            th = jnp.sum(jnp.where(lo, 0.0, prod), axis=-1, keepdims=True)
            t = jnp.where(lo, tl, th)
            d0[:, cs] = a0 * dbm
            t0[:, cs] = a0 * t
            out_scr[c] = a1 * dbm
            out_scr[nc + c] = a1 * t
            out_scr[2 * nc + c] = a2 * dbm
            out_scr[3 * nc + c] = a2 * t
        _from_tokens(out_scr, 0, d1, dils[1])
        _from_tokens(out_scr, nc, t1, dils[1])
        _from_tokens(out_scr, 2 * nc, d2, dils[2])
        _from_tokens(out_scr, 3 * nc, t2, dils[2])

    blk = pl.BlockSpec((tm, DIL_W), lambda i: (i, 0))
    views = [_view_spec(d, DIL_W) for d in dils]
    fs = [jax.ShapeDtypeStruct((S // d, d * DIL_W), F32) for d in dils]
    outs = pl.pallas_call(
        body, name=name, grid=(S // tm,),
        out_shape=(jax.ShapeDtypeStruct((S, DIL_W), BF16), *fs, *fs),
        in_specs=[blk] + views + views + [pl.BlockSpec((tm, DIL_W), lambda i: (i, CB_GB * LANE // DIL_W))],
        out_specs=(blk, *views, *views),
        scratch_shapes=[pltpu.VMEM((4 * nc, tm, LANE), F32), pltpu.VMEM((4 * nc, tm, LANE), F32)], compiler_params=_params(1),
    )(db, *os_, *ls_, p)
    return outs[0], outs[1:4], outs[4:7]


def _loss_head(x, target, g, name):
    S, D = x.shape
    tm = ROW_TILE

    def body(x_ref, t_ref, g_ref, dx_ref, dg_ref, loss_ref):
        xv = x_ref[...]
        gv = g_ref[...]
        r = lax.rsqrt(jnp.mean(xv * xv, axis=-1, keepdims=True) + EPS)
        xr = xv * r
        err = xr * gv - t_ref[...]
        lpart = 0.5 * jnp.sum(jnp.mean(err * err, axis=-1, keepdims=True), axis=0, keepdims=True)
        dy = err / D
        dyg = dy * gv
        c = jnp.mean(dyg * xv, axis=-1, keepdims=True)
        dx_ref[...] = r * dyg - xv * (r * r * r) * c
        gpart = jnp.sum(dy * xr, axis=0, keepdims=True)

        @pl.when(pl.program_id(0) == 0)
        def _():
            dg_ref[...] = gpart
            loss_ref[...] = jnp.broadcast_to(lpart, loss_ref.shape)

        @pl.when(pl.program_id(0) > 0)
        def _():
            dg_ref[...] += gpart
            loss_ref[...] += jnp.broadcast_to(lpart, loss_ref.shape)

    row = pl.BlockSpec((tm, D), lambda i: (i, 0))
    vec = pl.BlockSpec((1, D), lambda i: (0, 0))
    return pl.pallas_call(
        body, name=name, grid=(S // tm,),
        out_shape=(jax.ShapeDtypeStruct((S, D), F32), jax.ShapeDtypeStruct((1, D), F32), jax.ShapeDtypeStruct((1, D), F32)),
        in_specs=[row, row, vec], out_specs=(row, vec, vec), compiler_params=_params(1),
    )(x, target, g.reshape(1, D))


def _adamw(parts, w, m, v, name):
    R, C = w.shape
    tr = _pick(R, (128, 64, 32, 16, 8))

    def body(p_ref, w_ref, m_ref, v_ref, g_ref, d_ref, nm_ref, nv_ref):
        g = p_ref[0].astype(F32)
        for k in range(1, N_DEV):
            g = g + p_ref[k].astype(F32)
        m2 = ADAM_B1 * m_ref[...] + (1.0 - ADAM_B1) * g
        v2 = ADAM_B2 * v_ref[...] + (1.0 - ADAM_B2) * (g * g)
        m_hat = m2 / (1.0 - ADAM_B1 ** ADAM_STEP)
        v_hat = v2 / (1.0 - ADAM_B2 ** ADAM_STEP)
        g_ref[...] = g
        d_ref[...] = -ADAM_LR * (m_hat / (jnp.sqrt(v_hat) + ADAM_EPS) + ADAM_WD * w_ref[...])
        nm_ref[...] = m2
        nv_ref[...] = v2

    blk = pl.BlockSpec((tr, C), lambda i: (i, 0))
    f = jax.ShapeDtypeStruct((R, C), F32)
    return pl.pallas_call(
        body, name=name, grid=(R // tr,), out_shape=(f, f, f, f),
        in_specs=[pl.BlockSpec((N_DEV, tr, C), lambda i: (0, i, 0)), blk, blk, blk], out_specs=(blk,) * 4,
        compiler_params=_params(1),
    )(parts, w, m, v)


def _exchange_copies(src_refs, out_refs, send_sems, recv_sems, local_sems, same_src):
    n = len(src_refs)
    x, y, c = lax.axis_index("x"), lax.axis_index("y"), lax.axis_index("c")
    me = 4 * x + 2 * y + c

    def block(t, j):
        return src_refs[t] if same_src else src_refs[t].at[j]

    local = [pltpu.make_async_copy(block(t, me), out_refs[t].at[me], local_sems.at[t]) for t in range(n)]
    remote = []
    for k in range(1, N_DEV):
        px = 1 - x if (k >> 2) & 1 else x
        py = 1 - y if (k >> 1) & 1 else y
        pc = 1 - c if k & 1 else c
        for t in range(n):
            remote.append(pltpu.make_async_remote_copy(
                src_ref=block(t, 4 * px + 2 * py + pc), dst_ref=out_refs[t].at[me],
                send_sem=send_sems.at[(k - 1) * n + t], recv_sem=recv_sems.at[(k - 1) * n + t],
                device_id=(px, py, pc), device_id_type=pl.DeviceIdType.MESH))
    return local, remote


def _exchange_start(*args):
    local, remote = _exchange_copies(*args)
    for cp in local + remote:
        cp.start()


def _exchange_wait(*args):
    local, remote = _exchange_copies(*args)
    for cp in remote:
        cp.wait()
    for cp in local:
        cp.wait()


def _exchange_shapes(srcs, same_src):
    return tuple(jax.ShapeDtypeStruct((N_DEV,) + (tuple(s.shape) if same_src else tuple(s.shape[1:])), s.dtype) for s in srcs)


def _exchange_sems(n):
    return [pltpu.SemaphoreType.DMA(((N_DEV - 1) * n,)), pltpu.SemaphoreType.DMA(((N_DEV - 1) * n,)), pltpu.SemaphoreType.DMA((n,))]


def _exchange(srcs, same_src, name):
    n = len(srcs)

    def body(*refs):
        args = (refs[:n], refs[n:2 * n], *refs[2 * n:], same_src)
        _exchange_start(*args)
        _exchange_wait(*args)

    hbm = pl.BlockSpec(memory_space=pltpu.HBM)
    outs = pl.pallas_call(
        body, name=name, out_shape=_exchange_shapes(srcs, same_src),
        in_specs=[hbm] * n, out_specs=(hbm,) * n, scratch_shapes=_exchange_sems(n),
    )(*srcs)
    return list(outs)


def _full_weights(g_in, g_uq, g_ukv, g_out):
    cat_cols = lambda t: jnp.moveaxis(t, 0, 2).reshape(t.shape[1], t.shape[2], -1)
    return cat_cols(g_in), cat_cols(g_uq), cat_cols(g_ukv), jnp.moveaxis(g_out, 0, 1).reshape(g_out.shape[1], D_MODEL, D_MODEL)


def _grad_blocks(g_in, g_uq, g_ukv, g_out):
    split_cols = lambda t: jnp.moveaxis(t.astype(BF16).reshape(t.shape[0], N_DEV, -1), 1, 0)
    return [split_cols(g_in), split_cols(g_uq), split_cols(g_ukv), g_out.astype(BF16).reshape(N_DEV, D_MODEL // N_DEV, D_MODEL)]


def _layer_weights(w_in, w_uq, w_ukv, w_out):
    z = lambda r, n: jnp.zeros((r, n), BF16)
    c_q, c_kv, k_r = w_in[:, 0:384], w_in[:, 384:640], w_in[:, 640:672]
    gate_a, dil, gate_b = w_in[:, 672:1184], w_in[:, 1184:5792], w_in[:, 5792:6304]
    ga_pad = jnp.pad(gate_a.reshape(D_MODEL, HEADS, VDIM), ((0, 0), (0, 0), (0, LANE - VDIM))).reshape(D_MODEL, HP)
    w_p = jnp.concatenate([ga_pad, dil, gate_b, c_kv, z(D_MODEL, 64), k_r, z(D_MODEL, 32), c_q], axis=1)
    uq = jnp.pad(w_uq.reshape(Q_LORA, HEADS, NOPE + ROPE), ((0, 0), (0, 0), (0, LANE - NOPE - ROPE))).reshape(Q_LORA, HP)
    ukv = w_ukv.reshape(KV_LORA, HEADS, NOPE + VDIM)
    pad64 = lambda t: jnp.pad(t, ((0, 0), (0, 0), (0, LANE - 64))).reshape(KV_LORA, HP)
    uk, uv = pad64(ukv[..., :NOPE]), pad64(ukv[..., NOPE:])
    wa = w_out[:HEADS * VDIM]
    wb = w_out[HEADS * VDIM:]
    return dict(p=w_p, uq=uq, uk=uk, uv=uv, wa=wa, wb=wb)


def _unpad_grads(g):
    gp = g["p"]
    seg = lambda cb, n: gp[:, cb * LANE:cb * LANE + n]
    ga = seg(CB_GA, HP).reshape(D_MODEL, HEADS, LANE)[..., :VDIM].reshape(D_MODEL, HEADS * VDIM)
    k_r = gp[:, CB_KR * LANE + NOPE:CB_KR * LANE + NOPE + ROPE]
    g_in = jnp.concatenate([seg(CB_CQ, Q_LORA), seg(CB_CKV, KV_LORA), k_r, ga, seg(CB_DIL, 9 * DIL_W), seg(CB_GB, DIL_W)], axis=1)
    g_uq = g["uq"].reshape(Q_LORA, HEADS, LANE)[..., :NOPE + ROPE].reshape(Q_LORA, -1)
    uk = g["uk"].reshape(KV_LORA, HEADS, LANE)[..., :NOPE]
    uv = g["uv"].reshape(KV_LORA, HEADS, LANE)[..., :VDIM]
    g_ukv = jnp.concatenate([uk, uv], axis=-1).reshape(KV_LORA, -1)
    g_out = jnp.concatenate([g["wa"], g["wb"]], axis=0)
    return g_in, g_uq, g_ukv, g_out


def _layer_fwd(x, w, norm_g, q_norm_g, kv_norm_g, mla, dil, l, comm=None):
    n = lambda s: f"l{l}_{s}"
    h = _rms_fwd(x, 0, D_MODEL, norm_g, n("norm"))
    p = _mm(h, w["p"], "nn", n("in_proj"))
    cqn = _rms_fwd(p, CB_CQ * LANE // Q_LORA, Q_LORA, q_norm_g, n("q_norm"))
    ckvn = _rms_fwd(p, CB_CKV * LANE // KV_LORA, KV_LORA, kv_norm_g, n("kv_norm"))
    qp = _mm(cqn, w["uq"], "nn", n("q_up"))
    kpre = _mm(ckvn, w["uk"], "nn", n("k_up"))
    v = _mm(ckvn, w["uv"], "nn", n("v_up"), out_dtype=BF16)
    q = _rope_heads(qp, mla["c_q"], mla["s1"], mla["s2"], n("q_rope"), BF16)
    k = _k_assemble(kpre, p, mla, n("k_asm"))
    o, lse, received = _flash_fwd(q, k, v, n("mla_fwd"), comm)
    a = _gate_a(o, p, n("gate_a"))
    dilr, os_, ls_ = [], [], []
    for g, (_, d) in enumerate(DIL_PAIRS):
        dilr.append(_dil_prep(p, dil, g, d, n(f"dil_prep{g}")))
        og, lg = _band_fwd(dilr[g], d, n(f"band{g}_fwd"))
        os_.append(og)
        ls_.append(lg)
    b = _gate_b(os_, ls_, p, n("gate_b"))
    x1 = _mm(a, w["wa"], "nn", n("out_a"), res=x)
    x2 = _mm(b, w["wb"], "nn", n("out_b"), res=x1)
    saved = dict(x=x, h=h, p=p, cqn=cqn, ckvn=ckvn, q=q, k=k, v=v, o=o, lse=lse, a=a, dilr=dilr, os=os_, ls=ls_, b=b)
    return x2, saved, received


def _layer_bwd(dx, sv, w, norm_g, q_norm_g, kv_norm_g, mla, dil, l, comm=None):
    n = lambda s: f"l{l}_{s}"
    g = {}
    da = _mm(dx, w["wa"], "nt", n("d_a"))
    db = _mm(dx, w["wb"], "nt", n("d_b"))
    g["wa"] = _mm(sv["a"], dx, "tn", n("dw_a"))
    g["wb"] = _mm(sv["b"], dx, "tn", n("dw_b"))
    do, dga = _gate_a_bwd(da, sv["o"], sv["p"], n("gate_a_bwd"))
    dgb, dos, dts = _gate_b_bwd(db, sv["os"], sv["ls"], sv["p"], n("gate_b_bwd"))
    ddil = []
    for gi, (_, d) in enumerate(DIL_PAIRS):
        dq = _band_bwd_q(sv["dilr"][gi], d, dos[gi], sv["ls"][gi], dts[gi], dil, n(f"band{gi}_bwd_q"))
        dk, dv = _band_bwd_kv(sv["dilr"][gi], d, dos[gi], sv["ls"][gi], dts[gi], dil, n(f"band{gi}_bwd_kv"))
        ddil += [dq, dk, dv]
    dq, dk, dv, received = _flash_bwd(sv["q"], sv["k"], sv["v"], sv["o"], do, sv["lse"], n("mla_bwd"), comm)
    dqp = _rope_heads(dq, mla["c_q"], -mla["s1"], -mla["s2"], n("q_rope_bwd"), BF16)
    dkr = _kr_bwd(dk, mla, n("kr_bwd"))
    g["uq"] = _mm(sv["cqn"], dqp, "tn", n("dw_uq"))
    g["uk"] = _mm(sv["ckvn"], dk, "tn", n("dw_uk"))
    g["uv"] = _mm(sv["ckvn"], dv, "tn", n("dw_uv"))
    dcqn = _mm(dqp, w["uq"], "nt", n("d_cqn"))
    dckvn = _mm(dk, w["uk"], "nt", n("d_ckvn_k"))
    dckvn = _mm(dv, w["uv"], "nt", n("d_ckvn_v"), res=dckvn)
    dcq, g_qn = _rms_bwd(sv["p"], CB_CQ * LANE // Q_LORA, Q_LORA, dcqn, q_norm_g, n("q_norm_bwd"), BF16)
    dckv, g_kvn = _rms_bwd(sv["p"], CB_CKV * LANE // KV_LORA, KV_LORA, dckvn, kv_norm_g, n("kv_norm_bwd"), BF16)
    dp = _assemble_dp(dga, ddil, dgb, dckv, dkr, dcq, n("dp"))
    g["p"] = _mm(sv["h"], dp, "tn", n("dw_in"))
    dh = _mm(dp, w["p"], "nt", n("d_h"))
    dx_in, g_n = _rms_bwd(sv["x"], 0, D_MODEL, dh, norm_g, n("norm_bwd"), F32, res=dx)
    return dx_in, g, g_n, g_qn, g_kvn, received


_SMALL_ROWS = 16


def _pack_small(norm, qn, kvn, fin, loss_row=None):
    padc = lambda t: jnp.pad(t, ((0, 0), (0, D_MODEL - t.shape[1])))
    extra = jnp.zeros((1, D_MODEL), F32) if loss_row is None else loss_row
    return jnp.concatenate([norm, padc(qn), padc(kvn), fin.reshape(1, D_MODEL), extra, jnp.zeros((2, D_MODEL), F32)], axis=0)


def _unpack_small(p):
    return (p[0:4], p[4:8, :Q_LORA], p[8:12, :KV_LORA], p[12]), p[13, 0]


def kernel(x, norm_g, w_in, q_norm_g, kv_norm_g, w_uq, w_ukv, w_out, final_g, loss_target, m_norm_g, m_w_in, m_q_norm_g, m_kv_norm_g, m_w_uq, m_w_ukv, m_w_out, m_final_g, v_norm_g, v_w_in, v_q_norm_g, v_kv_norm_g, v_w_uq, v_w_ukv, v_w_out, v_final_g):
    S = x.shape[1]
    xs = x.reshape(S, D_MODEL)
    target = loss_target.reshape(S, D_MODEL)

    wb = [t.astype(BF16) for t in (w_in, w_uq, w_ukv, w_out)]
    first = _full_weights(*_exchange([t[0:1] for t in wb], True, "gather_weights0"))
    ws = [_layer_weights(*(t[0] for t in first))]
    mla, dil = _rope_tables(S)

    saved = []
    h = xs
    for l in range(DEPTH):
        comm = ([t[1:] for t in wb], True) if l == 0 else None
        h, sv, received = _layer_fwd(h, ws[l], norm_g[l], q_norm_g[l], kv_norm_g[l], mla, dil, l, comm)
        saved.append(sv)
        if l == 0:
            rest = _full_weights(*received)
            ws += [_layer_weights(*(t[i] for t in rest)) for i in range(DEPTH - 1)]
    dx, g_final, loss_row = _loss_head(h, target, final_g, "loss_head")
    g_norm, g_qn, g_kvn, parts_l = [None] * DEPTH, [None] * DEPTH, [None] * DEPTH, [None] * DEPTH
    blocks = None
    for l in reversed(range(DEPTH)):
        comm = (blocks, False) if blocks is not None else None
        dx, g, g_norm[l], g_qn[l], g_kvn[l], received = _layer_bwd(dx, saved[l], ws[l], norm_g[l], q_norm_g[l], kv_norm_g[l], mla, dil, l, comm)
        if blocks is not None:
            parts_l[l + 1] = received
        blocks = _grad_blocks(*_unpad_grads(g))
    parts_l[0] = _exchange(blocks, False, "exchange_grads0")

    parts = [jnp.stack([parts_l[l][t] for l in range(DEPTH)], axis=1) for t in range(4)]
    sh = []
    for t, (pt, w, m, v) in enumerate(zip(parts, (w_in, w_uq, w_ukv, w_out), (m_w_in, m_w_uq, m_w_ukv, m_w_out), (v_w_in, v_w_uq, v_w_ukv, v_w_out))):
        two = lambda a: a.reshape(-1, a.shape[-1])
        outs = _adamw(pt.reshape(N_DEV, -1, pt.shape[-1]), two(w), two(m), two(v), f"adamw_{t}")
        sh.append([o.reshape(w.shape) for o in outs])

    small = _pack_small(jnp.concatenate(g_norm, 0), jnp.concatenate(g_qn, 0), jnp.concatenate(g_kvn, 0), g_final, loss_row)
    (small_parts,) = _exchange([small], True, "gather_small")
    souts = _adamw(small_parts, _pack_small(norm_g, q_norm_g, kv_norm_g, final_g), _pack_small(m_norm_g, m_q_norm_g, m_kv_norm_g, m_final_g),
                   _pack_small(v_norm_g, v_q_norm_g, v_kv_norm_g, v_final_g), "adamw_small")
    (g_sm, loss), (d_sm, _), (m_sm, _), (v_sm, _) = (_unpack_small(t) for t in souts)

    def order(sm, k):
        return (sm[0], sh[0][k], sm[1], sm[2], sh[1][k], sh[2][k], sh[3][k], sm[3])

    return (loss, dx.reshape(1, S, D_MODEL), *order(g_sm, 0), *order(d_sm, 1), *order(m_sm, 2), *order(v_sm, 3))
```

```python
import functools
import math

import jax
import jax.numpy as jnp
from jax import lax
from jax.experimental import pallas as pl
from jax.experimental.pallas import tpu as pltpu

F32 = jnp.float32
BF16 = jnp.bfloat16

D_MODEL = 1024
DEPTH = 4
HEADS = 8
NOPE = 64
ROPE = 32
VDIM = 64
Q_LORA = 384
KV_LORA = 256
DIL_PAIRS = ((128, 1), (512, 4), (2048, 16))
DIL_HD = 64
DIL_W = 512
ROT = 16
HALF = 64
THETA = 500000.0
EPS = 1e-6
IN_WIDTH = 6304
N_DEV = 8

LANE = 128
CB_GA, CB_DIL, CB_GB, CB_CKV, CB_KR, CB_CQ = 0, 8, 44, 48, 50, 51
NP = 54 * LANE
HP = HEADS * LANE

ADAM_LR = 0.001
ADAM_B1 = 0.9
ADAM_B2 = 0.999
ADAM_EPS = 1e-08
ADAM_WD = 0.01
ADAM_STEP = 10

VMEM_LIMIT = 48 * 1024 * 1024
ROW_TILE = 512
SUB = 128

_NT = (((1,), (1,)), ((), ()))
_NN = (((1,), (0,)), ((), ()))
_TN = (((0,), (0,)), ((), ()))


def _params(n_axes):
    return pltpu.CompilerParams(dimension_semantics=("arbitrary",) * n_axes, vmem_limit_bytes=VMEM_LIMIT)


def _pick(n, cands):
    for c in cands:
        if n % c == 0:
            return c
    raise ValueError(f"no tile for {n}")


def _mm(a, b, mode, name, out_dtype=F32, res=None):
    if mode == "nn":
        (M, K), (K2, N) = a.shape, b.shape
    elif mode == "nt":
        (M, K), (N, K2) = a.shape, b.shape
    else:
        (K, M), (K2, N) = a.shape, b.shape
    assert K == K2, (a.shape, b.shape, mode)
    tm = _pick(M, (1024, 512, 384, 256, 128))
    tn = _pick(N, (1152, 1024, 768, 640, 512, 384, 256, 128))
    tk = _pick(K, (2304, 2048, 1152, 1024, 768, 640, 512, 384, 256, 128))
    nk = K // tk
    dims = {"nn": _NN, "nt": _NT, "tn": _TN}[mode]

    def body(*refs):
        if res is not None:
            a_ref, b_ref, r_ref, o_ref = refs[:4]
        else:
            a_ref, b_ref, o_ref = refs[:3]
            r_ref = None
        part = lax.dot_general(a_ref[...].astype(BF16), b_ref[...].astype(BF16), dims, preferred_element_type=F32)

        def finish(acc):
            if r_ref is not None:
                acc = acc + r_ref[...]
            o_ref[...] = acc.astype(out_dtype)

        if nk == 1:
            finish(part)
        else:
            acc_ref = refs[-1]
            k = pl.program_id(2)

            @pl.when(k == 0)
            def _():
                acc_ref[...] = part

            @pl.when(k > 0)
            def _():
                acc_ref[...] += part

            @pl.when(k == nk - 1)
            def _():
                finish(acc_ref[...])

    if mode == "nn":
        a_spec = pl.BlockSpec((tm, tk), lambda i, j, k: (i, k))
        b_spec = pl.BlockSpec((tk, tn), lambda i, j, k: (k, j))
    elif mode == "nt":
        a_spec = pl.BlockSpec((tm, tk), lambda i, j, k: (i, k))
        b_spec = pl.BlockSpec((tn, tk), lambda i, j, k: (j, k))
    else:
        a_spec = pl.BlockSpec((tk, tm), lambda i, j, k: (k, i))
        b_spec = pl.BlockSpec((tk, tn), lambda i, j, k: (k, j))
    o_spec = pl.BlockSpec((tm, tn), lambda i, j, k: (i, j))
    in_specs = [a_spec, b_spec] + ([o_spec] if res is not None else [])
    args = (a, b) + ((res,) if res is not None else ())
    return pl.pallas_call(
        body, name=name, grid=(M // tm, N // tn, nk), out_shape=jax.ShapeDtypeStruct((M, N), out_dtype),
        in_specs=in_specs, out_specs=o_spec,
        scratch_shapes=[pltpu.VMEM((tm, tn), F32)] if nk > 1 else [],
        compiler_params=_params(3),
    )(*args)


def _rms_fwd(src, cb, width, g, name):
    S = src.shape[0]
    tm = ROW_TILE

    def body(x_ref, g_ref, o_ref):
        x = x_ref[...]
        r = lax.rsqrt(jnp.mean(x * x, axis=-1, keepdims=True) + EPS)
        o_ref[...] = (x * r * g_ref[...]).astype(BF16)

    return pl.pallas_call(
        body, name=name, grid=(S // tm,), out_shape=jax.ShapeDtypeStruct((S, width), BF16),
        in_specs=[pl.BlockSpec((tm, width), lambda i: (i, cb)), pl.BlockSpec((1, width), lambda i: (0, 0))],
        out_specs=pl.BlockSpec((tm, width), lambda i: (i, 0)), compiler_params=_params(1),
    )(src, g.reshape(1, width))


def _rms_bwd(src, cb, width, dy, g, name, out_dtype, res=None):
    S = src.shape[0]
    tm = ROW_TILE

    def body(*refs):
        if res is not None:
            x_ref, dy_ref, g_ref, r_ref, dx_ref, dg_ref = refs
        else:
            x_ref, dy_ref, g_ref, dx_ref, dg_ref = refs
            r_ref = None
        x = x_ref[...]
        dy = dy_ref[...]
        r = lax.rsqrt(jnp.mean(x * x, axis=-1, keepdims=True) + EPS)
        dyg = dy * g_ref[...]
        c = jnp.mean(dyg * x, axis=-1, keepdims=True)
        dx = r * dyg - x * (r * r * r) * c
        if r_ref is not None:
            dx = dx + r_ref[...]
        dx_ref[...] = dx.astype(out_dtype)
        part = jnp.sum(dy * x * r, axis=0, keepdims=True)

        @pl.when(pl.program_id(0) == 0)
        def _():
            dg_ref[...] = part

        @pl.when(pl.program_id(0) > 0)
        def _():
            dg_ref[...] += part

    row = pl.BlockSpec((tm, width), lambda i: (i, 0))
    in_specs = [pl.BlockSpec((tm, width), lambda i: (i, cb)), row, pl.BlockSpec((1, width), lambda i: (0, 0))]
    args = [src, dy, g.reshape(1, width)]
    if res is not None:
        in_specs.append(row)
        args.append(res)
    return pl.pallas_call(
        body, name=name, grid=(S // tm,),
        out_shape=(jax.ShapeDtypeStruct((S, width), out_dtype), jax.ShapeDtypeStruct((1, width), F32)),
        in_specs=in_specs, out_specs=(row, pl.BlockSpec((1, width), lambda i: (0, 0))),
        compiler_params=_params(1),
    )(*args)


def _rot(x, c, s1, s2, h):
    return x * c + pltpu.roll(x, x.shape[1] - h, 1) * s1 + pltpu.roll(x, h, 1) * s2


def _rope_tables(S):
    def tables(dim):
        inv = 1.0 / (THETA ** (jnp.arange(0, dim, 2, dtype=F32) / dim))
        ang = jnp.arange(S, dtype=F32)[:, None] * inv[None, :]
        return jnp.cos(ang), jnp.sin(ang)

    cm, sm = tables(ROPE)
    cd, sd = tables(ROT)
    z = lambda n: jnp.zeros((S, n), F32)
    o = lambda n: jnp.ones((S, n), F32)
    mla = dict(
        c_q=jnp.concatenate([o(64), cm, cm, z(32)], 1),
        c_kr=jnp.concatenate([z(64), cm, cm, z(32)], 1),
        s1=jnp.concatenate([z(64), -sm, z(16), z(32)], 1),
        s2=jnp.concatenate([z(64), z(16), sm, z(32)], 1),
    )
    one = lambda a, b, c: jnp.concatenate([a, b, c, a, b, c], 1)
    dil = dict(c=one(cd, cd, o(48)), s1=one(-sd, z(8), z(48)), s2=one(z(8), sd, z(48)))
    return mla, dil


QK_LOG2_SCALE = (NOPE + ROPE) ** -0.5 * math.log2(math.e)


def _rope_heads(src, c, s1, s2, name, out_dtype, scale=1.0):
    S = src.shape[0]
    tm = ROW_TILE

    def body(x_ref, c_ref, s1_ref, s2_ref, o_ref):
        cv, s1v, s2v = c_ref[...], s1_ref[...], s2_ref[...]
        for h in range(HEADS):
            sl = slice(h * LANE, (h + 1) * LANE)
            o_ref[:, sl] = (_rot(x_ref[:, sl], cv, s1v, s2v, ROPE // 2) * scale).astype(out_dtype)

    tab = pl.BlockSpec((tm, LANE), lambda i: (i, 0))
    wide = pl.BlockSpec((tm, HP), lambda i: (i, 0))
    return pl.pallas_call(
        body, name=name, grid=(S // tm,), out_shape=jax.ShapeDtypeStruct((S, HP), out_dtype),
        in_specs=[wide, tab, tab, tab], out_specs=wide, compiler_params=_params(1),
    )(src, c, s1, s2)


def _k_assemble(kpre, p, mla, name):
    S = kpre.shape[0]
    tm = ROW_TILE

    def body(k_ref, kr_ref, c_ref, s1_ref, s2_ref, o_ref):
        r = _rot(kr_ref[...], c_ref[...], s1_ref[...], s2_ref[...], ROPE // 2)
        for h in range(HEADS):
            sl = slice(h * LANE, (h + 1) * LANE)
            o_ref[:, sl] = (k_ref[:, sl] + r).astype(BF16)

    tab = pl.BlockSpec((tm, LANE), lambda i: (i, 0))
    wide = pl.BlockSpec((tm, HP), lambda i: (i, 0))
    return pl.pallas_call(
        body, name=name, grid=(S // tm,), out_shape=jax.ShapeDtypeStruct((S, HP), BF16),
        in_specs=[wide, pl.BlockSpec((tm, LANE), lambda i: (i, CB_KR)), tab, tab, tab],
        out_specs=wide, compiler_params=_params(1),
    )(kpre, p, mla["c_kr"], mla["s1"], mla["s2"])


def _kr_bwd(dk, mla, name):
    S = dk.shape[0]
    tm = ROW_TILE

    def body(dk_ref, c_ref, s1_ref, s2_ref, o_ref):
        t = dk_ref[:, 0:LANE]
        for h in range(1, HEADS):
            t = t + dk_ref[:, h * LANE:(h + 1) * LANE]
        lane = lax.broadcasted_iota(jnp.int32, (1, LANE), 1)
        t = jnp.where((lane >= NOPE) & (lane < NOPE + ROPE), t, 0.0)
        o_ref[...] = _rot(t, c_ref[...], -s1_ref[...], -s2_ref[...], ROPE // 2).astype(BF16)

    tab = pl.BlockSpec((tm, LANE), lambda i: (i, 0))
    return pl.pallas_call(
        body, name=name, grid=(S // tm,), out_shape=jax.ShapeDtypeStruct((S, LANE), BF16),
        in_specs=[pl.BlockSpec((tm, HP), lambda i: (i, 0)), tab, tab, tab],
        out_specs=tab, compiler_params=_params(1),
    )(dk, mla["c_kr"], mla["s1"], mla["s2"])


def _dil_prep(p, dil, g, d, name):
    S = p.shape[0]
    tm = ROW_TILE
    first = CB_DIL * LANE // DIL_W + 3 * g
    nc = DIL_W // LANE
    n = tm // d

    def body(q_ref, k_ref, v_ref, c_ref, s1_ref, s2_ref, o_ref, scr):
        cv, s1v, s2v = (jnp.tile(t[...], (1, nc)) for t in (c_ref, s1_ref, s2_ref))
        ys = (_rot(q_ref[...], cv, s1v, s2v, ROT // 2) * (DIL_HD ** -0.5), _rot(k_ref[...], cv, s1v, s2v, ROT // 2), v_ref[...])
        if d == 1:
            for t, y in enumerate(ys):
                o_ref[:, t * DIL_W:(t + 1) * DIL_W] = y.astype(BF16)
        else:
            for t, y in enumerate(ys):
                for c in range(nc):
                    scr[t * nc + c] = y[:, c * LANE:(c + 1) * LANE]
            for r in range(d):
                for tc in range(3 * nc):
                    col = r * 3 * DIL_W + tc * LANE
                    o_ref[:, col:col + LANE] = scr.at[tc][pl.ds(r, n, stride=d), :].astype(BF16)

    tab = pl.BlockSpec((tm, LANE), lambda i: (i, 0))
    chunk = lambda t: pl.BlockSpec((tm, DIL_W), lambda i: (i, first + t))
    return pl.pallas_call(
        body, name=name, grid=(S // tm,), out_shape=jax.ShapeDtypeStruct((S // d, d * 3 * DIL_W), BF16),
        in_specs=[chunk(0), chunk(1), chunk(2), tab, tab, tab], out_specs=_view_spec(d, 3 * DIL_W),
        scratch_shapes=[pltpu.VMEM((3 * nc, tm, LANE), F32)], compiler_params=_params(1),
    )(p, p, p, dil["c"], dil["s1"], dil["s2"])


def _grid_ends(dims):
    ids = [pl.program_id(a) for a in range(len(dims))]
    first = functools.reduce(jnp.logical_and, [i == 0 for i in ids])
    last = functools.reduce(jnp.logical_and, [i == n - 1 for i, n in zip(ids, dims)])
    return first, last


def _flash_fwd(q, k, v, name, comm=None):
    S = q.shape[0]
    tq = _pick(S, (1024, 512))
    tk = _pick(S, (2048, 1024, 512))
    nk = S // tk
    c2 = (NOPE + ROPE) ** -0.5 * math.log2(math.e)
    srcs, same_src = comm if comm is not None else ([], True)
    n = len(srcs)
    grid = (HEADS, S // tq, nk)

    def body(*refs):
        q_ref, k_ref, v_ref = refs[:3]
        o_ref, lse_ref = refs[3 + n:5 + n]
        m_s, acc_s = refs[5 + 2 * n:7 + 2 * n]
        ex = (refs[3:3 + n], refs[5 + n:5 + 2 * n], *refs[7 + 2 * n:], same_src)
        j = pl.program_id(2)
        if n:
            first, last = _grid_ends(grid)
            pl.when(first)(lambda: _exchange_start(*ex))

        @pl.when(j == 0)
        def _():
            m_s[...] = jnp.full(m_s.shape, -jnp.inf, F32)
            acc_s[...] = jnp.zeros(acc_s.shape, F32)

        lane = lax.broadcasted_iota(jnp.int32, (1, LANE), 1)
        vv = jnp.where(lane == VDIM, jnp.ones((), BF16), v_ref[...])
        t = lax.dot_general(q_ref[...], k_ref[...], _NT, preferred_element_type=F32)
        m_prev = m_s[...]
        m_new = jnp.maximum(m_prev, jnp.max(t, axis=-1, keepdims=True))
        alpha = jnp.exp2(m_prev - m_new)
        e = jnp.exp2(t - jnp.tile(m_new, (1, tk // LANE)))
        acc_s[...] = alpha * acc_s[...] + jnp.dot(e.astype(BF16), vv, preferred_element_type=F32)
        m_s[...] = m_new

        @pl.when(j == nk - 1)
        def _():
            acc = acc_s[...]
            l = acc[:, VDIM:VDIM + 1]
            o_ref[...] = jnp.where(lane < VDIM, acc / l, 0.0)
            lse_ref[...] = (m_s[...] + jnp.log2(l)) * math.log(2.0)

        if n:
            pl.when(last)(lambda: _exchange_wait(*ex))

    qs = pl.BlockSpec((tq, LANE), lambda h, i, j: (i, h))
    ks = pl.BlockSpec((tk, LANE), lambda h, i, j: (j, h))
    hbm = pl.BlockSpec(memory_space=pltpu.HBM)
    outs = pl.pallas_call(
        body, name=name, grid=grid,
        out_shape=(jax.ShapeDtypeStruct((S, HP), F32), jax.ShapeDtypeStruct((S, HP), F32)) + _exchange_shapes(srcs, same_src),
        in_specs=[qs, ks, ks] + [hbm] * n, out_specs=(qs, qs) + (hbm,) * n,
        scratch_shapes=[pltpu.VMEM((tq, LANE), F32), pltpu.VMEM((tq, LANE), F32)] + (_exchange_sems(n) if n else []),
        compiler_params=_params(3),
    )(q, k, v, *srcs)
    return outs[0], outs[1], list(outs[2:])


def _flash_bwd(q, k, v, o, do, lse, name, comm=None):
    S = q.shape[0]
    tq = _pick(S, (1024, 512))
    tk = _pick(S, (2048, 1024, 512))
    nq = S // tq
    scale = (NOPE + ROPE) ** -0.5
    srcs, same_src = comm if comm is not None else ([], True)
    n = len(srcs)
    grid = (HEADS, S // tk, nq)

    def body(*refs):
        q_ref, k_ref, v_ref, o_ref, do_ref, lse_ref = refs[:6]
        dq_ref, dk_ref, dv_ref = refs[6 + n:9 + n]
        dk_s, dv_s = refs[9 + 2 * n:11 + 2 * n]
        ex = (refs[6:6 + n], refs[9 + n:9 + 2 * n], *refs[11 + 2 * n:], same_src)
        j = pl.program_id(1)
        i = pl.program_id(2)
        if n:
            first, last = _grid_ends(grid)
            pl.when(first)(lambda: _exchange_start(*ex))
        qv = q_ref[...]
        kv = k_ref[...]
        do = do_ref[...]
        dob = do.astype(BF16)
        lse_row = jnp.transpose(lse_ref[...])[0:1, :] * math.log2(math.e)
        delta_row = jnp.sum(jnp.transpose(do * o_ref[...]), axis=0, keepdims=True)
        st = lax.dot_general(kv, qv, _NT, preferred_element_type=F32)
        pt = jnp.exp2(st - lse_row)
        dpt = lax.dot_general(v_ref[...], dob, _NT, preferred_element_type=F32)
        dst = (pt * (dpt - delta_row)).astype(BF16)
        dv_part = jnp.dot(pt.astype(BF16), dob, preferred_element_type=F32)
        dk_part = jnp.dot(dst, qv, preferred_element_type=F32)
        dq_part = jnp.transpose(jnp.dot(jnp.transpose(kv), dst, preferred_element_type=F32)) * scale

        @pl.when(i == 0)
        def _():
            dk_s[...] = dk_part
            dv_s[...] = dv_part

        @pl.when(i > 0)
        def _():
            dk_s[...] += dk_part
            dv_s[...] += dv_part

        rows = pl.ds(pl.multiple_of(i * tq, tq), tq)

        @pl.when(j == 0)
        def _():
            dq_ref[rows, :] = dq_part

        @pl.when(j > 0)
        def _():
            dq_ref[rows, :] += dq_part

        @pl.when(i == nq - 1)
        def _():
            dk_ref[...] = dk_s[...] * math.log(2.0)
            dv_ref[...] = dv_s[...].astype(BF16)

        if n:
            pl.when(last)(lambda: _exchange_wait(*ex))

    qs = pl.BlockSpec((tq, LANE), lambda h, j, i: (i, h))
    ks = pl.BlockSpec((tk, LANE), lambda h, j, i: (j, h))
    hbm = pl.BlockSpec(memory_space=pltpu.HBM)
    outs = pl.pallas_call(
        body, name=name, grid=grid,
        out_shape=(jax.ShapeDtypeStruct((S, HP), F32), jax.ShapeDtypeStruct((S, HP), F32), jax.ShapeDtypeStruct((S, HP), BF16))
        + _exchange_shapes(srcs, same_src),
        in_specs=[qs, ks, ks, qs, qs, qs] + [hbm] * n,
        out_specs=(pl.BlockSpec((S, LANE), lambda h, j, i: (0, h)), ks, ks) + (hbm,) * n,
        scratch_shapes=[pltpu.VMEM((tk, LANE), F32), pltpu.VMEM((tk, LANE), F32)] + (_exchange_sems(n) if n else []),
        compiler_params=_params(3),
    )(q, k, v, o, do, lse, *srcs)
    return outs[0], outs[1], outs[2], list(outs[3:])


def _band_tiles(L):
    tq = min(512, L)
    return tq, tq // SUB, tq // HALF, L // HALF


def _halo_specs(tq, rpb, n64, col):
    prev = pl.BlockSpec((HALF, DIL_W), lambda r, i: (jnp.maximum(rpb * i - 1, 0), col(r)))
    cur = pl.BlockSpec((tq, DIL_W), lambda r, i: (i, col(r)))
    nxt = pl.BlockSpec((HALF, DIL_W), lambda r, i: (jnp.minimum(rpb * i + rpb, n64 - 1), col(r)))
    return [prev, cur, nxt]


def _fill(buf, prev_ref, cur_ref, next_ref, tq):
    buf[0:HALF, :] = prev_ref[...]
    buf[HALF:HALF + tq, :] = cur_ref[...]
    buf[HALF + tq:HALF + tq + HALF, :] = next_ref[...]


def _lo_lanes():
    return lax.broadcasted_iota(jnp.int32, (1, LANE), 1) < DIL_HD


def _stack_heads(x, lo):
    zero = jnp.zeros_like(x)
    return jnp.concatenate([jnp.where(lo, x, zero), jnp.where(lo, zero, x)], axis=0)


def _stack_cols(x):
    return jnp.concatenate([x[:, 0:1], x[:, DIL_HD:DIL_HD + 1]], axis=0)


def _band_valid(q0, k0, nq, nk, L, bound_q):
    qpos = q0 + lax.broadcasted_iota(jnp.int32, (nq, 1), 0)
    kpos = k0 + lax.broadcasted_iota(jnp.int32, (1, nk), 1)
    side = qpos if bound_q else kpos
    return (jnp.abs(qpos - kpos) <= HALF) & (side >= 0) & (side < L)


def _band_fwd(dilr, d, name):
    L = dilr.shape[0]
    S = L * d
    tq, nsub, rpb, n64 = _band_tiles(L)
    view = dilr
    win = SUB + 2 * HALF

    def body(q_ref, kp_ref, kc_ref, kn_ref, vp_ref, vc_ref, vn_ref, o_ref, lse_ref, kbuf, vbuf):
        i = pl.program_id(1)
        _fill(kbuf, kp_ref, kc_ref, kn_ref, tq)
        _fill(vbuf, vp_ref, vc_ref, vn_ref, tq)
        lo = _lo_lanes()
        for a in range(nsub):
            r0 = a * SUB
            rows = slice(r0, r0 + SUB)
            valid = _band_valid(i * tq + r0, i * tq + r0 - HALF, SUB, win, L, False)
            valid2 = jnp.concatenate([valid, valid], axis=0)
            for hp in range(4):
                cs = slice(hp * LANE, (hp + 1) * LANE)
                q = q_ref[rows, cs]
                kw = kbuf[r0:r0 + win, cs]
                vw = vbuf[r0:r0 + win, cs]
                q2 = _stack_heads(q, lo)
                s = lax.dot_general(q2, kw, _NT, preferred_element_type=F32)
                s = jnp.where(valid2, s, -jnp.inf)
                m = jnp.max(s, axis=-1, keepdims=True)
                e = jnp.exp(s - m)
                l = jnp.sum(e, axis=-1, keepdims=True)
                o2 = jnp.dot(e.astype(BF16), vw, preferred_element_type=F32) / l
                lse2 = m + jnp.log(l)
                o_ref[rows, cs] = jnp.where(lo, o2[:SUB], o2[SUB:])
                lse_ref[rows, cs] = jnp.where(lo, lse2[:SUB], lse2[SUB:])

    out_spec = pl.BlockSpec((tq, DIL_W), lambda r, i: (i, r))
    o, lse = pl.pallas_call(
        body, name=name, grid=(d, L // tq),
        out_shape=(jax.ShapeDtypeStruct((L, d * DIL_W), F32), jax.ShapeDtypeStruct((L, d * DIL_W), F32)),
        in_specs=[pl.BlockSpec((tq, DIL_W), lambda r, i: (i, r * 3))]
        + _halo_specs(tq, rpb, n64, lambda r: r * 3 + 1) + _halo_specs(tq, rpb, n64, lambda r: r * 3 + 2),
        out_specs=(out_spec, out_spec),
        scratch_shapes=[pltpu.VMEM((tq + 2 * HALF, DIL_W), BF16), pltpu.VMEM((tq + 2 * HALF, DIL_W), BF16)],
        compiler_params=_params(2),
    )(view, view, view, view, view, view, view)
    return o, lse


def _band_bwd_q(dilr, d, do, lse, dlt, dil, name):
    L = dilr.shape[0]
    S = L * d
    tq, nsub, rpb, n64 = _band_tiles(L)
    view = dilr
    v4 = lambda t: t.reshape(L, d * DIL_W)
    tv = lambda t: t.reshape(L, d * LANE)
    win = SUB + 2 * HALF

    def body(q_ref, kp_ref, kc_ref, kn_ref, vp_ref, vc_ref, vn_ref, do_ref, lse_ref, dlt_ref, c_ref, s1_ref, s2_ref, dq_ref,
             kbuf, vbuf):
        i = pl.program_id(1)
        _fill(kbuf, kp_ref, kc_ref, kn_ref, tq)
        _fill(vbuf, vp_ref, vc_ref, vn_ref, tq)
        lo = _lo_lanes()
        for a in range(nsub):
            r0 = a * SUB
            rows = slice(r0, r0 + SUB)
            valid = _band_valid(i * tq + r0, i * tq + r0 - HALF, SUB, win, L, False)
            valid2 = jnp.concatenate([valid, valid], axis=0)
            cv, s1v, s2v = c_ref[rows, :], s1_ref[rows, :], s2_ref[rows, :]
            for hp in range(4):
                cs = slice(hp * LANE, (hp + 1) * LANE)
                kw = kbuf[r0:r0 + win, cs]
                vw = vbuf[r0:r0 + win, cs]
                q2 = _stack_heads(q_ref[rows, cs], lo)
                do2 = _stack_heads(do_ref[rows, cs], lo).astype(BF16)
                s = lax.dot_general(q2, kw, _NT, preferred_element_type=F32)
                p = jnp.where(valid2, jnp.exp(s - _stack_cols(lse_ref[rows, cs])), 0.0)
                dp = lax.dot_general(do2, vw, _NT, preferred_element_type=F32)
                ds = (p * (dp - _stack_cols(dlt_ref[rows, cs]))).astype(BF16)
                dq2 = jnp.dot(ds, kw, preferred_element_type=F32)
                dq = jnp.where(lo, dq2[:SUB], dq2[SUB:])
                dq_ref[rows, cs] = (_rot(dq, cv, -s1v, -s2v, ROT // 2) * (DIL_HD ** -0.5)).astype(BF16)

    row = pl.BlockSpec((tq, DIL_W), lambda r, i: (i, r))
    tab = pl.BlockSpec((tq, LANE), lambda r, i: (i, r))
    dq = pl.pallas_call(
        body, name=name, grid=(d, L // tq), out_shape=jax.ShapeDtypeStruct((L, d * DIL_W), BF16),
        in_specs=[pl.BlockSpec((tq, DIL_W), lambda r, i: (i, r * 3))]
        + _halo_specs(tq, rpb, n64, lambda r: r * 3 + 1) + _halo_specs(tq, rpb, n64, lambda r: r * 3 + 2)
        + [row, row, row, tab, tab, tab],
        out_specs=row,
        scratch_shapes=[pltpu.VMEM((tq + 2 * HALF, DIL_W), BF16), pltpu.VMEM((tq + 2 * HALF, DIL_W), BF16)],
        compiler_params=_params(2),
    )(view, view, view, view, view, view, view, v4(do), v4(lse), v4(dlt), tv(dil["c"]), tv(dil["s1"]), tv(dil["s2"]))
    return dq


def _band_bwd_kv(dilr, d, do, lse, dlt, dil, name):
    L = dilr.shape[0]
    S = L * d
    tq, nsub, rpb, n64 = _band_tiles(L)
    view = dilr
    v4 = lambda t: t.reshape(L, d * DIL_W)
    tv = lambda t: t.reshape(L, d * LANE)
    win = SUB + 2 * HALF

    def body(k_ref, v_ref, qp_ref, qc_ref, qn_ref, dop_ref, doc_ref, don_ref, lp_ref, lc_ref, ln_ref, tp_ref, tc_ref, tn_ref,
             c_ref, s1_ref, s2_ref, dk_ref, dv_ref, qbuf, dobuf, lbuf, tbuf):
        j = pl.program_id(1)
        _fill(qbuf, qp_ref, qc_ref, qn_ref, tq)
        _fill(dobuf, dop_ref, doc_ref, don_ref, tq)
        _fill(lbuf, lp_ref, lc_ref, ln_ref, tq)
        _fill(tbuf, tp_ref, tc_ref, tn_ref, tq)
        lo = _lo_lanes()
        quarter = (lax.broadcasted_iota(jnp.int32, (1, LANE), 1) & (DIL_HD - 1)) < DIL_HD // 2
        for a in range(nsub):
            r0 = a * SUB
            rows = slice(r0, r0 + SUB)
            wrows = slice(r0, r0 + win)
            kpos = j * tq + r0 + lax.broadcasted_iota(jnp.int32, (SUB, 1), 0)
            qpos = j * tq + r0 - HALF + lax.broadcasted_iota(jnp.int32, (1, win), 1)
            valid = (jnp.abs(qpos - kpos) <= HALF) & (qpos >= 0) & (qpos < L)
            valid2 = jnp.concatenate([valid, valid], axis=1)
            cv, s1v, s2v = c_ref[rows, :], s1_ref[rows, :], s2_ref[rows, :]
            for hp in range(4):
                cs = slice(hp * LANE, (hp + 1) * LANE)
                k = k_ref[rows, cs]
                v = v_ref[rows, cs]
                q2 = _stack_heads(qbuf[wrows, cs], lo)
                do2 = _stack_heads(dobuf[wrows, cs], lo).astype(BF16)
                zt = jnp.transpose(jnp.where(quarter, lbuf[wrows, cs], tbuf[wrows, cs]))
                lse_row = jnp.concatenate([zt[0:1, :], zt[DIL_HD:DIL_HD + 1, :]], axis=1)
                dlt_row = jnp.concatenate([zt[DIL_HD // 2:DIL_HD // 2 + 1, :], zt[3 * DIL_HD // 2:3 * DIL_HD // 2 + 1, :]], axis=1)
                st = lax.dot_general(k, q2, _NT, preferred_element_type=F32)
                pt = jnp.where(valid2, jnp.exp(st - lse_row), 0.0)
                dv = jnp.dot(pt.astype(BF16), do2, preferred_element_type=F32)
                dpt = lax.dot_general(v, do2, _NT, preferred_element_type=F32)
                dst = (pt * (dpt - dlt_row)).astype(BF16)
                dk = jnp.dot(dst, q2, preferred_element_type=F32)
                dk_ref[rows, cs] = _rot(dk, cv, -s1v, -s2v, ROT // 2).astype(BF16)
                dv_ref[rows, cs] = dv.astype(BF16)

    row = pl.BlockSpec((tq, DIL_W), lambda r, i: (i, r))
    tab = pl.BlockSpec((tq, LANE), lambda r, i: (i, r))
    halo = _halo_specs(tq, rpb, n64, lambda r: r)
    hb = tq + 2 * HALF
    dk, dv = pl.pallas_call(
        body, name=name, grid=(d, L // tq),
        out_shape=(jax.ShapeDtypeStruct((L, d * DIL_W), BF16), jax.ShapeDtypeStruct((L, d * DIL_W), BF16)),
        in_specs=[pl.BlockSpec((tq, DIL_W), lambda r, i: (i, r * 3 + 1)), pl.BlockSpec((tq, DIL_W), lambda r, i: (i, r * 3 + 2))]
        + _halo_specs(tq, rpb, n64, lambda r: r * 3) + halo + halo + halo + [tab, tab, tab],
        out_specs=(row, row),
        scratch_shapes=[pltpu.VMEM((hb, DIL_W), BF16), pltpu.VMEM((hb, DIL_W), F32), pltpu.VMEM((hb, DIL_W), F32), pltpu.VMEM((hb, DIL_W), F32)],
        compiler_params=_params(2),
    )(view, view, view, view, view, v4(do), v4(do), v4(do), v4(lse), v4(lse), v4(lse), v4(dlt), v4(dlt), v4(dlt),
      tv(dil["c"]), tv(dil["s1"]), tv(dil["s2"]))
    return dk, dv


def _sigmoid(x):
    return 1.0 / (1.0 + jnp.exp(-x))


def _gate_a(o, p, name):
    S = o.shape[0]
    tm = ROW_TILE

    def body(o_ref, g_ref, a_ref):
        lo = _lo_lanes()
        for pr in range(HEADS // 2):
            halves = []
            for h in (2 * pr, 2 * pr + 1):
                hs = slice(h * LANE, (h + 1) * LANE)
                g = g_ref[:, hs]
                halves.append(o_ref[:, hs] * (g * _sigmoid(g)))
            a_ref[:, pr * LANE:(pr + 1) * LANE] = jnp.where(lo, halves[0], pltpu.roll(halves[1], VDIM, 1)).astype(BF16)

    blk = pl.BlockSpec((tm, HP), lambda i: (i, 0))
    return pl.pallas_call(
        body, name=name, grid=(S // tm,), out_shape=jax.ShapeDtypeStruct((S, HEADS * VDIM), BF16),
        in_specs=[blk, pl.BlockSpec((tm, HP), lambda i: (i, CB_GA * LANE // HP))],
        out_specs=pl.BlockSpec((tm, HEADS * VDIM), lambda i: (i, 0)), compiler_params=_params(1),
    )(o, p)


def _gate_a_bwd(da, o, p, name):
    S = o.shape[0]
    tm = ROW_TILE

    def body(da_ref, o_ref, g_ref, do_ref, dg_ref):
        lo = _lo_lanes()
        for pr in range(HEADS // 2):
            pair = da_ref[:, pr * LANE:(pr + 1) * LANE]
            das = (jnp.where(lo, pair, 0.0), jnp.where(lo, pltpu.roll(pair, VDIM, 1), 0.0))
            for da, h in zip(das, (2 * pr, 2 * pr + 1)):
                hs = slice(h * LANE, (h + 1) * LANE)
                g = g_ref[:, hs]
                sg = _sigmoid(g)
                do_ref[:, hs] = da * (g * sg)
                dg_ref[:, hs] = (da * o_ref[:, hs] * (sg * (1.0 + g * (1.0 - sg)))).astype(BF16)

    blk = pl.BlockSpec((tm, HP), lambda i: (i, 0))
    return pl.pallas_call(
        body, name=name, grid=(S // tm,),
        out_shape=(jax.ShapeDtypeStruct((S, HP), F32), jax.ShapeDtypeStruct((S, HP), BF16)),
        in_specs=[pl.BlockSpec((tm, HEADS * VDIM), lambda i: (i, 0)), blk, pl.BlockSpec((tm, HP), lambda i: (i, CB_GA * LANE // HP))],
        out_specs=(blk, blk), compiler_params=_params(1),
    )(da, o, p)


def _assemble_dp(dga, ddil, dgb, dckv, dkr, dcq, name):
    S = dga.shape[0]
    tm = ROW_TILE
    nc = DIL_W // LANE
    dils = [d for _, d in DIL_PAIRS for _ in range(3)]

    def body(*refs):
        ga_ref, dil_refs, (gb_ref, ckv_ref, kr_ref, cq_ref, o_ref, scr) = refs[0], refs[1:10], refs[10:]
        o_ref[:, CB_GA * LANE:CB_GA * LANE + HP] = ga_ref[...]
        for t, (x_ref, d) in enumerate(zip(dil_refs, dils)):
            off = CB_DIL * LANE + t * DIL_W
            if d == 1:
                o_ref[:, off:off + DIL_W] = x_ref[...]
            else:
                n = tm // d
                for r in range(d):
                    for c in range(nc):
                        scr.at[c][pl.ds(r, n, stride=d), :] = x_ref[:, r * DIL_W + c * LANE:r * DIL_W + (c + 1) * LANE].astype(F32)
                for c in range(nc):
                    o_ref[:, off + c * LANE:off + (c + 1) * LANE] = scr[c].astype(BF16)
        o_ref[:, CB_GB * LANE:CB_GB * LANE + DIL_W] = gb_ref[...]
        o_ref[:, CB_CKV * LANE:CB_CKV * LANE + KV_LORA] = ckv_ref[...]
        o_ref[:, CB_KR * LANE:(CB_KR + 1) * LANE] = kr_ref[...]
        o_ref[:, CB_CQ * LANE:CB_CQ * LANE + Q_LORA] = cq_ref[...]

    row = lambda w: pl.BlockSpec((tm, w), lambda i: (i, 0))
    return pl.pallas_call(
        body, name=name, grid=(S // tm,), out_shape=jax.ShapeDtypeStruct((S, NP), BF16),
        in_specs=[row(HP)] + [_view_spec(d, DIL_W) for d in dils] + [row(DIL_W), row(KV_LORA), row(LANE), row(Q_LORA)],
        out_specs=row(NP), scratch_shapes=[pltpu.VMEM((nc, tm, LANE), F32)], compiler_params=_params(1),
    )(dga, *ddil, dgb, dckv, dkr, dcq)


def _merge_weights(l0, l1, l2):
    mx = jnp.maximum(jnp.maximum(l0, l1), l2)
    e0, e1, e2 = jnp.exp(l0 - mx), jnp.exp(l1 - mx), jnp.exp(l2 - mx)
    den = e0 + e1 + e2
    return e0 / den, e1 / den, e2 / den


def _view_spec(d, width):
    return pl.BlockSpec((ROW_TILE // d, d * width), lambda i: (i, 0))


def _to_tokens(src_ref, scr, base, d):
    n = ROW_TILE // d
    for r in range(d):
        for c in range(DIL_W // LANE):
            scr.at[base + c][pl.ds(r, n, stride=d), :] = src_ref[:, r * DIL_W + c * LANE:r * DIL_W + (c + 1) * LANE]


def _from_tokens(scr, base, dst_ref, d):
    n = ROW_TILE // d
    for r in range(d):
        for c in range(DIL_W // LANE):
            dst_ref[:, r * DIL_W + c * LANE:r * DIL_W + (c + 1) * LANE] = scr.at[base + c][pl.ds(r, n, stride=d), :]


def _gate_b(os_, ls_, p, name):
    S = p.shape[0]
    tm = ROW_TILE
    nc = DIL_W // LANE
    dils = [d for _, d in DIL_PAIRS]

    def body(o0, o1, o2, l0, l1, l2, g_ref, b_ref, scr):
        for gi, (o_ref, l_ref) in enumerate(((o1, l1), (o2, l2))):
            _to_tokens(o_ref, scr, (2 * gi) * nc, dils[gi + 1])
            _to_tokens(l_ref, scr, (2 * gi + 1) * nc, dils[gi + 1])
        for c in range(nc):
            cs = slice(c * LANE, (c + 1) * LANE)
            a0, a1, a2 = _merge_weights(l0[:, cs], scr[nc + c], scr[3 * nc + c])
            bm = a0 * o0[:, cs] + a1 * scr[c] + a2 * scr[2 * nc + c]
            g = g_ref[:, cs]
            b_ref[:, cs] = (bm * (g * _sigmoid(g))).astype(BF16)

    blk = pl.BlockSpec((tm, DIL_W), lambda i: (i, 0))
    views = [_view_spec(d, DIL_W) for d in dils]
    return pl.pallas_call(
        body, name=name, grid=(S // tm,), out_shape=jax.ShapeDtypeStruct((S, DIL_W), BF16),
        in_specs=views + views + [pl.BlockSpec((tm, DIL_W), lambda i: (i, CB_GB * LANE // DIL_W))], out_specs=blk,
        scratch_shapes=[pltpu.VMEM((4 * nc, tm, LANE), F32)], compiler_params=_params(1),
    )(*os_, *ls_, p)


def _gate_b_bwd(db, os_, ls_, p, name):
    S = p.shape[0]
    tm = ROW_TILE
    nc = DIL_W // LANE
    dils = [d for _, d in DIL_PAIRS]

    def body(db_ref, o0, o1, o2, l0, l1, l2, g_ref, dg_ref, d0, d1, d2, t0, t1, t2, scr, out_scr):
        for gi, (o_ref, l_ref) in enumerate(((o1, l1), (o2, l2))):
            _to_tokens(o_ref, scr, (2 * gi) * nc, dils[gi + 1])
            _to_tokens(l_ref, scr, (2 * gi + 1) * nc, dils[gi + 1])
        lo = _lo_lanes()
        for c in range(nc):
            cs = slice(c * LANE, (c + 1) * LANE)
            a0, a1, a2 = _merge_weights(l0[:, cs], scr[nc + c], scr[3 * nc + c])
            bm = a0 * o0[:, cs] + a1 * scr[c] + a2 * scr[2 * nc + c]
            g = g_ref[:, cs]
            db = db_ref[:, cs]
            sg = _sigmoid(g)
            dbm = db * (g * sg)
            dg_ref[:, cs] = (db * bm * (sg * (1.0 + g * (1.0 - sg)))).astype(BF16)
            prod = dbm * bm
            tl = jnp.sum(jnp.where(lo, prod, 0.0), axis=-1, keepdims=True)
            th = jnp.sum(jnp.where(lo, 0.0, prod), axis=-1, keepdims=True)
            t = jnp.where(lo, tl, th)
            d0[:, cs] = a0 * dbm
            t0[:, cs] = a0 * t
            out_scr[c] = a1 * dbm
            out_scr[nc + c] = a1 * t
            out_scr[2 * nc + c] = a2 * dbm
            out_scr[3 * nc + c] = a2 * t
        _from_tokens(out_scr, 0, d1, dils[1])
        _from_tokens(out_scr, nc, t1, dils[1])
        _from_tokens(out_scr, 2 * nc, d2, dils[2])
        _from_tokens(out_scr, 3 * nc, t2, dils[2])

    blk = pl.BlockSpec((tm, DIL_W), lambda i: (i, 0))
    views = [_view_spec(d, DIL_W) for d in dils]
    fs = [jax.ShapeDtypeStruct((S // d, d * DIL_W), F32) for d in dils]
    outs = pl.pallas_call(
        body, name=name, grid=(S // tm,),
        out_shape=(jax.ShapeDtypeStruct((S, DIL_W), BF16), *fs, *fs),
        in_specs=[blk] + views + views + [pl.BlockSpec((tm, DIL_W), lambda i: (i, CB_GB * LANE // DIL_W))],
        out_specs=(blk, *views, *views),
        scratch_shapes=[pltpu.VMEM((4 * nc, tm, LANE), F32), pltpu.VMEM((4 * nc, tm, LANE), F32)], compiler_params=_params(1),
    )(db, *os_, *ls_, p)
    return outs[0], outs[1:4], outs[4:7]


def _loss_head(x, target, g, name):
    S, D = x.shape
    tm = ROW_TILE

    def body(x_ref, t_ref, g_ref, dx_ref, dg_ref, loss_ref):
        xv = x_ref[...]
        gv = g_ref[...]
        r = lax.rsqrt(jnp.mean(xv * xv, axis=-1, keepdims=True) + EPS)
        xr = xv * r
        err = xr * gv - t_ref[...]
        lpart = 0.5 * jnp.sum(jnp.mean(err * err, axis=-1, keepdims=True), axis=0, keepdims=True)
        dy = err / D
        dyg = dy * gv
        c = jnp.mean(dyg * xv, axis=-1, keepdims=True)
        dx_ref[...] = r * dyg - xv * (r * r * r) * c
        gpart = jnp.sum(dy * xr, axis=0, keepdims=True)

        @pl.when(pl.program_id(0) == 0)
        def _():
            dg_ref[...] = gpart
            loss_ref[...] = jnp.broadcast_to(lpart, loss_ref.shape)

        @pl.when(pl.program_id(0) > 0)
        def _():
            dg_ref[...] += gpart
            loss_ref[...] += jnp.broadcast_to(lpart, loss_ref.shape)

    row = pl.BlockSpec((tm, D), lambda i: (i, 0))
    vec = pl.BlockSpec((1, D), lambda i: (0, 0))
    return pl.pallas_call(
        body, name=name, grid=(S // tm,),
        out_shape=(jax.ShapeDtypeStruct((S, D), F32), jax.ShapeDtypeStruct((1, D), F32), jax.ShapeDtypeStruct((1, D), F32)),
        in_specs=[row, row, vec], out_specs=(row, vec, vec), compiler_params=_params(1),
    )(x, target, g.reshape(1, D))


def _adamw(parts, w, m, v, name):
    R, C = w.shape
    tr = _pick(R, (128, 64, 32, 16, 8))

    def body(p_ref, w_ref, m_ref, v_ref, g_ref, d_ref, nm_ref, nv_ref):
        g = p_ref[0].astype(F32)
        for k in range(1, N_DEV):
            g = g + p_ref[k].astype(F32)
        m2 = ADAM_B1 * m_ref[...] + (1.0 - ADAM_B1) * g
        v2 = ADAM_B2 * v_ref[...] + (1.0 - ADAM_B2) * (g * g)
        m_hat = m2 / (1.0 - ADAM_B1 ** ADAM_STEP)
        v_hat = v2 / (1.0 - ADAM_B2 ** ADAM_STEP)
        g_ref[...] = g
        d_ref[...] = -ADAM_LR * (m_hat / (jnp.sqrt(v_hat) + ADAM_EPS) + ADAM_WD * w_ref[...])
        nm_ref[...] = m2
        nv_ref[...] = v2

    blk = pl.BlockSpec((tr, C), lambda i: (i, 0))
    f = jax.ShapeDtypeStruct((R, C), F32)
    return pl.pallas_call(
        body, name=name, grid=(R // tr,), out_shape=(f, f, f, f),
        in_specs=[pl.BlockSpec((N_DEV, tr, C), lambda i: (0, i, 0)), blk, blk, blk], out_specs=(blk,) * 4,
        compiler_params=_params(1),
    )(parts, w, m, v)


def _exchange_copies(src_refs, out_refs, send_sems, recv_sems, local_sems, same_src):
    n = len(src_refs)
    x, y, c = lax.axis_index("x"), lax.axis_index("y"), lax.axis_index("c")
    me = 4 * x + 2 * y + c

    def block(t, j):
        return src_refs[t] if same_src else src_refs[t].at[j]

    local = [pltpu.make_async_copy(block(t, me), out_refs[t].at[me], local_sems.at[t]) for t in range(n)]
    remote = []
    for k in range(1, N_DEV):
        px = 1 - x if (k >> 2) & 1 else x
        py = 1 - y if (k >> 1) & 1 else y
        pc = 1 - c if k & 1 else c
        for t in range(n):
            remote.append(pltpu.make_async_remote_copy(
                src_ref=block(t, 4 * px + 2 * py + pc), dst_ref=out_refs[t].at[me],
                send_sem=send_sems.at[(k - 1) * n + t], recv_sem=recv_sems.at[(k - 1) * n + t],
                device_id=(px, py, pc), device_id_type=pl.DeviceIdType.MESH))
    return local, remote


def _exchange_start(*args):
    local, remote = _exchange_copies(*args)
    for cp in local + remote:
        cp.start()


def _exchange_wait(*args):
    local, remote = _exchange_copies(*args)
    for cp in remote:
        cp.wait()
    for cp in local:
        cp.wait()


def _exchange_shapes(srcs, same_src):
    return tuple(jax.ShapeDtypeStruct((N_DEV,) + (tuple(s.shape) if same_src else tuple(s.shape[1:])), s.dtype) for s in srcs)


def _exchange_sems(n):
    return [pltpu.SemaphoreType.DMA(((N_DEV - 1) * n,)), pltpu.SemaphoreType.DMA(((N_DEV - 1) * n,)), pltpu.SemaphoreType.DMA((n,))]


def _exchange(srcs, same_src, name):
    n = len(srcs)

    def body(*refs):
        args = (refs[:n], refs[n:2 * n], *refs[2 * n:], same_src)
        _exchange_start(*args)
        _exchange_wait(*args)

    hbm = pl.BlockSpec(memory_space=pltpu.HBM)
    outs = pl.pallas_call(
        body, name=name, out_shape=_exchange_shapes(srcs, same_src),
        in_specs=[hbm] * n, out_specs=(hbm,) * n, scratch_shapes=_exchange_sems(n),
    )(*srcs)
    return list(outs)


def _full_weights(g_in, g_uq, g_ukv, g_out):
    cat_cols = lambda t: jnp.moveaxis(t, 0, 2).reshape(t.shape[1], t.shape[2], -1)
    return cat_cols(g_in), cat_cols(g_uq), cat_cols(g_ukv), jnp.moveaxis(g_out, 0, 1).reshape(g_out.shape[1], D_MODEL, D_MODEL)


def _grad_blocks(g_in, g_uq, g_ukv, g_out):
    split_cols = lambda t: jnp.moveaxis(t.astype(BF16).reshape(t.shape[0], N_DEV, -1), 1, 0)
    return [split_cols(g_in), split_cols(g_uq), split_cols(g_ukv), g_out.astype(BF16).reshape(N_DEV, D_MODEL // N_DEV, D_MODEL)]


def _layer_weights(w_in, w_uq, w_ukv, w_out):
    z = lambda r, n: jnp.zeros((r, n), BF16)
    c_q, c_kv, k_r = w_in[:, 0:384], w_in[:, 384:640], w_in[:, 640:672]
    gate_a, dil, gate_b = w_in[:, 672:1184], w_in[:, 1184:5792], w_in[:, 5792:6304]
    ga_pad = jnp.pad(gate_a.reshape(D_MODEL, HEADS, VDIM), ((0, 0), (0, 0), (0, LANE - VDIM))).reshape(D_MODEL, HP)
    w_p = jnp.concatenate([ga_pad, dil, gate_b, c_kv, z(D_MODEL, 64), k_r, z(D_MODEL, 32), c_q], axis=1)
    uq = jnp.pad(w_uq.reshape(Q_LORA, HEADS, NOPE + ROPE), ((0, 0), (0, 0), (0, LANE - NOPE - ROPE))).reshape(Q_LORA, HP)
    ukv = w_ukv.reshape(KV_LORA, HEADS, NOPE + VDIM)
    pad64 = lambda t: jnp.pad(t, ((0, 0), (0, 0), (0, LANE - 64))).reshape(KV_LORA, HP)
    uk, uv = pad64(ukv[..., :NOPE]), pad64(ukv[..., NOPE:])
    wa = w_out[:HEADS * VDIM]
    wb = w_out[HEADS * VDIM:]
    return dict(p=w_p, uq=uq, uk=uk, uv=uv, wa=wa, wb=wb)


def _unpad_grads(g):
    gp = g["p"]
    seg = lambda cb, n: gp[:, cb * LANE:cb * LANE + n]
    ga = seg(CB_GA, HP).reshape(D_MODEL, HEADS, LANE)[..., :VDIM].reshape(D_MODEL, HEADS * VDIM)
    k_r = gp[:, CB_KR * LANE + NOPE:CB_KR * LANE + NOPE + ROPE]
    g_in = jnp.concatenate([seg(CB_CQ, Q_LORA), seg(CB_CKV, KV_LORA), k_r, ga, seg(CB_DIL, 9 * DIL_W), seg(CB_GB, DIL_W)], axis=1)
    g_uq = g["uq"].reshape(Q_LORA, HEADS, LANE)[..., :NOPE + ROPE].reshape(Q_LORA, -1)
    uk = g["uk"].reshape(KV_LORA, HEADS, LANE)[..., :NOPE]
    uv = g["uv"].reshape(KV_LORA, HEADS, LANE)[..., :VDIM]
    g_ukv = jnp.concatenate([uk, uv], axis=-1).reshape(KV_LORA, -1)
    g_out = jnp.concatenate([g["wa"], g["wb"]], axis=0)
    return g_in, g_uq, g_ukv, g_out


def _layer_fwd(x, w, norm_g, q_norm_g, kv_norm_g, mla, dil, l, comm=None):
    n = lambda s: f"l{l}_{s}"
    h = _rms_fwd(x, 0, D_MODEL, norm_g, n("norm"))
    p = _mm(h, w["p"], "nn", n("in_proj"))
    cqn = _rms_fwd(p, CB_CQ * LANE // Q_LORA, Q_LORA, q_norm_g, n("q_norm"))
    ckvn = _rms_fwd(p, CB_CKV * LANE // KV_LORA, KV_LORA, kv_norm_g, n("kv_norm"))
    qp = _mm(cqn, w["uq"], "nn", n("q_up"))
    kpre = _mm(ckvn, w["uk"], "nn", n("k_up"))
    v = _mm(ckvn, w["uv"], "nn", n("v_up"), out_dtype=BF16)
    q = _rope_heads(qp, mla["c_q"], mla["s1"], mla["s2"], n("q_rope"), BF16, scale=QK_LOG2_SCALE)
    k = _k_assemble(kpre, p, mla, n("k_asm"))
    o, lse, received = _flash_fwd(q, k, v, n("mla_fwd"), comm)
    a = _gate_a(o, p, n("gate_a"))
    dilr, os_, ls_ = [], [], []
    for g, (_, d) in enumerate(DIL_PAIRS):
        dilr.append(_dil_prep(p, dil, g, d, n(f"dil_prep{g}")))
        og, lg = _band_fwd(dilr[g], d, n(f"band{g}_fwd"))
        os_.append(og)
        ls_.append(lg)
    b = _gate_b(os_, ls_, p, n("gate_b"))
    x1 = _mm(a, w["wa"], "nn", n("out_a"), res=x)
    x2 = _mm(b, w["wb"], "nn", n("out_b"), res=x1)
    saved = dict(x=x, h=h, p=p, cqn=cqn, ckvn=ckvn, q=q, k=k, v=v, o=o, lse=lse, a=a, dilr=dilr, os=os_, ls=ls_, b=b)
    return x2, saved, received


def _layer_bwd(dx, sv, w, norm_g, q_norm_g, kv_norm_g, mla, dil, l, comm=None):
    n = lambda s: f"l{l}_{s}"
    g = {}
    da = _mm(dx, w["wa"], "nt", n("d_a"))
    db = _mm(dx, w["wb"], "nt", n("d_b"))
    g["wa"] = _mm(sv["a"], dx, "tn", n("dw_a"))
    g["wb"] = _mm(sv["b"], dx, "tn", n("dw_b"))
    do, dga = _gate_a_bwd(da, sv["o"], sv["p"], n("gate_a_bwd"))
    dgb, dos, dts = _gate_b_bwd(db, sv["os"], sv["ls"], sv["p"], n("gate_b_bwd"))
    ddil = []
    for gi, (_, d) in enumerate(DIL_PAIRS):
        dq = _band_bwd_q(sv["dilr"][gi], d, dos[gi], sv["ls"][gi], dts[gi], dil, n(f"band{gi}_bwd_q"))
        dk, dv = _band_bwd_kv(sv["dilr"][gi], d, dos[gi], sv["ls"][gi], dts[gi], dil, n(f"band{gi}_bwd_kv"))
        ddil += [dq, dk, dv]
    dq, dk, dv, received = _flash_bwd(sv["q"], sv["k"], sv["v"], sv["o"], do, sv["lse"], n("mla_bwd"), comm)
    dqp = _rope_heads(dq, mla["c_q"], -mla["s1"], -mla["s2"], n("q_rope_bwd"), BF16)
    dkr = _kr_bwd(dk, mla, n("kr_bwd"))
    g["uq"] = _mm(sv["cqn"], dqp, "tn", n("dw_uq"))
    g["uk"] = _mm(sv["ckvn"], dk, "tn", n("dw_uk"))
    g["uv"] = _mm(sv["ckvn"], dv, "tn", n("dw_uv"))
    dcqn = _mm(dqp, w["uq"], "nt", n("d_cqn"))
    dckvn = _mm(dk, w["uk"], "nt", n("d_ckvn_k"))
    dckvn = _mm(dv, w["uv"], "nt", n("d_ckvn_v"), res=dckvn)
    dcq, g_qn = _rms_bwd(sv["p"], CB_CQ * LANE // Q_LORA, Q_LORA, dcqn, q_norm_g, n("q_norm_bwd"), BF16)
    dckv, g_kvn = _rms_bwd(sv["p"], CB_CKV * LANE // KV_LORA, KV_LORA, dckvn, kv_norm_g, n("kv_norm_bwd"), BF16)
    dp = _assemble_dp(dga, ddil, dgb, dckv, dkr, dcq, n("dp"))
    g["p"] = _mm(sv["h"], dp, "tn", n("dw_in"))
    dh = _mm(dp, w["p"], "nt", n("d_h"))
    dx_in, g_n = _rms_bwd(sv["x"], 0, D_MODEL, dh, norm_g, n("norm_bwd"), F32, res=dx)
    return dx_in, g, g_n, g_qn, g_kvn, received


_SMALL_ROWS = 16


def _pack_small(norm, qn, kvn, fin, loss_row=None):
    padc = lambda t: jnp.pad(t, ((0, 0), (0, D_MODEL - t.shape[1])))
    extra = jnp.zeros((1, D_MODEL), F32) if loss_row is None else loss_row
    return jnp.concatenate([norm, padc(qn), padc(kvn), fin.reshape(1, D_MODEL), extra, jnp.zeros((2, D_MODEL), F32)], axis=0)


def _unpack_small(p):
    return (p[0:4], p[4:8, :Q_LORA], p[8:12, :KV_LORA], p[12]), p[13, 0]


def kernel(x, norm_g, w_in, q_norm_g, kv_norm_g, w_uq, w_ukv, w_out, final_g, loss_target, m_norm_g, m_w_in, m_q_norm_g, m_kv_norm_g, m_w_uq, m_w_ukv, m_w_out, m_final_g, v_norm_g, v_w_in, v_q_norm_g, v_kv_norm_g, v_w_uq, v_w_ukv, v_w_out, v_final_g):
    S = x.shape[1]
    xs = x.reshape(S, D_MODEL)
    target = loss_target.reshape(S, D_MODEL)

    wb = [t.astype(BF16) for t in (w_in, w_uq, w_ukv, w_out)]
    first = _full_weights(*_exchange([t[0:1] for t in wb], True, "gather_weights0"))
    ws = [_layer_weights(*(t[0] for t in first))]
    mla, dil = _rope_tables(S)

    saved = []
    h = xs
    for l in range(DEPTH):
        comm = ([t[1:] for t in wb], True) if l == 0 else None
        h, sv, received = _layer_fwd(h, ws[l], norm_g[l], q_norm_g[l], kv_norm_g[l], mla, dil, l, comm)
        saved.append(sv)
        if l == 0:
            rest = _full_weights(*received)
            ws += [_layer_weights(*(t[i] for t in rest)) for i in range(DEPTH - 1)]
    dx, g_final, loss_row = _loss_head(h, target, final_g, "loss_head")
    g_norm, g_qn, g_kvn, parts_l = [None] * DEPTH, [None] * DEPTH, [None] * DEPTH, [None] * DEPTH
    blocks = None
    for l in reversed(range(DEPTH)):
        comm = (blocks, False) if blocks is not None else None
        dx, g, g_norm[l], g_qn[l], g_kvn[l], received = _layer_bwd(dx, saved[l], ws[l], norm_g[l], q_norm_g[l], kv_norm_g[l], mla, dil, l, comm)
        if blocks is not None:
            parts_l[l + 1] = received
        blocks = _grad_blocks(*_unpad_grads(g))
    parts_l[0] = _exchange(blocks, False, "exchange_grads0")

    parts = [jnp.stack([parts_l[l][t] for l in range(DEPTH)], axis=1) for t in range(4)]
    sh = []
    for t, (pt, w, m, v) in enumerate(zip(parts, (w_in, w_uq, w_ukv, w_out), (m_w_in, m_w_uq, m_w_ukv, m_w_out), (v_w_in, v_w_uq, v_w_ukv, v_w_out))):
        two = lambda a: a.reshape(-1, a.shape[-1])
        outs = _adamw(pt.reshape(N_DEV, -1, pt.shape[-1]), two(w), two(m), two(v), f"adamw_{t}")
        sh.append([o.reshape(w.shape) for o in outs])

    small = _pack_small(jnp.concatenate(g_norm, 0), jnp.concatenate(g_qn, 0), jnp.concatenate(g_kvn, 0), g_final, loss_row)
    (small_parts,) = _exchange([small], True, "gather_small")
    souts = _adamw(small_parts, _pack_small(norm_g, q_norm_g, kv_norm_g, final_g), _pack_small(m_norm_g, m_q_norm_g, m_kv_norm_g, m_final_g),
                   _pack_small(v_norm_g, v_q_norm_g, v_kv_norm_g, v_final_g), "adamw_small")
    (g_sm, loss), (d_sm, _), (m_sm, _), (v_sm, _) = (_unpack_small(t) for t in souts)

    def order(sm, k):
        return (sm[0], sh[0][k], sm[1], sm[2], sh[1][k], sh[2][k], sh[3][k], sm[3])

    return (loss, dx.reshape(1, S, D_MODEL), *order(g_sm, 0), *order(d_sm, 1), *order(m_sm, 2), *order(v_sm, 3))
```
